```python
import math
import jax
import jax.numpy as jnp
from jax import lax
import numpy as np

D_MODEL = 1024
BATCH = 8
SEQ = 4096
DEPTH = 2

CTX_LEN = 256
GRID_W = 64
HEAD_DIM = 64
ROPE_THETA = 10000.0
Q_BLOCK = 128
EPS = 1e-6
A_HEADS = 8
A_KV_HEADS = 2
B_HEADS = 4
B_QK_DIM = 64
B_V_DIM = 128
C_HEADS = 4
C_DIM = 128
C_CHUNK = 64
CONV_W = 3
D_FF = 2816
N_EXPERTS = 8
TOP_K = 2
D_FF_EXPERT = 1408

IN_SPLITS = (A_HEADS * HEAD_DIM, A_KV_HEADS * HEAD_DIM, A_KV_HEADS * HEAD_DIM,
             B_HEADS * 2 * B_QK_DIM, B_HEADS * 2 * B_QK_DIM, B_HEADS * B_V_DIM,
             C_HEADS * C_DIM, C_HEADS * C_DIM, C_HEADS * C_DIM, C_HEADS * C_DIM, 4 * C_HEADS,
             3 * D_MODEL)
N_IN = sum(IN_SPLITS)

kernel_name = "hybrid_gqa_diffattn_mlstm_moe_dit"


def rms_norm(x, g):
    xf = x.astype(jnp.float32)
    y = xf * lax.rsqrt(jnp.mean(xf * xf, axis=-1, keepdims=True) + EPS)
    return (y * g.astype(jnp.float32)).astype(x.dtype)


def split_heads(x, n):
    b, t, _ = x.shape
    return x.reshape(b, t, n, -1).transpose(0, 2, 1, 3)


def merge_heads(x):
    b, h, t, d = x.shape
    return x.transpose(0, 2, 1, 3).reshape(b, t, h * d)


def split_columns(h):
    out, off = [], 0
    for w in IN_SPLITS:
        out.append(h[..., off:off + w])
        off += w
    return out


def axial_angles(n_tok, dim):
    rows = n_tok // GRID_W
    row = jnp.repeat(jnp.arange(rows, dtype=jnp.float32), GRID_W)
    col = (jnp.arange(n_tok) % GRID_W).astype(jnp.float32)
    n_freq = dim // 4
    inv = ROPE_THETA ** (-jnp.arange(n_freq, dtype=jnp.float32) / n_freq)
    return row[:, None] * inv, col[:, None] * inv


def rope_half(x, ang):
    nf = ang.shape[-1]
    x1, x2 = x[..., :nf], x[..., nf:]
    cs, sn = jnp.cos(ang).astype(x.dtype), jnp.sin(ang).astype(x.dtype)
    return jnp.concatenate([x1 * cs - x2 * sn, x2 * cs + x1 * sn], axis=-1)


def axial_rope(x, ang_r, ang_c):
    h = x.shape[-1] // 2
    return jnp.concatenate([rope_half(x[..., :h], ang_r), rope_half(x[..., h:], ang_c)], axis=-1)


def full_attention(q, k, v, scale):
    s = jnp.einsum('bgrqd,bgkd->bgrqk', q, k, preferred_element_type=jnp.float32) * scale
    p = jax.nn.softmax(s, axis=-1).astype(v.dtype)
    return jnp.einsum('bgrqk,bgkv->bgrqv', p, v)


def block_attention(q, k, v, kc, vc, scale):
    k_all = jnp.concatenate([kc, k], axis=2)
    v_all = jnp.concatenate([vc, v], axis=2)
    b, g, r, t, d = q.shape
    nb = t // Q_BLOCK
    qb = jnp.moveaxis(q.reshape(b, g, r, nb, Q_BLOCK, d), 3, 0)
    o = lax.map(lambda qq: full_attention(qq, k_all, v_all, scale), qb)
    return jnp.moveaxis(o, 0, 3).reshape(b, g, r, t, -1)


def gqa_mixer(pl, pc, q_g, k_g, angs, with_ctx):
    def prep(q, k, v):
        return (rms_norm(split_heads(q, A_HEADS), q_g), rms_norm(split_heads(k, A_KV_HEADS), k_g),
                split_heads(v, A_KV_HEADS))
    ql, kl, vl = prep(*pl)
    qc, kc, vc = prep(*pc)
    ql = axial_rope(ql, *angs)
    kl = axial_rope(kl, *angs)
    rep = A_HEADS // A_KV_HEADS

    def group(q):
        b, h, t, d = q.shape
        return q.reshape(b, A_KV_HEADS, rep, t, d)

    def ungroup(o):
        b, g, r, t, d = o.shape
        return merge_heads(o.reshape(b, g * r, t, d))
    scale = HEAD_DIM ** -0.5
    out_l = ungroup(block_attention(group(ql), kl, vl, kc, vc, scale))
    out_c = ungroup(full_attention(group(qc), kc, vc, scale)) if with_ctx else None
    return out_l, out_c


def diff_mixer(pl, pc, lam_vecs, sub_g, lam_init, angs, with_ctx):
    def prep(q, k, v):
        return (split_heads(q, 2 * B_HEADS), split_heads(k, 2 * B_HEADS),
                jnp.repeat(split_heads(v, B_HEADS), 2, axis=1))
    ql, kl, vl = prep(*pl)
    qc, kc, vc = prep(*pc)
    ql = axial_rope(ql, *angs)
    kl = axial_rope(kl, *angs)
    lq1, lk1, lq2, lk2 = [a.astype(jnp.float32) for a in lam_vecs]
    lam = jnp.exp(jnp.sum(lq1 * lk1)) - jnp.exp(jnp.sum(lq2 * lk2)) + lam_init
    scale = B_QK_DIM ** -0.5

    def combine(o):
        b, _, _, t, dv = o.shape
        o = o.reshape(b, B_HEADS, 2, t, dv).astype(jnp.float32)
        dif = o[:, :, 0] - lam * o[:, :, 1]
        dif = rms_norm(dif, sub_g) * (1.0 - lam_init)
        return merge_heads(dif.astype(vl.dtype))
    out_l = combine(block_attention(ql[:, :, None], kl, vl, kc, vc, scale))
    out_c = combine(full_attention(qc[:, :, None], kc, vc, scale)) if with_ctx else None
    return out_l, out_c


def centred_conv(x, w, b):
    k = w.shape[0]
    p = k // 2
    t = x.shape[1]
    xp = jnp.pad(x, ((0, 0), (p, k - 1 - p), (0, 0)))
    y = xp[:, 0:t] * w[0]
    for j in range(1, k):
        y = y + xp[:, j:j + t] * w[j]
    return y + b


def mlstm_chunk_scan(q, k, v, i_pre, logf, state):
    b, h, t, dk = q.shape
    dv = v.shape[-1]
    L = C_CHUNK
    nc = t // L

    def chunks(a):
        return jnp.moveaxis(a.reshape((b, h, nc, L) + a.shape[3:]), 2, 0)
    xs = (chunks(q), chunks(k), chunks(v), chunks(i_pre), chunks(logf))
    tri = jnp.tril(jnp.ones((L, L), dtype=bool))

    def step(carry, inp):
        C, n, m = carry
        qc, kc, vc, ic, fc = inp
        bcum = jnp.cumsum(fc, axis=-1)
        logD = bcum[..., :, None] - bcum[..., None, :] + ic[..., None, :]
        logD = jnp.where(tri, logD, -jnp.inf)
        inter = bcum + m[..., None]
        m_t = jnp.maximum(inter, jnp.max(logD, axis=-1))
        dmat = jnp.exp(logD - m_t[..., None])
        a_inter = jnp.exp(inter - m_t)
        s = jnp.einsum('bhtd,bhsd->bhts', qc, kc, preferred_element_type=jnp.float32) * dmat
        num = a_inter[..., None] * jnp.einsum('bhvd,bhtd->bhtv', C, qc) + jnp.einsum('bhts,bhsv->bhtv', s, vc)
        den = a_inter * jnp.einsum('bhd,bhtd->bht', n, qc) + jnp.sum(s, axis=-1)
        hc = num / jnp.maximum(jnp.abs(den), jnp.exp(-m_t))[..., None]
        b_last = bcum[..., -1]
        w_log = b_last[..., None] - bcum + ic
        m_new = jnp.maximum(b_last + m, jnp.max(w_log, axis=-1))
        decay = jnp.exp(b_last + m - m_new)
        ws = jnp.exp(w_log - m_new[..., None])
        C_new = decay[..., None, None] * C + jnp.einsum('bhs,bhsv,bhsd->bhvd', ws, vc, kc)
        n_new = decay[..., None] * n + jnp.einsum('bhs,bhsd->bhd', ws, kc)
        return (C_new, n_new, m_new), hc
    state, hs = lax.scan(step, state, xs)
    return jnp.moveaxis(hs, 0, 2).reshape(b, h, t, dv).astype(v.dtype), state


def mlstm_mixer(pl, pc, conv_w, conv_b, gate_b, norm_g, with_ctx):
    def prep(q_pre, k_pre, v, o_pre, g_pre):
        b, t, _ = v.shape
        qk = jax.nn.silu(centred_conv(jnp.concatenate([q_pre, k_pre], axis=-1), conv_w, conv_b))
        q = split_heads(qk[..., :C_HEADS * C_DIM], C_HEADS)
        k = split_heads(qk[..., C_HEADS * C_DIM:], C_HEADS) * (C_DIM ** -0.5)
        vh = split_heads(v, C_HEADS)
        g = (g_pre.astype(jnp.float32) + gate_b.astype(jnp.float32)).reshape(b, t, 4, C_HEADS)
        g = g.transpose(2, 0, 3, 1)
        fwd = (g[0], jax.nn.log_sigmoid(g[1]))
        bwd = (g[2], jax.nn.log_sigmoid(g[3]))
        return q, k, vh, o_pre, fwd, bwd
    ql, kl, vl, ol, fl, bl = prep(*pl)
    qc, kc, vc, oc, fc, bc = prep(*pc)

    def flip(a):
        return jnp.flip(a, axis=2)

    def run(q, k, v, gates, state, reverse):
        if reverse:
            hh, st = mlstm_chunk_scan(flip(q), flip(k), flip(v), flip(gates[0]), flip(gates[1]), state)
            return flip(hh), st
        return mlstm_chunk_scan(q, k, v, gates[0], gates[1], state)
    bsz = qc.shape[0]
    zero = (jnp.zeros((bsz, C_HEADS, C_DIM, C_DIM), jnp.float32),
            jnp.zeros((bsz, C_HEADS, C_DIM), jnp.float32),
            jnp.zeros((bsz, C_HEADS), jnp.float32))
    hcf, st_f = run(qc, kc, vc, fc, zero, False)
    hcb, st_b = run(qc, kc, vc, bc, zero, True)
    hlf, _ = run(ql, kl, vl, fl, st_f, False)
    hlb, _ = run(ql, kl, vl, bl, st_b, True)

    def out(hf, hb, o_pre):
        return merge_heads(rms_norm(hf + hb, norm_g)) * jax.nn.sigmoid(o_pre)
    out_l = out(hlf, hlb, ol)
    out_c = out(hcf, hcb, oc) if with_ctx else None
    return out_l, out_c


def token_mixer(xl, xc, angs, lam_init, with_ctx, w_in, q_g, k_g, lq1, lk1, lq2, lk2, diff_g,
                conv_w, conv_b, gate_b, mlstm_g, w_br_a, w_br_b, w_br_c, w_out):
    cl = split_columns(xl @ w_in)
    cc = split_columns(xc @ w_in)
    a_l, a_c = gqa_mixer(cl[0:3], cc[0:3], q_g, k_g, angs, with_ctx)
    b_l, b_c = diff_mixer(cl[3:6], cc[3:6], (lq1, lk1, lq2, lk2), diff_g, lam_init, angs, with_ctx)
    m_l, m_c = mlstm_mixer(cl[6:11], cc[6:11], conv_w, conv_b, gate_b, mlstm_g, with_ctx)

    def merge(a, b, m, gate_pre):
        g_a, g_b, g_m = jnp.split(jax.nn.sigmoid(gate_pre.astype(jnp.float32)).astype(a.dtype), 3, axis=-1)
        return (g_a * (a @ w_br_a) + g_b * (b @ w_br_b) + g_m * (m @ w_br_c)) @ w_out
    y_l = merge(a_l, b_l, m_l, cl[11])
    y_c = merge(a_c, b_c, m_c, cc[11]) if with_ctx else None
    return y_l, y_c


def swiglu(x, wg, wu, wd):
    return (jax.nn.silu(x @ wg) * (x @ wu)) @ wd


def moe_swiglu(x, w_r, b_r, wg, wu, wd):
    shp = x.shape
    xt = x.reshape(-1, shp[-1])
    logits = (xt @ w_r).astype(jnp.float32) + b_r.astype(jnp.float32)
    top_v, top_i = lax.top_k(logits, TOP_K)
    top_w = jax.nn.softmax(top_v, axis=-1)
    gates = jnp.sum(top_w[..., None] * jax.nn.one_hot(top_i, N_EXPERTS, dtype=jnp.float32), axis=1)
    out = jnp.zeros_like(xt)
    for e in range(N_EXPERTS):
        out = out + gates[:, e:e + 1].astype(xt.dtype) * swiglu(xt, wg[e], wu[e], wd[e])
    return out.reshape(shp)


def setup_inputs(seed: int = 0) -> dict:
    key = jax.random.key(seed)
    ks = iter(jax.random.split(key, 48))
    f32 = jnp.float32
    D = D_MODEL
    L = DEPTH
    nd = (DEPTH + 1) // 2
    nm = DEPTH // 2

    def nrm(shape, scale):
        return jax.random.normal(next(ks), shape, f32) * scale

    def gain(shape):
        return 1.0 + nrm(shape, 0.05)
    i_bias = nrm((L, 2, C_HEADS), 0.1)
    f_bias = jnp.linspace(3.0, 6.0, C_HEADS, dtype=f32)[None, None, :] + nrm((L, 2, C_HEADS), 0.1)
    gate_b = jnp.stack([i_bias[:, 0], f_bias[:, 0], i_bias[:, 1], f_bias[:, 1]], axis=1).reshape(L, 4 * C_HEADS)
    return {
        "x": nrm((BATCH, SEQ, D), 1.0),
        "c": nrm((BATCH, D), 1.0),
        "ctx": nrm((BATCH, CTX_LEN, D), 1.0),
        "c_ctx": nrm((D,), 1.0),
        "ada_w": nrm((L, D, 6 * D), 0.5 * D ** -0.5),
        "ada_b": nrm((L, 6 * D), 0.02),
        "pre_mix_g": gain((L, D)),
        "post_mix_g": gain((L, D)),
        "pre_ffn_g": gain((L, D)),
        "post_ffn_g": gain((L, D)),
        "w_in": nrm((L, D, N_IN), D ** -0.5),
        "q_norm_g": gain((L, HEAD_DIM)),
        "k_norm_g": gain((L, HEAD_DIM)),
        "lam_q1": nrm((L, B_QK_DIM), 0.1),
        "lam_k1": nrm((L, B_QK_DIM), 0.1),
        "lam_q2": nrm((L, B_QK_DIM), 0.1),
        "lam_k2": nrm((L, B_QK_DIM), 0.1),
        "diff_norm_g": gain((L, B_V_DIM)),
        "conv_w": nrm((L, CONV_W, 2 * C_HEADS * C_DIM), CONV_W ** -0.5),
        "conv_b": nrm((L, 2 * C_HEADS * C_DIM), 0.02),
        "mlstm_gate_b": gate_b,
        "mlstm_norm_g": gain((L, C_DIM)),
        "w_br_attn": nrm((L, A_HEADS * HEAD_DIM, D), (A_HEADS * HEAD_DIM) ** -0.5),
        "w_br_diff": nrm((L, B_HEADS * B_V_DIM, D), (B_HEADS * B_V_DIM) ** -0.5),
        "w_br_mlstm": nrm((L, C_HEADS * C_DIM, D), (C_HEADS * C_DIM) ** -0.5),
        "w_out": nrm((L, D, D), D ** -0.5),
        "w_ff_gate": nrm((nd, D, D_FF), D ** -0.5),
        "w_ff_up": nrm((nd, D, D_FF), D ** -0.5),
        "w_ff_down": nrm((nd, D_FF, D), D_FF ** -0.5),
        "w_router": nrm((nm, D, N_EXPERTS), D ** -0.5),
        "b_router": nrm((nm, N_EXPERTS), 0.01),
        "w_moe_gate": nrm((nm, N_EXPERTS, D, D_FF_EXPERT), D ** -0.5),
        "w_moe_up": nrm((nm, N_EXPERTS, D, D_FF_EXPERT), D ** -0.5),
        "w_moe_down": nrm((nm, N_EXPERTS, D_FF_EXPERT, D), D_FF_EXPERT ** -0.5),
    }


def reference(x, c, ctx, c_ctx, ada_w, ada_b, pre_mix_g, post_mix_g, pre_ffn_g, post_ffn_g, w_in,
              q_norm_g, k_norm_g, lam_q1, lam_k1, lam_q2, lam_k2, diff_norm_g, conv_w, conv_b,
              mlstm_gate_b, mlstm_norm_g, w_br_attn, w_br_diff, w_br_mlstm, w_out, w_ff_gate, w_ff_up,
              w_ff_down, w_router, b_router, w_moe_gate, w_moe_up, w_moe_down):
    n_tok = x.shape[1]
    angs = axial_angles(n_tok, HEAD_DIM)
    for l in range(DEPTH):
        last = l == DEPTH - 1
        lam_init = 0.8 - 0.6 * math.exp(-0.3 * l)
        mod = jax.nn.silu(c) @ ada_w[l] + ada_b[l]
        mod_c = jax.nn.silu(c_ctx) @ ada_w[l] + ada_b[l]
        sh1, sc1, g1, sh2, sc2, g2 = jnp.split(mod[:, None, :], 6, axis=-1)
        sh1c, sc1c, g1c, sh2c, sc2c, g2c = jnp.split(mod_c, 6)
        xn = rms_norm(x, pre_mix_g[l]) * (1.0 + sc1) + sh1
        cn = rms_norm(ctx, pre_mix_g[l]) * (1.0 + sc1c) + sh1c
        y, yc = token_mixer(xn, cn, angs, lam_init, not last, w_in[l], q_norm_g[l], k_norm_g[l],
                            lam_q1[l], lam_k1[l], lam_q2[l], lam_k2[l], diff_norm_g[l], conv_w[l], conv_b[l],
                            mlstm_gate_b[l], mlstm_norm_g[l], w_br_attn[l], w_br_diff[l], w_br_mlstm[l], w_out[l])
        x = x + g1 * rms_norm(y, post_mix_g[l])
        if not last:
            ctx = ctx + g1c * rms_norm(yc, post_mix_g[l])

        def channel(h):
            if l % 2 == 0:
                j = l // 2
                return swiglu(h, w_ff_gate[j], w_ff_up[j], w_ff_down[j])
            j = l // 2
            return moe_swiglu(h, w_router[j], b_router[j], w_moe_gate[j], w_moe_up[j], w_moe_down[j])
        xn = rms_norm(x, pre_ffn_g[l]) * (1.0 + sc2) + sh2
        x = x + g2 * rms_norm(channel(xn), post_ffn_g[l])
        if not last:
            cn = rms_norm(ctx, pre_ffn_g[l]) * (1.0 + sc2c) + sh2c
            ctx = ctx + g2c * rms_norm(channel(cn), post_ffn_g[l])
    return x
```

```python
import functools
import math

import jax
import jax.numpy as jnp
from jax import lax
from jax.experimental import pallas as pl
from jax.experimental.pallas import tpu as pltpu

F32 = jnp.float32
BF16 = jnp.bfloat16
HIGHEST = lax.Precision.HIGHEST

EPS = 1e-6
HEAD_DIM = 64
A_HEADS = 8
A_KV_HEADS = 2
B_HEADS = 4
C_HEADS = 4
C_DIM = 128
N_EXPERTS = 8
ROPE_THETA = 10000.0
GRID_W = 64
CONV_W = 3

LANES = 128
SUBLANES = 8
ROW = 256
MCHUNK = 128
NEG = -1e30
VMEM_LIMIT = 56 * 1024 * 1024

OFF_GATE = 0
REL_QA, REL_QB, REL_KB, REL_VB = 0, 512, 1024, 1536
REL_QC, REL_KC, REL_VC, REL_OC = 2048, 2560, 3072, 3584
REL_KA = 4096
REL_END = 4352


def _cparams(sem):
    return pltpu.CompilerParams(dimension_semantics=sem, vmem_limit_bytes=VMEM_LIMIT)


def _rms(x, g):
    y = x * lax.rsqrt(jnp.mean(x * x, axis=-1, keepdims=True) + EPS)
    return y * g


def _sigmoid(x):
    return 1.0 / (1.0 + jnp.exp(-x))


def _silu(x):
    return x * _sigmoid(x)


def _log_sigmoid(x):
    return jnp.minimum(x, 0.0) - jnp.log(1.0 + jnp.exp(-jnp.abs(x)))


def _lane_iota(shape):
    return lax.broadcasted_iota(jnp.int32, shape, len(shape) - 1)


def _row_iota(shape):
    return lax.broadcasted_iota(jnp.int32, shape, len(shape) - 2)


def _mod_kernel(c_ref, w_ref, b_ref, o_ref):
    c = c_ref[...]
    o_ref[...] = jnp.dot(_silu(c), w_ref[...], preferred_element_type=F32, precision=HIGHEST) + b_ref[...]


def _modulation(c_all, ada_w, ada_b):
    depth, d, _ = ada_w.shape
    nw = c_all.shape[0]
    out = pl.pallas_call(
        _mod_kernel,
        grid=(depth, 6),
        in_specs=[
            pl.BlockSpec((nw, d), lambda l, j: (0, 0)),
            pl.BlockSpec((None, d, d), lambda l, j: (l, 0, j)),
            pl.BlockSpec((None, 1, d), lambda l, j: (l, 0, j)),
        ],
        out_specs=pl.BlockSpec((None, None, nw, d), lambda l, j: (l, j, 0, 0)),
        out_shape=jax.ShapeDtypeStruct((depth, 6, nw, d), F32),
        compiler_params=_cparams(("arbitrary", "arbitrary")),
        name="modulation",
    )(c_all, ada_w, ada_b.reshape(depth, 1, 6 * d))
    return out.reshape(depth * 6 * nw, 1, d)


def _mod_spec(d, layer, chunk, nw, n_batch):
    base = (layer * 6 + chunk) * nw
    return pl.BlockSpec((None, 1, d), lambda b, i: (base + jnp.where(i == 0, n_batch, b), 0, 0))


def _inproj_kernel(x_ref, sh_ref, sc_ref, g_ref, w_ref, wg_ref, h_ref, gate_ref, *, chunks):
    xn = _rms(x_ref[...], g_ref[...]) * (1.0 + sc_ref[...]) + sh_ref[...]
    xb = xn.astype(BF16)
    for start, width in chunks:
        h_ref[:, start:start + width] = jnp.dot(
            xb, w_ref[:, start:start + width], preferred_element_type=F32).astype(BF16)
    gate_ref[...] = jnp.dot(xb, wg_ref[...], preferred_element_type=F32)


def _inproj(xs, mod, layer, nw, pre_g, w_main, w_gate):
    n_batch, ts, d = xs.shape
    n_main = w_main.shape[1]
    nt = ts // ROW
    chunks = [(s, min(1024, n_main - s)) for s in range(0, n_main, 1024)]
    return pl.pallas_call(
        functools.partial(_inproj_kernel, chunks=chunks),
        grid=(n_batch, nt),
        in_specs=[
            pl.BlockSpec((None, ROW, d), lambda b, i: (b, i, 0)),
            _mod_spec(d, layer, 0, nw, n_batch),
            _mod_spec(d, layer, 1, nw, n_batch),
            pl.BlockSpec((1, d), lambda b, i: (0, 0)),
            pl.BlockSpec((d, n_main), lambda b, i: (0, 0), pipeline_mode=pl.Buffered(1)),
            pl.BlockSpec((d, LANES), lambda b, i: (0, 0)),
        ],
        out_specs=[
            pl.BlockSpec((None, ROW, n_main), lambda b, i: (b, i, 0)),
            pl.BlockSpec((None, ROW, LANES), lambda b, i: (b, i, 0)),
        ],
        out_shape=[
            jax.ShapeDtypeStruct((n_batch, ts, n_main), BF16),
            jax.ShapeDtypeStruct((n_batch, ts, LANES), F32),
        ],
        compiler_params=_cparams(("parallel", "parallel")),
        name="inproj",
    )(xs, mod, mod, pre_g, w_main, w_gate)


def _head_mean_sq(x, bd_ref):
    sq = x * x
    hi = sq.astype(BF16)
    lo = (sq - hi.astype(F32)).astype(BF16)
    bd = bd_ref[...]
    return jnp.dot(hi, bd, preferred_element_type=F32) + jnp.dot(lo, bd, preferred_element_type=F32)


def _prep_kernel(qa_ref, kava_ref, qb_ref, kb_ref, vb_ref, qkc_ref, prev_ref, next_ref, g_ref,
                 cos_ref, sa_ref, sb_ref, qg_ref, kg_ref, bd4_ref, bd1_ref, cw_ref, cb_ref, gb_ref,
                 qaz_ref, ka_ref, va_ref, qbz_ref, kbo_ref, vbo_ref, qm_ref, km_ref, gl_ref, glt_ref):
    i = pl.program_id(1)
    nt = pl.num_programs(1)
    cos, sin_a, sin_b = cos_ref[...], sa_ref[...], sb_ref[...]

    def rope(x):
        width = x.shape[1]
        reps = width // LANES
        c = jnp.concatenate([cos] * reps, axis=1) if reps > 1 else cos
        a = jnp.concatenate([sin_a] * reps, axis=1) if reps > 1 else sin_a
        b = jnp.concatenate([sin_b] * reps, axis=1) if reps > 1 else sin_b
        return x * c + pltpu.roll(x, width - 16, 1) * a + pltpu.roll(x, 16, 1) * b

    lane = _lane_iota((ROW, LANES))
    ones = jnp.ones((ROW, LANES), BF16)
    scale = HEAD_DIM ** -0.5

    qa = qa_ref[...].astype(F32)
    qa = qa * lax.rsqrt(_head_mean_sq(qa, bd4_ref) + EPS) * qg_ref[...]
    qa = rope(qa) * scale
    heads_per_kv = A_HEADS // A_KV_HEADS
    for h in range(A_HEADS):
        g = h // heads_per_kv
        blk = qa[:, (h // 2) * LANES:(h // 2 + 1) * LANES]
        if h % 2 != g:
            blk = pltpu.roll(blk, HEAD_DIM, 1)
        qaz_ref[h] = jnp.where(lane // HEAD_DIM == g, blk, 0.0).astype(BF16)
    kava = kava_ref[...].astype(F32)
    ka = kava[:, :LANES]
    ka = ka * lax.rsqrt(_head_mean_sq(ka, bd1_ref) + EPS) * kg_ref[...]
    ka_ref[...] = rope(ka).astype(BF16)
    va_ref[:, :LANES] = kava_ref[:, LANES:]
    va_ref[:, LANES:] = ones

    qb = rope(qb_ref[...].astype(F32)) * scale
    kb = rope(kb_ref[...].astype(F32))
    for h in range(B_HEADS):
        blk = qb[:, h * LANES:(h + 1) * LANES]
        for m in range(2):
            qbz_ref[2 * h + m] = jnp.where(lane // HEAD_DIM == m, blk, 0.0).astype(BF16)
        kbo_ref[h] = kb[:, h * LANES:(h + 1) * LANES].astype(BF16)
        vbo_ref[h, :, :LANES] = vb_ref[:, h * LANES:(h + 1) * LANES]
        vbo_ref[h, :, LANES:] = ones

    cur = qkc_ref[...].astype(F32)
    row = _row_iota(cur.shape)
    prev_row = jnp.where(i >= 2, prev_ref[SUBLANES - 1:SUBLANES, :].astype(F32), 0.0)
    next_ok = jnp.logical_and(i >= 1, i < nt - 1)
    next_row = jnp.where(next_ok, next_ref[0:1, :].astype(F32), 0.0)
    up = jnp.where(row == 0, prev_row, pltpu.roll(cur, 1, 0))
    dn = jnp.where(row == ROW - 1, next_row, pltpu.roll(cur, ROW - 1, 0))
    y = up * cw_ref[0:1, :] + cur * cw_ref[1:2, :] + dn * cw_ref[2:3, :] + cb_ref[...]
    y = _silu(y)
    half = C_HEADS * C_DIM
    qm_ref[...] = y[:, :half].astype(BF16)
    km_ref[...] = (y[:, half:] * (C_DIM ** -0.5)).astype(BF16)

    gg = g_ref[...] + gb_ref[...]
    is_forget = (lane // C_HEADS) % 2 == 1
    gl = jnp.where(is_forget, _log_sigmoid(gg), gg)
    gl_ref[...] = gl
    glt_ref[...] = gl.T[:2 * SUBLANES, :]


def _prep(h, gates, tables, qg, kg, bd4, bd1, conv_w, conv_b, gate_b, d):
    n_batch, ts, _ = h.shape
    nt = ts // ROW
    base = 3 * d

    def col(rel, width):
        return (base + rel) // width

    cos_t, sa_t, sb_t = tables
    row_blocks = ts // SUBLANES
    per_tile = ROW // SUBLANES
    cqk = col(REL_QC, 1024)

    def const(shape):
        return pl.BlockSpec(shape, lambda b, i: (0,) * len(shape))

    in_specs = [
        pl.BlockSpec((None, ROW, 512), lambda b, i: (b, i, col(REL_QA, 512))),
        pl.BlockSpec((None, ROW, 256), lambda b, i: (b, i, col(REL_KA, 256))),
        pl.BlockSpec((None, ROW, 512), lambda b, i: (b, i, col(REL_QB, 512))),
        pl.BlockSpec((None, ROW, 512), lambda b, i: (b, i, col(REL_KB, 512))),
        pl.BlockSpec((None, ROW, 512), lambda b, i: (b, i, col(REL_VB, 512))),
        pl.BlockSpec((None, ROW, 1024), lambda b, i: (b, i, cqk)),
        pl.BlockSpec((None, SUBLANES, 1024), lambda b, i: (b, jnp.maximum(i * per_tile - 1, 0), cqk)),
        pl.BlockSpec((None, SUBLANES, 1024),
                     lambda b, i: (b, jnp.minimum((i + 1) * per_tile, row_blocks - 1), cqk)),
        pl.BlockSpec((None, ROW, LANES), lambda b, i: (b, i, 0)),
        pl.BlockSpec((ROW, LANES), lambda b, i: (i, 0)),
        pl.BlockSpec((ROW, LANES), lambda b, i: (i, 0)),
        pl.BlockSpec((ROW, LANES), lambda b, i: (i, 0)),
        const((1, 512)), const((1, LANES)), const((512, 512)), const((LANES, LANES)),
        const((CONV_W, 1024)), const((1, 1024)), const((1, LANES)),
    ]
    out_specs = [
        pl.BlockSpec((None, A_HEADS, ROW, LANES), lambda b, i: (b, 0, i, 0)),
        pl.BlockSpec((None, ROW, LANES), lambda b, i: (b, i, 0)),
        pl.BlockSpec((None, ROW, 2 * LANES), lambda b, i: (b, i, 0)),
        pl.BlockSpec((None, 2 * B_HEADS, ROW, LANES), lambda b, i: (b, 0, i, 0)),
        pl.BlockSpec((None, B_HEADS, ROW, LANES), lambda b, i: (b, 0, i, 0)),
        pl.BlockSpec((None, B_HEADS, ROW, 2 * LANES), lambda b, i: (b, 0, i, 0)),
        pl.BlockSpec((None, ROW, 512), lambda b, i: (b, i, 0)),
        pl.BlockSpec((None, ROW, 512), lambda b, i: (b, i, 0)),
        pl.BlockSpec((None, ROW, LANES), lambda b, i: (b, i, 0)),
        pl.BlockSpec((None, 2 * SUBLANES, ROW), lambda b, i: (b, 0, i)),
    ]
    out_shape = [
        jax.ShapeDtypeStruct((n_batch, A_HEADS, ts, LANES), BF16),
        jax.ShapeDtypeStruct((n_batch, ts, LANES), BF16),
        jax.ShapeDtypeStruct((n_batch, ts, 2 * LANES), BF16),
        jax.ShapeDtypeStruct((n_batch, 2 * B_HEADS, ts, LANES), BF16),
        jax.ShapeDtypeStruct((n_batch, B_HEADS, ts, LANES), BF16),
        jax.ShapeDtypeStruct((n_batch, B_HEADS, ts, 2 * LANES), BF16),
        jax.ShapeDtypeStruct((n_batch, ts, 512), BF16),
        jax.ShapeDtypeStruct((n_batch, ts, 512), BF16),
        jax.ShapeDtypeStruct((n_batch, ts, LANES), F32),
        jax.ShapeDtypeStruct((n_batch, 2 * SUBLANES, ts), F32),
    ]
    return pl.pallas_call(
        _prep_kernel,
        grid=(n_batch, nt),
        in_specs=in_specs,
        out_specs=out_specs,
        out_shape=out_shape,
        compiler_params=_cparams(("parallel", "parallel")),
        name="prep",
    )(h, h, h, h, h, h, h, h, gates, cos_t, sa_t, sb_t, qg, kg, bd4, bd1, conv_w, conv_b, gate_b)


def _attend(q, k_ref, v_ref, n_latent_chunks, chunk):
    m_rows = q.shape[0]

    def step(carry, start, size):
        m_run, acc = carry
        k = k_ref[pl.ds(start, size), :]
        v = v_ref[pl.ds(start, size), :]
        s = lax.dot_general(q, k, (((1,), (1,)), ((), ())), preferred_element_type=F32)
        m_new = jnp.maximum(m_run, jnp.max(s, axis=-1, keepdims=True))
        alpha = jnp.exp(m_run - m_new)
        p = jnp.exp(s - m_new).astype(BF16)
        acc = alpha * acc + jnp.dot(p, v, preferred_element_type=F32)
        return m_new, acc

    carry = (jnp.full((m_rows, 1), NEG, F32), jnp.zeros((m_rows, 2 * LANES), F32))
    carry = step(carry, 0, ROW)

    def body(c, cr):
        return step(cr, pl.multiple_of(ROW + c * chunk, ROW), chunk)

    _, acc = lax.fori_loop(0, n_latent_chunks, body, carry)
    return acc[:, :LANES] / acc[:, LANES:LANES + 1]


def _gqa_kernel(q_ref, k_ref, v_ref, o_ref, *, n_chunks, chunk):
    i = pl.program_id(1)
    pair = pl.program_id(2)
    q = q_ref[...].reshape(2 * ROW, LANES)
    o = _attend(q, k_ref, v_ref, jnp.where(i == 0, 0, n_chunks), chunk)
    o_even, o_odd = o[:ROW], o[ROW:]
    kv_first = pair < (A_HEADS // A_KV_HEADS) // 2
    left = jnp.where(kv_first, o_even, pltpu.roll(o_even, HEAD_DIM, 1))
    right = jnp.where(kv_first, pltpu.roll(o_odd, HEAD_DIM, 1), o_odd)
    o_ref[...] = jnp.where(_lane_iota((ROW, LANES)) < HEAD_DIM, left, right).astype(BF16)


def _diff_kernel(q_ref, k_ref, v_ref, lam_ref, g_ref, o_ref, *, n_chunks, chunk, lam_init):
    i = pl.program_id(1)
    q = q_ref[...].reshape(2 * ROW, LANES)
    o = _attend(q, k_ref, v_ref, jnp.where(i == 0, 0, n_chunks), chunk)
    lv = lam_ref[...]
    lam = (jnp.exp(jnp.sum(lv[0:1] * lv[1:2], axis=-1, keepdims=True))
           - jnp.exp(jnp.sum(lv[2:3] * lv[3:4], axis=-1, keepdims=True)) + lam_init)
    dif = o[:ROW] - lam * o[ROW:]
    o_ref[...] = (_rms(dif, g_ref[...]) * (1.0 - lam_init)).astype(BF16)


def _latent_chunk(t):
    return min(512, t)


def _gqa(qaz, ka, va):
    n_batch, _, ts, _ = qaz.shape
    nt = ts // ROW
    chunk = _latent_chunk(ts - ROW)
    n_pairs = A_HEADS // 2
    return pl.pallas_call(
        functools.partial(_gqa_kernel, n_chunks=(ts - ROW) // chunk, chunk=chunk),
        grid=(n_batch, nt, n_pairs),
        in_specs=[
            pl.BlockSpec((None, 2, ROW, LANES), lambda b, i, p: (b, p, i, 0)),
            pl.BlockSpec((None, ts, LANES), lambda b, i, p: (b, 0, 0)),
            pl.BlockSpec((None, ts, 2 * LANES), lambda b, i, p: (b, 0, 0)),
        ],
        out_specs=pl.BlockSpec((None, ROW, LANES), lambda b, i, p: (b, i, p)),
        out_shape=jax.ShapeDtypeStruct((n_batch, ts, A_HEADS * HEAD_DIM), BF16),
        compiler_params=_cparams(("parallel", "parallel", "parallel")),
        name="gqa_attention",
    )(qaz, ka, va)


def _diff(qbz, kb, vb, lam_vecs, sub_g, lam_init):
    n_batch, _, ts, _ = qbz.shape
    nt = ts // ROW
    chunk = _latent_chunk(ts - ROW)
    return pl.pallas_call(
        functools.partial(_diff_kernel, n_chunks=(ts - ROW) // chunk, chunk=chunk, lam_init=lam_init),
        grid=(n_batch, nt, B_HEADS),
        in_specs=[
            pl.BlockSpec((None, 2, ROW, LANES), lambda b, i, h: (b, h, i, 0)),
            pl.BlockSpec((None, None, ts, LANES), lambda b, i, h: (b, h, 0, 0)),
            pl.BlockSpec((None, None, ts, 2 * LANES), lambda b, i, h: (b, h, 0, 0)),
            pl.BlockSpec((4, HEAD_DIM), lambda b, i, h: (0, 0)),
            pl.BlockSpec((1, LANES), lambda b, i, h: (0, 0)),
        ],
        out_specs=pl.BlockSpec((None, ROW, LANES), lambda b, i, h: (b, i, h)),
        out_shape=jax.ShapeDtypeStruct((n_batch, ts, B_HEADS * LANES), BF16),
        compiler_params=_cparams(("parallel", "parallel", "parallel")),
        name="diff_attention",
    )(qbz, kb, vb, lam_vecs, sub_g)


def _mlstm_kernel(qf_ref, kf_ref, vf_ref, gf_ref, gtf_ref, qb_ref, kb_ref, vb_ref, gb_ref, gtb_ref,
                  tri_ref, trit_ref, hf_ref, hb_ref, c_ref, n_ref, m_ref):
    @pl.when(pl.program_id(1) == 0)
    def _():
        c_ref[...] = jnp.zeros_like(c_ref)
        n_ref[...] = jnp.zeros_like(n_ref)
        m_ref[...] = jnp.zeros_like(m_ref)

    length = MCHUNK
    tri = tri_ref[...]
    trit = trit_ref[...]
    r_idx = _row_iota((length, length))
    c_idx = _lane_iota((length, length))
    nt_dims = (((1,), (1,)), ((), ()))
    tn_dims = (((0,), (0,)), ((), ()))

    for direction, (q_ref, k_ref, v_ref, g_ref, gt_ref, h_ref) in enumerate(
            ((qf_ref, kf_ref, vf_ref, gf_ref, gtf_ref, hf_ref), (qb_ref, kb_ref, vb_ref, gb_ref, gtb_ref, hb_ref))):
        gcol = g_ref[...]
        grow = gt_ref[...]
        cs_col = jnp.dot(tri, gcol, preferred_element_type=F32, precision=HIGHEST)
        cs_row = jnp.dot(grow, trit, preferred_element_type=F32, precision=HIGHEST)
        mask = (r_idx >= c_idx) if direction == 0 else (c_idx >= r_idx)
        for hd in range(C_HEADS):
            chain = direction * C_HEADS + hd
            ii = direction * 2 * C_HEADS + hd
            fi = ii + C_HEADS
            icol, irow = gcol[:, ii:ii + 1], grow[ii:ii + 1, :]
            pcol, prow = cs_col[:, fi:fi + 1], cs_row[fi:fi + 1, :]
            total = pcol[length - 1:length, :]
            if direction == 1:
                pcol = total - pcol + gcol[:, fi:fi + 1]
                prow = total - prow + grow[fi:fi + 1, :]
            m_prev = m_ref[chain][:, :1]
            log_d = jnp.where(mask, pcol - prow + irow, NEG)
            inter = pcol + m_prev
            m_t = jnp.maximum(inter, jnp.max(log_d, axis=-1, keepdims=True))
            dmat = jnp.exp(log_d - m_t)
            a_inter = jnp.exp(inter - m_t)
            sl = slice(hd * C_DIM, (hd + 1) * C_DIM)
            q, k, v = q_ref[:, sl], k_ref[:, sl], v_ref[:, sl]
            s = lax.dot_general(q, k, nt_dims, preferred_element_type=F32) * dmat
            c_state = c_ref[chain]
            n_state = n_ref[chain]
            num = (a_inter * lax.dot_general(q, c_state.astype(BF16), nt_dims, preferred_element_type=F32)
                   + jnp.dot(s.astype(BF16), v, preferred_element_type=F32))
            den = (a_inter * jnp.sum(q.astype(F32) * n_state, axis=-1, keepdims=True)
                   + jnp.sum(s, axis=-1, keepdims=True))
            h_ref[:, sl] = num / jnp.maximum(jnp.abs(den), jnp.exp(-m_t))
            w_log = total - pcol + icol
            m_new = jnp.maximum(total + m_prev, jnp.max(w_log, axis=0, keepdims=True))
            decay = jnp.exp(total + m_prev - m_new)
            ws = jnp.exp(w_log - m_new)
            wv = (ws * v.astype(F32)).astype(BF16)
            c_ref[chain] = decay * c_state + lax.dot_general(wv, k, tn_dims, preferred_element_type=F32)
            n_ref[chain] = decay * n_state + jnp.sum(ws * k.astype(F32), axis=0, keepdims=True)
            m_ref[chain] = jnp.broadcast_to(m_new, (1, LANES))


def _mlstm(qm, km, h, gl, glt, tri, trit, d):
    n_batch, ts, width = qm.shape
    nc = ts // MCHUNK
    ctx_chunks = ROW // MCHUNK
    vcol = (3 * d + REL_VC) // width

    def bwd(j):
        return jnp.where(j < ctx_chunks, ctx_chunks - 1 - j, nc + ctx_chunks - 1 - j)

    def specs(idx):
        return [
            pl.BlockSpec((None, MCHUNK, width), lambda b, j: (b, idx(j), 0)),
            pl.BlockSpec((None, MCHUNK, width), lambda b, j: (b, idx(j), 0)),
            pl.BlockSpec((None, MCHUNK, width), lambda b, j: (b, idx(j), vcol)),
            pl.BlockSpec((None, MCHUNK, LANES), lambda b, j: (b, idx(j), 0)),
            pl.BlockSpec((None, 2 * SUBLANES, MCHUNK), lambda b, j: (b, 0, idx(j))),
        ]

    fwd = lambda j: j
    n_chain = 2 * C_HEADS
    return pl.pallas_call(
        _mlstm_kernel,
        grid=(n_batch, nc),
        in_specs=specs(fwd) + specs(bwd) + [
            pl.BlockSpec((MCHUNK, MCHUNK), lambda b, j: (0, 0)),
            pl.BlockSpec((MCHUNK, MCHUNK), lambda b, j: (0, 0)),
        ],
        out_specs=[
            pl.BlockSpec((None, MCHUNK, width), lambda b, j: (b, j, 0)),
            pl.BlockSpec((None, MCHUNK, width), lambda b, j: (b, bwd(j), 0)),
        ],
        out_shape=[jax.ShapeDtypeStruct((n_batch, ts, width), F32)] * 2,
        scratch_shapes=[
            pltpu.VMEM((n_chain, C_DIM, C_DIM), F32),
            pltpu.VMEM((n_chain, 1, C_DIM), F32),
            pltpu.VMEM((n_chain, 1, LANES), F32),
        ],
        compiler_params=_cparams(("parallel", "arbitrary")),
        name="mlstm",
    )(qm, km, h, gl, glt, qm, km, h, gl, glt, tri, trit)


def _mix_kernel(a_ref, d_ref, hf_ref, hb_ref, oc_ref, gate_ref, x_ref, g1_ref, pg_ref, mg_ref,
                wa_ref, wb_ref, wc_ref, wo_ref, o_ref):
    d_model = x_ref.shape[-1]
    hsum = hf_ref[...] + hb_ref[...]
    mg = mg_ref[...]
    m = jnp.concatenate([_rms(hsum[:, hd * C_DIM:(hd + 1) * C_DIM], mg) for hd in range(C_HEADS)], axis=1)
    m = m * _sigmoid(oc_ref[...].astype(F32))
    u = (_sigmoid(gate_ref[:, :d_model].astype(F32))
         * jnp.dot(a_ref[...], wa_ref[...], preferred_element_type=F32)
         + _sigmoid(gate_ref[:, d_model:2 * d_model].astype(F32))
         * jnp.dot(d_ref[...], wb_ref[...], preferred_element_type=F32)
         + _sigmoid(gate_ref[:, 2 * d_model:].astype(F32))
         * jnp.dot(m.astype(BF16), wc_ref[...], preferred_element_type=F32))
    y = jnp.dot(u.astype(BF16), wo_ref[...], preferred_element_type=F32)
    o_ref[...] = x_ref[...] + g1_ref[...] * _rms(y, pg_ref[...])


def _mix(a, dd, hf, hb, h, xs, mod, layer, nw, post_g, mlstm_g, wa, wb, wc, wo):
    n_batch, ts, d = xs.shape
    nt = ts // ROW
    width = a.shape[-1]
    ocol = (3 * d + REL_OC) // width

    def tile(w):
        return pl.BlockSpec((None, ROW, w), lambda b, i: (b, i, 0))

    def const(shape):
        return pl.BlockSpec(shape, lambda b, i: (0,) * len(shape))

    return pl.pallas_call(
        _mix_kernel,
        grid=(n_batch, nt),
        in_specs=[
            tile(width), tile(width), tile(width), tile(width),
            pl.BlockSpec((None, ROW, width), lambda b, i: (b, i, ocol)),
            pl.BlockSpec((None, ROW, 3 * d), lambda b, i: (b, i, 0)),
            tile(d),
            _mod_spec(d, layer, 2, nw, n_batch),
            const((1, d)), const((1, C_DIM)),
            const((width, d)), const((width, d)), const((width, d)), const((d, d)),
        ],
        out_specs=tile(d),
        out_shape=jax.ShapeDtypeStruct((n_batch, ts, d), F32),
        compiler_params=_cparams(("parallel", "parallel")),
        name="mix_out",
    )(a, dd, hf, hb, h, h, xs, mod, post_g, mlstm_g, wa, wb, wc, wo)


def _ffn_kernel(x_ref, sh_ref, sc_ref, g2_ref, pre_ref, post_ref, wg_ref, wu_ref, wd_ref, o_ref):
    x = x_ref[...]
    xb = (_rms(x, pre_ref[...]) * (1.0 + sc_ref[...]) + sh_ref[...]).astype(BF16)
    gate = jnp.dot(xb, wg_ref[...], preferred_element_type=F32)
    up = jnp.dot(xb, wu_ref[...], preferred_element_type=F32)
    z = jnp.dot((_silu(gate) * up).astype(BF16), wd_ref[...], preferred_element_type=F32)
    o_ref[...] = x + g2_ref[...] * _rms(z, post_ref[...])


def _ffn(xs, mod, layer, nw, pre_g, post_g, wg, wu, wd):
    n_batch, ts, d = xs.shape
    nt = ts // ROW
    dff = wg.shape[1]

    def resident(shape):
        return pl.BlockSpec(shape, lambda b, i: (0,) * len(shape), pipeline_mode=pl.Buffered(1))

    tile = pl.BlockSpec((None, ROW, d), lambda b, i: (b, i, 0))
    return pl.pallas_call(
        _ffn_kernel,
        grid=(n_batch, nt),
        in_specs=[
            tile,
            _mod_spec(d, layer, 3, nw, n_batch), _mod_spec(d, layer, 4, nw, n_batch),
            _mod_spec(d, layer, 5, nw, n_batch),
            pl.BlockSpec((1, d), lambda b, i: (0, 0)), pl.BlockSpec((1, d), lambda b, i: (0, 0)),
            resident((d, dff)), resident((d, dff)), resident((dff, d)),
        ],
        out_specs=tile,
        out_shape=jax.ShapeDtypeStruct((n_batch, ts, d), F32),
        compiler_params=_cparams(("parallel", "parallel")),
        name="ffn",
    )(xs, mod, mod, mod, pre_g, post_g, wg, wu, wd)


def _route_kernel(x_ref, sh_ref, sc_ref, pre_ref, wr_ref, br_ref, xn_ref, gates_ref):
    xn = _rms(x_ref[...], pre_ref[...]) * (1.0 + sc_ref[...]) + sh_ref[...]
    xn_ref[...] = xn.astype(BF16)
    lane = _lane_iota((ROW, LANES))
    logits = jnp.dot(xn, wr_ref[...], preferred_element_type=F32, precision=HIGHEST) + br_ref[...]
    logits = jnp.where(lane < N_EXPERTS, logits, NEG)
    v1 = jnp.max(logits, axis=-1, keepdims=True)
    i1 = jnp.min(jnp.where(logits == v1, lane, LANES), axis=-1, keepdims=True)
    rest = jnp.where(lane == i1, NEG, logits)
    v2 = jnp.max(rest, axis=-1, keepdims=True)
    i2 = jnp.min(jnp.where(rest == v2, lane, LANES), axis=-1, keepdims=True)
    e2 = jnp.exp(v2 - v1)
    w1 = 1.0 / (1.0 + e2)
    w2 = e2 / (1.0 + e2)
    gates_ref[...] = jnp.where(lane == i1, w1, 0.0) + jnp.where(lane == i2, w2, 0.0)


def _route(xs, mod, layer, nw, pre_g, w_r, b_r):
    n_batch, ts, d = xs.shape
    nt = ts // ROW
    tile = pl.BlockSpec((None, ROW, d), lambda b, i: (b, i, 0))
    return pl.pallas_call(
        _route_kernel,
        grid=(n_batch, nt),
        in_specs=[
            tile, _mod_spec(d, layer, 3, nw, n_batch), _mod_spec(d, layer, 4, nw, n_batch),
            pl.BlockSpec((1, d), lambda b, i: (0, 0)),
            pl.BlockSpec((d, LANES), lambda b, i: (0, 0)),
            pl.BlockSpec((1, LANES), lambda b, i: (0, 0)),
        ],
        out_specs=[tile, pl.BlockSpec((None, ROW, LANES), lambda b, i: (b, i, 0))],
        out_shape=[jax.ShapeDtypeStruct((n_batch, ts, d), BF16),
                   jax.ShapeDtypeStruct((n_batch, ts, LANES), F32)],
        compiler_params=_cparams(("parallel", "parallel")),
        name="route",
    )(xs, mod, mod, pre_g, w_r, b_r)


def _experts_kernel(xn_ref, gates_ref, wg_ref, wu_ref, wd_ref, z_ref, acc_ref):
    e = pl.program_id(1)

    @pl.when(e == 0)
    def _():
        acc_ref[...] = jnp.zeros_like(acc_ref)

    xb = xn_ref[...]
    gates = gates_ref[...]
    w_e = jnp.sum(jnp.where(_lane_iota(gates.shape) == e, gates, 0.0), axis=-1, keepdims=True)
    gate = jnp.dot(xb, wg_ref[...], preferred_element_type=F32)
    up = jnp.dot(xb, wu_ref[...], preferred_element_type=F32)
    acc_ref[...] += w_e * jnp.dot((_silu(gate) * up).astype(BF16), wd_ref[...], preferred_element_type=F32)

    @pl.when(e == pl.num_programs(1) - 1)
    def _():
        z_ref[...] = acc_ref[...].astype(BF16)


def _experts(xn, gates, wg, wu, wd, tm):
    m_rows, d = xn.shape
    n_e, _, dff = wg.shape
    return pl.pallas_call(
        _experts_kernel,
        grid=(m_rows // tm, n_e),
        in_specs=[
            pl.BlockSpec((tm, d), lambda r, e: (r, 0)),
            pl.BlockSpec((tm, LANES), lambda r, e: (r, 0)),
            pl.BlockSpec((None, d, dff), lambda r, e: (e, 0, 0)),
            pl.BlockSpec((None, d, dff), lambda r, e: (e, 0, 0)),
            pl.BlockSpec((None, dff, d), lambda r, e: (e, 0, 0)),
        ],
        out_specs=pl.BlockSpec((tm, d), lambda r, e: (r, 0)),
        out_shape=jax.ShapeDtypeStruct((m_rows, d), BF16),
        scratch_shapes=[pltpu.VMEM((tm, d), F32)],
        compiler_params=_cparams(("parallel", "arbitrary")),
        name="experts",
    )(xn, gates, wg, wu, wd)


def _resid_kernel(x_ref, z_ref, g2_ref, post_ref, o_ref):
    o_ref[...] = x_ref[...] + g2_ref[...] * _rms(z_ref[...].astype(F32), post_ref[...])


def _resid(xs, z, mod, layer, nw, post_g):
    n_batch, ts, d = xs.shape
    nt = ts // ROW
    tile = pl.BlockSpec((None, ROW, d), lambda b, i: (b, i, 0))
    return pl.pallas_call(
        _resid_kernel,
        grid=(n_batch, nt),
        in_specs=[tile, tile, _mod_spec(d, layer, 5, nw, n_batch), pl.BlockSpec((1, d), lambda b, i: (0, 0))],
        out_specs=tile,
        out_shape=jax.ShapeDtypeStruct((n_batch, ts, d), F32),
        compiler_params=_cparams(("parallel", "parallel")),
        name="resid",
    )(xs, z, mod, post_g)


def _rope_tables(n_tok, n_ctx):
    n_freq = HEAD_DIM // 4
    pos = jnp.arange(n_tok)
    row = (pos // GRID_W).astype(F32)
    colp = (pos % GRID_W).astype(F32)
    inv = ROPE_THETA ** (-jnp.arange(n_freq, dtype=F32) / n_freq)
    lane = jnp.arange(LANES)
    in_head = lane % HEAD_DIM
    use_col = (in_head // (HEAD_DIM // 2)) == 1
    freq = inv[in_head % n_freq]
    ang = jnp.where(use_col[None, :], colp[:, None], row[:, None]) * freq[None, :]
    lower = (in_head % (HEAD_DIM // 2)) < n_freq
    cos_t = jnp.cos(ang)
    sin_t = jnp.sin(ang)
    sin_a = jnp.where(lower[None, :], -sin_t, 0.0)
    sin_b = jnp.where(lower[None, :], 0.0, sin_t)
    pad = lambda t, v: jnp.concatenate([jnp.full((n_ctx, LANES), v, F32), t], axis=0)
    return pad(cos_t, 1.0), pad(sin_a, 0.0), pad(sin_b, 0.0)


def _block_diag_mean(width):
    idx = jnp.arange(width) // HEAD_DIM
    return jnp.where(idx[:, None] == idx[None, :], 1.0 / HEAD_DIM, 0.0).astype(BF16)


def _pack_w_in(w):
    a_q = A_HEADS * HEAD_DIM
    a_kv = 2 * A_KV_HEADS * HEAD_DIM
    n_gates = 4 * C_HEADS
    gate_start = w.shape[1] - 3 * w.shape[0]
    g_start = gate_start - n_gates
    main = jnp.concatenate([w[:, gate_start:], w[:, :a_q], w[:, a_q + a_kv:g_start], w[:, a_q:a_q + a_kv]], axis=1)
    gates = jnp.pad(w[:, g_start:gate_start], ((0, 0), (0, LANES - n_gates)))
    return main.astype(BF16), gates.astype(BF16)


def kernel(x, c, ctx, c_ctx, ada_w, ada_b, pre_mix_g, post_mix_g, pre_ffn_g, post_ffn_g, w_in, q_norm_g, k_norm_g, lam_q1, lam_k1, lam_q2, lam_k2, diff_norm_g, conv_w, conv_b, mlstm_gate_b, mlstm_norm_g, w_br_attn, w_br_diff, w_br_mlstm, w_out, w_ff_gate, w_ff_up, w_ff_down, w_router, b_router, w_moe_gate, w_moe_up, w_moe_down):
    n_batch, n_tok, d = x.shape
    n_ctx = ctx.shape[1]
    depth = ada_w.shape[0]
    assert n_ctx == ROW and n_tok % ROW == 0 and d == 1024
    ts = n_ctx + n_tok
    nw = -(-(n_batch + 1) // SUBLANES) * SUBLANES

    c_all = jnp.concatenate([c, c_ctx[None, :], jnp.zeros((nw - n_batch - 1, d), F32)], axis=0)
    mod = _modulation(c_all, ada_w, ada_b)
    tables = _rope_tables(n_tok, n_ctx)
    bd4, bd1 = _block_diag_mean(A_HEADS * HEAD_DIM), _block_diag_mean(LANES)
    t_idx = jnp.arange(MCHUNK)
    tri = (t_idx[:, None] >= t_idx[None, :]).astype(F32)
    trit = tri.T

    xs = jnp.concatenate([ctx, x], axis=1)
    for l in range(depth):
        lam_init = 0.8 - 0.6 * math.exp(-0.3 * l)
        w_main, w_gate = _pack_w_in(w_in[l])
        h, gates = _inproj(xs, mod, l, nw, pre_mix_g[l][None, :], w_main, w_gate)
        qg = jnp.tile(q_norm_g[l], A_HEADS)[None, :]
        kg = jnp.tile(k_norm_g[l], A_KV_HEADS)[None, :]
        gate_b = jnp.pad(mlstm_gate_b[l], (0, LANES - 4 * C_HEADS))[None, :]
        qaz, ka, va, qbz, kb, vb, qm, km, gl, glt = _prep(
            h, gates, tables, qg, kg, bd4, bd1, conv_w[l], conv_b[l][None, :], gate_b, d)
        a_out = _gqa(qaz, ka, va)
        lam_vecs = jnp.stack([lam_q1[l], lam_k1[l], lam_q2[l], lam_k2[l]], axis=0)
        d_out = _diff(qbz, kb, vb, lam_vecs, diff_norm_g[l][None, :], lam_init)
        hf, hb = _mlstm(qm, km, h, gl, glt, tri, trit, d)
        xs = _mix(a_out, d_out, hf, hb, h, xs, mod, l, nw, post_mix_g[l][None, :], mlstm_norm_g[l][None, :],
                  w_br_attn[l].astype(BF16), w_br_diff[l].astype(BF16), w_br_mlstm[l].astype(BF16),
                  w_out[l].astype(BF16))
        j = l // 2
        if l % 2 == 0:
            xs = _ffn(xs, mod, l, nw, pre_ffn_g[l][None, :], post_ffn_g[l][None, :],
                      w_ff_gate[j].astype(BF16), w_ff_up[j].astype(BF16), w_ff_down[j].astype(BF16))
        else:
            w_r = jnp.pad(w_router[j], ((0, 0), (0, LANES - N_EXPERTS)))
            b_r = jnp.pad(b_router[j], (0, LANES - N_EXPERTS))[None, :]
            xn, route = _route(xs, mod, l, nw, pre_ffn_g[l][None, :], w_r, b_r)
            n_tiles = n_batch * ts // ROW
            z = _experts(xn.reshape(n_batch * ts, d), route.reshape(n_batch * ts, LANES),
                         w_moe_gate[j].astype(BF16), w_moe_up[j].astype(BF16), w_moe_down[j].astype(BF16),
                         tm=ROW * max(k for k in (4, 3, 2, 1) if n_tiles % k == 0))
            xs = _resid(xs, z.reshape(n_batch, ts, d), mod, l, nw, post_ffn_g[l][None, :])
    return xs[:, n_ctx:, :]
```

```python
import functools
import math

import jax
import jax.numpy as jnp
from jax import lax
from jax.experimental import pallas as pl
from jax.experimental.pallas import tpu as pltpu

F32 = jnp.float32
BF16 = jnp.bfloat16
HIGHEST = lax.Precision.HIGHEST

EPS = 1e-6
HEAD_DIM = 64
A_HEADS = 8
A_KV_HEADS = 2
B_HEADS = 4
C_HEADS = 4
C_DIM = 128
N_EXPERTS = 8
ROPE_THETA = 10000.0
GRID_W = 64
CONV_W = 3

LANES = 128
SUBLANES = 8
ROW = 256
MCHUNK = 128
LOG2E = math.log2(math.e)
NEG = -1e30
VMEM_LIMIT = 56 * 1024 * 1024

OFF_GATE = 0
REL_QA, REL_QB, REL_KB, REL_VB = 0, 512, 1024, 1536
REL_QC, REL_KC, REL_VC, REL_OC = 2048, 2560, 3072, 3584
REL_KA = 4096
REL_END = 4352


def _cparams(sem):
    return pltpu.CompilerParams(dimension_semantics=sem, vmem_limit_bytes=VMEM_LIMIT)


def _rms(x, g):
    y = x * lax.rsqrt(jnp.mean(x * x, axis=-1, keepdims=True) + EPS)
    return y * g


def _sigmoid(x):
    return 1.0 / (1.0 + jnp.exp(-x))


def _silu(x):
    return x * _sigmoid(x)


def _log_sigmoid(x):
    return jnp.minimum(x, 0.0) - jnp.log(1.0 + jnp.exp(-jnp.abs(x)))


def _lane_iota(shape):
    return lax.broadcasted_iota(jnp.int32, shape, len(shape) - 1)


def _row_iota(shape):
    return lax.broadcasted_iota(jnp.int32, shape, len(shape) - 2)


def _mod_kernel(c_ref, w_ref, b_ref, o_ref):
    c = c_ref[...]
    o_ref[...] = jnp.dot(_silu(c), w_ref[...], preferred_element_type=F32, precision=HIGHEST) + b_ref[...]


def _modulation(c_all, ada_w, ada_b):
    depth, d, _ = ada_w.shape
    nw = c_all.shape[0]
    out = pl.pallas_call(
        _mod_kernel,
        grid=(depth, 6),
        in_specs=[
            pl.BlockSpec((nw, d), lambda l, j: (0, 0)),
            pl.BlockSpec((None, d, d), lambda l, j: (l, 0, j)),
            pl.BlockSpec((None, 1, d), lambda l, j: (l, 0, j)),
        ],
        out_specs=pl.BlockSpec((None, None, nw, d), lambda l, j: (l, j, 0, 0)),
        out_shape=jax.ShapeDtypeStruct((depth, 6, nw, d), F32),
        compiler_params=_cparams(("arbitrary", "arbitrary")),
        name="modulation",
    )(c_all, ada_w, ada_b.reshape(depth, 1, 6 * d))
    return out.reshape(depth * 6 * nw, 1, d)


def _mod_spec(d, layer, chunk, nw, n_batch):
    base = (layer * 6 + chunk) * nw
    return pl.BlockSpec((None, 1, d), lambda b, i: (base + jnp.where(i == 0, n_batch, b), 0, 0))


def _inproj_kernel(x_ref, sh_ref, sc_ref, g_ref, w_ref, wg_ref, h_ref, gate_ref, *, chunks):
    xn = _rms(x_ref[...], g_ref[...]) * (1.0 + sc_ref[...]) + sh_ref[...]
    xb = xn.astype(BF16)
    for start, width in chunks:
        h_ref[:, start:start + width] = jnp.dot(
            xb, w_ref[:, start:start + width], preferred_element_type=F32).astype(BF16)
    gate_ref[...] = jnp.dot(xb, wg_ref[...], preferred_element_type=F32)


def _inproj(xs, mod, layer, nw, pre_g, w_main, w_gate):
    n_batch, ts, d = xs.shape
    n_main = w_main.shape[1]
    nt = ts // ROW
    chunks = [(s, min(1024, n_main - s)) for s in range(0, n_main, 1024)]
    return pl.pallas_call(
        functools.partial(_inproj_kernel, chunks=chunks),
        grid=(n_batch, nt),
        in_specs=[
            pl.BlockSpec((None, ROW, d), lambda b, i: (b, i, 0)),
            _mod_spec(d, layer, 0, nw, n_batch),
            _mod_spec(d, layer, 1, nw, n_batch),
            pl.BlockSpec((1, d), lambda b, i: (0, 0)),
            pl.BlockSpec((d, n_main), lambda b, i: (0, 0), pipeline_mode=pl.Buffered(1)),
            pl.BlockSpec((d, LANES), lambda b, i: (0, 0)),
        ],
        out_specs=[
            pl.BlockSpec((None, ROW, n_main), lambda b, i: (b, i, 0)),
            pl.BlockSpec((None, ROW, LANES), lambda b, i: (b, i, 0)),
        ],
        out_shape=[
            jax.ShapeDtypeStruct((n_batch, ts, n_main), BF16),
            jax.ShapeDtypeStruct((n_batch, ts, LANES), F32),
        ],
        compiler_params=_cparams(("parallel", "parallel")),
        name="inproj",
    )(xs, mod, mod, pre_g, w_main, w_gate)


def _head_mean_sq(x, bd_ref):
    sq = x * x
    hi = sq.astype(BF16)
    lo = (sq - hi.astype(F32)).astype(BF16)
    bd = bd_ref[...]
    return jnp.dot(hi, bd, preferred_element_type=F32) + jnp.dot(lo, bd, preferred_element_type=F32)


def _prep_kernel(qa_ref, kava_ref, qb_ref, kb_ref, vb_ref, qkc_ref, prev_ref, next_ref, g_ref,
                 cos_ref, sa_ref, sb_ref, qg_ref, kg_ref, bd4_ref, bd1_ref, cw_ref, cb_ref, gb_ref,
                 qaz_ref, ka_ref, va_ref, qbz_ref, kbo_ref, vbo_ref, qm_ref, km_ref, gl_ref, glt_ref):
    i = pl.program_id(1)
    nt = pl.num_programs(1)
    cos, sin_a, sin_b = cos_ref[...], sa_ref[...], sb_ref[...]

    def rope(x):
        width = x.shape[1]
        reps = width // LANES
        c = jnp.concatenate([cos] * reps, axis=1) if reps > 1 else cos
        a = jnp.concatenate([sin_a] * reps, axis=1) if reps > 1 else sin_a
        b = jnp.concatenate([sin_b] * reps, axis=1) if reps > 1 else sin_b
        return x * c + pltpu.roll(x, width - 16, 1) * a + pltpu.roll(x, 16, 1) * b

    lane = _lane_iota((ROW, LANES))
    ones = jnp.ones((ROW, LANES), BF16)
    scale = HEAD_DIM ** -0.5 * LOG2E

    qa = qa_ref[...].astype(F32)
    qa = qa * lax.rsqrt(_head_mean_sq(qa, bd4_ref) + EPS) * qg_ref[...]
    qa = rope(qa) * scale
    heads_per_kv = A_HEADS // A_KV_HEADS
    for h in range(A_HEADS):
        g = h // heads_per_kv
        blk = qa[:, (h // 2) * LANES:(h // 2 + 1) * LANES]
        if h % 2 != g:
            blk = pltpu.roll(blk, HEAD_DIM, 1)
        qaz_ref[h] = jnp.where(lane // HEAD_DIM == g, blk, 0.0).astype(BF16)
    kava = kava_ref[...].astype(F32)
    ka = kava[:, :LANES]
    ka = ka * lax.rsqrt(_head_mean_sq(ka, bd1_ref) + EPS) * kg_ref[...]
    ka_ref[...] = rope(ka).astype(BF16)
    va = kava_ref[:, LANES:]
    for g in range(A_KV_HEADS):
        va_ref[g] = jnp.where(lane // HEAD_DIM == g, va, ones)

    qb = rope(qb_ref[...].astype(F32)) * scale
    kb = rope(kb_ref[...].astype(F32))
    for h in range(B_HEADS):
        blk = qb[:, h * LANES:(h + 1) * LANES]
        for m in range(2):
            qbz_ref[2 * h + m] = jnp.where(lane // HEAD_DIM == m, blk, 0.0).astype(BF16)
        kbo_ref[h] = kb[:, h * LANES:(h + 1) * LANES].astype(BF16)
        vbo_ref[h, :, :LANES] = vb_ref[:, h * LANES:(h + 1) * LANES]
        vbo_ref[h, :, LANES:] = ones

    cur = qkc_ref[...].astype(F32)
    row = _row_iota(cur.shape)
    prev_row = jnp.where(i >= 2, prev_ref[SUBLANES - 1:SUBLANES, :].astype(F32), 0.0)
    next_ok = jnp.logical_and(i >= 1, i < nt - 1)
    next_row = jnp.where(next_ok, next_ref[0:1, :].astype(F32), 0.0)
    up = jnp.where(row == 0, prev_row, pltpu.roll(cur, 1, 0))
    dn = jnp.where(row == ROW - 1, next_row, pltpu.roll(cur, ROW - 1, 0))
    y = up * cw_ref[0:1, :] + cur * cw_ref[1:2, :] + dn * cw_ref[2:3, :] + cb_ref[...]
    y = _silu(y)
    half = C_HEADS * C_DIM
    qm_ref[...] = y[:, :half].astype(BF16)
    km_ref[...] = (y[:, half:] * (C_DIM ** -0.5)).astype(BF16)

    gg = g_ref[...] + gb_ref[...]
    is_forget = (lane // C_HEADS) % 2 == 1
    gl = jnp.where(is_forget, _log_sigmoid(gg), gg)
    gl_ref[...] = gl
    glt_ref[...] = gl.T[:2 * SUBLANES, :]


def _prep(h, gates, tables, qg, kg, bd4, bd1, conv_w, conv_b, gate_b, d):
    n_batch, ts, _ = h.shape
    nt = ts // ROW
    base = 3 * d

    def col(rel, width):
        return (base + rel) // width

    cos_t, sa_t, sb_t = tables
    row_blocks = ts // SUBLANES
    per_tile = ROW // SUBLANES
    cqk = col(REL_QC, 1024)

    def const(shape):
        return pl.BlockSpec(shape, lambda b, i: (0,) * len(shape))

    in_specs = [
        pl.BlockSpec((None, ROW, 512), lambda b, i: (b, i, col(REL_QA, 512))),
        pl.BlockSpec((None, ROW, 256), lambda b, i: (b, i, col(REL_KA, 256))),
        pl.BlockSpec((None, ROW, 512), lambda b, i: (b, i, col(REL_QB, 512))),
        pl.BlockSpec((None, ROW, 512), lambda b, i: (b, i, col(REL_KB, 512))),
        pl.BlockSpec((None, ROW, 512), lambda b, i: (b, i, col(REL_VB, 512))),
        pl.BlockSpec((None, ROW, 1024), lambda b, i: (b, i, cqk)),
        pl.BlockSpec((None, SUBLANES, 1024), lambda b, i: (b, jnp.maximum(i * per_tile - 1, 0), cqk)),
        pl.BlockSpec((None, SUBLANES, 1024),
                     lambda b, i: (b, jnp.minimum((i + 1) * per_tile, row_blocks - 1), cqk)),
        pl.BlockSpec((None, ROW, LANES), lambda b, i: (b, i, 0)),
        pl.BlockSpec((ROW, LANES), lambda b, i: (i, 0)),
        pl.BlockSpec((ROW, LANES), lambda b, i: (i, 0)),
        pl.BlockSpec((ROW, LANES), lambda b, i: (i, 0)),
        const((1, 512)), const((1, LANES)), const((512, 512)), const((LANES, LANES)),
        const((CONV_W, 1024)), const((1, 1024)), const((1, LANES)),
    ]
    out_specs = [
        pl.BlockSpec((None, A_HEADS, ROW, LANES), lambda b, i: (b, 0, i, 0)),
        pl.BlockSpec((None, ROW, LANES), lambda b, i: (b, i, 0)),
        pl.BlockSpec((None, A_KV_HEADS, ROW, LANES), lambda b, i: (b, 0, i, 0)),
        pl.BlockSpec((None, 2 * B_HEADS, ROW, LANES), lambda b, i: (b, 0, i, 0)),
        pl.BlockSpec((None, B_HEADS, ROW, LANES), lambda b, i: (b, 0, i, 0)),
        pl.BlockSpec((None, B_HEADS, ROW, 2 * LANES), lambda b, i: (b, 0, i, 0)),
        pl.BlockSpec((None, ROW, 512), lambda b, i: (b, i, 0)),
        pl.BlockSpec((None, ROW, 512), lambda b, i: (b, i, 0)),
        pl.BlockSpec((None, ROW, LANES), lambda b, i: (b, i, 0)),
        pl.BlockSpec((None, 2 * SUBLANES, ROW), lambda b, i: (b, 0, i)),
    ]
    out_shape = [
        jax.ShapeDtypeStruct((n_batch, A_HEADS, ts, LANES), BF16),
        jax.ShapeDtypeStruct((n_batch, ts, LANES), BF16),
        jax.ShapeDtypeStruct((n_batch, A_KV_HEADS, ts, LANES), BF16),
        jax.ShapeDtypeStruct((n_batch, 2 * B_HEADS, ts, LANES), BF16),
        jax.ShapeDtypeStruct((n_batch, B_HEADS, ts, LANES), BF16),
        jax.ShapeDtypeStruct((n_batch, B_HEADS, ts, 2 * LANES), BF16),
        jax.ShapeDtypeStruct((n_batch, ts, 512), BF16),
        jax.ShapeDtypeStruct((n_batch, ts, 512), BF16),
        jax.ShapeDtypeStruct((n_batch, ts, LANES), F32),
        jax.ShapeDtypeStruct((n_batch, 2 * SUBLANES, ts), F32),
    ]
    return pl.pallas_call(
        _prep_kernel,
        grid=(n_batch, nt),
        in_specs=in_specs,
        out_specs=out_specs,
        out_shape=out_shape,
        compiler_params=_cparams(("parallel", "parallel")),
        name="prep",
    )(h, h, h, h, h, h, h, h, gates, cos_t, sa_t, sb_t, qg, kg, bd4, bd1, conv_w, conv_b, gate_b)


def _attend_blocks(blocks, n_keys):
    def scores(q, k_ref):
        return lax.dot_general(q, k_ref[:n_keys, :], (((1,), (1,)), ((), ())), preferred_element_type=F32)

    def weighted(s, v_ref):
        p = jnp.exp2(s - jnp.max(s, axis=-1, keepdims=True)).astype(BF16)
        return jnp.dot(p, v_ref[:n_keys, :], preferred_element_type=F32)

    outs = []
    s_cur = scores(blocks[0][0], blocks[0][1])
    for j in range(1, len(blocks)):
        s_next = scores(blocks[j][0], blocks[j][1])
        outs.append(weighted(s_cur, blocks[j - 1][2]))
        s_cur = s_next
    outs.append(weighted(s_cur, blocks[-1][2]))
    return outs


def _per_tile_keys(body, n_all):
    @pl.when(pl.program_id(1) == 0)
    def _():
        body(ROW)

    @pl.when(pl.program_id(1) > 0)
    def _():
        body(n_all)


def _gqa_kernel(q_ref, k_ref, v_ref, o_ref):
    heads_per_kv = A_HEADS // A_KV_HEADS
    lane = _lane_iota((ROW, LANES))

    def body(n_keys):
        blocks = []
        for j in range(A_HEADS // 2):
            q = q_ref[2 * j:2 * j + 2].reshape(2 * ROW, LANES)
            blocks.append((q, k_ref, v_ref.at[(2 * j) // heads_per_kv]))
        for j, o in enumerate(_attend_blocks(blocks, n_keys)):
            g = (2 * j) // heads_per_kv
            den_lane = (1 - g) * HEAD_DIM
            o = o / o[:, den_lane:den_lane + 1]
            even, odd = o[:ROW], o[ROW:]
            even = even if g == 0 else pltpu.roll(even, HEAD_DIM, 1)
            odd = odd if g == 1 else pltpu.roll(odd, HEAD_DIM, 1)
            o_ref[:, j * LANES:(j + 1) * LANES] = jnp.where(lane < HEAD_DIM, even, odd).astype(BF16)

    _per_tile_keys(body, k_ref.shape[0])


def _diff_kernel(q_ref, k_ref, v_ref, lam_ref, g_ref, o_ref, *, lam_init):
    lv = lam_ref[...]
    lam = (jnp.exp(jnp.sum(lv[0:1] * lv[1:2], axis=-1, keepdims=True))
           - jnp.exp(jnp.sum(lv[2:3] * lv[3:4], axis=-1, keepdims=True)) + lam_init)

    def body(n_keys):
        blocks = [(q_ref[2 * h:2 * h + 2].reshape(2 * ROW, LANES), k_ref.at[h], v_ref.at[h]) for h in range(B_HEADS)]
        for h, o in enumerate(_attend_blocks(blocks, n_keys)):
            o = o[:, :LANES] / o[:, LANES:LANES + 1]
            dif = o[:ROW] - lam * o[ROW:]
            o_ref[:, h * LANES:(h + 1) * LANES] = (_rms(dif, g_ref[...]) * (1.0 - lam_init)).astype(BF16)

    _per_tile_keys(body, k_ref.shape[1])


def _gqa(qaz, ka, va):
    n_batch, _, ts, _ = qaz.shape
    nt = ts // ROW
    return pl.pallas_call(
        _gqa_kernel,
        grid=(n_batch, nt),
        in_specs=[
            pl.BlockSpec((None, A_HEADS, ROW, LANES), lambda b, i: (b, 0, i, 0)),
            pl.BlockSpec((None, ts, LANES), lambda b, i: (b, 0, 0)),
            pl.BlockSpec((None, A_KV_HEADS, ts, LANES), lambda b, i: (b, 0, 0, 0)),
        ],
        out_specs=pl.BlockSpec((None, ROW, A_HEADS * HEAD_DIM), lambda b, i: (b, i, 0)),
        out_shape=jax.ShapeDtypeStruct((n_batch, ts, A_HEADS * HEAD_DIM), BF16),
        compiler_params=_cparams(("parallel", "parallel")),
        name="gqa_attention",
    )(qaz, ka, va)


def _diff(qbz, kb, vb, lam_vecs, sub_g, lam_init):
    n_batch, _, ts, _ = qbz.shape
    nt = ts // ROW
    return pl.pallas_call(
        functools.partial(_diff_kernel, lam_init=lam_init),
        grid=(n_batch, nt),
        in_specs=[
            pl.BlockSpec((None, 2 * B_HEADS, ROW, LANES), lambda b, i: (b, 0, i, 0)),
            pl.BlockSpec((None, B_HEADS, ts, LANES), lambda b, i: (b, 0, 0, 0)),
            pl.BlockSpec((None, B_HEADS, ts, 2 * LANES), lambda b, i: (b, 0, 0, 0)),
            pl.BlockSpec((4, HEAD_DIM), lambda b, i: (0, 0)),
            pl.BlockSpec((1, LANES), lambda b, i: (0, 0)),
        ],
        out_specs=pl.BlockSpec((None, ROW, B_HEADS * LANES), lambda b, i: (b, i, 0)),
        out_shape=jax.ShapeDtypeStruct((n_batch, ts, B_HEADS * LANES), BF16),
        compiler_params=_cparams(("parallel", "parallel")),
        name="diff_attention",
    )(qbz, kb, vb, lam_vecs, sub_g)


def _mlstm_kernel(qf_ref, kf_ref, vf_ref, gf_ref, gtf_ref, qb_ref, kb_ref, vb_ref, gb_ref, gtb_ref,
                  tri_ref, trit_ref, hf_ref, hb_ref, c_ref, n_ref, m_ref):
    @pl.when(pl.program_id(1) == 0)
    def _():
        c_ref[...] = jnp.zeros_like(c_ref)
        n_ref[...] = jnp.zeros_like(n_ref)
        m_ref[...] = jnp.zeros_like(m_ref)

    length = MCHUNK
    tri = tri_ref[...]
    trit = trit_ref[...]
    r_idx = _row_iota((length, length))
    c_idx = _lane_iota((length, length))
    nt_dims = (((1,), (1,)), ((), ()))
    tn_dims = (((0,), (0,)), ((), ()))

    for direction, (q_ref, k_ref, v_ref, g_ref, gt_ref, h_ref) in enumerate(
            ((qf_ref, kf_ref, vf_ref, gf_ref, gtf_ref, hf_ref), (qb_ref, kb_ref, vb_ref, gb_ref, gtb_ref, hb_ref))):
        gcol = g_ref[...]
        grow = gt_ref[...]
        cs_col = jnp.dot(tri, gcol, preferred_element_type=F32, precision=HIGHEST)
        cs_row = jnp.dot(grow, trit, preferred_element_type=F32, precision=HIGHEST)
        mask = (r_idx >= c_idx) if direction == 0 else (c_idx >= r_idx)
        for hd in range(C_HEADS):
            chain = direction * C_HEADS + hd
            ii = direction * 2 * C_HEADS + hd
            fi = ii + C_HEADS
            icol, irow = gcol[:, ii:ii + 1], grow[ii:ii + 1, :]
            pcol, prow = cs_col[:, fi:fi + 1], cs_row[fi:fi + 1, :]
            total = pcol[length - 1:length, :]
            if direction == 1:
                pcol = total - pcol + gcol[:, fi:fi + 1]
                prow = total - prow + grow[fi:fi + 1, :]
            m_prev = m_ref[chain][:, :1]
            log_d = jnp.where(mask, pcol - prow + irow, NEG)
            inter = pcol + m_prev
            m_t = jnp.maximum(inter, jnp.max(log_d, axis=-1, keepdims=True))
            dmat = jnp.exp(log_d - m_t)
            a_inter = jnp.exp(inter - m_t)
            sl = slice(hd * C_DIM, (hd + 1) * C_DIM)
            q, k, v = q_ref[:, sl], k_ref[:, sl], v_ref[:, sl]
            s = lax.dot_general(q, k, nt_dims, preferred_element_type=F32) * dmat
            c_state = c_ref[chain]
            n_state = n_ref[chain]
            num = (a_inter * lax.dot_general(q, c_state.astype(BF16), nt_dims, preferred_element_type=F32)
                   + jnp.dot(s.astype(BF16), v, preferred_element_type=F32))
            den = (a_inter * jnp.sum(q.astype(F32) * n_state, axis=-1, keepdims=True)
                   + jnp.sum(s, axis=-1, keepdims=True))
            h_ref[:, sl] = num / jnp.maximum(jnp.abs(den), jnp.exp(-m_t))
            w_log = total - pcol + icol
            m_new = jnp.maximum(total + m_prev, jnp.max(w_log, axis=0, keepdims=True))
            decay = jnp.exp(total + m_prev - m_new)
            ws = jnp.exp(w_log - m_new)
            wv = (ws * v.astype(F32)).astype(BF16)
            c_ref[chain] = decay * c_state + lax.dot_general(wv, k, tn_dims, preferred_element_type=F32)
            n_ref[chain] = decay * n_state + jnp.sum(ws * k.astype(F32), axis=0, keepdims=True)
            m_ref[chain] = jnp.broadcast_to(m_new, (1, LANES))


def _mlstm(qm, km, h, gl, glt, tri, trit, d):
    n_batch, ts, width = qm.shape
    nc = ts // MCHUNK
    ctx_chunks = ROW // MCHUNK
    vcol = (3 * d + REL_VC) // width

    def bwd(j):
        return jnp.where(j < ctx_chunks, ctx_chunks - 1 - j, nc + ctx_chunks - 1 - j)

    def specs(idx):
        return [
            pl.BlockSpec((None, MCHUNK, width), lambda b, j: (b, idx(j), 0)),
            pl.BlockSpec((None, MCHUNK, width), lambda b, j: (b, idx(j), 0)),
            pl.BlockSpec((None, MCHUNK, width), lambda b, j: (b, idx(j), vcol)),
            pl.BlockSpec((None, MCHUNK, LANES), lambda b, j: (b, idx(j), 0)),
            pl.BlockSpec((None, 2 * SUBLANES, MCHUNK), lambda b, j: (b, 0, idx(j))),
        ]

    fwd = lambda j: j
    n_chain = 2 * C_HEADS
    return pl.pallas_call(
        _mlstm_kernel,
        grid=(n_batch, nc),
        in_specs=specs(fwd) + specs(bwd) + [
            pl.BlockSpec((MCHUNK, MCHUNK), lambda b, j: (0, 0)),
            pl.BlockSpec((MCHUNK, MCHUNK), lambda b, j: (0, 0)),
        ],
        out_specs=[
            pl.BlockSpec((None, MCHUNK, width), lambda b, j: (b, j, 0)),
            pl.BlockSpec((None, MCHUNK, width), lambda b, j: (b, bwd(j), 0)),
        ],
        out_shape=[jax.ShapeDtypeStruct((n_batch, ts, width), F32)] * 2,
        scratch_shapes=[
            pltpu.VMEM((n_chain, C_DIM, C_DIM), F32),
            pltpu.VMEM((n_chain, 1, C_DIM), F32),
            pltpu.VMEM((n_chain, 1, LANES), F32),
        ],
        compiler_params=_cparams(("parallel", "arbitrary")),
        name="mlstm",
    )(qm, km, h, gl, glt, qm, km, h, gl, glt, tri, trit)


def _mix_kernel(a_ref, d_ref, hf_ref, hb_ref, oc_ref, gate_ref, x_ref, g1_ref, pg_ref, mg_ref,
                wa_ref, wb_ref, wc_ref, wo_ref, o_ref):
    d_model = x_ref.shape[-1]
    hsum = hf_ref[...] + hb_ref[...]
    mg = mg_ref[...]
    m = jnp.concatenate([_rms(hsum[:, hd * C_DIM:(hd + 1) * C_DIM], mg) for hd in range(C_HEADS)], axis=1)
    m = m * _sigmoid(oc_ref[...].astype(F32))
    u = (_sigmoid(gate_ref[:, :d_model].astype(F32))
         * jnp.dot(a_ref[...], wa_ref[...], preferred_element_type=F32)
         + _sigmoid(gate_ref[:, d_model:2 * d_model].astype(F32))
         * jnp.dot(d_ref[...], wb_ref[...], preferred_element_type=F32)
         + _sigmoid(gate_ref[:, 2 * d_model:].astype(F32))
         * jnp.dot(m.astype(BF16), wc_ref[...], preferred_element_type=F32))
    y = jnp.dot(u.astype(BF16), wo_ref[...], preferred_element_type=F32)
    o_ref[...] = x_ref[...] + g1_ref[...] * _rms(y, pg_ref[...])


def _mix(a, dd, hf, hb, h, xs, mod, layer, nw, post_g, mlstm_g, wa, wb, wc, wo):
    n_batch, ts, d = xs.shape
    nt = ts // ROW
    width = a.shape[-1]
    ocol = (3 * d + REL_OC) // width

    def tile(w):
        return pl.BlockSpec((None, ROW, w), lambda b, i: (b, i, 0))

    def const(shape):
        return pl.BlockSpec(shape, lambda b, i: (0,) * len(shape))

    return pl.pallas_call(
        _mix_kernel,
        grid=(n_batch, nt),
        in_specs=[
            tile(width), tile(width), tile(width), tile(width),
            pl.BlockSpec((None, ROW, width), lambda b, i: (b, i, ocol)),
            pl.BlockSpec((None, ROW, 3 * d), lambda b, i: (b, i, 0)),
            tile(d),
            _mod_spec(d, layer, 2, nw, n_batch),
            const((1, d)), const((1, C_DIM)),
            const((width, d)), const((width, d)), const((width, d)), const((d, d)),
        ],
        out_specs=tile(d),
        out_shape=jax.ShapeDtypeStruct((n_batch, ts, d), F32),
        compiler_params=_cparams(("parallel", "parallel")),
        name="mix_out",
    )(a, dd, hf, hb, h, h, xs, mod, post_g, mlstm_g, wa, wb, wc, wo)


def _ffn_kernel(x_ref, sh_ref, sc_ref, g2_ref, pre_ref, post_ref, wg_ref, wu_ref, wd_ref, o_ref):
    x = x_ref[...]
    xb = (_rms(x, pre_ref[...]) * (1.0 + sc_ref[...]) + sh_ref[...]).astype(BF16)
    gate = jnp.dot(xb, wg_ref[...], preferred_element_type=F32)
    up = jnp.dot(xb, wu_ref[...], preferred_element_type=F32)
    z = jnp.dot((_silu(gate) * up).astype(BF16), wd_ref[...], preferred_element_type=F32)
    o_ref[...] = x + g2_ref[...] * _rms(z, post_ref[...])


def _ffn(xs, mod, layer, nw, pre_g, post_g, wg, wu, wd):
    n_batch, ts, d = xs.shape
    nt = ts // ROW
    dff = wg.shape[1]

    def resident(shape):
        return pl.BlockSpec(shape, lambda b, i: (0,) * len(shape), pipeline_mode=pl.Buffered(1))

    tile = pl.BlockSpec((None, ROW, d), lambda b, i: (b, i, 0))
    return pl.pallas_call(
        _ffn_kernel,
        grid=(n_batch, nt),
        in_specs=[
            tile,
            _mod_spec(d, layer, 3, nw, n_batch), _mod_spec(d, layer, 4, nw, n_batch),
            _mod_spec(d, layer, 5, nw, n_batch),
            pl.BlockSpec((1, d), lambda b, i: (0, 0)), pl.BlockSpec((1, d), lambda b, i: (0, 0)),
            resident((d, dff)), resident((d, dff)), resident((dff, d)),
        ],
        out_specs=tile,
        out_shape=jax.ShapeDtypeStruct((n_batch, ts, d), F32),
        compiler_params=_cparams(("parallel", "parallel")),
        name="ffn",
    )(xs, mod, mod, mod, pre_g, post_g, wg, wu, wd)


def _route_kernel(x_ref, sh_ref, sc_ref, pre_ref, wr_ref, br_ref, xn_ref, gates_ref):
    xn = _rms(x_ref[...], pre_ref[...]) * (1.0 + sc_ref[...]) + sh_ref[...]
    xn_ref[...] = xn.astype(BF16)
    lane = _lane_iota((ROW, LANES))
    logits = jnp.dot(xn, wr_ref[...], preferred_element_type=F32, precision=HIGHEST) + br_ref[...]
    logits = jnp.where(lane < N_EXPERTS, logits, NEG)
    v1 = jnp.max(logits, axis=-1, keepdims=True)
    i1 = jnp.min(jnp.where(logits == v1, lane, LANES), axis=-1, keepdims=True)
    rest = jnp.where(lane == i1, NEG, logits)
    v2 = jnp.max(rest, axis=-1, keepdims=True)
    i2 = jnp.min(jnp.where(rest == v2, lane, LANES), axis=-1, keepdims=True)
    e2 = jnp.exp(v2 - v1)
    w1 = 1.0 / (1.0 + e2)
    w2 = e2 / (1.0 + e2)
    gates_ref[...] = jnp.where(lane == i1, w1, 0.0) + jnp.where(lane == i2, w2, 0.0)


def _route(xs, mod, layer, nw, pre_g, w_r, b_r):
    n_batch, ts, d = xs.shape
    nt = ts // ROW
    tile = pl.BlockSpec((None, ROW, d), lambda b, i: (b, i, 0))
    return pl.pallas_call(
        _route_kernel,
        grid=(n_batch, nt),
        in_specs=[
            tile, _mod_spec(d, layer, 3, nw, n_batch), _mod_spec(d, layer, 4, nw, n_batch),
            pl.BlockSpec((1, d), lambda b, i: (0, 0)),
            pl.BlockSpec((d, LANES), lambda b, i: (0, 0)),
            pl.BlockSpec((1, LANES), lambda b, i: (0, 0)),
        ],
        out_specs=[tile, pl.BlockSpec((None, ROW, LANES), lambda b, i: (b, i, 0))],
        out_shape=[jax.ShapeDtypeStruct((n_batch, ts, d), BF16),
                   jax.ShapeDtypeStruct((n_batch, ts, LANES), F32)],
        compiler_params=_cparams(("parallel", "parallel")),
        name="route",
    )(xs, mod, mod, pre_g, w_r, b_r)


def _experts_kernel(xn_ref, gates_ref, wg_ref, wu_ref, wd_ref, z_ref, acc_ref):
    e = pl.program_id(1)

    @pl.when(e == 0)
    def _():
        acc_ref[...] = jnp.zeros_like(acc_ref)

    xb = xn_ref[...]
    gates = gates_ref[...]
    w_e = jnp.sum(jnp.where(_lane_iota(gates.shape) == e, gates, 0.0), axis=-1, keepdims=True)
    gate = jnp.dot(xb, wg_ref[...], preferred_element_type=F32)
    up = jnp.dot(xb, wu_ref[...], preferred_element_type=F32)
    acc_ref[...] += w_e * jnp.dot((_silu(gate) * up).astype(BF16), wd_ref[...], preferred_element_type=F32)

    @pl.when(e == pl.num_programs(1) - 1)
    def _():
        z_ref[...] = acc_ref[...].astype(BF16)


def _experts(xn, gates, wg, wu, wd, tm):
    m_rows, d = xn.shape
    n_e, _, dff = wg.shape
    return pl.pallas_call(
        _experts_kernel,
        grid=(m_rows // tm, n_e),
        in_specs=[
            pl.BlockSpec((tm, d), lambda r, e: (r, 0)),
            pl.BlockSpec((tm, LANES), lambda r, e: (r, 0)),
            pl.BlockSpec((None, d, dff), lambda r, e: (e, 0, 0)),
            pl.BlockSpec((None, d, dff), lambda r, e: (e, 0, 0)),
            pl.BlockSpec((None, dff, d), lambda r, e: (e, 0, 0)),
        ],
        out_specs=pl.BlockSpec((tm, d), lambda r, e: (r, 0)),
        out_shape=jax.ShapeDtypeStruct((m_rows, d), BF16),
        scratch_shapes=[pltpu.VMEM((tm, d), F32)],
        compiler_params=_cparams(("parallel", "arbitrary")),
        name="experts",
    )(xn, gates, wg, wu, wd)


def _resid_kernel(x_ref, z_ref, g2_ref, post_ref, o_ref):
    o_ref[...] = x_ref[...] + g2_ref[...] * _rms(z_ref[...].astype(F32), post_ref[...])


def _resid(xs, z, mod, layer, nw, post_g):
    n_batch, ts, d = xs.shape
    nt = ts // ROW
    tile = pl.BlockSpec((None, ROW, d), lambda b, i: (b, i, 0))
    return pl.pallas_call(
        _resid_kernel,
        grid=(n_batch, nt),
        in_specs=[tile, tile, _mod_spec(d, layer, 5, nw, n_batch), pl.BlockSpec((1, d), lambda b, i: (0, 0))],
        out_specs=tile,
        out_shape=jax.ShapeDtypeStruct((n_batch, ts, d), F32),
        compiler_params=_cparams(("parallel", "parallel")),
        name="resid",
    )(xs, z, mod, post_g)


def _rope_tables(n_tok, n_ctx):
    n_freq = HEAD_DIM // 4
    pos = jnp.arange(n_tok)
    row = (pos // GRID_W).astype(F32)
    colp = (pos % GRID_W).astype(F32)
    inv = ROPE_THETA ** (-jnp.arange(n_freq, dtype=F32) / n_freq)
    lane = jnp.arange(LANES)
    in_head = lane % HEAD_DIM
    use_col = (in_head // (HEAD_DIM // 2)) == 1
    freq = inv[in_head % n_freq]
    ang = jnp.where(use_col[None, :], colp[:, None], row[:, None]) * freq[None, :]
    lower = (in_head % (HEAD_DIM // 2)) < n_freq
    cos_t = jnp.cos(ang)
    sin_t = jnp.sin(ang)
    sin_a = jnp.where(lower[None, :], -sin_t, 0.0)
    sin_b = jnp.where(lower[None, :], 0.0, sin_t)
    pad = lambda t, v: jnp.concatenate([jnp.full((n_ctx, LANES), v, F32), t], axis=0)
    return pad(cos_t, 1.0), pad(sin_a, 0.0), pad(sin_b, 0.0)


def _block_diag_mean(width):
    idx = jnp.arange(width) // HEAD_DIM
    return jnp.where(idx[:, None] == idx[None, :], 1.0 / HEAD_DIM, 0.0).astype(BF16)


def _pack_w_in(w):
    a_q = A_HEADS * HEAD_DIM
    a_kv = 2 * A_KV_HEADS * HEAD_DIM
    n_gates = 4 * C_HEADS
    gate_start = w.shape[1] - 3 * w.shape[0]
    g_start = gate_start - n_gates
    main = jnp.concatenate([w[:, gate_start:], w[:, :a_q], w[:, a_q + a_kv:g_start], w[:, a_q:a_q + a_kv]], axis=1)
    gates = jnp.pad(w[:, g_start:gate_start], ((0, 0), (0, LANES - n_gates)))
    return main.astype(BF16), gates.astype(BF16)


def kernel(x, c, ctx, c_ctx, ada_w, ada_b, pre_mix_g, post_mix_g, pre_ffn_g, post_ffn_g, w_in, q_norm_g, k_norm_g, lam_q1, lam_k1, lam_q2, lam_k2, diff_norm_g, conv_w, conv_b, mlstm_gate_b, mlstm_norm_g, w_br_attn, w_br_diff, w_br_mlstm, w_out, w_ff_gate, w_ff_up, w_ff_down, w_router, b_router, w_moe_gate, w_moe_up, w_moe_down):
    n_batch, n_tok, d = x.shape
    n_ctx = ctx.shape[1]
    depth = ada_w.shape[0]
    assert n_ctx == ROW and n_tok % ROW == 0 and d == 1024
    ts = n_ctx + n_tok
    nw = -(-(n_batch + 1) // SUBLANES) * SUBLANES

    c_all = jnp.concatenate([c, c_ctx[None, :], jnp.zeros((nw - n_batch - 1, d), F32)], axis=0)
    mod = _modulation(c_all, ada_w, ada_b)
    tables = _rope_tables(n_tok, n_ctx)
    bd4, bd1 = _block_diag_mean(A_HEADS * HEAD_DIM), _block_diag_mean(LANES)
    t_idx = jnp.arange(MCHUNK)
    tri = (t_idx[:, None] >= t_idx[None, :]).astype(F32)
    trit = tri.T

    xs = jnp.concatenate([ctx, x], axis=1)
    for l in range(depth):
        lam_init = 0.8 - 0.6 * math.exp(-0.3 * l)
        w_main, w_gate = _pack_w_in(w_in[l])
        h, gates = _inproj(xs, mod, l, nw, pre_mix_g[l][None, :], w_main, w_gate)
        qg = jnp.tile(q_norm_g[l], A_HEADS)[None, :]
        kg = jnp.tile(k_norm_g[l], A_KV_HEADS)[None, :]
        gate_b = jnp.pad(mlstm_gate_b[l], (0, LANES - 4 * C_HEADS))[None, :]
        qaz, ka, va, qbz, kb, vb, qm, km, gl, glt = _prep(
            h, gates, tables, qg, kg, bd4, bd1, conv_w[l], conv_b[l][None, :], gate_b, d)
        a_out = _gqa(qaz, ka, va)
        lam_vecs = jnp.stack([lam_q1[l], lam_k1[l], lam_q2[l], lam_k2[l]], axis=0)
        d_out = _diff(qbz, kb, vb, lam_vecs, diff_norm_g[l][None, :], lam_init)
        hf, hb = _mlstm(qm, km, h, gl, glt, tri, trit, d)
        xs = _mix(a_out, d_out, hf, hb, h, xs, mod, l, nw, post_mix_g[l][None, :], mlstm_norm_g[l][None, :],
                  w_br_attn[l].astype(BF16), w_br_diff[l].astype(BF16), w_br_mlstm[l].astype(BF16),
                  w_out[l].astype(BF16))
        j = l // 2
        if l % 2 == 0:
            xs = _ffn(xs, mod, l, nw, pre_ffn_g[l][None, :], post_ffn_g[l][None, :],
                      w_ff_gate[j].astype(BF16), w_ff_up[j].astype(BF16), w_ff_down[j].astype(BF16))
        else:
            w_r = jnp.pad(w_router[j], ((0, 0), (0, LANES - N_EXPERTS)))
            b_r = jnp.pad(b_router[j], (0, LANES - N_EXPERTS))[None, :]
            xn, route = _route(xs, mod, l, nw, pre_ffn_g[l][None, :], w_r, b_r)
            n_tiles = n_batch * ts // ROW
            z = _experts(xn.reshape(n_batch * ts, d), route.reshape(n_batch * ts, LANES),
                         w_moe_gate[j].astype(BF16), w_moe_up[j].astype(BF16), w_moe_down[j].astype(BF16),
                         tm=ROW * max(k for k in (4, 3, 2, 1) if n_tiles % k == 0))
            xs = _resid(xs, z.reshape(n_batch, ts, d), mod, l, nw, post_ffn_g[l][None, :])
    return xs[:, n_ctx:, :]
```

```python
import functools
import math

import jax
import jax.numpy as jnp
from jax import lax
from jax.experimental import pallas as pl
from jax.experimental.pallas import tpu as pltpu

F32 = jnp.float32
BF16 = jnp.bfloat16
HIGHEST = lax.Precision.HIGHEST

EPS = 1e-6
HEAD_DIM = 64
A_HEADS = 8
A_KV_HEADS = 2
B_HEADS = 4
C_HEADS = 4
C_DIM = 128
N_EXPERTS = 8
ROPE_THETA = 10000.0
GRID_W = 64
CONV_W = 3

LANES = 128
SUBLANES = 8
ROW = 256
MCHUNK = 128
LOG2E = math.log2(math.e)
NEG = -1e30
VMEM_LIMIT = 56 * 1024 * 1024

OFF_GATE = 0
REL_QA, REL_QB, REL_KB, REL_VB = 0, 512, 1024, 1536
REL_QC, REL_KC, REL_VC, REL_OC = 2048, 2560, 3072, 3584
REL_KA = 4096
REL_END = 4352


def _cparams(sem):
    return pltpu.CompilerParams(dimension_semantics=sem, vmem_limit_bytes=VMEM_LIMIT)


def _rms(x, g):
    y = x * lax.rsqrt(jnp.mean(x * x, axis=-1, keepdims=True) + EPS)
    return y * g


def _sigmoid(x):
    return 1.0 / (1.0 + jnp.exp(-x))


def _silu(x):
    return x * _sigmoid(x)


def _log_sigmoid(x):
    return jnp.minimum(x, 0.0) - jnp.log(1.0 + jnp.exp(-jnp.abs(x)))


def _lane_iota(shape):
    return lax.broadcasted_iota(jnp.int32, shape, len(shape) - 1)


def _row_iota(shape):
    return lax.broadcasted_iota(jnp.int32, shape, len(shape) - 2)


def _mod_kernel(c_ref, w_ref, b_ref, o_ref):
    c = c_ref[...]
    o_ref[...] = jnp.dot(_silu(c), w_ref[...], preferred_element_type=F32, precision=HIGHEST) + b_ref[...]


def _modulation(c_all, ada_w, ada_b):
    depth, d, _ = ada_w.shape
    nw = c_all.shape[0]
    out = pl.pallas_call(
        _mod_kernel,
        grid=(depth, 6),
        in_specs=[
            pl.BlockSpec((nw, d), lambda l, j: (0, 0)),
            pl.BlockSpec((None, d, d), lambda l, j: (l, 0, j)),
            pl.BlockSpec((None, 1, d), lambda l, j: (l, 0, j)),
        ],
        out_specs=pl.BlockSpec((None, None, nw, d), lambda l, j: (l, j, 0, 0)),
        out_shape=jax.ShapeDtypeStruct((depth, 6, nw, d), F32),
        compiler_params=_cparams(("arbitrary", "arbitrary")),
        name="modulation",
    )(c_all, ada_w, ada_b.reshape(depth, 1, 6 * d))
    return out.reshape(depth * 6 * nw, 1, d)


def _mod_spec(d, layer, chunk, nw, n_batch):
    base = (layer * 6 + chunk) * nw
    return pl.BlockSpec((None, 1, d), lambda b, i: (base + jnp.where(i == 0, n_batch, b), 0, 0))


def _inproj_kernel(x_ref, sh_ref, sc_ref, g_ref, w_ref, wg_ref, h_ref, gate_ref, *, chunks):
    xn = _rms(x_ref[...], g_ref[...]) * (1.0 + sc_ref[...]) + sh_ref[...]
    xb = xn.astype(BF16)
    for start, width in chunks:
        h_ref[:, start:start + width] = jnp.dot(
            xb, w_ref[:, start:start + width], preferred_element_type=F32).astype(BF16)
    gate_ref[...] = jnp.dot(xb, wg_ref[...], preferred_element_type=F32)


def _inproj(xs, mod, layer, nw, pre_g, w_main, w_gate):
    n_batch, ts, d = xs.shape
    n_main = w_main.shape[1]
    nt = ts // ROW
    chunks = [(s, min(1024, n_main - s)) for s in range(0, n_main, 1024)]
    return pl.pallas_call(
        functools.partial(_inproj_kernel, chunks=chunks),
        grid=(n_batch, nt),
        in_specs=[
            pl.BlockSpec((None, ROW, d), lambda b, i: (b, i, 0)),
            _mod_spec(d, layer, 0, nw, n_batch),
            _mod_spec(d, layer, 1, nw, n_batch),
            pl.BlockSpec((1, d), lambda b, i: (0, 0)),
            pl.BlockSpec((d, n_main), lambda b, i: (0, 0), pipeline_mode=pl.Buffered(1)),
            pl.BlockSpec((d, LANES), lambda b, i: (0, 0)),
        ],
        out_specs=[
            pl.BlockSpec((None, ROW, n_main), lambda b, i: (b, i, 0)),
            pl.BlockSpec((None, ROW, LANES), lambda b, i: (b, i, 0)),
        ],
        out_shape=[
            jax.ShapeDtypeStruct((n_batch, ts, n_main), BF16),
            jax.ShapeDtypeStruct((n_batch, ts, LANES), F32),
        ],
        compiler_params=_cparams(("parallel", "parallel")),
        name="inproj",
    )(xs, mod, mod, pre_g, w_main, w_gate)


def _head_mean_sq(x, bd_ref):
    sq = x * x
    hi = sq.astype(BF16)
    lo = (sq - hi.astype(F32)).astype(BF16)
    bd = bd_ref[...]
    return jnp.dot(hi, bd, preferred_element_type=F32) + jnp.dot(lo, bd, preferred_element_type=F32)


def _prep_kernel(qa_ref, kava_ref, qb_ref, kb_ref, vb_ref, qkc_ref, prev_ref, next_ref, g_ref,
                 cos_ref, sa_ref, sb_ref, qg_ref, kg_ref, bd4_ref, bd1_ref, cw_ref, cb_ref, gb_ref,
                 qaz_ref, ka_ref, va_ref, qbz_ref, kbo_ref, vbo_ref, qm_ref, km_ref, gl_ref, glt_ref):
    i = pl.program_id(1)
    nt = pl.num_programs(1)
    cos, sin_a, sin_b = cos_ref[...], sa_ref[...], sb_ref[...]

    def rope(x):
        width = x.shape[1]
        reps = width // LANES
        c = jnp.concatenate([cos] * reps, axis=1) if reps > 1 else cos
        a = jnp.concatenate([sin_a] * reps, axis=1) if reps > 1 else sin_a
        b = jnp.concatenate([sin_b] * reps, axis=1) if reps > 1 else sin_b
        return x * c + pltpu.roll(x, width - 16, 1) * a + pltpu.roll(x, 16, 1) * b

    lane = _lane_iota((ROW, LANES))
    ones = jnp.ones((ROW, LANES), BF16)
    scale = HEAD_DIM ** -0.5 * LOG2E

    qa = qa_ref[...].astype(F32)
    qa = qa * lax.rsqrt(_head_mean_sq(qa, bd4_ref) + EPS) * qg_ref[...]
    qa = rope(qa) * scale
    heads_per_kv = A_HEADS // A_KV_HEADS
    for h in range(A_HEADS):
        g = h // heads_per_kv
        blk = qa[:, (h // 2) * LANES:(h // 2 + 1) * LANES]
        if h % 2 != g:
            blk = pltpu.roll(blk, HEAD_DIM, 1)
        qaz_ref[h] = jnp.where(lane // HEAD_DIM == g, blk, 0.0).astype(BF16)
    kava = kava_ref[...].astype(F32)
    ka = kava[:, :LANES]
    ka = ka * lax.rsqrt(_head_mean_sq(ka, bd1_ref) + EPS) * kg_ref[...]
    ka_ref[...] = rope(ka).astype(BF16)
    va = kava_ref[:, LANES:]
    for g in range(A_KV_HEADS):
        va_ref[g] = jnp.where(lane // HEAD_DIM == g, va, ones)

    qb = rope(qb_ref[...].astype(F32)) * scale
    kb = rope(kb_ref[...].astype(F32))
    for h in range(B_HEADS):
        blk = qb[:, h * LANES:(h + 1) * LANES]
        for m in range(2):
            qbz_ref[2 * h + m] = jnp.where(lane // HEAD_DIM == m, blk, 0.0).astype(BF16)
        kbo_ref[h] = kb[:, h * LANES:(h + 1) * LANES].astype(BF16)
        vbo_ref[h, :, :LANES] = vb_ref[:, h * LANES:(h + 1) * LANES]
        vbo_ref[h, :, LANES:] = ones

    cur = qkc_ref[...].astype(F32)
    row = _row_iota(cur.shape)
    prev_row = jnp.where(i >= 2, prev_ref[SUBLANES - 1:SUBLANES, :].astype(F32), 0.0)
    next_ok = jnp.logical_and(i >= 1, i < nt - 1)
    next_row = jnp.where(next_ok, next_ref[0:1, :].astype(F32), 0.0)
    up = jnp.where(row == 0, prev_row, pltpu.roll(cur, 1, 0))
    dn = jnp.where(row == ROW - 1, next_row, pltpu.roll(cur, ROW - 1, 0))
    y = up * cw_ref[0:1, :] + cur * cw_ref[1:2, :] + dn * cw_ref[2:3, :] + cb_ref[...]
    y = _silu(y)
    half = C_HEADS * C_DIM
    qm_ref[...] = y[:, :half].astype(BF16)
    km_ref[...] = (y[:, half:] * (C_DIM ** -0.5)).astype(BF16)

    gg = g_ref[...] + gb_ref[...]
    is_forget = (lane // C_HEADS) % 2 == 1
    gl = jnp.where(is_forget, _log_sigmoid(gg), gg)
    gl_ref[...] = gl
    glt_ref[...] = gl.T[:2 * SUBLANES, :]


def _prep(h, gates, tables, qg, kg, bd4, bd1, conv_w, conv_b, gate_b, d):
    n_batch, ts, _ = h.shape
    nt = ts // ROW
    base = 3 * d

    def col(rel, width):
        return (base + rel) // width

    cos_t, sa_t, sb_t = tables
    row_blocks = ts // SUBLANES
    per_tile = ROW // SUBLANES
    cqk = col(REL_QC, 1024)

    def const(shape):
        return pl.BlockSpec(shape, lambda b, i: (0,) * len(shape))

    in_specs = [
        pl.BlockSpec((None, ROW, 512), lambda b, i: (b, i, col(REL_QA, 512))),
        pl.BlockSpec((None, ROW, 256), lambda b, i: (b, i, col(REL_KA, 256))),
        pl.BlockSpec((None, ROW, 512), lambda b, i: (b, i, col(REL_QB, 512))),
        pl.BlockSpec((None, ROW, 512), lambda b, i: (b, i, col(REL_KB, 512))),
        pl.BlockSpec((None, ROW, 512), lambda b, i: (b, i, col(REL_VB, 512))),
        pl.BlockSpec((None, ROW, 1024), lambda b, i: (b, i, cqk)),
        pl.BlockSpec((None, SUBLANES, 1024), lambda b, i: (b, jnp.maximum(i * per_tile - 1, 0), cqk)),
        pl.BlockSpec((None, SUBLANES, 1024),
                     lambda b, i: (b, jnp.minimum((i + 1) * per_tile, row_blocks - 1), cqk)),
        pl.BlockSpec((None, ROW, LANES), lambda b, i: (b, i, 0)),
        pl.BlockSpec((ROW, LANES), lambda b, i: (i, 0)),
        pl.BlockSpec((ROW, LANES), lambda b, i: (i, 0)),
        pl.BlockSpec((ROW, LANES), lambda b, i: (i, 0)),
        const((1, 512)), const((1, LANES)), const((512, 512)), const((LANES, LANES)),
        const((CONV_W, 1024)), const((1, 1024)), const((1, LANES)),
    ]
    out_specs = [
        pl.BlockSpec((None, A_HEADS, ROW, LANES), lambda b, i: (b, 0, i, 0)),
        pl.BlockSpec((None, ROW, LANES), lambda b, i: (b, i, 0)),
        pl.BlockSpec((None, A_KV_HEADS, ROW, LANES), lambda b, i: (b, 0, i, 0)),
        pl.BlockSpec((None, 2 * B_HEADS, ROW, LANES), lambda b, i: (b, 0, i, 0)),
        pl.BlockSpec((None, B_HEADS, ROW, LANES), lambda b, i: (b, 0, i, 0)),
        pl.BlockSpec((None, B_HEADS, ROW, 2 * LANES), lambda b, i: (b, 0, i, 0)),
        pl.BlockSpec((None, ROW, 512), lambda b, i: (b, i, 0)),
        pl.BlockSpec((None, ROW, 512), lambda b, i: (b, i, 0)),
        pl.BlockSpec((None, ROW, LANES), lambda b, i: (b, i, 0)),
        pl.BlockSpec((None, 2 * SUBLANES, ROW), lambda b, i: (b, 0, i)),
    ]
    out_shape = [
        jax.ShapeDtypeStruct((n_batch, A_HEADS, ts, LANES), BF16),
        jax.ShapeDtypeStruct((n_batch, ts, LANES), BF16),
        jax.ShapeDtypeStruct((n_batch, A_KV_HEADS, ts, LANES), BF16),
        jax.ShapeDtypeStruct((n_batch, 2 * B_HEADS, ts, LANES), BF16),
        jax.ShapeDtypeStruct((n_batch, B_HEADS, ts, LANES), BF16),
        jax.ShapeDtypeStruct((n_batch, B_HEADS, ts, 2 * LANES), BF16),
        jax.ShapeDtypeStruct((n_batch, ts, 512), BF16),
        jax.ShapeDtypeStruct((n_batch, ts, 512), BF16),
        jax.ShapeDtypeStruct((n_batch, ts, LANES), F32),
        jax.ShapeDtypeStruct((n_batch, 2 * SUBLANES, ts), F32),
    ]
    return pl.pallas_call(
        _prep_kernel,
        grid=(n_batch, nt),
        in_specs=in_specs,
        out_specs=out_specs,
        out_shape=out_shape,
        compiler_params=_cparams(("parallel", "parallel")),
        name="prep",
    )(h, h, h, h, h, h, h, h, gates, cos_t, sa_t, sb_t, qg, kg, bd4, bd1, conv_w, conv_b, gate_b)


def _attend_blocks(blocks, n_keys):
    def scores(q, k_ref):
        return lax.dot_general(q, k_ref[:n_keys, :], (((1,), (1,)), ((), ())), preferred_element_type=F32)

    def weighted(s, v_ref):
        p = jnp.exp2(s - jnp.max(s, axis=-1, keepdims=True)).astype(BF16)
        return jnp.dot(p, v_ref[:n_keys, :], preferred_element_type=F32)

    outs = []
    s_cur = scores(blocks[0][0], blocks[0][1])
    for j in range(1, len(blocks)):
        s_next = scores(blocks[j][0], blocks[j][1])
        outs.append(weighted(s_cur, blocks[j - 1][2]))
        s_cur = s_next
    outs.append(weighted(s_cur, blocks[-1][2]))
    return outs


def _per_tile_keys(body, n_all):
    @pl.when(pl.program_id(1) == 0)
    def _():
        body(ROW)

    @pl.when(pl.program_id(1) > 0)
    def _():
        body(n_all)


def _gqa_kernel(q_ref, k_ref, v_ref, o_ref):
    heads_per_kv = A_HEADS // A_KV_HEADS
    lane = _lane_iota((ROW, LANES))

    def body(n_keys):
        blocks = []
        for j in range(A_HEADS // 2):
            q = q_ref[2 * j:2 * j + 2].reshape(2 * ROW, LANES)
            blocks.append((q, k_ref, v_ref.at[(2 * j) // heads_per_kv]))
        for j, o in enumerate(_attend_blocks(blocks, n_keys)):
            g = (2 * j) // heads_per_kv
            den_lane = (1 - g) * HEAD_DIM
            o = o / o[:, den_lane:den_lane + 1]
            even, odd = o[:ROW], o[ROW:]
            even = even if g == 0 else pltpu.roll(even, HEAD_DIM, 1)
            odd = odd if g == 1 else pltpu.roll(odd, HEAD_DIM, 1)
            o_ref[:, j * LANES:(j + 1) * LANES] = jnp.where(lane < HEAD_DIM, even, odd).astype(BF16)

    _per_tile_keys(body, k_ref.shape[0])


def _diff_kernel(q_ref, k_ref, v_ref, lam_ref, g_ref, o_ref, *, lam_init):
    lv = lam_ref[...]
    lam = (jnp.exp(jnp.sum(lv[0:1] * lv[1:2], axis=-1, keepdims=True))
           - jnp.exp(jnp.sum(lv[2:3] * lv[3:4], axis=-1, keepdims=True)) + lam_init)

    def body(n_keys):
        blocks = [(q_ref[2 * h:2 * h + 2].reshape(2 * ROW, LANES), k_ref.at[h], v_ref.at[h]) for h in range(B_HEADS)]
        for h, o in enumerate(_attend_blocks(blocks, n_keys)):
            o = o[:, :LANES] / o[:, LANES:LANES + 1]
            dif = o[:ROW] - lam * o[ROW:]
            o_ref[:, h * LANES:(h + 1) * LANES] = (_rms(dif, g_ref[...]) * (1.0 - lam_init)).astype(BF16)

    _per_tile_keys(body, k_ref.shape[1])


def _gqa(qaz, ka, va):
    n_batch, _, ts, _ = qaz.shape
    nt = ts // ROW
    return pl.pallas_call(
        _gqa_kernel,
        grid=(n_batch, nt),
        in_specs=[
            pl.BlockSpec((None, A_HEADS, ROW, LANES), lambda b, i: (b, 0, i, 0)),
            pl.BlockSpec((None, ts, LANES), lambda b, i: (b, 0, 0)),
            pl.BlockSpec((None, A_KV_HEADS, ts, LANES), lambda b, i: (b, 0, 0, 0)),
        ],
        out_specs=pl.BlockSpec((None, ROW, A_HEADS * HEAD_DIM), lambda b, i: (b, i, 0)),
        out_shape=jax.ShapeDtypeStruct((n_batch, ts, A_HEADS * HEAD_DIM), BF16),
        compiler_params=_cparams(("parallel", "parallel")),
        name="gqa_attention",
    )(qaz, ka, va)


def _diff(qbz, kb, vb, lam_vecs, sub_g, lam_init):
    n_batch, _, ts, _ = qbz.shape
    nt = ts // ROW
    return pl.pallas_call(
        functools.partial(_diff_kernel, lam_init=lam_init),
        grid=(n_batch, nt),
        in_specs=[
            pl.BlockSpec((None, 2 * B_HEADS, ROW, LANES), lambda b, i: (b, 0, i, 0)),
            pl.BlockSpec((None, B_HEADS, ts, LANES), lambda b, i: (b, 0, 0, 0)),
            pl.BlockSpec((None, B_HEADS, ts, 2 * LANES), lambda b, i: (b, 0, 0, 0)),
            pl.BlockSpec((4, HEAD_DIM), lambda b, i: (0, 0)),
            pl.BlockSpec((1, LANES), lambda b, i: (0, 0)),
        ],
        out_specs=pl.BlockSpec((None, ROW, B_HEADS * LANES), lambda b, i: (b, i, 0)),
        out_shape=jax.ShapeDtypeStruct((n_batch, ts, B_HEADS * LANES), BF16),
        compiler_params=_cparams(("parallel", "parallel")),
        name="diff_attention",
    )(qbz, kb, vb, lam_vecs, sub_g)


def _mlstm_kernel(qf_ref, kf_ref, vf_ref, gf_ref, gtf_ref, qb_ref, kb_ref, vb_ref, gb_ref, gtb_ref,
                  tri_ref, trit_ref, hf_ref, hb_ref, c_ref, n_ref, m_ref):
    @pl.when(pl.program_id(1) == 0)
    def _():
        c_ref[...] = jnp.zeros_like(c_ref)
        n_ref[...] = jnp.zeros_like(n_ref)
        m_ref[...] = jnp.zeros_like(m_ref)

    length = MCHUNK
    tri = tri_ref[...]
    trit = trit_ref[...]
    r_idx = _row_iota((length, length))
    c_idx = _lane_iota((length, length))
    nt_dims = (((1,), (1,)), ((), ()))
    tn_dims = (((0,), (0,)), ((), ()))

    for direction, (q_ref, k_ref, v_ref, g_ref, gt_ref, h_ref) in enumerate(
            ((qf_ref, kf_ref, vf_ref, gf_ref, gtf_ref, hf_ref), (qb_ref, kb_ref, vb_ref, gb_ref, gtb_ref, hb_ref))):
        gcol = g_ref[...]
        grow = gt_ref[...]
        cs_col = jnp.dot(tri, gcol, preferred_element_type=F32, precision=HIGHEST)
        cs_row = jnp.dot(grow, trit, preferred_element_type=F32, precision=HIGHEST)
        mask = (r_idx >= c_idx) if direction == 0 else (c_idx >= r_idx)
        for hd in range(C_HEADS):
            chain = direction * C_HEADS + hd
            ii = direction * 2 * C_HEADS + hd
            fi = ii + C_HEADS
            icol, irow = gcol[:, ii:ii + 1], grow[ii:ii + 1, :]
            pcol, prow = cs_col[:, fi:fi + 1], cs_row[fi:fi + 1, :]
            total = pcol[length - 1:length, :]
            if direction == 1:
                pcol = total - pcol + gcol[:, fi:fi + 1]
                prow = total - prow + grow[fi:fi + 1, :]
            m_prev = m_ref[chain][:, :1]
            log_d = jnp.where(mask, pcol - prow + irow, NEG)
            inter = pcol + m_prev
            m_t = jnp.maximum(inter, jnp.max(log_d, axis=-1, keepdims=True))
            dmat = jnp.exp(log_d - m_t)
            a_inter = jnp.exp(inter - m_t)
            sl = slice(hd * C_DIM, (hd + 1) * C_DIM)
            q, k, v = q_ref[:, sl], k_ref[:, sl], v_ref[:, sl]
            s = lax.dot_general(q, k, nt_dims, preferred_element_type=F32) * dmat
            c_state = c_ref[chain]
            n_state = n_ref[chain]
            num = (a_inter * lax.dot_general(q, c_state.astype(BF16), nt_dims, preferred_element_type=F32)
                   + jnp.dot(s.astype(BF16), v, preferred_element_type=F32))
            den = (a_inter * jnp.sum(q.astype(F32) * n_state, axis=-1, keepdims=True)
                   + jnp.sum(s, axis=-1, keepdims=True))
            h_ref[:, sl] = num / jnp.maximum(jnp.abs(den), jnp.exp(-m_t))
            w_log = total - pcol + icol
            m_new = jnp.maximum(total + m_prev, jnp.max(w_log, axis=0, keepdims=True))
            decay = jnp.exp(total + m_prev - m_new)
            ws = jnp.exp(w_log - m_new)
            wv = (ws * v.astype(F32)).astype(BF16)
            c_ref[chain] = decay * c_state + lax.dot_general(wv, k, tn_dims, preferred_element_type=F32)
            n_ref[chain] = decay * n_state + jnp.sum(ws * k.astype(F32), axis=0, keepdims=True)
            m_ref[chain] = jnp.broadcast_to(m_new, (1, LANES))


def _mlstm(qm, km, h, gl, glt, tri, trit, d):
    n_batch, ts, width = qm.shape
    nc = ts // MCHUNK
    ctx_chunks = ROW // MCHUNK
    vcol = (3 * d + REL_VC) // width

    def bwd(j):
        return jnp.where(j < ctx_chunks, ctx_chunks - 1 - j, nc + ctx_chunks - 1 - j)

    def specs(idx):
        return [
            pl.BlockSpec((None, MCHUNK, width), lambda b, j: (b, idx(j), 0)),
            pl.BlockSpec((None, MCHUNK, width), lambda b, j: (b, idx(j), 0)),
            pl.BlockSpec((None, MCHUNK, width), lambda b, j: (b, idx(j), vcol)),
            pl.BlockSpec((None, MCHUNK, LANES), lambda b, j: (b, idx(j), 0)),
            pl.BlockSpec((None, 2 * SUBLANES, MCHUNK), lambda b, j: (b, 0, idx(j))),
        ]

    fwd = lambda j: j
    n_chain = 2 * C_HEADS
    return pl.pallas_call(
        _mlstm_kernel,
        grid=(n_batch, nc),
        in_specs=specs(fwd) + specs(bwd) + [
            pl.BlockSpec((MCHUNK, MCHUNK), lambda b, j: (0, 0)),
            pl.BlockSpec((MCHUNK, MCHUNK), lambda b, j: (0, 0)),
        ],
        out_specs=[
            pl.BlockSpec((None, MCHUNK, width), lambda b, j: (b, j, 0)),
            pl.BlockSpec((None, MCHUNK, width), lambda b, j: (b, bwd(j), 0)),
        ],
        out_shape=[jax.ShapeDtypeStruct((n_batch, ts, width), F32)] * 2,
        scratch_shapes=[
            pltpu.VMEM((n_chain, C_DIM, C_DIM), F32),
            pltpu.VMEM((n_chain, 1, C_DIM), F32),
            pltpu.VMEM((n_chain, 1, LANES), F32),
        ],
        compiler_params=_cparams(("parallel", "arbitrary")),
        name="mlstm",
    )(qm, km, h, gl, glt, qm, km, h, gl, glt, tri, trit)


def _mix_kernel(a_ref, d_ref, hf_ref, hb_ref, oc_ref, gate_ref, x_ref, g1_ref, pg_ref, mg_ref,
                wa_ref, wb_ref, wc_ref, wo_ref, o_ref):
    d_model = x_ref.shape[-1]
    hsum = hf_ref[...] + hb_ref[...]
    mg = mg_ref[...]
    m = jnp.concatenate([_rms(hsum[:, hd * C_DIM:(hd + 1) * C_DIM], mg) for hd in range(C_HEADS)], axis=1)
    m = m * _sigmoid(oc_ref[...].astype(F32))
    u = (_sigmoid(gate_ref[:, :d_model].astype(F32))
         * jnp.dot(a_ref[...], wa_ref[...], preferred_element_type=F32)
         + _sigmoid(gate_ref[:, d_model:2 * d_model].astype(F32))
         * jnp.dot(d_ref[...], wb_ref[...], preferred_element_type=F32)
         + _sigmoid(gate_ref[:, 2 * d_model:].astype(F32))
         * jnp.dot(m.astype(BF16), wc_ref[...], preferred_element_type=F32))
    y = jnp.dot(u.astype(BF16), wo_ref[...], preferred_element_type=F32)
    o_ref[...] = x_ref[...] + g1_ref[...] * _rms(y, pg_ref[...])


def _mix(a, dd, hf, hb, h, xs, mod, layer, nw, post_g, mlstm_g, wa, wb, wc, wo):
    n_batch, ts, d = xs.shape
    nt = ts // ROW
    width = a.shape[-1]
    ocol = (3 * d + REL_OC) // width

    def tile(w):
        return pl.BlockSpec((None, ROW, w), lambda b, i: (b, i, 0))

    def const(shape):
        return pl.BlockSpec(shape, lambda b, i: (0,) * len(shape))

    return pl.pallas_call(
        _mix_kernel,
        grid=(n_batch, nt),
        in_specs=[
            tile(width), tile(width), tile(width), tile(width),
            pl.BlockSpec((None, ROW, width), lambda b, i: (b, i, ocol)),
            pl.BlockSpec((None, ROW, 3 * d), lambda b, i: (b, i, 0)),
            tile(d),
            _mod_spec(d, layer, 2, nw, n_batch),
            const((1, d)), const((1, C_DIM)),
            const((width, d)), const((width, d)), const((width, d)), const((d, d)),
        ],
        out_specs=tile(d),
        out_shape=jax.ShapeDtypeStruct((n_batch, ts, d), F32),
        compiler_params=_cparams(("parallel", "parallel")),
        name="mix_out",
    )(a, dd, hf, hb, h, h, xs, mod, post_g, mlstm_g, wa, wb, wc, wo)


def _ffn_kernel(x_ref, sh_ref, sc_ref, g2_ref, pre_ref, post_ref, wg_ref, wu_ref, wd_ref, o_ref):
    x = x_ref[...]
    xb = (_rms(x, pre_ref[...]) * (1.0 + sc_ref[...]) + sh_ref[...]).astype(BF16)
    gate = jnp.dot(xb, wg_ref[...], preferred_element_type=F32)
    up = jnp.dot(xb, wu_ref[...], preferred_element_type=F32)
    z = jnp.dot((_silu(gate) * up).astype(BF16), wd_ref[...], preferred_element_type=F32)
    o_ref[...] = x + g2_ref[...] * _rms(z, post_ref[...])


def _ffn(xs, mod, layer, nw, pre_g, post_g, wg, wu, wd):
    n_batch, ts, d = xs.shape
    nt = ts // ROW
    dff = wg.shape[1]

    def resident(shape):
        return pl.BlockSpec(shape, lambda b, i: (0,) * len(shape), pipeline_mode=pl.Buffered(1))

    tile = pl.BlockSpec((None, ROW, d), lambda b, i: (b, i, 0))
    return pl.pallas_call(
        _ffn_kernel,
        grid=(n_batch, nt),
        in_specs=[
            tile,
            _mod_spec(d, layer, 3, nw, n_batch), _mod_spec(d, layer, 4, nw, n_batch),
            _mod_spec(d, layer, 5, nw, n_batch),
            pl.BlockSpec((1, d), lambda b, i: (0, 0)), pl.BlockSpec((1, d), lambda b, i: (0, 0)),
            resident((d, dff)), resident((d, dff)), resident((dff, d)),
        ],
        out_specs=tile,
        out_shape=jax.ShapeDtypeStruct((n_batch, ts, d), F32),
        compiler_params=_cparams(("parallel", "parallel")),
        name="ffn",
    )(xs, mod, mod, mod, pre_g, post_g, wg, wu, wd)


def _route_kernel(x_ref, sh_ref, sc_ref, pre_ref, wr_ref, br_ref, tri_ref, xn_ref, meta_ref, cnt_ref, carry_ref):
    @pl.when(jnp.logical_and(pl.program_id(0) == 0, pl.program_id(1) == 0))
    def _():
        carry_ref[...] = jnp.zeros_like(carry_ref)

    xn = _rms(x_ref[...], pre_ref[...]) * (1.0 + sc_ref[...]) + sh_ref[...]
    xn_ref[...] = xn
    lane = _lane_iota((ROW, LANES))
    logits = jnp.dot(xn, wr_ref[...], preferred_element_type=F32, precision=HIGHEST) + br_ref[...]
    logits = jnp.where(lane < N_EXPERTS, logits, NEG)
    v1 = jnp.max(logits, axis=-1, keepdims=True)
    i1 = jnp.min(jnp.where(logits == v1, lane, LANES), axis=-1, keepdims=True)
    rest = jnp.where(lane == i1, NEG, logits)
    v2 = jnp.max(rest, axis=-1, keepdims=True)
    i2 = jnp.min(jnp.where(rest == v2, lane, LANES), axis=-1, keepdims=True)
    e2 = jnp.exp(v2 - v1)
    w1 = 1.0 / (1.0 + e2)
    w2 = e2 / (1.0 + e2)
    assigned = jnp.where(lane == i1, 1.0, jnp.where(lane == i2, 1.0, 0.0))
    before = jnp.dot(tri_ref[...], assigned.astype(BF16), preferred_element_type=F32) + carry_ref[0:1, :]
    r1 = jnp.sum(jnp.where(lane == i1, before, 0.0), axis=-1, keepdims=True)
    r2 = jnp.sum(jnp.where(lane == i2, before, 0.0), axis=-1, keepdims=True)
    carry_ref[...] = carry_ref[...] + jnp.sum(assigned, axis=0, keepdims=True)
    cnt_ref[...] = carry_ref[...]
    fields = (i1.astype(F32), i2.astype(F32), w1, w2, r1, r2)
    meta = jnp.zeros((ROW, LANES), F32)
    for f, val in enumerate(fields):
        meta = jnp.where(lane == f, val, meta)
    meta_ref[...] = meta


def _route(xs, mod, layer, nw, pre_g, w_r, b_r):
    n_batch, ts, d = xs.shape
    nt = ts // ROW
    tile = pl.BlockSpec((None, ROW, d), lambda b, i: (b, i, 0))
    t_idx = jnp.arange(ROW)
    tri_strict = (t_idx[:, None] > t_idx[None, :]).astype(BF16)
    return pl.pallas_call(
        _route_kernel,
        grid=(n_batch, nt),
        in_specs=[
            tile, _mod_spec(d, layer, 3, nw, n_batch), _mod_spec(d, layer, 4, nw, n_batch),
            pl.BlockSpec((1, d), lambda b, i: (0, 0)),
            pl.BlockSpec((d, LANES), lambda b, i: (0, 0)),
            pl.BlockSpec((1, LANES), lambda b, i: (0, 0)),
            pl.BlockSpec((ROW, ROW), lambda b, i: (0, 0)),
        ],
        out_specs=[tile, pl.BlockSpec((None, ROW, LANES), lambda b, i: (b, i, 0)),
                   pl.BlockSpec((SUBLANES, LANES), lambda b, i: (0, 0))],
        out_shape=[jax.ShapeDtypeStruct((n_batch, ts, d), F32),
                   jax.ShapeDtypeStruct((n_batch, ts, LANES), F32),
                   jax.ShapeDtypeStruct((SUBLANES, LANES), F32)],
        scratch_shapes=[pltpu.VMEM((SUBLANES, LANES), F32)],
        compiler_params=_cparams(("arbitrary", "arbitrary")),
        name="route",
    )(xs, mod, mod, pre_g, w_r, b_r, tri_strict)


def _start_rows(make_copy):
    def issue(r, carry):
        for slot in range(2):
            make_copy(r, slot).start()
        return carry

    lax.fori_loop(0, ROW, issue, 0, unroll=4)


def _wait_rows(make_copy):
    def drain(r, carry):
        for slot in range(2):
            make_copy(r, slot).wait()
        return carry

    lax.fori_loop(0, ROW, drain, 0, unroll=4)


def _dispatch_kernel(pos_ref, ends_ref, xn_ref, out_ref, zero_ref, sems, zero_sem):
    step = pl.program_id(0)
    n_steps = pl.num_programs(0)
    n_tok = pos_ref.shape[0] // 2
    tm = zero_ref.shape[0]
    n_sorted = out_ref.shape[0]

    @pl.when(step == 0)
    def _():
        zero_ref[...] = jnp.zeros_like(zero_ref)

        def fill(row):
            copy = pltpu.make_async_copy(zero_ref, out_ref.at[pl.ds(pl.multiple_of(row, tm), tm)], zero_sem)
            copy.start()
            copy.wait()

        for e in range(N_EXPERTS):
            prev_end = ends_ref[e - 1] if e > 0 else 0

            @pl.when(ends_ref[e] > prev_end)
            def _():
                fill(ends_ref[e] - tm)

        last_end = ends_ref[N_EXPERTS - 1]

        def tail(k, carry):
            fill(last_end + k * tm)
            return carry

        lax.fori_loop(0, (n_sorted - last_end) // tm, tail, 0)

    def copies(tile):
        def make_copy(r, slot):
            row = tile * ROW + r
            p = pos_ref[slot * n_tok + row]
            return pltpu.make_async_copy(xn_ref.at[pl.ds(row, 1)], out_ref.at[pl.ds(p, 1)], sems.at[tile % 2])
        return make_copy

    _start_rows(copies(step))

    @pl.when(step > 0)
    def _():
        _wait_rows(copies(step - 1))

    @pl.when(step == n_steps - 1)
    def _():
        _wait_rows(copies(step))


def _dispatch(pos, ends, xn, n_sorted, tm):
    m_rows, d = xn.shape
    return pl.pallas_call(
        _dispatch_kernel,
        grid_spec=pltpu.PrefetchScalarGridSpec(
            num_scalar_prefetch=2,
            grid=(m_rows // ROW,),
            in_specs=[pl.BlockSpec(memory_space=pl.ANY)],
            out_specs=pl.BlockSpec(memory_space=pl.ANY),
            scratch_shapes=[pltpu.VMEM((tm, d), F32), pltpu.SemaphoreType.DMA((2,)), pltpu.SemaphoreType.DMA(())],
        ),
        out_shape=jax.ShapeDtypeStruct((n_sorted, d), F32),
        compiler_params=_cparams(("arbitrary",)),
        name="dispatch",
    )(pos, ends, xn)


def _experts_kernel(te_ref, x_ref, wg_ref, wu_ref, wd_ref, y_ref):
    used = te_ref[pl.program_id(0)] < N_EXPERTS

    @pl.when(used)
    def _():
        x = x_ref[...].astype(BF16)
        gate = jnp.dot(x, wg_ref[...], preferred_element_type=F32)
        up = jnp.dot(x, wu_ref[...], preferred_element_type=F32)
        y_ref[...] = jnp.dot((_silu(gate) * up).astype(BF16), wd_ref[...], preferred_element_type=F32)

    @pl.when(jnp.logical_not(used))
    def _():
        y_ref[...] = jnp.zeros_like(y_ref)


def _experts(tile_expert, x_sorted, wg, wu, wd, tm):
    n_sorted, d = x_sorted.shape
    n_e, _, dff = wg.shape

    def weight(shape):
        return pl.BlockSpec((None,) + shape, lambda i, te: (jnp.minimum(te[i], n_e - 1), 0, 0))

    return pl.pallas_call(
        _experts_kernel,
        grid_spec=pltpu.PrefetchScalarGridSpec(
            num_scalar_prefetch=1,
            grid=(n_sorted // tm,),
            in_specs=[pl.BlockSpec((tm, d), lambda i, te: (i, 0)),
                      weight((d, dff)), weight((d, dff)), weight((dff, d))],
            out_specs=pl.BlockSpec((tm, d), lambda i, te: (i, 0)),
        ),
        out_shape=jax.ShapeDtypeStruct((n_sorted, d), F32),
        compiler_params=_cparams(("arbitrary",)),
        name="experts",
    )(tile_expert, x_sorted, wg, wu, wd)


def _combine_kernel(pos_ref, x_ref, meta_ref, g2_ref, post_ref, y_ref, o_ref, buf_ref, sems):
    step = pl.program_id(0)
    n_steps = pl.num_programs(0)
    n_tok = pos_ref.shape[0] // 2

    def copies(tile):
        def make_copy(r, slot):
            p = pos_ref[slot * n_tok + tile * ROW + r]
            return pltpu.make_async_copy(y_ref.at[pl.ds(p, 1)], buf_ref.at[tile % 2, slot, pl.ds(r, 1)],
                                         sems.at[tile % 2])
        return make_copy

    @pl.when(step == 0)
    def _():
        _start_rows(copies(step))

    @pl.when(step + 1 < n_steps)
    def _():
        _start_rows(copies(step + 1))

    _wait_rows(copies(step))
    meta = meta_ref[...]
    cur = step % 2
    z = meta[:, 2:3] * buf_ref[cur, 0] + meta[:, 3:4] * buf_ref[cur, 1]
    o_ref[...] = x_ref[...] + g2_ref[...] * _rms(z, post_ref[...])


def _combine(pos, xs, meta, y_sorted, mod, layer, nw, post_g):
    n_batch, ts, d = xs.shape
    nt = ts // ROW
    m_rows = n_batch * ts
    base = (layer * 6 + 5) * nw
    tile = pl.BlockSpec((ROW, d), lambda i, pos: (i, 0))
    out = pl.pallas_call(
        _combine_kernel,
        grid_spec=pltpu.PrefetchScalarGridSpec(
            num_scalar_prefetch=1,
            grid=(m_rows // ROW,),
            in_specs=[
                tile,
                pl.BlockSpec((ROW, LANES), lambda i, pos: (i, 0)),
                pl.BlockSpec((None, 1, d), lambda i, pos: (base + jnp.where(i % nt == 0, n_batch, i // nt), 0, 0)),
                pl.BlockSpec((1, d), lambda i, pos: (0, 0)),
                pl.BlockSpec(memory_space=pl.ANY),
            ],
            out_specs=tile,
            scratch_shapes=[pltpu.VMEM((2, 2, ROW, d), F32), pltpu.SemaphoreType.DMA((2,))],
        ),
        out_shape=jax.ShapeDtypeStruct((m_rows, d), F32),
        compiler_params=_cparams(("arbitrary",)),
        name="combine",
    )(pos, xs.reshape(m_rows, d), meta.reshape(m_rows, LANES), mod, post_g, y_sorted)
    return out.reshape(n_batch, ts, d)


def _moe(xs, mod, layer, nw, pre_g, post_g, w_r, b_r, wg, wu, wd, tm=512):
    n_batch, ts, d = xs.shape
    m_rows = n_batch * ts
    xn, meta, cnt = _route(xs, mod, layer, nw, pre_g, w_r, b_r)
    m2 = meta.reshape(m_rows, LANES)
    i1, i2 = m2[:, 0].astype(jnp.int32), m2[:, 1].astype(jnp.int32)
    r1, r2 = m2[:, 4].astype(jnp.int32), m2[:, 5].astype(jnp.int32)
    counts = cnt[0, :N_EXPERTS].astype(jnp.int32)
    padded = -(-counts // tm) * tm
    ends = jnp.cumsum(padded)
    start = ends - padded
    pos = jnp.concatenate([start[i1] + r1, start[i2] + r2])
    n_tiles = 2 * m_rows // tm + N_EXPERTS
    tile_row = jnp.arange(n_tiles, dtype=jnp.int32) * tm
    tile_expert = jnp.sum((ends[None, :] <= tile_row[:, None]).astype(jnp.int32), axis=1)
    x_sorted = _dispatch(pos, ends, xn.reshape(m_rows, d), n_tiles * tm, tm)
    y_sorted = _experts(tile_expert, x_sorted, wg, wu, wd, tm)
    return _combine(pos, xs, meta, y_sorted, mod, layer, nw, post_g)


def _rope_tables(n_tok, n_ctx):
    n_freq = HEAD_DIM // 4
    pos = jnp.arange(n_tok)
    row = (pos // GRID_W).astype(F32)
    colp = (pos % GRID_W).astype(F32)
    inv = ROPE_THETA ** (-jnp.arange(n_freq, dtype=F32) / n_freq)
    lane = jnp.arange(LANES)
    in_head = lane % HEAD_DIM
    use_col = (in_head // (HEAD_DIM // 2)) == 1
    freq = inv[in_head % n_freq]
    ang = jnp.where(use_col[None, :], colp[:, None], row[:, None]) * freq[None, :]
    lower = (in_head % (HEAD_DIM // 2)) < n_freq
    cos_t = jnp.cos(ang)
    sin_t = jnp.sin(ang)
    sin_a = jnp.where(lower[None, :], -sin_t, 0.0)
    sin_b = jnp.where(lower[None, :], 0.0, sin_t)
    pad = lambda t, v: jnp.concatenate([jnp.full((n_ctx, LANES), v, F32), t], axis=0)
    return pad(cos_t, 1.0), pad(sin_a, 0.0), pad(sin_b, 0.0)


def _block_diag_mean(width):
    idx = jnp.arange(width) // HEAD_DIM
    return jnp.where(idx[:, None] == idx[None, :], 1.0 / HEAD_DIM, 0.0).astype(BF16)


def _pack_w_in(w):
    a_q = A_HEADS * HEAD_DIM
    a_kv = 2 * A_KV_HEADS * HEAD_DIM
    n_gates = 4 * C_HEADS
    gate_start = w.shape[1] - 3 * w.shape[0]
    g_start = gate_start - n_gates
    main = jnp.concatenate([w[:, gate_start:], w[:, :a_q], w[:, a_q + a_kv:g_start], w[:, a_q:a_q + a_kv]], axis=1)
    gates = jnp.pad(w[:, g_start:gate_start], ((0, 0), (0, LANES - n_gates)))
    return main.astype(BF16), gates.astype(BF16)


def kernel(x, c, ctx, c_ctx, ada_w, ada_b, pre_mix_g, post_mix_g, pre_ffn_g, post_ffn_g, w_in, q_norm_g, k_norm_g, lam_q1, lam_k1, lam_q2, lam_k2, diff_norm_g, conv_w, conv_b, mlstm_gate_b, mlstm_norm_g, w_br_attn, w_br_diff, w_br_mlstm, w_out, w_ff_gate, w_ff_up, w_ff_down, w_router, b_router, w_moe_gate, w_moe_up, w_moe_down):
    n_batch, n_tok, d = x.shape
    n_ctx = ctx.shape[1]
    depth = ada_w.shape[0]
    assert n_ctx == ROW and n_tok % ROW == 0 and d == 1024
    ts = n_ctx + n_tok
    nw = -(-(n_batch + 1) // SUBLANES) * SUBLANES

    c_all = jnp.concatenate([c, c_ctx[None, :], jnp.zeros((nw - n_batch - 1, d), F32)], axis=0)
    mod = _modulation(c_all, ada_w, ada_b)
    tables = _rope_tables(n_tok, n_ctx)
    bd4, bd1 = _block_diag_mean(A_HEADS * HEAD_DIM), _block_diag_mean(LANES)
    t_idx = jnp.arange(MCHUNK)
    tri = (t_idx[:, None] >= t_idx[None, :]).astype(F32)
    trit = tri.T

    xs = jnp.concatenate([ctx, x], axis=1)
    for l in range(depth):
        lam_init = 0.8 - 0.6 * math.exp(-0.3 * l)
        w_main, w_gate = _pack_w_in(w_in[l])
        h, gates = _inproj(xs, mod, l, nw, pre_mix_g[l][None, :], w_main, w_gate)
        qg = jnp.tile(q_norm_g[l], A_HEADS)[None, :]
        kg = jnp.tile(k_norm_g[l], A_KV_HEADS)[None, :]
        gate_b = jnp.pad(mlstm_gate_b[l], (0, LANES - 4 * C_HEADS))[None, :]
        qaz, ka, va, qbz, kb, vb, qm, km, gl, glt = _prep(
            h, gates, tables, qg, kg, bd4, bd1, conv_w[l], conv_b[l][None, :], gate_b, d)
        a_out = _gqa(qaz, ka, va)
        lam_vecs = jnp.stack([lam_q1[l], lam_k1[l], lam_q2[l], lam_k2[l]], axis=0)
        d_out = _diff(qbz, kb, vb, lam_vecs, diff_norm_g[l][None, :], lam_init)
        hf, hb = _mlstm(qm, km, h, gl, glt, tri, trit, d)
        xs = _mix(a_out, d_out, hf, hb, h, xs, mod, l, nw, post_mix_g[l][None, :], mlstm_norm_g[l][None, :],
                  w_br_attn[l].astype(BF16), w_br_diff[l].astype(BF16), w_br_mlstm[l].astype(BF16),
                  w_out[l].astype(BF16))
        j = l // 2
        if l % 2 == 0:
            xs = _ffn(xs, mod, l, nw, pre_ffn_g[l][None, :], post_ffn_g[l][None, :],
                      w_ff_gate[j].astype(BF16), w_ff_up[j].astype(BF16), w_ff_down[j].astype(BF16))
        else:
            w_r = jnp.pad(w_router[j], ((0, 0), (0, LANES - N_EXPERTS)))
            b_r = jnp.pad(b_router[j], (0, LANES - N_EXPERTS))[None, :]
            xs = _moe(xs, mod, l, nw, pre_ffn_g[l][None, :], post_ffn_g[l][None, :], w_r, b_r,
                      w_moe_gate[j].astype(BF16), w_moe_up[j].astype(BF16), w_moe_down[j].astype(BF16))
    return xs[:, n_ctx:, :]
```

```python
import functools
import math

import jax
import jax.numpy as jnp
from jax import lax
from jax.experimental import pallas as pl
from jax.experimental.pallas import tpu as pltpu

F32 = jnp.float32
BF16 = jnp.bfloat16
HIGHEST = lax.Precision.HIGHEST

EPS = 1e-6
HEAD_DIM = 64
A_HEADS = 8
A_KV_HEADS = 2
B_HEADS = 4
C_HEADS = 4
C_DIM = 128
N_EXPERTS = 8
ROPE_THETA = 10000.0
GRID_W = 64
CONV_W = 3

LANES = 128
SUBLANES = 8
ROW = 256
MCHUNK = 128
LOG2E = math.log2(math.e)
NEG = -1e30
VMEM_LIMIT = 56 * 1024 * 1024

OFF_GATE = 0
REL_QA, REL_QB, REL_KB, REL_VB = 0, 512, 1024, 1536
REL_QC, REL_KC, REL_VC, REL_OC = 2048, 2560, 3072, 3584
REL_KA = 4096
REL_END = 4352


def _cparams(sem):
    return pltpu.CompilerParams(dimension_semantics=sem, vmem_limit_bytes=VMEM_LIMIT)


def _rms(x, g):
    y = x * lax.rsqrt(jnp.mean(x * x, axis=-1, keepdims=True) + EPS)
    return y * g


def _sigmoid(x):
    return 1.0 / (1.0 + jnp.exp(-x))


def _silu(x):
    return x * _sigmoid(x)


def _log_sigmoid(x):
    return jnp.minimum(x, 0.0) - jnp.log(1.0 + jnp.exp(-jnp.abs(x)))


def _lane_iota(shape):
    return lax.broadcasted_iota(jnp.int32, shape, len(shape) - 1)


def _row_iota(shape):
    return lax.broadcasted_iota(jnp.int32, shape, len(shape) - 2)


def _mod_kernel(c_ref, w_ref, b_ref, o_ref):
    c = c_ref[...]
    o_ref[...] = jnp.dot(_silu(c), w_ref[...], preferred_element_type=F32, precision=HIGHEST) + b_ref[...]


def _modulation(c_all, ada_w, ada_b):
    depth, d, _ = ada_w.shape
    nw = c_all.shape[0]
    out = pl.pallas_call(
        _mod_kernel,
        grid=(depth, 6),
        in_specs=[
            pl.BlockSpec((nw, d), lambda l, j: (0, 0)),
            pl.BlockSpec((None, d, d), lambda l, j: (l, 0, j)),
            pl.BlockSpec((None, 1, d), lambda l, j: (l, 0, j)),
        ],
        out_specs=pl.BlockSpec((None, None, nw, d), lambda l, j: (l, j, 0, 0)),
        out_shape=jax.ShapeDtypeStruct((depth, 6, nw, d), F32),
        compiler_params=_cparams(("arbitrary", "arbitrary")),
        name="modulation",
    )(c_all, ada_w, ada_b.reshape(depth, 1, 6 * d))
    return out.reshape(depth * 6 * nw, 1, d)


def _mod_spec(d, layer, chunk, nw, n_batch):
    base = (layer * 6 + chunk) * nw
    return pl.BlockSpec((None, 1, d), lambda b, i: (base + jnp.where(i == 0, n_batch, b), 0, 0))


def _inproj_kernel(x_ref, sh_ref, sc_ref, g_ref, w_ref, wg_ref, h_ref, gate_ref, *, chunks):
    xn = _rms(x_ref[...], g_ref[...]) * (1.0 + sc_ref[...]) + sh_ref[...]
    xb = xn.astype(BF16)
    for start, width in chunks:
        h_ref[:, start:start + width] = jnp.dot(
            xb, w_ref[:, start:start + width], preferred_element_type=F32).astype(BF16)
    gate_ref[...] = jnp.dot(xb, wg_ref[...], preferred_element_type=F32)


def _inproj(xs, mod, layer, nw, pre_g, w_main, w_gate):
    n_batch, ts, d = xs.shape
    n_main = w_main.shape[1]
    nt = ts // ROW
    chunks = [(s, min(1024, n_main - s)) for s in range(0, n_main, 1024)]
    return pl.pallas_call(
        functools.partial(_inproj_kernel, chunks=chunks),
        grid=(n_batch, nt),
        in_specs=[
            pl.BlockSpec((None, ROW, d), lambda b, i: (b, i, 0)),
            _mod_spec(d, layer, 0, nw, n_batch),
            _mod_spec(d, layer, 1, nw, n_batch),
            pl.BlockSpec((1, d), lambda b, i: (0, 0)),
            pl.BlockSpec((d, n_main), lambda b, i: (0, 0), pipeline_mode=pl.Buffered(1)),
            pl.BlockSpec((d, LANES), lambda b, i: (0, 0)),
        ],
        out_specs=[
            pl.BlockSpec((None, ROW, n_main), lambda b, i: (b, i, 0)),
            pl.BlockSpec((None, ROW, LANES), lambda b, i: (b, i, 0)),
        ],
        out_shape=[
            jax.ShapeDtypeStruct((n_batch, ts, n_main), BF16),
            jax.ShapeDtypeStruct((n_batch, ts, LANES), F32),
        ],
        compiler_params=_cparams(("parallel", "parallel")),
        name="inproj",
    )(xs, mod, mod, pre_g, w_main, w_gate)


def _head_mean_sq(x, bd_ref):
    sq = x * x
    hi = sq.astype(BF16)
    lo = (sq - hi.astype(F32)).astype(BF16)
    bd = bd_ref[...]
    return jnp.dot(hi, bd, preferred_element_type=F32) + jnp.dot(lo, bd, preferred_element_type=F32)


def _prep_kernel(qa_ref, kava_ref, qb_ref, kb_ref, vb_ref, qkc_ref, prev_ref, next_ref, g_ref,
                 cos_ref, sa_ref, sb_ref, qg_ref, kg_ref, bd4_ref, bd1_ref, cw_ref, cb_ref, gb_ref,
                 qaz_ref, ka_ref, va_ref, qbz_ref, kbo_ref, vbo_ref, qm_ref, km_ref, gl_ref, glt_ref):
    i = pl.program_id(1)
    nt = pl.num_programs(1)
    cos, sin_a, sin_b = cos_ref[...], sa_ref[...], sb_ref[...]

    def rope(x):
        width = x.shape[1]
        reps = width // LANES
        c = jnp.concatenate([cos] * reps, axis=1) if reps > 1 else cos
        a = jnp.concatenate([sin_a] * reps, axis=1) if reps > 1 else sin_a
        b = jnp.concatenate([sin_b] * reps, axis=1) if reps > 1 else sin_b
        return x * c + pltpu.roll(x, width - 16, 1) * a + pltpu.roll(x, 16, 1) * b

    lane = _lane_iota((ROW, LANES))
    ones = jnp.ones((ROW, LANES), BF16)
    scale = HEAD_DIM ** -0.5 * LOG2E

    qa = qa_ref[...].astype(F32)
    qa = qa * lax.rsqrt(_head_mean_sq(qa, bd4_ref) + EPS) * qg_ref[...]
    qa = rope(qa) * scale
    heads_per_kv = A_HEADS // A_KV_HEADS
    for h in range(A_HEADS):
        g = h // heads_per_kv
        blk = qa[:, (h // 2) * LANES:(h // 2 + 1) * LANES]
        if h % 2 != g:
            blk = pltpu.roll(blk, HEAD_DIM, 1)
        qaz_ref[h] = jnp.where(lane // HEAD_DIM == g, blk, 0.0).astype(BF16)
    kava = kava_ref[...].astype(F32)
    ka = kava[:, :LANES]
    ka = ka * lax.rsqrt(_head_mean_sq(ka, bd1_ref) + EPS) * kg_ref[...]
    ka_ref[...] = rope(ka).astype(BF16)
    va = kava_ref[:, LANES:]
    for g in range(A_KV_HEADS):
        va_ref[g] = jnp.where(lane // HEAD_DIM == g, va, ones)

    qb = rope(qb_ref[...].astype(F32)) * scale
    kb = rope(kb_ref[...].astype(F32))
    for h in range(B_HEADS):
        blk = qb[:, h * LANES:(h + 1) * LANES]
        for m in range(2):
            qbz_ref[2 * h + m] = jnp.where(lane // HEAD_DIM == m, blk, 0.0).astype(BF16)
        kbo_ref[h] = kb[:, h * LANES:(h + 1) * LANES].astype(BF16)
        vbo_ref[h, :, :LANES] = vb_ref[:, h * LANES:(h + 1) * LANES]
        vbo_ref[h, :, LANES:] = ones

    cur = qkc_ref[...].astype(F32)
    row = _row_iota(cur.shape)
    prev_row = jnp.where(i >= 2, prev_ref[SUBLANES - 1:SUBLANES, :].astype(F32), 0.0)
    next_ok = jnp.logical_and(i >= 1, i < nt - 1)
    next_row = jnp.where(next_ok, next_ref[0:1, :].astype(F32), 0.0)
    up = jnp.where(row == 0, prev_row, pltpu.roll(cur, 1, 0))
    dn = jnp.where(row == ROW - 1, next_row, pltpu.roll(cur, ROW - 1, 0))
    y = up * cw_ref[0:1, :] + cur * cw_ref[1:2, :] + dn * cw_ref[2:3, :] + cb_ref[...]
    y = _silu(y)
    half = C_HEADS * C_DIM
    qm_ref[...] = y[:, :half].astype(BF16)
    km_ref[...] = (y[:, half:] * (C_DIM ** -0.5)).astype(BF16)

    gg = g_ref[...] + gb_ref[...]
    is_forget = (lane // C_HEADS) % 2 == 1
    gl = jnp.where(is_forget, _log_sigmoid(gg), gg)
    gl_ref[...] = gl
    glt_ref[...] = gl.T[:2 * SUBLANES, :]


def _prep(h, gates, tables, qg, kg, bd4, bd1, conv_w, conv_b, gate_b, d):
    n_batch, ts, _ = h.shape
    nt = ts // ROW
    base = 3 * d

    def col(rel, width):
        return (base + rel) // width

    cos_t, sa_t, sb_t = tables
    row_blocks = ts // SUBLANES
    per_tile = ROW // SUBLANES
    cqk = col(REL_QC, 1024)

    def const(shape):
        return pl.BlockSpec(shape, lambda b, i: (0,) * len(shape))

    in_specs = [
        pl.BlockSpec((None, ROW, 512), lambda b, i: (b, i, col(REL_QA, 512))),
        pl.BlockSpec((None, ROW, 256), lambda b, i: (b, i, col(REL_KA, 256))),
        pl.BlockSpec((None, ROW, 512), lambda b, i: (b, i, col(REL_QB, 512))),
        pl.BlockSpec((None, ROW, 512), lambda b, i: (b, i, col(REL_KB, 512))),
        pl.BlockSpec((None, ROW, 512), lambda b, i: (b, i, col(REL_VB, 512))),
        pl.BlockSpec((None, ROW, 1024), lambda b, i: (b, i, cqk)),
        pl.BlockSpec((None, SUBLANES, 1024), lambda b, i: (b, jnp.maximum(i * per_tile - 1, 0), cqk)),
        pl.BlockSpec((None, SUBLANES, 1024),
                     lambda b, i: (b, jnp.minimum((i + 1) * per_tile, row_blocks - 1), cqk)),
        pl.BlockSpec((None, ROW, LANES), lambda b, i: (b, i, 0)),
        pl.BlockSpec((ROW, LANES), lambda b, i: (i, 0)),
        pl.BlockSpec((ROW, LANES), lambda b, i: (i, 0)),
        pl.BlockSpec((ROW, LANES), lambda b, i: (i, 0)),
        const((1, 512)), const((1, LANES)), const((512, 512)), const((LANES, LANES)),
        const((CONV_W, 1024)), const((1, 1024)), const((1, LANES)),
    ]
    out_specs = [
        pl.BlockSpec((None, A_HEADS, ROW, LANES), lambda b, i: (b, 0, i, 0)),
        pl.BlockSpec((None, ROW, LANES), lambda b, i: (b, i, 0)),
        pl.BlockSpec((None, A_KV_HEADS, ROW, LANES), lambda b, i: (b, 0, i, 0)),
        pl.BlockSpec((None, 2 * B_HEADS, ROW, LANES), lambda b, i: (b, 0, i, 0)),
        pl.BlockSpec((None, B_HEADS, ROW, LANES), lambda b, i: (b, 0, i, 0)),
        pl.BlockSpec((None, B_HEADS, ROW, 2 * LANES), lambda b, i: (b, 0, i, 0)),
        pl.BlockSpec((None, ROW, 512), lambda b, i: (b, i, 0)),
        pl.BlockSpec((None, ROW, 512), lambda b, i: (b, i, 0)),
        pl.BlockSpec((None, ROW, LANES), lambda b, i: (b, i, 0)),
        pl.BlockSpec((None, 2 * SUBLANES, ROW), lambda b, i: (b, 0, i)),
    ]
    out_shape = [
        jax.ShapeDtypeStruct((n_batch, A_HEADS, ts, LANES), BF16),
        jax.ShapeDtypeStruct((n_batch, ts, LANES), BF16),
        jax.ShapeDtypeStruct((n_batch, A_KV_HEADS, ts, LANES), BF16),
        jax.ShapeDtypeStruct((n_batch, 2 * B_HEADS, ts, LANES), BF16),
        jax.ShapeDtypeStruct((n_batch, B_HEADS, ts, LANES), BF16),
        jax.ShapeDtypeStruct((n_batch, B_HEADS, ts, 2 * LANES), BF16),
        jax.ShapeDtypeStruct((n_batch, ts, 512), BF16),
        jax.ShapeDtypeStruct((n_batch, ts, 512), BF16),
        jax.ShapeDtypeStruct((n_batch, ts, LANES), F32),
        jax.ShapeDtypeStruct((n_batch, 2 * SUBLANES, ts), F32),
    ]
    return pl.pallas_call(
        _prep_kernel,
        grid=(n_batch, nt),
        in_specs=in_specs,
        out_specs=out_specs,
        out_shape=out_shape,
        compiler_params=_cparams(("parallel", "parallel")),
        name="prep",
    )(h, h, h, h, h, h, h, h, gates, cos_t, sa_t, sb_t, qg, kg, bd4, bd1, conv_w, conv_b, gate_b)


def _attend_blocks(blocks, n_keys):
    def scores(q, k_ref):
        return lax.dot_general(q, k_ref[:n_keys, :], (((1,), (1,)), ((), ())), preferred_element_type=F32)

    def weighted(s, v_ref):
        p = jnp.exp2(s - jnp.max(s, axis=-1, keepdims=True)).astype(BF16)
        return jnp.dot(p, v_ref[:n_keys, :], preferred_element_type=F32)

    outs = []
    s_cur = scores(blocks[0][0], blocks[0][1])
    for j in range(1, len(blocks)):
        s_next = scores(blocks[j][0], blocks[j][1])
        outs.append(weighted(s_cur, blocks[j - 1][2]))
        s_cur = s_next
    outs.append(weighted(s_cur, blocks[-1][2]))
    return outs


def _per_tile_keys(body, n_all):
    @pl.when(pl.program_id(1) == 0)
    def _():
        body(ROW)

    @pl.when(pl.program_id(1) > 0)
    def _():
        body(n_all)


def _gqa_kernel(q_ref, k_ref, v_ref, o_ref):
    heads_per_kv = A_HEADS // A_KV_HEADS
    lane = _lane_iota((ROW, LANES))

    def body(n_keys):
        blocks = []
        for j in range(A_HEADS // 2):
            q = q_ref[2 * j:2 * j + 2].reshape(2 * ROW, LANES)
            blocks.append((q, k_ref, v_ref.at[(2 * j) // heads_per_kv]))
        for j, o in enumerate(_attend_blocks(blocks, n_keys)):
            g = (2 * j) // heads_per_kv
            den_lane = (1 - g) * HEAD_DIM
            o = o / o[:, den_lane:den_lane + 1]
            even, odd = o[:ROW], o[ROW:]
            even = even if g == 0 else pltpu.roll(even, HEAD_DIM, 1)
            odd = odd if g == 1 else pltpu.roll(odd, HEAD_DIM, 1)
            o_ref[:, j * LANES:(j + 1) * LANES] = jnp.where(lane < HEAD_DIM, even, odd).astype(BF16)

    _per_tile_keys(body, k_ref.shape[0])


def _diff_kernel(q_ref, k_ref, v_ref, lam_ref, g_ref, o_ref, *, lam_init):
    lv = lam_ref[...]
    lam = (jnp.exp(jnp.sum(lv[0:1] * lv[1:2], axis=-1, keepdims=True))
           - jnp.exp(jnp.sum(lv[2:3] * lv[3:4], axis=-1, keepdims=True)) + lam_init)

    def body(n_keys):
        blocks = [(q_ref[2 * h:2 * h + 2].reshape(2 * ROW, LANES), k_ref.at[h], v_ref.at[h]) for h in range(B_HEADS)]
        for h, o in enumerate(_attend_blocks(blocks, n_keys)):
            o = o[:, :LANES] / o[:, LANES:LANES + 1]
            dif = o[:ROW] - lam * o[ROW:]
            o_ref[:, h * LANES:(h + 1) * LANES] = (_rms(dif, g_ref[...]) * (1.0 - lam_init)).astype(BF16)

    _per_tile_keys(body, k_ref.shape[1])


def _gqa(qaz, ka, va):
    n_batch, _, ts, _ = qaz.shape
    nt = ts // ROW
    return pl.pallas_call(
        _gqa_kernel,
        grid=(n_batch, nt),
        in_specs=[
            pl.BlockSpec((None, A_HEADS, ROW, LANES), lambda b, i: (b, 0, i, 0)),
            pl.BlockSpec((None, ts, LANES), lambda b, i: (b, 0, 0)),
            pl.BlockSpec((None, A_KV_HEADS, ts, LANES), lambda b, i: (b, 0, 0, 0)),
        ],
        out_specs=pl.BlockSpec((None, ROW, A_HEADS * HEAD_DIM), lambda b, i: (b, i, 0)),
        out_shape=jax.ShapeDtypeStruct((n_batch, ts, A_HEADS * HEAD_DIM), BF16),
        compiler_params=_cparams(("parallel", "parallel")),
        name="gqa_attention",
    )(qaz, ka, va)


def _diff(qbz, kb, vb, lam_vecs, sub_g, lam_init):
    n_batch, _, ts, _ = qbz.shape
    nt = ts // ROW
    return pl.pallas_call(
        functools.partial(_diff_kernel, lam_init=lam_init),
        grid=(n_batch, nt),
        in_specs=[
            pl.BlockSpec((None, 2 * B_HEADS, ROW, LANES), lambda b, i: (b, 0, i, 0)),
            pl.BlockSpec((None, B_HEADS, ts, LANES), lambda b, i: (b, 0, 0, 0)),
            pl.BlockSpec((None, B_HEADS, ts, 2 * LANES), lambda b, i: (b, 0, 0, 0)),
            pl.BlockSpec((4, HEAD_DIM), lambda b, i: (0, 0)),
            pl.BlockSpec((1, LANES), lambda b, i: (0, 0)),
        ],
        out_specs=pl.BlockSpec((None, ROW, B_HEADS * LANES), lambda b, i: (b, i, 0)),
        out_shape=jax.ShapeDtypeStruct((n_batch, ts, B_HEADS * LANES), BF16),
        compiler_params=_cparams(("parallel", "parallel")),
        name="diff_attention",
    )(qbz, kb, vb, lam_vecs, sub_g)


def _mlstm_kernel(qf_ref, kf_ref, vf_ref, gf_ref, gtf_ref, qb_ref, kb_ref, vb_ref, gb_ref, gtb_ref,
                  tri_ref, trit_ref, hf_ref, hb_ref, c_ref, n_ref, m_ref):
    @pl.when(pl.program_id(1) == 0)
    def _():
        c_ref[...] = jnp.zeros_like(c_ref)
        n_ref[...] = jnp.zeros_like(n_ref)
        m_ref[...] = jnp.zeros_like(m_ref)

    length = MCHUNK
    tri = tri_ref[...]
    trit = trit_ref[...]
    r_idx = _row_iota((length, length))
    c_idx = _lane_iota((length, length))
    nt_dims = (((1,), (1,)), ((), ()))
    tn_dims = (((0,), (0,)), ((), ()))

    for direction, (q_ref, k_ref, v_ref, g_ref, gt_ref, h_ref) in enumerate(
            ((qf_ref, kf_ref, vf_ref, gf_ref, gtf_ref, hf_ref), (qb_ref, kb_ref, vb_ref, gb_ref, gtb_ref, hb_ref))):
        gcol = g_ref[...]
        grow = gt_ref[...]
        cs_col = jnp.dot(tri, gcol, preferred_element_type=F32, precision=HIGHEST)
        cs_row = jnp.dot(grow, trit, preferred_element_type=F32, precision=HIGHEST)
        mask = (r_idx >= c_idx) if direction == 0 else (c_idx >= r_idx)
        for hd in range(C_HEADS):
            chain = direction * C_HEADS + hd
            ii = direction * 2 * C_HEADS + hd
            fi = ii + C_HEADS
            icol, irow = gcol[:, ii:ii + 1], grow[ii:ii + 1, :]
            pcol, prow = cs_col[:, fi:fi + 1], cs_row[fi:fi + 1, :]
            total = pcol[length - 1:length, :]
            if direction == 1:
                pcol = total - pcol + gcol[:, fi:fi + 1]
                prow = total - prow + grow[fi:fi + 1, :]
            m_prev = m_ref[chain][:, :1]
            log_d = jnp.where(mask, pcol - prow + irow, NEG)
            inter = pcol + m_prev
            m_t = jnp.maximum(inter, jnp.max(log_d, axis=-1, keepdims=True))
            dmat = jnp.exp(log_d - m_t)
            a_inter = jnp.exp(inter - m_t)
            sl = slice(hd * C_DIM, (hd + 1) * C_DIM)
            q, k, v = q_ref[:, sl], k_ref[:, sl], v_ref[:, sl]
            s = lax.dot_general(q, k, nt_dims, preferred_element_type=F32) * dmat
            c_state = c_ref[chain]
            n_state = n_ref[chain]
            num = (a_inter * lax.dot_general(q, c_state.astype(BF16), nt_dims, preferred_element_type=F32)
                   + jnp.dot(s.astype(BF16), v, preferred_element_type=F32))
            den = (a_inter * jnp.sum(q.astype(F32) * n_state, axis=-1, keepdims=True)
                   + jnp.sum(s, axis=-1, keepdims=True))
            h_ref[:, sl] = num / jnp.maximum(jnp.abs(den), jnp.exp(-m_t))
            w_log = total - pcol + icol
            m_new = jnp.maximum(total + m_prev, jnp.max(w_log, axis=0, keepdims=True))
            decay = jnp.exp(total + m_prev - m_new)
            ws = jnp.exp(w_log - m_new)
            wv = (ws * v.astype(F32)).astype(BF16)
            c_ref[chain] = decay * c_state + lax.dot_general(wv, k, tn_dims, preferred_element_type=F32)
            n_ref[chain] = decay * n_state + jnp.sum(ws * k.astype(F32), axis=0, keepdims=True)
            m_ref[chain] = jnp.broadcast_to(m_new, (1, LANES))


def _mlstm(qm, km, h, gl, glt, tri, trit, d):
    n_batch, ts, width = qm.shape
    nc = ts // MCHUNK
    ctx_chunks = ROW // MCHUNK
    vcol = (3 * d + REL_VC) // width

    def bwd(j):
        return jnp.where(j < ctx_chunks, ctx_chunks - 1 - j, nc + ctx_chunks - 1 - j)

    def specs(idx):
        return [
            pl.BlockSpec((None, MCHUNK, width), lambda b, j: (b, idx(j), 0)),
            pl.BlockSpec((None, MCHUNK, width), lambda b, j: (b, idx(j), 0)),
            pl.BlockSpec((None, MCHUNK, width), lambda b, j: (b, idx(j), vcol)),
            pl.BlockSpec((None, MCHUNK, LANES), lambda b, j: (b, idx(j), 0)),
            pl.BlockSpec((None, 2 * SUBLANES, MCHUNK), lambda b, j: (b, 0, idx(j))),
        ]

    fwd = lambda j: j
    n_chain = 2 * C_HEADS
    return pl.pallas_call(
        _mlstm_kernel,
        grid=(n_batch, nc),
        in_specs=specs(fwd) + specs(bwd) + [
            pl.BlockSpec((MCHUNK, MCHUNK), lambda b, j: (0, 0)),
            pl.BlockSpec((MCHUNK, MCHUNK), lambda b, j: (0, 0)),
        ],
        out_specs=[
            pl.BlockSpec((None, MCHUNK, width), lambda b, j: (b, j, 0)),
            pl.BlockSpec((None, MCHUNK, width), lambda b, j: (b, bwd(j), 0)),
        ],
        out_shape=[jax.ShapeDtypeStruct((n_batch, ts, width), F32)] * 2,
        scratch_shapes=[
            pltpu.VMEM((n_chain, C_DIM, C_DIM), F32),
            pltpu.VMEM((n_chain, 1, C_DIM), F32),
            pltpu.VMEM((n_chain, 1, LANES), F32),
        ],
        compiler_params=_cparams(("parallel", "arbitrary")),
        name="mlstm",
    )(qm, km, h, gl, glt, qm, km, h, gl, glt, tri, trit)


def _mix_kernel(a_ref, d_ref, hf_ref, hb_ref, oc_ref, gate_ref, x_ref, g1_ref, pg_ref, mg_ref,
                wa_ref, wb_ref, wc_ref, wo_ref, o_ref):
    d_model = x_ref.shape[-1]
    hsum = hf_ref[...] + hb_ref[...]
    mg = mg_ref[...]
    m = jnp.concatenate([_rms(hsum[:, hd * C_DIM:(hd + 1) * C_DIM], mg) for hd in range(C_HEADS)], axis=1)
    m = m * _sigmoid(oc_ref[...].astype(F32))
    u = (_sigmoid(gate_ref[:, :d_model].astype(F32))
         * jnp.dot(a_ref[...], wa_ref[...], preferred_element_type=F32)
         + _sigmoid(gate_ref[:, d_model:2 * d_model].astype(F32))
         * jnp.dot(d_ref[...], wb_ref[...], preferred_element_type=F32)
         + _sigmoid(gate_ref[:, 2 * d_model:].astype(F32))
         * jnp.dot(m.astype(BF16), wc_ref[...], preferred_element_type=F32))
    y = jnp.dot(u.astype(BF16), wo_ref[...], preferred_element_type=F32)
    o_ref[...] = x_ref[...] + g1_ref[...] * _rms(y, pg_ref[...])


def _mix(a, dd, hf, hb, h, xs, mod, layer, nw, post_g, mlstm_g, wa, wb, wc, wo):
    n_batch, ts, d = xs.shape
    nt = ts // ROW
    width = a.shape[-1]
    ocol = (3 * d + REL_OC) // width

    def tile(w):
        return pl.BlockSpec((None, ROW, w), lambda b, i: (b, i, 0))

    def const(shape):
        return pl.BlockSpec(shape, lambda b, i: (0,) * len(shape))

    return pl.pallas_call(
        _mix_kernel,
        grid=(n_batch, nt),
        in_specs=[
            tile(width), tile(width), tile(width), tile(width),
            pl.BlockSpec((None, ROW, width), lambda b, i: (b, i, ocol)),
            pl.BlockSpec((None, ROW, 3 * d), lambda b, i: (b, i, 0)),
            tile(d),
            _mod_spec(d, layer, 2, nw, n_batch),
            const((1, d)), const((1, C_DIM)),
            const((width, d)), const((width, d)), const((width, d)), const((d, d)),
        ],
        out_specs=tile(d),
        out_shape=jax.ShapeDtypeStruct((n_batch, ts, d), F32),
        compiler_params=_cparams(("parallel", "parallel")),
        name="mix_out",
    )(a, dd, hf, hb, h, h, xs, mod, post_g, mlstm_g, wa, wb, wc, wo)


def _ffn_kernel(x_ref, sh_ref, sc_ref, g2_ref, pre_ref, post_ref, wg_ref, wu_ref, wd_ref, o_ref):
    x = x_ref[...]
    xb = (_rms(x, pre_ref[...]) * (1.0 + sc_ref[...]) + sh_ref[...]).astype(BF16)
    gate = jnp.dot(xb, wg_ref[...], preferred_element_type=F32)
    up = jnp.dot(xb, wu_ref[...], preferred_element_type=F32)
    z = jnp.dot((_silu(gate) * up).astype(BF16), wd_ref[...], preferred_element_type=F32)
    o_ref[...] = x + g2_ref[...] * _rms(z, post_ref[...])


def _ffn(xs, mod, layer, nw, pre_g, post_g, wg, wu, wd):
    n_batch, ts, d = xs.shape
    nt = ts // ROW
    dff = wg.shape[1]

    def resident(shape):
        return pl.BlockSpec(shape, lambda b, i: (0,) * len(shape), pipeline_mode=pl.Buffered(1))

    tile = pl.BlockSpec((None, ROW, d), lambda b, i: (b, i, 0))
    return pl.pallas_call(
        _ffn_kernel,
        grid=(n_batch, nt),
        in_specs=[
            tile,
            _mod_spec(d, layer, 3, nw, n_batch), _mod_spec(d, layer, 4, nw, n_batch),
            _mod_spec(d, layer, 5, nw, n_batch),
            pl.BlockSpec((1, d), lambda b, i: (0, 0)), pl.BlockSpec((1, d), lambda b, i: (0, 0)),
            resident((d, dff)), resident((d, dff)), resident((dff, d)),
        ],
        out_specs=tile,
        out_shape=jax.ShapeDtypeStruct((n_batch, ts, d), F32),
        compiler_params=_cparams(("parallel", "parallel")),
        name="ffn",
    )(xs, mod, mod, mod, pre_g, post_g, wg, wu, wd)


def _route_kernel(x_ref, sh_ref, sc_ref, pre_ref, wr_ref, br_ref, tri_ref, xn_ref, meta_ref, cnt_ref, carry_ref):
    @pl.when(jnp.logical_and(pl.program_id(0) == 0, pl.program_id(1) == 0))
    def _():
        carry_ref[...] = jnp.zeros_like(carry_ref)

    xn = _rms(x_ref[...], pre_ref[...]) * (1.0 + sc_ref[...]) + sh_ref[...]
    xn_ref[...] = xn
    lane = _lane_iota((ROW, LANES))
    logits = jnp.dot(xn, wr_ref[...], preferred_element_type=F32, precision=HIGHEST) + br_ref[...]
    logits = jnp.where(lane < N_EXPERTS, logits, NEG)
    v1 = jnp.max(logits, axis=-1, keepdims=True)
    i1 = jnp.min(jnp.where(logits == v1, lane, LANES), axis=-1, keepdims=True)
    rest = jnp.where(lane == i1, NEG, logits)
    v2 = jnp.max(rest, axis=-1, keepdims=True)
    i2 = jnp.min(jnp.where(rest == v2, lane, LANES), axis=-1, keepdims=True)
    e2 = jnp.exp(v2 - v1)
    w1 = 1.0 / (1.0 + e2)
    w2 = e2 / (1.0 + e2)
    assigned = jnp.where(lane == i1, 1.0, jnp.where(lane == i2, 1.0, 0.0))
    before = jnp.dot(tri_ref[...], assigned.astype(BF16), preferred_element_type=F32) + carry_ref[0:1, :]
    r1 = jnp.sum(jnp.where(lane == i1, before, 0.0), axis=-1, keepdims=True)
    r2 = jnp.sum(jnp.where(lane == i2, before, 0.0), axis=-1, keepdims=True)
    carry_ref[...] = carry_ref[...] + jnp.sum(assigned, axis=0, keepdims=True)
    cnt_ref[...] = carry_ref[...]
    fields = (i1.astype(F32), i2.astype(F32), w1, w2, r1, r2)
    meta = jnp.zeros((ROW, LANES), F32)
    for f, val in enumerate(fields):
        meta = jnp.where(lane == f, val, meta)
    meta_ref[...] = meta


def _route(xs, mod, layer, nw, pre_g, w_r, b_r):
    n_batch, ts, d = xs.shape
    nt = ts // ROW
    tile = pl.BlockSpec((None, ROW, d), lambda b, i: (b, i, 0))
    t_idx = jnp.arange(ROW)
    tri_strict = (t_idx[:, None] > t_idx[None, :]).astype(BF16)
    return pl.pallas_call(
        _route_kernel,
        grid=(n_batch, nt),
        in_specs=[
            tile, _mod_spec(d, layer, 3, nw, n_batch), _mod_spec(d, layer, 4, nw, n_batch),
            pl.BlockSpec((1, d), lambda b, i: (0, 0)),
            pl.BlockSpec((d, LANES), lambda b, i: (0, 0)),
            pl.BlockSpec((1, LANES), lambda b, i: (0, 0)),
            pl.BlockSpec((ROW, ROW), lambda b, i: (0, 0)),
        ],
        out_specs=[tile, pl.BlockSpec((None, ROW, LANES), lambda b, i: (b, i, 0)),
                   pl.BlockSpec((SUBLANES, LANES), lambda b, i: (0, 0))],
        out_shape=[jax.ShapeDtypeStruct((n_batch, ts, d), F32),
                   jax.ShapeDtypeStruct((n_batch, ts, LANES), F32),
                   jax.ShapeDtypeStruct((SUBLANES, LANES), F32)],
        scratch_shapes=[pltpu.VMEM((SUBLANES, LANES), F32)],
        compiler_params=_cparams(("arbitrary", "arbitrary")),
        name="route",
    )(xs, mod, mod, pre_g, w_r, b_r, tri_strict)


GROUPS = ROW // SUBLANES


def _start_rows(make_copy):
    def issue(g, carry):
        for j in range(SUBLANES):
            for slot in range(2):
                make_copy(g, j, slot).start()
        return carry

    lax.fori_loop(0, GROUPS, issue, 0)


def _wait_rows(make_copy):
    def drain(g, carry):
        for j in range(SUBLANES):
            for slot in range(2):
                make_copy(0, 0, slot).wait()
        return carry

    lax.fori_loop(0, GROUPS, drain, 0)


def _dispatch_kernel(pos_ref, ends_ref, xn_ref, out_ref, zero_ref, sem, zero_sem):
    step = pl.program_id(0)
    n_tok = pos_ref.shape[0] // 2
    tm = zero_ref.shape[0]
    n_sorted = out_ref.shape[0]

    @pl.when(step == 0)
    def _():
        zero_ref[...] = jnp.zeros_like(zero_ref)

        def fill(row):
            copy = pltpu.make_async_copy(zero_ref, out_ref.at[pl.ds(pl.multiple_of(row, tm), tm)], zero_sem)
            copy.start()
            copy.wait()

        for e in range(N_EXPERTS):
            prev_end = ends_ref[e - 1] if e > 0 else 0

            @pl.when(ends_ref[e] > prev_end)
            def _():
                fill(ends_ref[e] - tm)

        last_end = ends_ref[N_EXPERTS - 1]

        def tail(k, carry):
            fill(last_end + k * tm)
            return carry

        lax.fori_loop(0, (n_sorted - last_end) // tm, tail, 0)

    def make_copy(g, j, slot):
        p = pos_ref[slot * n_tok + step * ROW + g * SUBLANES + j]
        return pltpu.make_async_copy(xn_ref.at[g, pl.ds(j, 1)], out_ref.at[pl.ds(p, 1)], sem)

    _start_rows(make_copy)
    _wait_rows(make_copy)


def _dispatch(pos, ends, xn, n_sorted, tm):
    m_rows, d = xn.shape
    return pl.pallas_call(
        _dispatch_kernel,
        grid_spec=pltpu.PrefetchScalarGridSpec(
            num_scalar_prefetch=2,
            grid=(m_rows // ROW,),
            in_specs=[pl.BlockSpec((GROUPS, SUBLANES, d), lambda i, pos, ends: (i, 0, 0))],
            out_specs=pl.BlockSpec(memory_space=pl.ANY),
            scratch_shapes=[pltpu.VMEM((tm, d), F32), pltpu.SemaphoreType.DMA(()), pltpu.SemaphoreType.DMA(())],
        ),
        out_shape=jax.ShapeDtypeStruct((n_sorted, d), F32),
        compiler_params=_cparams(("arbitrary",)),
        name="dispatch",
    )(pos, ends, xn.reshape(m_rows // SUBLANES, SUBLANES, d))


def _experts_kernel(te_ref, x_ref, wg_ref, wu_ref, wd_ref, y_ref):
    used = te_ref[pl.program_id(0)] < N_EXPERTS

    @pl.when(used)
    def _():
        x = x_ref[...].astype(BF16)
        gate = jnp.dot(x, wg_ref[...], preferred_element_type=F32)
        up = jnp.dot(x, wu_ref[...], preferred_element_type=F32)
        y_ref[...] = jnp.dot((_silu(gate) * up).astype(BF16), wd_ref[...], preferred_element_type=F32)

    @pl.when(jnp.logical_not(used))
    def _():
        y_ref[...] = jnp.zeros_like(y_ref)


def _experts(tile_expert, x_sorted, wg, wu, wd, tm):
    n_sorted, d = x_sorted.shape
    n_e, _, dff = wg.shape

    def weight(shape):
        return pl.BlockSpec((None,) + shape, lambda i, te: (jnp.minimum(te[i], n_e - 1), 0, 0))

    return pl.pallas_call(
        _experts_kernel,
        grid_spec=pltpu.PrefetchScalarGridSpec(
            num_scalar_prefetch=1,
            grid=(n_sorted // tm,),
            in_specs=[pl.BlockSpec((tm, d), lambda i, te: (i, 0)),
                      weight((d, dff)), weight((d, dff)), weight((dff, d))],
            out_specs=pl.BlockSpec((tm, d), lambda i, te: (i, 0)),
        ),
        out_shape=jax.ShapeDtypeStruct((n_sorted, d), F32),
        compiler_params=_cparams(("arbitrary",)),
        name="experts",
    )(tile_expert, x_sorted, wg, wu, wd)


def _combine_kernel(pos_ref, x_ref, meta_ref, g2_ref, post_ref, y_ref, o_ref, buf_ref, sems):
    step = pl.program_id(0)
    n_steps = pl.num_programs(0)
    n_tok = pos_ref.shape[0] // 2
    d_model = x_ref.shape[-1]

    def copies(tile):
        def make_copy(g, j, slot):
            p = pos_ref[slot * n_tok + tile * ROW + g * SUBLANES + j]
            return pltpu.make_async_copy(y_ref.at[pl.ds(p, 1)], buf_ref.at[tile % 2, slot, g, pl.ds(j, 1)],
                                         sems.at[tile % 2])
        return make_copy

    @pl.when(step == 0)
    def _():
        _start_rows(copies(step))

    @pl.when(step + 1 < n_steps)
    def _():
        _start_rows(copies(step + 1))

    _wait_rows(copies(step))
    meta = meta_ref[...]
    cur = step % 2
    y1 = buf_ref[cur, 0].reshape(ROW, d_model)
    y2 = buf_ref[cur, 1].reshape(ROW, d_model)
    z = meta[:, 2:3] * y1 + meta[:, 3:4] * y2
    o_ref[...] = x_ref[...] + g2_ref[...] * _rms(z, post_ref[...])


def _combine(pos, xs, meta, y_sorted, mod, layer, nw, post_g):
    n_batch, ts, d = xs.shape
    nt = ts // ROW
    m_rows = n_batch * ts
    base = (layer * 6 + 5) * nw
    tile = pl.BlockSpec((ROW, d), lambda i, pos: (i, 0))
    out = pl.pallas_call(
        _combine_kernel,
        grid_spec=pltpu.PrefetchScalarGridSpec(
            num_scalar_prefetch=1,
            grid=(m_rows // ROW,),
            in_specs=[
                tile,
                pl.BlockSpec((ROW, LANES), lambda i, pos: (i, 0)),
                pl.BlockSpec((None, 1, d), lambda i, pos: (base + jnp.where(i % nt == 0, n_batch, i // nt), 0, 0)),
                pl.BlockSpec((1, d), lambda i, pos: (0, 0)),
                pl.BlockSpec(memory_space=pl.ANY),
            ],
            out_specs=tile,
            scratch_shapes=[pltpu.VMEM((2, 2, GROUPS, SUBLANES, d), F32), pltpu.SemaphoreType.DMA((2,))],
        ),
        out_shape=jax.ShapeDtypeStruct((m_rows, d), F32),
        compiler_params=_cparams(("arbitrary",)),
        name="combine",
    )(pos, xs.reshape(m_rows, d), meta.reshape(m_rows, LANES), mod, post_g, y_sorted)
    return out.reshape(n_batch, ts, d)


def _moe(xs, mod, layer, nw, pre_g, post_g, w_r, b_r, wg, wu, wd, tm=512):
    n_batch, ts, d = xs.shape
    m_rows = n_batch * ts
    xn, meta, cnt = _route(xs, mod, layer, nw, pre_g, w_r, b_r)
    m2 = meta.reshape(m_rows, LANES)
    i1, i2 = m2[:, 0].astype(jnp.int32), m2[:, 1].astype(jnp.int32)
    r1, r2 = m2[:, 4].astype(jnp.int32), m2[:, 5].astype(jnp.int32)
    counts = cnt[0, :N_EXPERTS].astype(jnp.int32)
    padded = -(-counts // tm) * tm
    ends = jnp.cumsum(padded)
    start = ends - padded
    pos = jnp.concatenate([start[i1] + r1, start[i2] + r2])
    n_tiles = 2 * m_rows // tm + N_EXPERTS
    tile_row = jnp.arange(n_tiles, dtype=jnp.int32) * tm
    tile_expert = jnp.sum((ends[None, :] <= tile_row[:, None]).astype(jnp.int32), axis=1)
    x_sorted = _dispatch(pos, ends, xn.reshape(m_rows, d), n_tiles * tm, tm)
    y_sorted = _experts(tile_expert, x_sorted, wg, wu, wd, tm)
    return _combine(pos, xs, meta, y_sorted, mod, layer, nw, post_g)


def _rope_tables(n_tok, n_ctx):
    n_freq = HEAD_DIM // 4
    pos = jnp.arange(n_tok)
    row = (pos // GRID_W).astype(F32)
    colp = (pos % GRID_W).astype(F32)
    inv = ROPE_THETA ** (-jnp.arange(n_freq, dtype=F32) / n_freq)
    lane = jnp.arange(LANES)
    in_head = lane % HEAD_DIM
    use_col = (in_head // (HEAD_DIM // 2)) == 1
    freq = inv[in_head % n_freq]
    ang = jnp.where(use_col[None, :], colp[:, None], row[:, None]) * freq[None, :]
    lower = (in_head % (HEAD_DIM // 2)) < n_freq
    cos_t = jnp.cos(ang)
    sin_t = jnp.sin(ang)
    sin_a = jnp.where(lower[None, :], -sin_t, 0.0)
    sin_b = jnp.where(lower[None, :], 0.0, sin_t)
    pad = lambda t, v: jnp.concatenate([jnp.full((n_ctx, LANES), v, F32), t], axis=0)
    return pad(cos_t, 1.0), pad(sin_a, 0.0), pad(sin_b, 0.0)


def _block_diag_mean(width):
    idx = jnp.arange(width) // HEAD_DIM
    return jnp.where(idx[:, None] == idx[None, :], 1.0 / HEAD_DIM, 0.0).astype(BF16)


def _pack_w_in(w):
    a_q = A_HEADS * HEAD_DIM
    a_kv = 2 * A_KV_HEADS * HEAD_DIM
    n_gates = 4 * C_HEADS
    gate_start = w.shape[1] - 3 * w.shape[0]
    g_start = gate_start - n_gates
    main = jnp.concatenate([w[:, gate_start:], w[:, :a_q], w[:, a_q + a_kv:g_start], w[:, a_q:a_q + a_kv]], axis=1)
    gates = jnp.pad(w[:, g_start:gate_start], ((0, 0), (0, LANES - n_gates)))
    return main.astype(BF16), gates.astype(BF16)


def kernel(x, c, ctx, c_ctx, ada_w, ada_b, pre_mix_g, post_mix_g, pre_ffn_g, post_ffn_g, w_in, q_norm_g, k_norm_g, lam_q1, lam_k1, lam_q2, lam_k2, diff_norm_g, conv_w, conv_b, mlstm_gate_b, mlstm_norm_g, w_br_attn, w_br_diff, w_br_mlstm, w_out, w_ff_gate, w_ff_up, w_ff_down, w_router, b_router, w_moe_gate, w_moe_up, w_moe_down):
    n_batch, n_tok, d = x.shape
    n_ctx = ctx.shape[1]
    depth = ada_w.shape[0]
    assert n_ctx == ROW and n_tok % ROW == 0 and d == 1024
    ts = n_ctx + n_tok
    nw = -(-(n_batch + 1) // SUBLANES) * SUBLANES

    c_all = jnp.concatenate([c, c_ctx[None, :], jnp.zeros((nw - n_batch - 1, d), F32)], axis=0)
    mod = _modulation(c_all, ada_w, ada_b)
    tables = _rope_tables(n_tok, n_ctx)
    bd4, bd1 = _block_diag_mean(A_HEADS * HEAD_DIM), _block_diag_mean(LANES)
    t_idx = jnp.arange(MCHUNK)
    tri = (t_idx[:, None] >= t_idx[None, :]).astype(F32)
    trit = tri.T

    xs = jnp.concatenate([ctx, x], axis=1)
    for l in range(depth):
        lam_init = 0.8 - 0.6 * math.exp(-0.3 * l)
        w_main, w_gate = _pack_w_in(w_in[l])
        h, gates = _inproj(xs, mod, l, nw, pre_mix_g[l][None, :], w_main, w_gate)
        qg = jnp.tile(q_norm_g[l], A_HEADS)[None, :]
        kg = jnp.tile(k_norm_g[l], A_KV_HEADS)[None, :]
        gate_b = jnp.pad(mlstm_gate_b[l], (0, LANES - 4 * C_HEADS))[None, :]
        qaz, ka, va, qbz, kb, vb, qm, km, gl, glt = _prep(
            h, gates, tables, qg, kg, bd4, bd1, conv_w[l], conv_b[l][None, :], gate_b, d)
        a_out = _gqa(qaz, ka, va)
        lam_vecs = jnp.stack([lam_q1[l], lam_k1[l], lam_q2[l], lam_k2[l]], axis=0)
        d_out = _diff(qbz, kb, vb, lam_vecs, diff_norm_g[l][None, :], lam_init)
        hf, hb = _mlstm(qm, km, h, gl, glt, tri, trit, d)
        xs = _mix(a_out, d_out, hf, hb, h, xs, mod, l, nw, post_mix_g[l][None, :], mlstm_norm_g[l][None, :],
                  w_br_attn[l].astype(BF16), w_br_diff[l].astype(BF16), w_br_mlstm[l].astype(BF16),
                  w_out[l].astype(BF16))
        j = l // 2
        if l % 2 == 0:
            xs = _ffn(xs, mod, l, nw, pre_ffn_g[l][None, :], post_ffn_g[l][None, :],
                      w_ff_gate[j].astype(BF16), w_ff_up[j].astype(BF16), w_ff_down[j].astype(BF16))
        else:
            w_r = jnp.pad(w_router[j], ((0, 0), (0, LANES - N_EXPERTS)))
            b_r = jnp.pad(b_router[j], (0, LANES - N_EXPERTS))[None, :]
            xs = _moe(xs, mod, l, nw, pre_ffn_g[l][None, :], post_ffn_g[l][None, :], w_r, b_r,
                      w_moe_gate[j].astype(BF16), w_moe_up[j].astype(BF16), w_moe_down[j].astype(BF16))
    return xs[:, n_ctx:, :]
```

```python
import functools
import math

import jax
import jax.numpy as jnp
from jax import lax
from jax.experimental import pallas as pl
from jax.experimental.pallas import tpu as pltpu

F32 = jnp.float32
BF16 = jnp.bfloat16
HIGHEST = lax.Precision.HIGHEST

EPS = 1e-6
HEAD_DIM = 64
A_HEADS = 8
A_KV_HEADS = 2
B_HEADS = 4
C_HEADS = 4
C_DIM = 128
N_EXPERTS = 8
ROPE_THETA = 10000.0
GRID_W = 64
CONV_W = 3

LANES = 128
SUBLANES = 8
ROW = 256
MCHUNK = 128
LOG2E = math.log2(math.e)
NEG = -1e30
VMEM_LIMIT = 56 * 1024 * 1024

OFF_GATE = 0
REL_QA, REL_QB, REL_KB, REL_VB = 0, 512, 1024, 1536
REL_QC, REL_KC, REL_VC, REL_OC = 2048, 2560, 3072, 3584
REL_KA = 4096
REL_END = 4352


def _cparams(sem):
    return pltpu.CompilerParams(dimension_semantics=sem, vmem_limit_bytes=VMEM_LIMIT)


def _rms(x, g):
    y = x * lax.rsqrt(jnp.mean(x * x, axis=-1, keepdims=True) + EPS)
    return y * g


def _sigmoid(x):
    return 1.0 / (1.0 + jnp.exp(-x))


def _silu(x):
    return x * _sigmoid(x)


def _log_sigmoid(x):
    return jnp.minimum(x, 0.0) - jnp.log(1.0 + jnp.exp(-jnp.abs(x)))


def _lane_iota(shape):
    return lax.broadcasted_iota(jnp.int32, shape, len(shape) - 1)


def _row_iota(shape):
    return lax.broadcasted_iota(jnp.int32, shape, len(shape) - 2)


def _mod_kernel(c_ref, w_ref, b_ref, o_ref):
    c = c_ref[...]
    o_ref[...] = jnp.dot(_silu(c), w_ref[...], preferred_element_type=F32, precision=HIGHEST) + b_ref[...]


def _modulation(c_all, ada_w, ada_b):
    depth, d, _ = ada_w.shape
    nw = c_all.shape[0]
    out = pl.pallas_call(
        _mod_kernel,
        grid=(depth, 6),
        in_specs=[
            pl.BlockSpec((nw, d), lambda l, j: (0, 0)),
            pl.BlockSpec((None, d, d), lambda l, j: (l, 0, j)),
            pl.BlockSpec((None, 1, d), lambda l, j: (l, 0, j)),
        ],
        out_specs=pl.BlockSpec((None, None, nw, d), lambda l, j: (l, j, 0, 0)),
        out_shape=jax.ShapeDtypeStruct((depth, 6, nw, d), F32),
        compiler_params=_cparams(("arbitrary", "arbitrary")),
        name="modulation",
    )(c_all, ada_w, ada_b.reshape(depth, 1, 6 * d))
    return out.reshape(depth * 6 * nw, 1, d)


def _mod_spec(d, layer, chunk, nw, n_batch):
    base = (layer * 6 + chunk) * nw
    return pl.BlockSpec((None, 1, d), lambda b, i: (base + jnp.where(i == 0, n_batch, b), 0, 0))


def _inproj_kernel(x_ref, sh_ref, sc_ref, g_ref, w_ref, wg_ref, h_ref, gate_ref, *, chunks):
    xn = _rms(x_ref[...], g_ref[...]) * (1.0 + sc_ref[...]) + sh_ref[...]
    xb = xn.astype(BF16)
    for start, width in chunks:
        h_ref[:, start:start + width] = jnp.dot(
            xb, w_ref[:, start:start + width], preferred_element_type=F32).astype(BF16)
    gate_ref[...] = jnp.dot(xb, wg_ref[...], preferred_element_type=F32)


def _inproj(xs, mod, layer, nw, pre_g, w_main, w_gate):
    n_batch, ts, d = xs.shape
    n_main = w_main.shape[1]
    nt = ts // ROW
    chunks = [(s, min(1024, n_main - s)) for s in range(0, n_main, 1024)]
    return pl.pallas_call(
        functools.partial(_inproj_kernel, chunks=chunks),
        grid=(n_batch, nt),
        in_specs=[
            pl.BlockSpec((None, ROW, d), lambda b, i: (b, i, 0)),
            _mod_spec(d, layer, 0, nw, n_batch),
            _mod_spec(d, layer, 1, nw, n_batch),
            pl.BlockSpec((1, d), lambda b, i: (0, 0)),
            pl.BlockSpec((d, n_main), lambda b, i: (0, 0), pipeline_mode=pl.Buffered(1)),
            pl.BlockSpec((d, LANES), lambda b, i: (0, 0)),
        ],
        out_specs=[
            pl.BlockSpec((None, ROW, n_main), lambda b, i: (b, i, 0)),
            pl.BlockSpec((None, ROW, LANES), lambda b, i: (b, i, 0)),
        ],
        out_shape=[
            jax.ShapeDtypeStruct((n_batch, ts, n_main), BF16),
            jax.ShapeDtypeStruct((n_batch, ts, LANES), F32),
        ],
        compiler_params=_cparams(("parallel", "parallel")),
        name="inproj",
    )(xs, mod, mod, pre_g, w_main, w_gate)


def _head_mean_sq(x, bd_ref):
    sq = x * x
    hi = sq.astype(BF16)
    lo = (sq - hi.astype(F32)).astype(BF16)
    bd = bd_ref[...]
    return jnp.dot(hi, bd, preferred_element_type=F32) + jnp.dot(lo, bd, preferred_element_type=F32)


def _prep_kernel(qa_ref, kava_ref, qb_ref, kb_ref, vb_ref, qkc_ref, prev_ref, next_ref, vc_ref, g_ref,
                 cos_ref, sa_ref, sb_ref, qg_ref, kg_ref, bd4_ref, bd1_ref, cw_ref, cb_ref, gb_ref, tri_ref,
                 qaz_ref, ka_ref, va_ref, qbz_ref, kbo_ref, vbo_ref, qmt_ref, km_ref, vmt_ref, gcol_ref, grow_ref):
    i = pl.program_id(1)
    nt = pl.num_programs(1)
    cos, sin_a, sin_b = cos_ref[...], sa_ref[...], sb_ref[...]

    def rope(x):
        width = x.shape[1]
        reps = width // LANES
        c = jnp.concatenate([cos] * reps, axis=1) if reps > 1 else cos
        a = jnp.concatenate([sin_a] * reps, axis=1) if reps > 1 else sin_a
        b = jnp.concatenate([sin_b] * reps, axis=1) if reps > 1 else sin_b
        return x * c + pltpu.roll(x, width - 16, 1) * a + pltpu.roll(x, 16, 1) * b

    lane = _lane_iota((ROW, LANES))
    ones = jnp.ones((ROW, LANES), BF16)
    scale = HEAD_DIM ** -0.5 * LOG2E

    qa = qa_ref[...].astype(F32)
    qa = qa * lax.rsqrt(_head_mean_sq(qa, bd4_ref) + EPS) * qg_ref[...]
    qa = rope(qa) * scale
    heads_per_kv = A_HEADS // A_KV_HEADS
    for h in range(A_HEADS):
        g = h // heads_per_kv
        blk = qa[:, (h // 2) * LANES:(h // 2 + 1) * LANES]
        if h % 2 != g:
            blk = pltpu.roll(blk, HEAD_DIM, 1)
        qaz_ref[h] = jnp.where(lane // HEAD_DIM == g, blk, 0.0).astype(BF16)
    kava = kava_ref[...].astype(F32)
    ka = kava[:, :LANES]
    ka = ka * lax.rsqrt(_head_mean_sq(ka, bd1_ref) + EPS) * kg_ref[...]
    ka_ref[...] = rope(ka).astype(BF16)
    va = kava_ref[:, LANES:]
    for g in range(A_KV_HEADS):
        va_ref[g] = jnp.where(lane // HEAD_DIM == g, va, ones)

    qb = rope(qb_ref[...].astype(F32)) * scale
    kb = rope(kb_ref[...].astype(F32))
    for h in range(B_HEADS):
        blk = qb[:, h * LANES:(h + 1) * LANES]
        for m in range(2):
            qbz_ref[2 * h + m] = jnp.where(lane // HEAD_DIM == m, blk, 0.0).astype(BF16)
        kbo_ref[h] = kb[:, h * LANES:(h + 1) * LANES].astype(BF16)
        vbo_ref[h, :, :LANES] = vb_ref[:, h * LANES:(h + 1) * LANES]
        vbo_ref[h, :, LANES:] = ones

    cur = qkc_ref[...].astype(F32)
    row = _row_iota(cur.shape)
    prev_row = jnp.where(i >= 2, prev_ref[SUBLANES - 1:SUBLANES, :].astype(F32), 0.0)
    next_ok = jnp.logical_and(i >= 1, i < nt - 1)
    next_row = jnp.where(next_ok, next_ref[0:1, :].astype(F32), 0.0)
    up = jnp.where(row == 0, prev_row, pltpu.roll(cur, 1, 0))
    dn = jnp.where(row == ROW - 1, next_row, pltpu.roll(cur, ROW - 1, 0))
    y = up * cw_ref[0:1, :] + cur * cw_ref[1:2, :] + dn * cw_ref[2:3, :] + cb_ref[...]
    y = _silu(y)
    half = C_HEADS * C_DIM
    qmt_ref[...] = y[:, :half].T.astype(BF16)
    km_ref[...] = (y[:, half:] * (C_DIM ** -0.5)).astype(BF16)
    vmt_ref[...] = vc_ref[...].astype(F32).T.astype(BF16)

    gg = g_ref[...] + gb_ref[...]
    is_forget = (lane // C_HEADS) % 2 == 1
    gl = jnp.where(is_forget, _log_sigmoid(gg), gg) * LOG2E
    tri = tri_ref[...]
    ones_sq = jnp.ones((MCHUNK, MCHUNK), F32)
    n_rows = 2 * SUBLANES
    for c in range(ROW // MCHUNK):
        rows = slice(c * MCHUNK, (c + 1) * MCHUNK)
        glc = gl[rows]
        cs = jnp.dot(tri, glc, preferred_element_type=F32, precision=HIGHEST)
        tot = jnp.dot(ones_sq, glc, preferred_element_type=F32, precision=HIGHEST)
        for f, val in enumerate((glc, cs, tot)):
            gcol_ref[rows, f * LANES:(f + 1) * LANES] = val
            grow_ref[f * n_rows:(f + 1) * n_rows, rows] = val.T[:n_rows, :]


def _prep(h, gates, tables, qg, kg, bd4, bd1, conv_w, conv_b, gate_b, tri, d):
    n_batch, ts, _ = h.shape
    nt = ts // ROW
    base = 3 * d

    def col(rel, width):
        return (base + rel) // width

    cos_t, sa_t, sb_t = tables
    row_blocks = ts // SUBLANES
    per_tile = ROW // SUBLANES
    cqk = col(REL_QC, 1024)

    def const(shape):
        return pl.BlockSpec(shape, lambda b, i: (0,) * len(shape))

    in_specs = [
        pl.BlockSpec((None, ROW, 512), lambda b, i: (b, i, col(REL_QA, 512))),
        pl.BlockSpec((None, ROW, 256), lambda b, i: (b, i, col(REL_KA, 256))),
        pl.BlockSpec((None, ROW, 512), lambda b, i: (b, i, col(REL_QB, 512))),
        pl.BlockSpec((None, ROW, 512), lambda b, i: (b, i, col(REL_KB, 512))),
        pl.BlockSpec((None, ROW, 512), lambda b, i: (b, i, col(REL_VB, 512))),
        pl.BlockSpec((None, ROW, 1024), lambda b, i: (b, i, cqk)),
        pl.BlockSpec((None, SUBLANES, 1024), lambda b, i: (b, jnp.maximum(i * per_tile - 1, 0), cqk)),
        pl.BlockSpec((None, SUBLANES, 1024),
                     lambda b, i: (b, jnp.minimum((i + 1) * per_tile, row_blocks - 1), cqk)),
        pl.BlockSpec((None, ROW, 512), lambda b, i: (b, i, col(REL_VC, 512))),
        pl.BlockSpec((None, ROW, LANES), lambda b, i: (b, i, 0)),
        pl.BlockSpec((ROW, LANES), lambda b, i: (i, 0)),
        pl.BlockSpec((ROW, LANES), lambda b, i: (i, 0)),
        pl.BlockSpec((ROW, LANES), lambda b, i: (i, 0)),
        const((1, 512)), const((1, LANES)), const((512, 512)), const((LANES, LANES)),
        const((CONV_W, 1024)), const((1, 1024)), const((1, LANES)), const((MCHUNK, MCHUNK)),
    ]
    out_specs = [
        pl.BlockSpec((None, A_HEADS, ROW, LANES), lambda b, i: (b, 0, i, 0)),
        pl.BlockSpec((None, ROW, LANES), lambda b, i: (b, i, 0)),
        pl.BlockSpec((None, A_KV_HEADS, ROW, LANES), lambda b, i: (b, 0, i, 0)),
        pl.BlockSpec((None, 2 * B_HEADS, ROW, LANES), lambda b, i: (b, 0, i, 0)),
        pl.BlockSpec((None, B_HEADS, ROW, LANES), lambda b, i: (b, 0, i, 0)),
        pl.BlockSpec((None, B_HEADS, ROW, 2 * LANES), lambda b, i: (b, 0, i, 0)),
        pl.BlockSpec((None, 512, ROW), lambda b, i: (b, 0, i)),
        pl.BlockSpec((None, ROW, 512), lambda b, i: (b, i, 0)),
        pl.BlockSpec((None, 512, ROW), lambda b, i: (b, 0, i)),
        pl.BlockSpec((None, ROW, 3 * LANES), lambda b, i: (b, i, 0)),
        pl.BlockSpec((None, 6 * SUBLANES, ROW), lambda b, i: (b, 0, i)),
    ]
    out_shape = [
        jax.ShapeDtypeStruct((n_batch, A_HEADS, ts, LANES), BF16),
        jax.ShapeDtypeStruct((n_batch, ts, LANES), BF16),
        jax.ShapeDtypeStruct((n_batch, A_KV_HEADS, ts, LANES), BF16),
        jax.ShapeDtypeStruct((n_batch, 2 * B_HEADS, ts, LANES), BF16),
        jax.ShapeDtypeStruct((n_batch, B_HEADS, ts, LANES), BF16),
        jax.ShapeDtypeStruct((n_batch, B_HEADS, ts, 2 * LANES), BF16),
        jax.ShapeDtypeStruct((n_batch, 512, ts), BF16),
        jax.ShapeDtypeStruct((n_batch, ts, 512), BF16),
        jax.ShapeDtypeStruct((n_batch, 512, ts), BF16),
        jax.ShapeDtypeStruct((n_batch, ts, 3 * LANES), F32),
        jax.ShapeDtypeStruct((n_batch, 6 * SUBLANES, ts), F32),
    ]
    return pl.pallas_call(
        _prep_kernel,
        grid=(n_batch, nt),
        in_specs=in_specs,
        out_specs=out_specs,
        out_shape=out_shape,
        compiler_params=_cparams(("parallel", "parallel")),
        name="prep",
    )(h, h, h, h, h, h, h, h, h, gates, cos_t, sa_t, sb_t, qg, kg, bd4, bd1, conv_w, conv_b, gate_b, tri)


def _attend_blocks(blocks, n_keys):
    def scores(q, k_ref):
        return lax.dot_general(q, k_ref[:n_keys, :], (((1,), (1,)), ((), ())), preferred_element_type=F32)

    def weighted(s, v_ref):
        p = jnp.exp2(s - jnp.max(s, axis=-1, keepdims=True)).astype(BF16)
        return jnp.dot(p, v_ref[:n_keys, :], preferred_element_type=F32)

    outs = []
    s_cur = scores(blocks[0][0], blocks[0][1])
    for j in range(1, len(blocks)):
        s_next = scores(blocks[j][0], blocks[j][1])
        outs.append(weighted(s_cur, blocks[j - 1][2]))
        s_cur = s_next
    outs.append(weighted(s_cur, blocks[-1][2]))
    return outs


def _per_tile_keys(body, n_all):
    @pl.when(pl.program_id(1) == 0)
    def _():
        body(ROW)

    @pl.when(pl.program_id(1) > 0)
    def _():
        body(n_all)


def _gqa_kernel(q_ref, k_ref, v_ref, o_ref):
    heads_per_kv = A_HEADS // A_KV_HEADS
    lane = _lane_iota((ROW, LANES))

    def body(n_keys):
        blocks = []
        for j in range(A_HEADS // 2):
            q = q_ref[2 * j:2 * j + 2].reshape(2 * ROW, LANES)
            blocks.append((q, k_ref, v_ref.at[(2 * j) // heads_per_kv]))
        for j, o in enumerate(_attend_blocks(blocks, n_keys)):
            g = (2 * j) // heads_per_kv
            den_lane = (1 - g) * HEAD_DIM
            o = o / o[:, den_lane:den_lane + 1]
            even, odd = o[:ROW], o[ROW:]
            even = even if g == 0 else pltpu.roll(even, HEAD_DIM, 1)
            odd = odd if g == 1 else pltpu.roll(odd, HEAD_DIM, 1)
            o_ref[:, j * LANES:(j + 1) * LANES] = jnp.where(lane < HEAD_DIM, even, odd).astype(BF16)

    _per_tile_keys(body, k_ref.shape[0])


def _diff_kernel(q_ref, k_ref, v_ref, lam_ref, g_ref, o_ref, *, lam_init):
    lv = lam_ref[...]
    lam = (jnp.exp(jnp.sum(lv[0:1] * lv[1:2], axis=-1, keepdims=True))
           - jnp.exp(jnp.sum(lv[2:3] * lv[3:4], axis=-1, keepdims=True)) + lam_init)

    def body(n_keys):
        blocks = [(q_ref[2 * h:2 * h + 2].reshape(2 * ROW, LANES), k_ref.at[h], v_ref.at[h]) for h in range(B_HEADS)]
        for h, o in enumerate(_attend_blocks(blocks, n_keys)):
            o = o[:, :LANES] / o[:, LANES:LANES + 1]
            dif = o[:ROW] - lam * o[ROW:]
            o_ref[:, h * LANES:(h + 1) * LANES] = (_rms(dif, g_ref[...]) * (1.0 - lam_init)).astype(BF16)

    _per_tile_keys(body, k_ref.shape[1])


def _gqa(qaz, ka, va):
    n_batch, _, ts, _ = qaz.shape
    nt = ts // ROW
    return pl.pallas_call(
        _gqa_kernel,
        grid=(n_batch, nt),
        in_specs=[
            pl.BlockSpec((None, A_HEADS, ROW, LANES), lambda b, i: (b, 0, i, 0)),
            pl.BlockSpec((None, ts, LANES), lambda b, i: (b, 0, 0)),
            pl.BlockSpec((None, A_KV_HEADS, ts, LANES), lambda b, i: (b, 0, 0, 0)),
        ],
        out_specs=pl.BlockSpec((None, ROW, A_HEADS * HEAD_DIM), lambda b, i: (b, i, 0)),
        out_shape=jax.ShapeDtypeStruct((n_batch, ts, A_HEADS * HEAD_DIM), BF16),
        compiler_params=_cparams(("parallel", "parallel")),
        name="gqa_attention",
    )(qaz, ka, va)


def _diff(qbz, kb, vb, lam_vecs, sub_g, lam_init):
    n_batch, _, ts, _ = qbz.shape
    nt = ts // ROW
    return pl.pallas_call(
        functools.partial(_diff_kernel, lam_init=lam_init),
        grid=(n_batch, nt),
        in_specs=[
            pl.BlockSpec((None, 2 * B_HEADS, ROW, LANES), lambda b, i: (b, 0, i, 0)),
            pl.BlockSpec((None, B_HEADS, ts, LANES), lambda b, i: (b, 0, 0, 0)),
            pl.BlockSpec((None, B_HEADS, ts, 2 * LANES), lambda b, i: (b, 0, 0, 0)),
            pl.BlockSpec((4, HEAD_DIM), lambda b, i: (0, 0)),
            pl.BlockSpec((1, LANES), lambda b, i: (0, 0)),
        ],
        out_specs=pl.BlockSpec((None, ROW, B_HEADS * LANES), lambda b, i: (b, i, 0)),
        out_shape=jax.ShapeDtypeStruct((n_batch, ts, B_HEADS * LANES), BF16),
        compiler_params=_cparams(("parallel", "parallel")),
        name="diff_attention",
    )(qbz, kb, vb, lam_vecs, sub_g)


AUG = 2 * SUBLANES


def _mlstm_kernel(qtf_ref, kf_ref, vtf_ref, gcf_ref, grf_ref, qtb_ref, kb_ref, vtb_ref, gcb_ref, grb_ref,
                  hf_ref, hb_ref, c_ref, m_ref):
    @pl.when(pl.program_id(1) == 0)
    def _():
        c_ref[...] = jnp.zeros_like(c_ref)
        m_ref[...] = jnp.zeros_like(m_ref)

    length = MCHUNK
    key_idx = _row_iota((length, length))
    qry_idx = _lane_iota((length, length))
    ones_rows = jnp.ones((AUG, length), BF16)
    n_rows = 2 * SUBLANES
    n_chain = 2 * C_HEADS
    c_states = [c_ref[ch] for ch in range(n_chain)]
    m_prevs = [m_ref[ch] for ch in range(n_chain)]
    c_news, m_news, h_outs, pending = [], [], [], []

    for direction, (qt_ref, k_ref, vt_ref, gc_ref, gr_ref, h_ref) in enumerate(
            ((qtf_ref, kf_ref, vtf_ref, gcf_ref, grf_ref, hf_ref), (qtb_ref, kb_ref, vtb_ref, gcb_ref, grb_ref, hb_ref))):
        reverse = direction == 1
        gate_c, cs_c, tot_c = gc_ref[:, :LANES], gc_ref[:, LANES:2 * LANES], gc_ref[:, 2 * LANES:]
        p_c = (tot_c - cs_c + gate_c) if reverse else cs_c
        g_c = gate_c - pltpu.roll(p_c, LANES - C_HEADS, 1)
        mask = (key_idx >= qry_idx) if reverse else (key_idx <= qry_idx)
        for hd in range(C_HEADS):
            chain = direction * C_HEADS + hd
            ii = direction * 2 * C_HEADS + hd
            fi = ii + C_HEADS
            i_row = gr_ref[ii:ii + 1, :]
            f_row = gr_ref[fi:fi + 1, :]
            cs_row = gr_ref[n_rows + fi:n_rows + fi + 1, :]
            tot_row = gr_ref[2 * n_rows + fi:2 * n_rows + fi + 1, :]
            p_row = (tot_row - cs_row + f_row) if reverse else cs_row
            m_prev = m_prevs[chain]
            inter = p_row + m_prev
            log_dt = jnp.where(mask, jnp.broadcast_to(g_c[:, ii:ii + 1], (length, length)) + p_row, NEG)
            m_t = jnp.maximum(inter, jnp.max(log_dt, axis=0, keepdims=True))
            d_t = jnp.exp2(log_dt - m_t)
            a_row = jnp.exp2(inter - m_t)
            sl = slice(hd * C_DIM, (hd + 1) * C_DIM)
            k_h, qt_h = k_ref[:, sl], qt_ref[sl, :]
            vt_aug = jnp.concatenate([vt_ref[sl, :], ones_rows], axis=0)
            s_raw = jnp.dot(k_h, qt_h, preferred_element_type=F32)
            c_state = c_states[chain]
            x_t = jnp.dot(c_state.astype(BF16), qt_h, preferred_element_type=F32)
            w_row = tot_row - p_row + i_row
            m_new = jnp.maximum(tot_row + m_prev, jnp.max(w_row, axis=-1, keepdims=True))
            decay = jnp.exp2(tot_row + m_prev - m_new)
            ws = jnp.exp2(w_row - m_new)
            update = jnp.dot((vt_aug.astype(F32) * ws).astype(BF16), k_h, preferred_element_type=F32)
            c_news.append(decay * c_state + update)
            m_news.append(m_new)
            pending.append((s_raw, d_t, vt_aug, a_row, x_t, m_t))

    for direction in range(2):
        h_parts = []
        for hd in range(C_HEADS):
            s_raw, d_t, vt_aug, a_row, x_t, m_t = pending[direction * C_HEADS + hd]
            y_t = jnp.dot(vt_aug, (s_raw * d_t).astype(BF16), preferred_element_type=F32)
            num_t = a_row * x_t[:C_DIM] + y_t[:C_DIM]
            den = a_row * x_t[C_DIM:C_DIM + 1] + y_t[C_DIM:C_DIM + 1]
            h_t = num_t / jnp.maximum(jnp.abs(den), jnp.exp2(-m_t))
            h_parts.append(h_t.T)
        h_outs.append(jnp.concatenate(h_parts, axis=1))

    hf_ref[...] = h_outs[0]
    hb_ref[...] = h_outs[1]
    for ch in range(n_chain):
        c_ref[ch] = c_news[ch]
        m_ref[ch] = m_news[ch]


def _mlstm(qmt, km, vmt, gcol, grow):
    n_batch, ts, width = km.shape
    nc = ts // MCHUNK
    ctx_chunks = ROW // MCHUNK

    def bwd(j):
        return jnp.where(j < ctx_chunks, ctx_chunks - 1 - j, nc + ctx_chunks - 1 - j)

    def specs(idx):
        return [
            pl.BlockSpec((None, width, MCHUNK), lambda b, j: (b, 0, idx(j))),
            pl.BlockSpec((None, MCHUNK, width), lambda b, j: (b, idx(j), 0)),
            pl.BlockSpec((None, width, MCHUNK), lambda b, j: (b, 0, idx(j))),
            pl.BlockSpec((None, MCHUNK, 3 * LANES), lambda b, j: (b, idx(j), 0)),
            pl.BlockSpec((None, 6 * SUBLANES, MCHUNK), lambda b, j: (b, 0, idx(j))),
        ]

    fwd = lambda j: j
    n_chain = 2 * C_HEADS
    return pl.pallas_call(
        _mlstm_kernel,
        grid=(n_batch, nc),
        in_specs=specs(fwd) + specs(bwd),
        out_specs=[
            pl.BlockSpec((None, MCHUNK, width), lambda b, j: (b, j, 0)),
            pl.BlockSpec((None, MCHUNK, width), lambda b, j: (b, bwd(j), 0)),
        ],
        out_shape=[jax.ShapeDtypeStruct((n_batch, ts, width), F32)] * 2,
        scratch_shapes=[
            pltpu.VMEM((n_chain, C_DIM + AUG, C_DIM), F32),
            pltpu.VMEM((n_chain, 1, LANES), F32),
        ],
        compiler_params=_cparams(("parallel", "arbitrary")),
        name="mlstm",
    )(qmt, km, vmt, gcol, grow, qmt, km, vmt, gcol, grow)


def _mix_kernel(a_ref, d_ref, hf_ref, hb_ref, oc_ref, gate_ref, x_ref, g1_ref, pg_ref, mg_ref,
                wa_ref, wb_ref, wc_ref, wo_ref, o_ref):
    d_model = x_ref.shape[-1]
    hsum = hf_ref[...] + hb_ref[...]
    mg = mg_ref[...]
    m = jnp.concatenate([_rms(hsum[:, hd * C_DIM:(hd + 1) * C_DIM], mg) for hd in range(C_HEADS)], axis=1)
    m = m * _sigmoid(oc_ref[...].astype(F32))
    u = (_sigmoid(gate_ref[:, :d_model].astype(F32))
         * jnp.dot(a_ref[...], wa_ref[...], preferred_element_type=F32)
         + _sigmoid(gate_ref[:, d_model:2 * d_model].astype(F32))
         * jnp.dot(d_ref[...], wb_ref[...], preferred_element_type=F32)
         + _sigmoid(gate_ref[:, 2 * d_model:].astype(F32))
         * jnp.dot(m.astype(BF16), wc_ref[...], preferred_element_type=F32))
    y = jnp.dot(u.astype(BF16), wo_ref[...], preferred_element_type=F32)
    o_ref[...] = x_ref[...] + g1_ref[...] * _rms(y, pg_ref[...])


def _mix(a, dd, hf, hb, h, xs, mod, layer, nw, post_g, mlstm_g, wa, wb, wc, wo):
    n_batch, ts, d = xs.shape
    nt = ts // ROW
    width = a.shape[-1]
    ocol = (3 * d + REL_OC) // width

    def tile(w):
        return pl.BlockSpec((None, ROW, w), lambda b, i: (b, i, 0))

    def const(shape):
        return pl.BlockSpec(shape, lambda b, i: (0,) * len(shape))

    return pl.pallas_call(
        _mix_kernel,
        grid=(n_batch, nt),
        in_specs=[
            tile(width), tile(width), tile(width), tile(width),
            pl.BlockSpec((None, ROW, width), lambda b, i: (b, i, ocol)),
            pl.BlockSpec((None, ROW, 3 * d), lambda b, i: (b, i, 0)),
            tile(d),
            _mod_spec(d, layer, 2, nw, n_batch),
            const((1, d)), const((1, C_DIM)),
            const((width, d)), const((width, d)), const((width, d)), const((d, d)),
        ],
        out_specs=tile(d),
        out_shape=jax.ShapeDtypeStruct((n_batch, ts, d), F32),
        compiler_params=_cparams(("parallel", "parallel")),
        name="mix_out",
    )(a, dd, hf, hb, h, h, xs, mod, post_g, mlstm_g, wa, wb, wc, wo)


def _ffn_kernel(x_ref, sh_ref, sc_ref, g2_ref, pre_ref, post_ref, wg_ref, wu_ref, wd_ref, o_ref):
    x = x_ref[...]
    xb = (_rms(x, pre_ref[...]) * (1.0 + sc_ref[...]) + sh_ref[...]).astype(BF16)
    gate = jnp.dot(xb, wg_ref[...], preferred_element_type=F32)
    up = jnp.dot(xb, wu_ref[...], preferred_element_type=F32)
    z = jnp.dot((_silu(gate) * up).astype(BF16), wd_ref[...], preferred_element_type=F32)
    o_ref[...] = x + g2_ref[...] * _rms(z, post_ref[...])


def _ffn(xs, mod, layer, nw, pre_g, post_g, wg, wu, wd):
    n_batch, ts, d = xs.shape
    nt = ts // ROW
    dff = wg.shape[1]

    def resident(shape):
        return pl.BlockSpec(shape, lambda b, i: (0,) * len(shape), pipeline_mode=pl.Buffered(1))

    tile = pl.BlockSpec((None, ROW, d), lambda b, i: (b, i, 0))
    return pl.pallas_call(
        _ffn_kernel,
        grid=(n_batch, nt),
        in_specs=[
            tile,
            _mod_spec(d, layer, 3, nw, n_batch), _mod_spec(d, layer, 4, nw, n_batch),
            _mod_spec(d, layer, 5, nw, n_batch),
            pl.BlockSpec((1, d), lambda b, i: (0, 0)), pl.BlockSpec((1, d), lambda b, i: (0, 0)),
            resident((d, dff)), resident((d, dff)), resident((dff, d)),
        ],
        out_specs=tile,
        out_shape=jax.ShapeDtypeStruct((n_batch, ts, d), F32),
        compiler_params=_cparams(("parallel", "parallel")),
        name="ffn",
    )(xs, mod, mod, mod, pre_g, post_g, wg, wu, wd)


def _route_kernel(x_ref, sh_ref, sc_ref, pre_ref, wr_ref, br_ref, tri_ref, xn_ref, meta_ref, cnt_ref, carry_ref):
    @pl.when(jnp.logical_and(pl.program_id(0) == 0, pl.program_id(1) == 0))
    def _():
        carry_ref[...] = jnp.zeros_like(carry_ref)

    xn = _rms(x_ref[...], pre_ref[...]) * (1.0 + sc_ref[...]) + sh_ref[...]
    xn_ref[...] = xn
    lane = _lane_iota((ROW, LANES))
    logits = jnp.dot(xn, wr_ref[...], preferred_element_type=F32, precision=HIGHEST) + br_ref[...]
    logits = jnp.where(lane < N_EXPERTS, logits, NEG)
    v1 = jnp.max(logits, axis=-1, keepdims=True)
    i1 = jnp.min(jnp.where(logits == v1, lane, LANES), axis=-1, keepdims=True)
    rest = jnp.where(lane == i1, NEG, logits)
    v2 = jnp.max(rest, axis=-1, keepdims=True)
    i2 = jnp.min(jnp.where(rest == v2, lane, LANES), axis=-1, keepdims=True)
    e2 = jnp.exp(v2 - v1)
    w1 = 1.0 / (1.0 + e2)
    w2 = e2 / (1.0 + e2)
    assigned = jnp.where(lane == i1, 1.0, jnp.where(lane == i2, 1.0, 0.0))
    before = jnp.dot(tri_ref[...], assigned.astype(BF16), preferred_element_type=F32) + carry_ref[0:1, :]
    r1 = jnp.sum(jnp.where(lane == i1, before, 0.0), axis=-1, keepdims=True)
    r2 = jnp.sum(jnp.where(lane == i2, before, 0.0), axis=-1, keepdims=True)
    carry_ref[...] = carry_ref[...] + jnp.sum(assigned, axis=0, keepdims=True)
    cnt_ref[...] = carry_ref[...]
    fields = (i1.astype(F32), i2.astype(F32), w1, w2, r1, r2)
    meta = jnp.zeros((ROW, LANES), F32)
    for f, val in enumerate(fields):
        meta = jnp.where(lane == f, val, meta)
    meta_ref[...] = meta


def _route(xs, mod, layer, nw, pre_g, w_r, b_r):
    n_batch, ts, d = xs.shape
    nt = ts // ROW
    tile = pl.BlockSpec((None, ROW, d), lambda b, i: (b, i, 0))
    t_idx = jnp.arange(ROW)
    tri_strict = (t_idx[:, None] > t_idx[None, :]).astype(BF16)
    return pl.pallas_call(
        _route_kernel,
        grid=(n_batch, nt),
        in_specs=[
            tile, _mod_spec(d, layer, 3, nw, n_batch), _mod_spec(d, layer, 4, nw, n_batch),
            pl.BlockSpec((1, d), lambda b, i: (0, 0)),
            pl.BlockSpec((d, LANES), lambda b, i: (0, 0)),
            pl.BlockSpec((1, LANES), lambda b, i: (0, 0)),
            pl.BlockSpec((ROW, ROW), lambda b, i: (0, 0)),
        ],
        out_specs=[tile, pl.BlockSpec((None, ROW, LANES), lambda b, i: (b, i, 0)),
                   pl.BlockSpec((SUBLANES, LANES), lambda b, i: (0, 0))],
        out_shape=[jax.ShapeDtypeStruct((n_batch, ts, d), F32),
                   jax.ShapeDtypeStruct((n_batch, ts, LANES), F32),
                   jax.ShapeDtypeStruct((SUBLANES, LANES), F32)],
        scratch_shapes=[pltpu.VMEM((SUBLANES, LANES), F32)],
        compiler_params=_cparams(("arbitrary", "arbitrary")),
        name="route",
    )(xs, mod, mod, pre_g, w_r, b_r, tri_strict)


GROUPS = ROW // SUBLANES


def _start_rows(make_copy):
    def issue(g, carry):
        for j in range(SUBLANES):
            for slot in range(2):
                make_copy(g, j, slot).start()
        return carry

    lax.fori_loop(0, GROUPS, issue, 0)


def _wait_rows(make_copy):
    def drain(g, carry):
        for j in range(SUBLANES):
            for slot in range(2):
                make_copy(0, 0, slot).wait()
        return carry

    lax.fori_loop(0, GROUPS, drain, 0)


def _dispatch_kernel(pos_ref, ends_ref, xn_ref, out_ref, zero_ref, sem, zero_sem):
    step = pl.program_id(0)
    n_tok = pos_ref.shape[0] // 2
    tm = zero_ref.shape[0]
    n_sorted = out_ref.shape[0]

    @pl.when(step == 0)
    def _():
        zero_ref[...] = jnp.zeros_like(zero_ref)

        def fill(row):
            copy = pltpu.make_async_copy(zero_ref, out_ref.at[pl.ds(pl.multiple_of(row, tm), tm)], zero_sem)
            copy.start()
            copy.wait()

        for e in range(N_EXPERTS):
            prev_end = ends_ref[e - 1] if e > 0 else 0

            @pl.when(ends_ref[e] > prev_end)
            def _():
                fill(ends_ref[e] - tm)

        last_end = ends_ref[N_EXPERTS - 1]

        def tail(k, carry):
            fill(last_end + k * tm)
            return carry

        lax.fori_loop(0, (n_sorted - last_end) // tm, tail, 0)

    def make_copy(g, j, slot):
        p = pos_ref[slot * n_tok + step * ROW + g * SUBLANES + j]
        return pltpu.make_async_copy(xn_ref.at[g, pl.ds(j, 1)], out_ref.at[pl.ds(p, 1)], sem)

    _start_rows(make_copy)
    _wait_rows(make_copy)


def _dispatch(pos, ends, xn, n_sorted, tm):
    m_rows, d = xn.shape
    return pl.pallas_call(
        _dispatch_kernel,
        grid_spec=pltpu.PrefetchScalarGridSpec(
            num_scalar_prefetch=2,
            grid=(m_rows // ROW,),
            in_specs=[pl.BlockSpec((GROUPS, SUBLANES, d), lambda i, pos, ends: (i, 0, 0))],
            out_specs=pl.BlockSpec(memory_space=pl.ANY),
            scratch_shapes=[pltpu.VMEM((tm, d), F32), pltpu.SemaphoreType.DMA(()), pltpu.SemaphoreType.DMA(())],
        ),
        out_shape=jax.ShapeDtypeStruct((n_sorted, d), F32),
        compiler_params=_cparams(("arbitrary",)),
        name="dispatch",
    )(pos, ends, xn.reshape(m_rows // SUBLANES, SUBLANES, d))


def _experts_kernel(te_ref, x_ref, wg_ref, wu_ref, wd_ref, y_ref):
    used = te_ref[pl.program_id(0)] < N_EXPERTS

    @pl.when(used)
    def _():
        x = x_ref[...].astype(BF16)
        gate = jnp.dot(x, wg_ref[...], preferred_element_type=F32)
        up = jnp.dot(x, wu_ref[...], preferred_element_type=F32)
        y_ref[...] = jnp.dot((_silu(gate) * up).astype(BF16), wd_ref[...], preferred_element_type=F32)

    @pl.when(jnp.logical_not(used))
    def _():
        y_ref[...] = jnp.zeros_like(y_ref)


def _experts(tile_expert, x_sorted, wg, wu, wd, tm):
    n_sorted, d = x_sorted.shape
    n_e, _, dff = wg.shape

    def weight(shape):
        return pl.BlockSpec((None,) + shape, lambda i, te: (jnp.minimum(te[i], n_e - 1), 0, 0))

    return pl.pallas_call(
        _experts_kernel,
        grid_spec=pltpu.PrefetchScalarGridSpec(
            num_scalar_prefetch=1,
            grid=(n_sorted // tm,),
            in_specs=[pl.BlockSpec((tm, d), lambda i, te: (i, 0)),
                      weight((d, dff)), weight((d, dff)), weight((dff, d))],
            out_specs=pl.BlockSpec((tm, d), lambda i, te: (i, 0)),
        ),
        out_shape=jax.ShapeDtypeStruct((n_sorted, d), F32),
        compiler_params=_cparams(("arbitrary",)),
        name="experts",
    )(tile_expert, x_sorted, wg, wu, wd)


def _combine_kernel(pos_ref, x_ref, meta_ref, g2_ref, post_ref, y_ref, o_ref, buf_ref, sems, *,
                    tiles_per_sample, latent_only):
    step = pl.program_id(0)
    n_steps = pl.num_programs(0)
    n_tok = pos_ref.shape[0] // 2
    d_model = x_ref.shape[-1]

    def wanted(tile):
        return (tile % tiles_per_sample != 0) if latent_only else (tile >= 0)

    def copies(tile):
        def make_copy(g, j, slot):
            p = pos_ref[slot * n_tok + tile * ROW + g * SUBLANES + j]
            return pltpu.make_async_copy(y_ref.at[pl.ds(p, 1)], buf_ref.at[tile % 2, slot, g, pl.ds(j, 1)],
                                         sems.at[tile % 2])
        return make_copy

    @pl.when(jnp.logical_and(step == 0, wanted(step)))
    def _():
        _start_rows(copies(step))

    @pl.when(jnp.logical_and(step + 1 < n_steps, wanted(step + 1)))
    def _():
        _start_rows(copies(step + 1))

    @pl.when(wanted(step))
    def _():
        _wait_rows(copies(step))
        meta = meta_ref[...]
        cur = step % 2
        y1 = buf_ref[cur, 0].reshape(ROW, d_model)
        y2 = buf_ref[cur, 1].reshape(ROW, d_model)
        z = meta[:, 2:3] * y1 + meta[:, 3:4] * y2
        o_ref[...] = x_ref[...] + g2_ref[...] * _rms(z, post_ref[...])


def _combine(pos, xs, meta, y_sorted, mod, layer, nw, post_g, latent_only):
    n_batch, ts, d = xs.shape
    nt = ts // ROW
    m_rows = n_batch * ts
    base = (layer * 6 + 5) * nw
    tile = pl.BlockSpec((ROW, d), lambda i, pos: (i, 0))
    if latent_only:
        out_rows = n_batch * (ts - ROW)
        out_tile = pl.BlockSpec((ROW, d), lambda i, pos: ((i // nt) * (nt - 1) + jnp.maximum(i % nt - 1, 0), 0))
    else:
        out_rows, out_tile = m_rows, tile
    out = pl.pallas_call(
        functools.partial(_combine_kernel, tiles_per_sample=nt, latent_only=latent_only),
        grid_spec=pltpu.PrefetchScalarGridSpec(
            num_scalar_prefetch=1,
            grid=(m_rows // ROW,),
            in_specs=[
                tile,
                pl.BlockSpec((ROW, LANES), lambda i, pos: (i, 0)),
                pl.BlockSpec((None, 1, d), lambda i, pos: (base + jnp.where(i % nt == 0, n_batch, i // nt), 0, 0)),
                pl.BlockSpec((1, d), lambda i, pos: (0, 0)),
                pl.BlockSpec(memory_space=pl.ANY),
            ],
            out_specs=out_tile,
            scratch_shapes=[pltpu.VMEM((2, 2, GROUPS, SUBLANES, d), F32), pltpu.SemaphoreType.DMA((2,))],
        ),
        out_shape=jax.ShapeDtypeStruct((out_rows, d), F32),
        compiler_params=_cparams(("arbitrary",)),
        name="combine",
    )(pos, xs.reshape(m_rows, d), meta.reshape(m_rows, LANES), mod, post_g, y_sorted)
    return out.reshape(n_batch, out_rows // n_batch, d)


def _moe(xs, mod, layer, nw, pre_g, post_g, w_r, b_r, wg, wu, wd, latent_only, tm=512):
    n_batch, ts, d = xs.shape
    m_rows = n_batch * ts
    xn, meta, cnt = _route(xs, mod, layer, nw, pre_g, w_r, b_r)
    m2 = meta.reshape(m_rows, LANES)
    i1, i2 = m2[:, 0].astype(jnp.int32), m2[:, 1].astype(jnp.int32)
    r1, r2 = m2[:, 4].astype(jnp.int32), m2[:, 5].astype(jnp.int32)
    counts = cnt[0, :N_EXPERTS].astype(jnp.int32)
    padded = -(-counts // tm) * tm
    ends = jnp.cumsum(padded)
    start = ends - padded
    pos = jnp.concatenate([start[i1] + r1, start[i2] + r2])
    n_tiles = 2 * m_rows // tm + N_EXPERTS
    tile_row = jnp.arange(n_tiles, dtype=jnp.int32) * tm
    tile_expert = jnp.sum((ends[None, :] <= tile_row[:, None]).astype(jnp.int32), axis=1)
    x_sorted = _dispatch(pos, ends, xn.reshape(m_rows, d), n_tiles * tm, tm)
    y_sorted = _experts(tile_expert, x_sorted, wg, wu, wd, tm)
    return _combine(pos, xs, meta, y_sorted, mod, layer, nw, post_g, latent_only)


def _rope_tables(n_tok, n_ctx):
    n_freq = HEAD_DIM // 4
    pos = jnp.arange(n_tok)
    row = (pos // GRID_W).astype(F32)
    colp = (pos % GRID_W).astype(F32)
    inv = ROPE_THETA ** (-jnp.arange(n_freq, dtype=F32) / n_freq)
    lane = jnp.arange(LANES)
    in_head = lane % HEAD_DIM
    use_col = (in_head // (HEAD_DIM // 2)) == 1
    freq = inv[in_head % n_freq]
    ang = jnp.where(use_col[None, :], colp[:, None], row[:, None]) * freq[None, :]
    lower = (in_head % (HEAD_DIM // 2)) < n_freq
    cos_t = jnp.cos(ang)
    sin_t = jnp.sin(ang)
    sin_a = jnp.where(lower[None, :], -sin_t, 0.0)
    sin_b = jnp.where(lower[None, :], 0.0, sin_t)
    pad = lambda t, v: jnp.concatenate([jnp.full((n_ctx, LANES), v, F32), t], axis=0)
    return pad(cos_t, 1.0), pad(sin_a, 0.0), pad(sin_b, 0.0)


def _block_diag_mean(width):
    idx = jnp.arange(width) // HEAD_DIM
    return jnp.where(idx[:, None] == idx[None, :], 1.0 / HEAD_DIM, 0.0).astype(BF16)


def _pack_w_in(w):
    a_q = A_HEADS * HEAD_DIM
    a_kv = 2 * A_KV_HEADS * HEAD_DIM
    n_gates = 4 * C_HEADS
    gate_start = w.shape[1] - 3 * w.shape[0]
    g_start = gate_start - n_gates
    main = jnp.concatenate([w[:, gate_start:], w[:, :a_q], w[:, a_q + a_kv:g_start], w[:, a_q:a_q + a_kv]], axis=1)
    gates = jnp.pad(w[:, g_start:gate_start], ((0, 0), (0, LANES - n_gates)))
    return main.astype(BF16), gates.astype(BF16)


def kernel(x, c, ctx, c_ctx, ada_w, ada_b, pre_mix_g, post_mix_g, pre_ffn_g, post_ffn_g, w_in, q_norm_g, k_norm_g, lam_q1, lam_k1, lam_q2, lam_k2, diff_norm_g, conv_w, conv_b, mlstm_gate_b, mlstm_norm_g, w_br_attn, w_br_diff, w_br_mlstm, w_out, w_ff_gate, w_ff_up, w_ff_down, w_router, b_router, w_moe_gate, w_moe_up, w_moe_down):
    n_batch, n_tok, d = x.shape
    n_ctx = ctx.shape[1]
    depth = ada_w.shape[0]
    assert n_ctx == ROW and n_tok % ROW == 0 and d == 1024
    ts = n_ctx + n_tok
    nw = -(-(n_batch + 1) // SUBLANES) * SUBLANES

    c_all = jnp.concatenate([c, c_ctx[None, :], jnp.zeros((nw - n_batch - 1, d), F32)], axis=0)
    mod = _modulation(c_all, ada_w, ada_b)
    tables = _rope_tables(n_tok, n_ctx)
    bd4, bd1 = _block_diag_mean(A_HEADS * HEAD_DIM), _block_diag_mean(LANES)
    t_idx = jnp.arange(MCHUNK)
    tri = (t_idx[:, None] >= t_idx[None, :]).astype(F32)

    xs = jnp.concatenate([ctx, x], axis=1)
    for l in range(depth):
        lam_init = 0.8 - 0.6 * math.exp(-0.3 * l)
        w_main, w_gate = _pack_w_in(w_in[l])
        h, gates = _inproj(xs, mod, l, nw, pre_mix_g[l][None, :], w_main, w_gate)
        qg = jnp.tile(q_norm_g[l], A_HEADS)[None, :]
        kg = jnp.tile(k_norm_g[l], A_KV_HEADS)[None, :]
        gate_b = jnp.pad(mlstm_gate_b[l], (0, LANES - 4 * C_HEADS))[None, :]
        qaz, ka, va, qbz, kb, vb, qmt, km, vmt, gcol, grow = _prep(
            h, gates, tables, qg, kg, bd4, bd1, conv_w[l], conv_b[l][None, :], gate_b, tri, d)
        a_out = _gqa(qaz, ka, va)
        lam_vecs = jnp.stack([lam_q1[l], lam_k1[l], lam_q2[l], lam_k2[l]], axis=0)
        d_out = _diff(qbz, kb, vb, lam_vecs, diff_norm_g[l][None, :], lam_init)
        hf, hb = _mlstm(qmt, km, vmt, gcol, grow)
        xs = _mix(a_out, d_out, hf, hb, h, xs, mod, l, nw, post_mix_g[l][None, :], mlstm_norm_g[l][None, :],
                  w_br_attn[l].astype(BF16), w_br_diff[l].astype(BF16), w_br_mlstm[l].astype(BF16),
                  w_out[l].astype(BF16))
        j = l // 2
        if l % 2 == 0:
            xs = _ffn(xs, mod, l, nw, pre_ffn_g[l][None, :], post_ffn_g[l][None, :],
                      w_ff_gate[j].astype(BF16), w_ff_up[j].astype(BF16), w_ff_down[j].astype(BF16))
        else:
            w_r = jnp.pad(w_router[j], ((0, 0), (0, LANES - N_EXPERTS)))
            b_r = jnp.pad(b_router[j], (0, LANES - N_EXPERTS))[None, :]
            xs = _moe(xs, mod, l, nw, pre_ffn_g[l][None, :], post_ffn_g[l][None, :], w_r, b_r,
                      w_moe_gate[j].astype(BF16), w_moe_up[j].astype(BF16), w_moe_down[j].astype(BF16),
                      latent_only=l == depth - 1)
    return xs if xs.shape[1] == n_tok else xs[:, n_ctx:, :]
```

```python
import functools
import math

import jax
import jax.numpy as jnp
from jax import lax
from jax.experimental import pallas as pl
from jax.experimental.pallas import tpu as pltpu

F32 = jnp.float32
BF16 = jnp.bfloat16
HIGHEST = lax.Precision.HIGHEST

EPS = 1e-6
HEAD_DIM = 64
A_HEADS = 8
A_KV_HEADS = 2
B_HEADS = 4
C_HEADS = 4
C_DIM = 128
N_EXPERTS = 8
ROPE_THETA = 10000.0
GRID_W = 64
CONV_W = 3

LANES = 128
SUBLANES = 8
ROW = 256
MCHUNK = 256
LOG2E = math.log2(math.e)
NEG = -1e30
VMEM_LIMIT = 56 * 1024 * 1024

OFF_GATE = 0
REL_QA, REL_QB, REL_KB, REL_VB = 0, 512, 1024, 1536
REL_QC, REL_KC, REL_VC, REL_OC = 2048, 2560, 3072, 3584
REL_KA = 4096
REL_END = 4352


def _cparams(sem):
    return pltpu.CompilerParams(dimension_semantics=sem, vmem_limit_bytes=VMEM_LIMIT)


def _rms(x, g):
    y = x * lax.rsqrt(jnp.mean(x * x, axis=-1, keepdims=True) + EPS)
    return y * g


def _sigmoid(x):
    return 1.0 / (1.0 + jnp.exp(-x))


def _silu(x):
    return x * _sigmoid(x)


def _log_sigmoid(x):
    return jnp.minimum(x, 0.0) - jnp.log(1.0 + jnp.exp(-jnp.abs(x)))


def _lane_iota(shape):
    return lax.broadcasted_iota(jnp.int32, shape, len(shape) - 1)


def _row_iota(shape):
    return lax.broadcasted_iota(jnp.int32, shape, len(shape) - 2)


def _mod_kernel(c_ref, w_ref, b_ref, o_ref):
    c = c_ref[...]
    o_ref[...] = jnp.dot(_silu(c), w_ref[...], preferred_element_type=F32, precision=HIGHEST) + b_ref[...]


def _modulation(c_all, ada_w, ada_b):
    depth, d, _ = ada_w.shape
    nw = c_all.shape[0]
    out = pl.pallas_call(
        _mod_kernel,
        grid=(depth, 6),
        in_specs=[
            pl.BlockSpec((nw, d), lambda l, j: (0, 0)),
            pl.BlockSpec((None, d, d), lambda l, j: (l, 0, j)),
            pl.BlockSpec((None, 1, d), lambda l, j: (l, 0, j)),
        ],
        out_specs=pl.BlockSpec((None, None, nw, d), lambda l, j: (l, j, 0, 0)),
        out_shape=jax.ShapeDtypeStruct((depth, 6, nw, d), F32),
        compiler_params=_cparams(("arbitrary", "arbitrary")),
        name="modulation",
    )(c_all, ada_w, ada_b.reshape(depth, 1, 6 * d))
    return out.reshape(depth * 6 * nw, 1, d)


def _mod_spec(d, layer, chunk, nw, n_batch):
    base = (layer * 6 + chunk) * nw
    return pl.BlockSpec((None, 1, d), lambda b, i: (base + jnp.where(i == 0, n_batch, b), 0, 0))


def _inproj_kernel(x_ref, xprev_ref, xnext_ref, sh_ref, sc_ref, g_ref, w_ref, wg_ref,
                   cos_ref, sa_ref, sb_ref, qg_ref, kg_ref, bd4_ref, bd1_ref, cw_ref, cb_ref, gb_ref, tri_ref,
                   gate_ref, oc_ref, *mixer_refs):
    d_model = x_ref.shape[-1]

    def normed(x):
        return _rms(x, g_ref[...]) * (1.0 + sc_ref[...]) + sh_ref[...]

    xn = normed(x_ref[...])
    xb = xn.astype(BF16)
    ext = jnp.concatenate([normed(xprev_ref[...]), xn, normed(xnext_ref[...])], axis=0).astype(BF16)

    def proj(lhs, start, width):
        return jnp.dot(lhs, w_ref[:, start:start + width], preferred_element_type=F32)

    base = 3 * d_model
    half = C_HEADS * C_DIM
    qab = proj(xb, base + REL_QA, 2 * half)
    kvb = proj(xb, base + REL_KB, 2 * half)
    qkc = proj(ext, base + REL_QC, 2 * half)
    vo = proj(xb, base + REL_VC, 2 * half)
    oc_ref[...] = vo[:, half:].astype(BF16)
    kava = proj(xb, base + REL_KA, 2 * LANES)
    gates = jnp.dot(xb, wg_ref[...], preferred_element_type=F32)
    _prep_math(pl.program_id(1), pl.num_programs(1),
               qab[:, :half], kava, qab[:, half:], kvb[:, :half], kvb[:, half:],
               qkc[SUBLANES:SUBLANES + ROW], qkc[SUBLANES - 1:SUBLANES], qkc[SUBLANES + ROW:SUBLANES + ROW + 1],
               vo[:, :half], gates,
               cos_ref, sa_ref, sb_ref, qg_ref, kg_ref, bd4_ref, bd1_ref, cw_ref, cb_ref, gb_ref, tri_ref,
               *mixer_refs)
    for c in range(3):
        gate_ref[:, c * d_model:(c + 1) * d_model] = proj(xb, c * d_model, d_model).astype(BF16)


def _inproj(xs, mod, layer, nw, pre_g, w_main, w_gate, tables, qg, kg, bd4, bd1, conv_w, conv_b, gate_b, tri):
    n_batch, ts, d = xs.shape
    n_main = w_main.shape[1]
    nt = ts // ROW
    cos_t, sa_t, sb_t = tables
    row_blocks = ts // SUBLANES
    per_tile = ROW // SUBLANES
    half = C_HEADS * C_DIM

    def const(shape):
        return pl.BlockSpec(shape, lambda b, i: (0,) * len(shape))

    def rows(width):
        return pl.BlockSpec((None, ROW, width), lambda b, i: (b, i, 0))

    def heads(n, width):
        return pl.BlockSpec((None, n, ROW, width), lambda b, i: (b, 0, i, 0))

    def cols(height):
        return pl.BlockSpec((None, height, ROW), lambda b, i: (b, 0, i))

    table = pl.BlockSpec((ROW, LANES), lambda b, i: (i, 0))
    in_specs = [
        rows(d),
        pl.BlockSpec((None, SUBLANES, d), lambda b, i: (b, jnp.maximum(i * per_tile - 1, 0), 0)),
        pl.BlockSpec((None, SUBLANES, d), lambda b, i: (b, jnp.minimum((i + 1) * per_tile, row_blocks - 1), 0)),
        _mod_spec(d, layer, 0, nw, n_batch),
        _mod_spec(d, layer, 1, nw, n_batch),
        const((1, d)),
        pl.BlockSpec((d, n_main), lambda b, i: (0, 0), pipeline_mode=pl.Buffered(1)),
        const((d, LANES)),
        table, table, table,
        const((1, half)), const((1, LANES)), const((half, half)), const((LANES, LANES)),
        const((CONV_W, 2 * half)), const((1, 2 * half)), const((1, LANES)), const((MCHUNK, MCHUNK)),
    ]
    outs = [
        (rows(3 * d), (ts, 3 * d), BF16),
        (rows(half), (ts, half), BF16),
        (heads(A_HEADS, LANES), (A_HEADS, ts, LANES), BF16),
        (rows(LANES), (ts, LANES), BF16),
        (heads(A_KV_HEADS, LANES), (A_KV_HEADS, ts, LANES), BF16),
        (heads(2 * B_HEADS, LANES), (2 * B_HEADS, ts, LANES), BF16),
        (heads(B_HEADS, LANES), (B_HEADS, ts, LANES), BF16),
        (heads(B_HEADS, 2 * LANES), (B_HEADS, ts, 2 * LANES), BF16),
        (cols(half), (half, ts), BF16),
        (rows(half), (ts, half), BF16),
        (cols(half), (half, ts), BF16),
        (rows(3 * LANES), (ts, 3 * LANES), F32),
        (cols(6 * SUBLANES), (6 * SUBLANES, ts), F32),
    ]
    return pl.pallas_call(
        _inproj_kernel,
        grid=(n_batch, nt),
        in_specs=in_specs,
        out_specs=[spec for spec, _, _ in outs],
        out_shape=[jax.ShapeDtypeStruct((n_batch,) + shape, dtype) for _, shape, dtype in outs],
        compiler_params=_cparams(("parallel", "parallel")),
        name="inproj",
    )(xs, xs, xs, mod, mod, pre_g, w_main, w_gate, cos_t, sa_t, sb_t, qg, kg, bd4, bd1, conv_w, conv_b, gate_b, tri)


def _head_mean_sq(x, bd_ref):
    sq = x * x
    hi = sq.astype(BF16)
    lo = (sq - hi.astype(F32)).astype(BF16)
    bd = bd_ref[...]
    return jnp.dot(hi, bd, preferred_element_type=F32) + jnp.dot(lo, bd, preferred_element_type=F32)


def _prep_math(i, nt, qa, kava, qb, kb, vb, cur, prev_row, next_row, vc, gates,
               cos_ref, sa_ref, sb_ref, qg_ref, kg_ref, bd4_ref, bd1_ref, cw_ref, cb_ref, gb_ref, tri_ref,
               qaz_ref, ka_ref, va_ref, qbz_ref, kbo_ref, vbo_ref, qmt_ref, km_ref, vmt_ref, gcol_ref, grow_ref):
    cos, sin_a, sin_b = cos_ref[...], sa_ref[...], sb_ref[...]

    def rope(x):
        width = x.shape[1]
        reps = width // LANES
        c = jnp.concatenate([cos] * reps, axis=1) if reps > 1 else cos
        a = jnp.concatenate([sin_a] * reps, axis=1) if reps > 1 else sin_a
        b = jnp.concatenate([sin_b] * reps, axis=1) if reps > 1 else sin_b
        return x * c + pltpu.roll(x, width - 16, 1) * a + pltpu.roll(x, 16, 1) * b

    lane = _lane_iota((ROW, LANES))
    ones = jnp.ones((ROW, LANES), BF16)
    scale = HEAD_DIM ** -0.5 * LOG2E

    qa = qa * lax.rsqrt(_head_mean_sq(qa, bd4_ref) + EPS) * qg_ref[...]
    qa = rope(qa) * scale
    heads_per_kv = A_HEADS // A_KV_HEADS
    for h in range(A_HEADS):
        g = h // heads_per_kv
        blk = qa[:, (h // 2) * LANES:(h // 2 + 1) * LANES]
        if h % 2 != g:
            blk = pltpu.roll(blk, HEAD_DIM, 1)
        qaz_ref[h] = jnp.where(lane // HEAD_DIM == g, blk, 0.0).astype(BF16)
    ka = kava[:, :LANES]
    ka = ka * lax.rsqrt(_head_mean_sq(ka, bd1_ref) + EPS) * kg_ref[...]
    ka_ref[...] = rope(ka).astype(BF16)
    va = kava[:, LANES:].astype(BF16)
    for g in range(A_KV_HEADS):
        va_ref[g] = jnp.where(lane // HEAD_DIM == g, va, ones)

    qb = rope(qb) * scale
    kb = rope(kb)
    for h in range(B_HEADS):
        blk = qb[:, h * LANES:(h + 1) * LANES]
        for m in range(2):
            qbz_ref[2 * h + m] = jnp.where(lane // HEAD_DIM == m, blk, 0.0).astype(BF16)
        kbo_ref[h] = kb[:, h * LANES:(h + 1) * LANES].astype(BF16)
        vbo_ref[h, :, :LANES] = vb[:, h * LANES:(h + 1) * LANES].astype(BF16)
        vbo_ref[h, :, LANES:] = ones

    row = _row_iota(cur.shape)
    prev_row = jnp.where(i >= 2, prev_row, 0.0)
    next_row = jnp.where(jnp.logical_and(i >= 1, i < nt - 1), next_row, 0.0)
    up = jnp.where(row == 0, prev_row, pltpu.roll(cur, 1, 0))
    dn = jnp.where(row == ROW - 1, next_row, pltpu.roll(cur, ROW - 1, 0))
    y = up * cw_ref[0:1, :] + cur * cw_ref[1:2, :] + dn * cw_ref[2:3, :] + cb_ref[...]
    y = _silu(y)
    half = C_HEADS * C_DIM
    qmt_ref[...] = y[:, :half].T.astype(BF16)
    km_ref[...] = (y[:, half:] * (C_DIM ** -0.5)).astype(BF16)
    vmt_ref[...] = vc.T.astype(BF16)

    gg = gates + gb_ref[...]
    is_forget = (lane // C_HEADS) % 2 == 1
    gl = jnp.where(is_forget, _log_sigmoid(gg), gg) * LOG2E
    tri = tri_ref[...]
    ones_sq = jnp.ones((MCHUNK, MCHUNK), F32)
    n_rows = 2 * SUBLANES
    for c in range(ROW // MCHUNK):
        rows = slice(c * MCHUNK, (c + 1) * MCHUNK)
        glc = gl[rows]
        cs = jnp.dot(tri, glc, preferred_element_type=F32, precision=HIGHEST)
        tot = jnp.dot(ones_sq, glc, preferred_element_type=F32, precision=HIGHEST)
        for f, val in enumerate((glc, cs, tot)):
            gcol_ref[rows, f * LANES:(f + 1) * LANES] = val
            grow_ref[f * n_rows:(f + 1) * n_rows, rows] = val.T[:n_rows, :]


def _attend_blocks(blocks, n_keys):
    def scores(q, k_ref):
        return lax.dot_general(q, k_ref[:n_keys, :], (((1,), (1,)), ((), ())), preferred_element_type=F32)

    def weighted(s, v_ref):
        p = jnp.exp2(s - jnp.max(s, axis=-1, keepdims=True)).astype(BF16)
        return jnp.dot(p, v_ref[:n_keys, :], preferred_element_type=F32)

    outs = []
    s_cur = scores(blocks[0][0], blocks[0][1])
    for j in range(1, len(blocks)):
        s_next = scores(blocks[j][0], blocks[j][1])
        outs.append(weighted(s_cur, blocks[j - 1][2]))
        s_cur = s_next
    outs.append(weighted(s_cur, blocks[-1][2]))
    return outs


def _per_tile_keys(body, n_all):
    @pl.when(pl.program_id(1) == 0)
    def _():
        body(ROW)

    @pl.when(pl.program_id(1) > 0)
    def _():
        body(n_all)


def _gqa_kernel(q_ref, k_ref, v_ref, o_ref):
    heads_per_kv = A_HEADS // A_KV_HEADS
    lane = _lane_iota((ROW, LANES))

    def body(n_keys):
        blocks = []
        for j in range(A_HEADS // 2):
            q = q_ref[2 * j:2 * j + 2].reshape(2 * ROW, LANES)
            blocks.append((q, k_ref, v_ref.at[(2 * j) // heads_per_kv]))
        for j, o in enumerate(_attend_blocks(blocks, n_keys)):
            g = (2 * j) // heads_per_kv
            den_lane = (1 - g) * HEAD_DIM
            o = o / o[:, den_lane:den_lane + 1]
            even, odd = o[:ROW], o[ROW:]
            even = even if g == 0 else pltpu.roll(even, HEAD_DIM, 1)
            odd = odd if g == 1 else pltpu.roll(odd, HEAD_DIM, 1)
            o_ref[:, j * LANES:(j + 1) * LANES] = jnp.where(lane < HEAD_DIM, even, odd).astype(BF16)

    _per_tile_keys(body, k_ref.shape[0])


def _diff_kernel(q_ref, k_ref, v_ref, lam_ref, g_ref, o_ref, *, lam_init):
    lv = lam_ref[...]
    lam = (jnp.exp(jnp.sum(lv[0:1] * lv[1:2], axis=-1, keepdims=True))
           - jnp.exp(jnp.sum(lv[2:3] * lv[3:4], axis=-1, keepdims=True)) + lam_init)

    def body(n_keys):
        blocks = [(q_ref[2 * h:2 * h + 2].reshape(2 * ROW, LANES), k_ref.at[h], v_ref.at[h]) for h in range(B_HEADS)]
        for h, o in enumerate(_attend_blocks(blocks, n_keys)):
            o = o[:, :LANES] / o[:, LANES:LANES + 1]
            dif = o[:ROW] - lam * o[ROW:]
            o_ref[:, h * LANES:(h + 1) * LANES] = (_rms(dif, g_ref[...]) * (1.0 - lam_init)).astype(BF16)

    _per_tile_keys(body, k_ref.shape[1])


def _gqa(qaz, ka, va):
    n_batch, _, ts, _ = qaz.shape
    nt = ts // ROW
    return pl.pallas_call(
        _gqa_kernel,
        grid=(n_batch, nt),
        in_specs=[
            pl.BlockSpec((None, A_HEADS, ROW, LANES), lambda b, i: (b, 0, i, 0)),
            pl.BlockSpec((None, ts, LANES), lambda b, i: (b, 0, 0)),
            pl.BlockSpec((None, A_KV_HEADS, ts, LANES), lambda b, i: (b, 0, 0, 0)),
        ],
        out_specs=pl.BlockSpec((None, ROW, A_HEADS * HEAD_DIM), lambda b, i: (b, i, 0)),
        out_shape=jax.ShapeDtypeStruct((n_batch, ts, A_HEADS * HEAD_DIM), BF16),
        compiler_params=_cparams(("parallel", "parallel")),
        name="gqa_attention",
    )(qaz, ka, va)


def _diff(qbz, kb, vb, lam_vecs, sub_g, lam_init):
    n_batch, _, ts, _ = qbz.shape
    nt = ts // ROW
    return pl.pallas_call(
        functools.partial(_diff_kernel, lam_init=lam_init),
        grid=(n_batch, nt),
        in_specs=[
            pl.BlockSpec((None, 2 * B_HEADS, ROW, LANES), lambda b, i: (b, 0, i, 0)),
            pl.BlockSpec((None, B_HEADS, ts, LANES), lambda b, i: (b, 0, 0, 0)),
            pl.BlockSpec((None, B_HEADS, ts, 2 * LANES), lambda b, i: (b, 0, 0, 0)),
            pl.BlockSpec((4, HEAD_DIM), lambda b, i: (0, 0)),
            pl.BlockSpec((1, LANES), lambda b, i: (0, 0)),
        ],
        out_specs=pl.BlockSpec((None, ROW, B_HEADS * LANES), lambda b, i: (b, i, 0)),
        out_shape=jax.ShapeDtypeStruct((n_batch, ts, B_HEADS * LANES), BF16),
        compiler_params=_cparams(("parallel", "parallel")),
        name="diff_attention",
    )(qbz, kb, vb, lam_vecs, sub_g)


AUG = 2 * SUBLANES


def _mlstm_kernel(qtf_ref, kf_ref, vtf_ref, gcf_ref, grf_ref, qtb_ref, kb_ref, vtb_ref, gcb_ref, grb_ref,
                  hf_ref, hb_ref, c_ref, m_ref):
    @pl.when(pl.program_id(1) == 0)
    def _():
        c_ref[...] = jnp.zeros_like(c_ref)
        m_ref[...] = jnp.zeros_like(m_ref)

    length = MCHUNK
    key_idx = _row_iota((length, length))
    qry_idx = _lane_iota((length, length))
    ones_rows = jnp.ones((AUG, length), BF16)
    n_rows = 2 * SUBLANES
    n_chain = 2 * C_HEADS
    c_states = [c_ref[ch] for ch in range(n_chain)]
    m_prevs = [m_ref[ch] for ch in range(n_chain)]
    c_news, m_news, h_outs, pending = [], [], [], []

    for direction, (qt_ref, k_ref, vt_ref, gc_ref, gr_ref, h_ref) in enumerate(
            ((qtf_ref, kf_ref, vtf_ref, gcf_ref, grf_ref, hf_ref), (qtb_ref, kb_ref, vtb_ref, gcb_ref, grb_ref, hb_ref))):
        reverse = direction == 1
        gate_c, cs_c, tot_c = gc_ref[:, :LANES], gc_ref[:, LANES:2 * LANES], gc_ref[:, 2 * LANES:]
        p_c = (tot_c - cs_c + gate_c) if reverse else cs_c
        g_c = gate_c - pltpu.roll(p_c, LANES - C_HEADS, 1)
        mask = (key_idx >= qry_idx) if reverse else (key_idx <= qry_idx)
        for hd in range(C_HEADS):
            chain = direction * C_HEADS + hd
            ii = direction * 2 * C_HEADS + hd
            fi = ii + C_HEADS
            i_row = gr_ref[ii:ii + 1, :]
            f_row = gr_ref[fi:fi + 1, :]
            cs_row = gr_ref[n_rows + fi:n_rows + fi + 1, :]
            tot_row = gr_ref[2 * n_rows + fi:2 * n_rows + fi + 1, :]
            p_row = (tot_row - cs_row + f_row) if reverse else cs_row
            m_prev = m_prevs[chain]
            inter = p_row + m_prev
            log_dt = jnp.where(mask, jnp.broadcast_to(g_c[:, ii:ii + 1], (length, length)) + p_row, NEG)
            m_t = jnp.maximum(inter, jnp.max(log_dt, axis=0, keepdims=True))
            d_t = jnp.exp2(log_dt - m_t)
            a_row = jnp.exp2(inter - m_t)
            sl = slice(hd * C_DIM, (hd + 1) * C_DIM)
            k_h, qt_h = k_ref[:, sl], qt_ref[sl, :]
            vt_aug = jnp.concatenate([vt_ref[sl, :], ones_rows], axis=0)
            s_raw = jnp.dot(k_h, qt_h, preferred_element_type=F32)
            c_state = c_states[chain]
            x_t = jnp.dot(c_state.astype(BF16), qt_h, preferred_element_type=F32)
            w_row = tot_row - p_row + i_row
            m_new = jnp.maximum(tot_row + m_prev, jnp.max(w_row, axis=-1, keepdims=True))
            decay = jnp.exp2(tot_row + m_prev - m_new)
            ws = jnp.exp2(w_row - m_new)
            update = jnp.dot((vt_aug.astype(F32) * ws).astype(BF16), k_h, preferred_element_type=F32)
            c_news.append(decay[:, :C_DIM] * c_state + update)
            m_news.append(m_new)
            pending.append((s_raw, d_t, vt_aug, a_row, x_t, m_t))

    for direction in range(2):
        h_parts = []
        for hd in range(C_HEADS):
            s_raw, d_t, vt_aug, a_row, x_t, m_t = pending[direction * C_HEADS + hd]
            y_t = jnp.dot(vt_aug, (s_raw * d_t).astype(BF16), preferred_element_type=F32)
            num_t = a_row * x_t[:C_DIM] + y_t[:C_DIM]
            den = a_row * x_t[C_DIM:C_DIM + 1] + y_t[C_DIM:C_DIM + 1]
            h_t = num_t / jnp.maximum(jnp.abs(den), jnp.exp2(-m_t))
            h_parts.append(h_t.T)
        h_outs.append(jnp.concatenate(h_parts, axis=1))

    hf_ref[...] = h_outs[0]
    hb_ref[...] = h_outs[1]
    for ch in range(n_chain):
        c_ref[ch] = c_news[ch]
        m_ref[ch] = m_news[ch]


def _mlstm(qmt, km, vmt, gcol, grow):
    n_batch, ts, width = km.shape
    nc = ts // MCHUNK
    ctx_chunks = ROW // MCHUNK

    def bwd(j):
        return jnp.where(j < ctx_chunks, ctx_chunks - 1 - j, nc + ctx_chunks - 1 - j)

    def specs(idx):
        return [
            pl.BlockSpec((None, width, MCHUNK), lambda b, j: (b, 0, idx(j))),
            pl.BlockSpec((None, MCHUNK, width), lambda b, j: (b, idx(j), 0)),
            pl.BlockSpec((None, width, MCHUNK), lambda b, j: (b, 0, idx(j))),
            pl.BlockSpec((None, MCHUNK, 3 * LANES), lambda b, j: (b, idx(j), 0)),
            pl.BlockSpec((None, 6 * SUBLANES, MCHUNK), lambda b, j: (b, 0, idx(j))),
        ]

    fwd = lambda j: j
    n_chain = 2 * C_HEADS
    return pl.pallas_call(
        _mlstm_kernel,
        grid=(n_batch, nc),
        in_specs=specs(fwd) + specs(bwd),
        out_specs=[
            pl.BlockSpec((None, MCHUNK, width), lambda b, j: (b, j, 0)),
            pl.BlockSpec((None, MCHUNK, width), lambda b, j: (b, bwd(j), 0)),
        ],
        out_shape=[jax.ShapeDtypeStruct((n_batch, ts, width), F32)] * 2,
        scratch_shapes=[
            pltpu.VMEM((n_chain, C_DIM + AUG, C_DIM), F32),
            pltpu.VMEM((n_chain, 1, MCHUNK), F32),
        ],
        compiler_params=_cparams(("parallel", "arbitrary")),
        name="mlstm",
    )(qmt, km, vmt, gcol, grow, qmt, km, vmt, gcol, grow)


def _mix_kernel(a_ref, d_ref, hf_ref, hb_ref, oc_ref, gate_ref, x_ref, g1_ref, pg_ref, mg_ref,
                wa_ref, wb_ref, wc_ref, wo_ref, o_ref):
    d_model = x_ref.shape[-1]
    hsum = hf_ref[...] + hb_ref[...]
    mg = mg_ref[...]
    m = jnp.concatenate([_rms(hsum[:, hd * C_DIM:(hd + 1) * C_DIM], mg) for hd in range(C_HEADS)], axis=1)
    m = m * _sigmoid(oc_ref[...].astype(F32))
    u = (_sigmoid(gate_ref[:, :d_model].astype(F32))
         * jnp.dot(a_ref[...], wa_ref[...], preferred_element_type=F32)
         + _sigmoid(gate_ref[:, d_model:2 * d_model].astype(F32))
         * jnp.dot(d_ref[...], wb_ref[...], preferred_element_type=F32)
         + _sigmoid(gate_ref[:, 2 * d_model:].astype(F32))
         * jnp.dot(m.astype(BF16), wc_ref[...], preferred_element_type=F32))
    y = jnp.dot(u.astype(BF16), wo_ref[...], preferred_element_type=F32)
    o_ref[...] = x_ref[...] + g1_ref[...] * _rms(y, pg_ref[...])


def _mix(a, dd, hf, hb, out_gate, merge_gate, xs, mod, layer, nw, post_g, mlstm_g, wa, wb, wc, wo):
    n_batch, ts, d = xs.shape
    nt = ts // ROW
    width = a.shape[-1]

    def tile(w):
        return pl.BlockSpec((None, ROW, w), lambda b, i: (b, i, 0))

    def const(shape):
        return pl.BlockSpec(shape, lambda b, i: (0,) * len(shape))

    return pl.pallas_call(
        _mix_kernel,
        grid=(n_batch, nt),
        in_specs=[
            tile(width), tile(width), tile(width), tile(width),
            tile(width), tile(3 * d),
            tile(d),
            _mod_spec(d, layer, 2, nw, n_batch),
            const((1, d)), const((1, C_DIM)),
            const((width, d)), const((width, d)), const((width, d)), const((d, d)),
        ],
        out_specs=tile(d),
        out_shape=jax.ShapeDtypeStruct((n_batch, ts, d), F32),
        compiler_params=_cparams(("parallel", "parallel")),
        name="mix_out",
    )(a, dd, hf, hb, out_gate, merge_gate, xs, mod, post_g, mlstm_g, wa, wb, wc, wo)


def _ffn_kernel(x_ref, sh_ref, sc_ref, g2_ref, pre_ref, post_ref, wg_ref, wu_ref, wd_ref, o_ref):
    x = x_ref[...]
    xb = (_rms(x, pre_ref[...]) * (1.0 + sc_ref[...]) + sh_ref[...]).astype(BF16)
    gate = jnp.dot(xb, wg_ref[...], preferred_element_type=F32)
    up = jnp.dot(xb, wu_ref[...], preferred_element_type=F32)
    z = jnp.dot((_silu(gate) * up).astype(BF16), wd_ref[...], preferred_element_type=F32)
    o_ref[...] = x + g2_ref[...] * _rms(z, post_ref[...])


def _ffn(xs, mod, layer, nw, pre_g, post_g, wg, wu, wd):
    n_batch, ts, d = xs.shape
    nt = ts // ROW
    dff = wg.shape[1]

    def resident(shape):
        return pl.BlockSpec(shape, lambda b, i: (0,) * len(shape), pipeline_mode=pl.Buffered(1))

    tile = pl.BlockSpec((None, ROW, d), lambda b, i: (b, i, 0))
    return pl.pallas_call(
        _ffn_kernel,
        grid=(n_batch, nt),
        in_specs=[
            tile,
            _mod_spec(d, layer, 3, nw, n_batch), _mod_spec(d, layer, 4, nw, n_batch),
            _mod_spec(d, layer, 5, nw, n_batch),
            pl.BlockSpec((1, d), lambda b, i: (0, 0)), pl.BlockSpec((1, d), lambda b, i: (0, 0)),
            resident((d, dff)), resident((d, dff)), resident((dff, d)),
        ],
        out_specs=tile,
        out_shape=jax.ShapeDtypeStruct((n_batch, ts, d), F32),
        compiler_params=_cparams(("parallel", "parallel")),
        name="ffn",
    )(xs, mod, mod, mod, pre_g, post_g, wg, wu, wd)


def _route_kernel(x_ref, sh_ref, sc_ref, pre_ref, wr_ref, br_ref, tri_ref, xn_ref, meta_ref, cnt_ref, carry_ref):
    @pl.when(jnp.logical_and(pl.program_id(0) == 0, pl.program_id(1) == 0))
    def _():
        carry_ref[...] = jnp.zeros_like(carry_ref)

    xn = _rms(x_ref[...], pre_ref[...]) * (1.0 + sc_ref[...]) + sh_ref[...]
    xn_ref[...] = xn
    lane = _lane_iota((ROW, LANES))
    logits = jnp.dot(xn, wr_ref[...], preferred_element_type=F32, precision=HIGHEST) + br_ref[...]
    logits = jnp.where(lane < N_EXPERTS, logits, NEG)
    v1 = jnp.max(logits, axis=-1, keepdims=True)
    i1 = jnp.min(jnp.where(logits == v1, lane, LANES), axis=-1, keepdims=True)
    rest = jnp.where(lane == i1, NEG, logits)
    v2 = jnp.max(rest, axis=-1, keepdims=True)
    i2 = jnp.min(jnp.where(rest == v2, lane, LANES), axis=-1, keepdims=True)
    e2 = jnp.exp(v2 - v1)
    w1 = 1.0 / (1.0 + e2)
    w2 = e2 / (1.0 + e2)
    assigned = jnp.where(lane == i1, 1.0, jnp.where(lane == i2, 1.0, 0.0))
    before = jnp.dot(tri_ref[...], assigned.astype(BF16), preferred_element_type=F32) + carry_ref[0:1, :]
    r1 = jnp.sum(jnp.where(lane == i1, before, 0.0), axis=-1, keepdims=True)
    r2 = jnp.sum(jnp.where(lane == i2, before, 0.0), axis=-1, keepdims=True)
    carry_ref[...] = carry_ref[...] + jnp.sum(assigned, axis=0, keepdims=True)
    cnt_ref[...] = carry_ref[...]
    fields = (i1.astype(F32), i2.astype(F32), w1, w2, r1, r2)
    meta = jnp.zeros((ROW, LANES), F32)
    for f, val in enumerate(fields):
        meta = jnp.where(lane == f, val, meta)
    meta_ref[...] = meta


def _route(xs, mod, layer, nw, pre_g, w_r, b_r):
    n_batch, ts, d = xs.shape
    nt = ts // ROW
    tile = pl.BlockSpec((None, ROW, d), lambda b, i: (b, i, 0))
    t_idx = jnp.arange(ROW)
    tri_strict = (t_idx[:, None] > t_idx[None, :]).astype(BF16)
    return pl.pallas_call(
        _route_kernel,
        grid=(n_batch, nt),
        in_specs=[
            tile, _mod_spec(d, layer, 3, nw, n_batch), _mod_spec(d, layer, 4, nw, n_batch),
            pl.BlockSpec((1, d), lambda b, i: (0, 0)),
            pl.BlockSpec((d, LANES), lambda b, i: (0, 0)),
            pl.BlockSpec((1, LANES), lambda b, i: (0, 0)),
            pl.BlockSpec((ROW, ROW), lambda b, i: (0, 0)),
        ],
        out_specs=[tile, pl.BlockSpec((None, ROW, LANES), lambda b, i: (b, i, 0)),
                   pl.BlockSpec((SUBLANES, LANES), lambda b, i: (0, 0))],
        out_shape=[jax.ShapeDtypeStruct((n_batch, ts, d), F32),
                   jax.ShapeDtypeStruct((n_batch, ts, LANES), F32),
                   jax.ShapeDtypeStruct((SUBLANES, LANES), F32)],
        scratch_shapes=[pltpu.VMEM((SUBLANES, LANES), F32)],
        compiler_params=_cparams(("arbitrary", "arbitrary")),
        name="route",
    )(xs, mod, mod, pre_g, w_r, b_r, tri_strict)


GROUPS = ROW // SUBLANES


def _start_rows(make_copy):
    def issue(g, carry):
        for j in range(SUBLANES):
            for slot in range(2):
                make_copy(g, j, slot).start()
        return carry

    lax.fori_loop(0, GROUPS, issue, 0)


def _wait_rows(make_copy):
    def drain(g, carry):
        for j in range(SUBLANES):
            for slot in range(2):
                make_copy(0, 0, slot).wait()
        return carry

    lax.fori_loop(0, GROUPS, drain, 0)


def _dispatch_kernel(pos_ref, ends_ref, xn_ref, out_ref, zero_ref, sem, zero_sem):
    step = pl.program_id(0)
    n_tok = pos_ref.shape[0] // 2
    tm = zero_ref.shape[0]
    n_sorted = out_ref.shape[0]

    @pl.when(step == 0)
    def _():
        zero_ref[...] = jnp.zeros_like(zero_ref)

        def fill(row):
            copy = pltpu.make_async_copy(zero_ref, out_ref.at[pl.ds(pl.multiple_of(row, tm), tm)], zero_sem)
            copy.start()
            copy.wait()

        for e in range(N_EXPERTS):
            prev_end = ends_ref[e - 1] if e > 0 else 0

            @pl.when(ends_ref[e] > prev_end)
            def _():
                fill(ends_ref[e] - tm)

        last_end = ends_ref[N_EXPERTS - 1]

        def tail(k, carry):
            fill(last_end + k * tm)
            return carry

        lax.fori_loop(0, (n_sorted - last_end) // tm, tail, 0)

    def make_copy(g, j, slot):
        p = pos_ref[slot * n_tok + step * ROW + g * SUBLANES + j]
        return pltpu.make_async_copy(xn_ref.at[g, pl.ds(j, 1)], out_ref.at[pl.ds(p, 1)], sem)

    _start_rows(make_copy)
    _wait_rows(make_copy)


def _dispatch(pos, ends, xn, n_sorted, tm):
    m_rows, d = xn.shape
    return pl.pallas_call(
        _dispatch_kernel,
        grid_spec=pltpu.PrefetchScalarGridSpec(
            num_scalar_prefetch=2,
            grid=(m_rows // ROW,),
            in_specs=[pl.BlockSpec((GROUPS, SUBLANES, d), lambda i, pos, ends: (i, 0, 0))],
            out_specs=pl.BlockSpec(memory_space=pl.ANY),
            scratch_shapes=[pltpu.VMEM((tm, d), F32), pltpu.SemaphoreType.DMA(()), pltpu.SemaphoreType.DMA(())],
        ),
        out_shape=jax.ShapeDtypeStruct((n_sorted, d), F32),
        compiler_params=_cparams(("arbitrary",)),
        name="dispatch",
    )(pos, ends, xn.reshape(m_rows // SUBLANES, SUBLANES, d))


def _experts_kernel(te_ref, x_ref, wg_ref, wu_ref, wd_ref, y_ref):
    used = te_ref[pl.program_id(0)] < N_EXPERTS

    @pl.when(used)
    def _():
        x = x_ref[...].astype(BF16)
        gate = jnp.dot(x, wg_ref[...], preferred_element_type=F32)
        up = jnp.dot(x, wu_ref[...], preferred_element_type=F32)
        y_ref[...] = jnp.dot((_silu(gate) * up).astype(BF16), wd_ref[...], preferred_element_type=F32)

    @pl.when(jnp.logical_not(used))
    def _():
        y_ref[...] = jnp.zeros_like(y_ref)


def _experts(tile_expert, x_sorted, wg, wu, wd, tm):
    n_sorted, d = x_sorted.shape
    n_e, _, dff = wg.shape

    def weight(shape):
        return pl.BlockSpec((None,) + shape, lambda i, te: (jnp.minimum(te[i], n_e - 1), 0, 0))

    return pl.pallas_call(
        _experts_kernel,
        grid_spec=pltpu.PrefetchScalarGridSpec(
            num_scalar_prefetch=1,
            grid=(n_sorted // tm,),
            in_specs=[pl.BlockSpec((tm, d), lambda i, te: (i, 0)),
                      weight((d, dff)), weight((d, dff)), weight((dff, d))],
            out_specs=pl.BlockSpec((tm, d), lambda i, te: (i, 0)),
        ),
        out_shape=jax.ShapeDtypeStruct((n_sorted, d), F32),
        compiler_params=_cparams(("arbitrary",)),
        name="experts",
    )(tile_expert, x_sorted, wg, wu, wd)


def _combine_kernel(pos_ref, x_ref, meta_ref, g2_ref, post_ref, y_ref, o_ref, buf_ref, sems, *,
                    tiles_per_sample, latent_only):
    step = pl.program_id(0)
    n_steps = pl.num_programs(0)
    n_tok = pos_ref.shape[0] // 2
    d_model = x_ref.shape[-1]

    def wanted(tile):
        return (tile % tiles_per_sample != 0) if latent_only else (tile >= 0)

    def copies(tile):
        def make_copy(g, j, slot):
            p = pos_ref[slot * n_tok + tile * ROW + g * SUBLANES + j]
            return pltpu.make_async_copy(y_ref.at[pl.ds(p, 1)], buf_ref.at[tile % 2, slot, g, pl.ds(j, 1)],
                                         sems.at[tile % 2])
        return make_copy

    @pl.when(jnp.logical_and(step == 0, wanted(step)))
    def _():
        _start_rows(copies(step))

    @pl.when(jnp.logical_and(step + 1 < n_steps, wanted(step + 1)))
    def _():
        _start_rows(copies(step + 1))

    @pl.when(wanted(step))
    def _():
        _wait_rows(copies(step))
        meta = meta_ref[...]
        cur = step % 2
        y1 = buf_ref[cur, 0].reshape(ROW, d_model)
        y2 = buf_ref[cur, 1].reshape(ROW, d_model)
        z = meta[:, 2:3] * y1 + meta[:, 3:4] * y2
        o_ref[...] = x_ref[...] + g2_ref[...] * _rms(z, post_ref[...])


def _combine(pos, xs, meta, y_sorted, mod, layer, nw, post_g, latent_only):
    n_batch, ts, d = xs.shape
    nt = ts // ROW
    m_rows = n_batch * ts
    base = (layer * 6 + 5) * nw
    tile = pl.BlockSpec((ROW, d), lambda i, pos: (i, 0))
    if latent_only:
        out_rows = n_batch * (ts - ROW)
        out_tile = pl.BlockSpec((ROW, d), lambda i, pos: ((i // nt) * (nt - 1) + jnp.maximum(i % nt - 1, 0), 0))
    else:
        out_rows, out_tile = m_rows, tile
    out = pl.pallas_call(
        functools.partial(_combine_kernel, tiles_per_sample=nt, latent_only=latent_only),
        grid_spec=pltpu.PrefetchScalarGridSpec(
            num_scalar_prefetch=1,
            grid=(m_rows // ROW,),
            in_specs=[
                tile,
                pl.BlockSpec((ROW, LANES), lambda i, pos: (i, 0)),
                pl.BlockSpec((None, 1, d), lambda i, pos: (base + jnp.where(i % nt == 0, n_batch, i // nt), 0, 0)),
                pl.BlockSpec((1, d), lambda i, pos: (0, 0)),
                pl.BlockSpec(memory_space=pl.ANY),
            ],
            out_specs=out_tile,
            scratch_shapes=[pltpu.VMEM((2, 2, GROUPS, SUBLANES, d), F32), pltpu.SemaphoreType.DMA((2,))],
        ),
        out_shape=jax.ShapeDtypeStruct((out_rows, d), F32),
        compiler_params=_cparams(("arbitrary",)),
        name="combine",
    )(pos, xs.reshape(m_rows, d), meta.reshape(m_rows, LANES), mod, post_g, y_sorted)
    return out.reshape(n_batch, out_rows // n_batch, d)


def _moe(xs, mod, layer, nw, pre_g, post_g, w_r, b_r, wg, wu, wd, latent_only, tm=512):
    n_batch, ts, d = xs.shape
    m_rows = n_batch * ts
    xn, meta, cnt = _route(xs, mod, layer, nw, pre_g, w_r, b_r)
    m2 = meta.reshape(m_rows, LANES)
    i1, i2 = m2[:, 0].astype(jnp.int32), m2[:, 1].astype(jnp.int32)
    r1, r2 = m2[:, 4].astype(jnp.int32), m2[:, 5].astype(jnp.int32)
    counts = cnt[0, :N_EXPERTS].astype(jnp.int32)
    padded = -(-counts // tm) * tm
    ends = jnp.cumsum(padded)
    start = ends - padded
    pos = jnp.concatenate([start[i1] + r1, start[i2] + r2])
    n_tiles = 2 * m_rows // tm + N_EXPERTS
    tile_row = jnp.arange(n_tiles, dtype=jnp.int32) * tm
    tile_expert = jnp.sum((ends[None, :] <= tile_row[:, None]).astype(jnp.int32), axis=1)
    x_sorted = _dispatch(pos, ends, xn.reshape(m_rows, d), n_tiles * tm, tm)
    y_sorted = _experts(tile_expert, x_sorted, wg, wu, wd, tm)
    return _combine(pos, xs, meta, y_sorted, mod, layer, nw, post_g, latent_only)


def _rope_tables(n_tok, n_ctx):
    n_freq = HEAD_DIM // 4
    pos = jnp.arange(n_tok)
    row = (pos // GRID_W).astype(F32)
    colp = (pos % GRID_W).astype(F32)
    inv = ROPE_THETA ** (-jnp.arange(n_freq, dtype=F32) / n_freq)
    lane = jnp.arange(LANES)
    in_head = lane % HEAD_DIM
    use_col = (in_head // (HEAD_DIM // 2)) == 1
    freq = inv[in_head % n_freq]
    ang = jnp.where(use_col[None, :], colp[:, None], row[:, None]) * freq[None, :]
    lower = (in_head % (HEAD_DIM // 2)) < n_freq
    cos_t = jnp.cos(ang)
    sin_t = jnp.sin(ang)
    sin_a = jnp.where(lower[None, :], -sin_t, 0.0)
    sin_b = jnp.where(lower[None, :], 0.0, sin_t)
    pad = lambda t, v: jnp.concatenate([jnp.full((n_ctx, LANES), v, F32), t], axis=0)
    return pad(cos_t, 1.0), pad(sin_a, 0.0), pad(sin_b, 0.0)


def _block_diag_mean(width):
    idx = jnp.arange(width) // HEAD_DIM
    return jnp.where(idx[:, None] == idx[None, :], 1.0 / HEAD_DIM, 0.0).astype(BF16)


def _pack_w_in(w):
    a_q = A_HEADS * HEAD_DIM
    a_kv = 2 * A_KV_HEADS * HEAD_DIM
    n_gates = 4 * C_HEADS
    gate_start = w.shape[1] - 3 * w.shape[0]
    g_start = gate_start - n_gates
    main = jnp.concatenate([w[:, gate_start:], w[:, :a_q], w[:, a_q + a_kv:g_start], w[:, a_q:a_q + a_kv]], axis=1)
    gates = jnp.pad(w[:, g_start:gate_start], ((0, 0), (0, LANES - n_gates)))
    return main.astype(BF16), gates.astype(BF16)


def kernel(x, c, ctx, c_ctx, ada_w, ada_b, pre_mix_g, post_mix_g, pre_ffn_g, post_ffn_g, w_in, q_norm_g, k_norm_g, lam_q1, lam_k1, lam_q2, lam_k2, diff_norm_g, conv_w, conv_b, mlstm_gate_b, mlstm_norm_g, w_br_attn, w_br_diff, w_br_mlstm, w_out, w_ff_gate, w_ff_up, w_ff_down, w_router, b_router, w_moe_gate, w_moe_up, w_moe_down):
    n_batch, n_tok, d = x.shape
    n_ctx = ctx.shape[1]
    depth = ada_w.shape[0]
    assert n_ctx == ROW and n_tok % ROW == 0 and d == 1024
    ts = n_ctx + n_tok
    nw = -(-(n_batch + 1) // SUBLANES) * SUBLANES

    c_all = jnp.concatenate([c, c_ctx[None, :], jnp.zeros((nw - n_batch - 1, d), F32)], axis=0)
    mod = _modulation(c_all, ada_w, ada_b)
    tables = _rope_tables(n_tok, n_ctx)
    bd4, bd1 = _block_diag_mean(A_HEADS * HEAD_DIM), _block_diag_mean(LANES)
    t_idx = jnp.arange(MCHUNK)
    tri = (t_idx[:, None] >= t_idx[None, :]).astype(F32)

    xs = jnp.concatenate([ctx, x], axis=1)
    for l in range(depth):
        lam_init = 0.8 - 0.6 * math.exp(-0.3 * l)
        w_main, w_gate = _pack_w_in(w_in[l])
        qg = jnp.tile(q_norm_g[l], A_HEADS)[None, :]
        kg = jnp.tile(k_norm_g[l], A_KV_HEADS)[None, :]
        gate_b = jnp.pad(mlstm_gate_b[l], (0, LANES - 4 * C_HEADS))[None, :]
        merge_gate, out_gate, qaz, ka, va, qbz, kb, vb, qmt, km, vmt, gcol, grow = _inproj(
            xs, mod, l, nw, pre_mix_g[l][None, :], w_main, w_gate,
            tables, qg, kg, bd4, bd1, conv_w[l], conv_b[l][None, :], gate_b, tri)
        a_out = _gqa(qaz, ka, va)
        lam_vecs = jnp.stack([lam_q1[l], lam_k1[l], lam_q2[l], lam_k2[l]], axis=0)
        d_out = _diff(qbz, kb, vb, lam_vecs, diff_norm_g[l][None, :], lam_init)
        hf, hb = _mlstm(qmt, km, vmt, gcol, grow)
        xs = _mix(a_out, d_out, hf, hb, out_gate, merge_gate, xs, mod, l, nw,
                  post_mix_g[l][None, :], mlstm_norm_g[l][None, :],
                  w_br_attn[l].astype(BF16), w_br_diff[l].astype(BF16), w_br_mlstm[l].astype(BF16),
                  w_out[l].astype(BF16))
        j = l // 2
        if l % 2 == 0:
            xs = _ffn(xs, mod, l, nw, pre_ffn_g[l][None, :], post_ffn_g[l][None, :],
                      w_ff_gate[j].astype(BF16), w_ff_up[j].astype(BF16), w_ff_down[j].astype(BF16))
        else:
            w_r = jnp.pad(w_router[j], ((0, 0), (0, LANES - N_EXPERTS)))
            b_r = jnp.pad(b_router[j], (0, LANES - N_EXPERTS))[None, :]
            xs = _moe(xs, mod, l, nw, pre_ffn_g[l][None, :], post_ffn_g[l][None, :], w_r, b_r,
                      w_moe_gate[j].astype(BF16), w_moe_up[j].astype(BF16), w_moe_down[j].astype(BF16),
                      latent_only=l == depth - 1)
    return xs if xs.shape[1] == n_tok else xs[:, n_ctx:, :]
```

```python
import functools
import math

import jax
import jax.numpy as jnp
from jax import lax
from jax.experimental import pallas as pl
from jax.experimental.pallas import tpu as pltpu

F32 = jnp.float32
BF16 = jnp.bfloat16
HIGHEST = lax.Precision.HIGHEST

EPS = 1e-6
HEAD_DIM = 64
A_HEADS = 8
A_KV_HEADS = 2
B_HEADS = 4
C_HEADS = 4
C_DIM = 128
N_EXPERTS = 8
ROPE_THETA = 10000.0
GRID_W = 64
CONV_W = 3

LANES = 128
SUBLANES = 8
ROW = 256
MCHUNK = 256
LOG2E = math.log2(math.e)
NEG = -1e30
VMEM_LIMIT = 56 * 1024 * 1024

OFF_GATE = 0
REL_QA, REL_QB, REL_KB, REL_VB = 0, 512, 1024, 1536
REL_QC, REL_KC, REL_VC, REL_OC = 2048, 2560, 3072, 3584
REL_KA = 4096
REL_END = 4352


def _cparams(sem):
    return pltpu.CompilerParams(dimension_semantics=sem, vmem_limit_bytes=VMEM_LIMIT)


def _rms(x, g):
    y = x * lax.rsqrt(jnp.mean(x * x, axis=-1, keepdims=True) + EPS)
    return y * g


def _sigmoid(x):
    return 1.0 / (1.0 + jnp.exp(-x))


def _silu(x):
    return x * _sigmoid(x)


def _log_sigmoid(x):
    return jnp.minimum(x, 0.0) - jnp.log(1.0 + jnp.exp(-jnp.abs(x)))


def _lane_iota(shape):
    return lax.broadcasted_iota(jnp.int32, shape, len(shape) - 1)


def _row_iota(shape):
    return lax.broadcasted_iota(jnp.int32, shape, len(shape) - 2)


def _mod_kernel(c_ref, w_ref, b_ref, o_ref):
    c = c_ref[...]
    o_ref[...] = jnp.dot(_silu(c), w_ref[...], preferred_element_type=F32, precision=HIGHEST) + b_ref[...]


def _modulation(c_all, ada_w, ada_b):
    depth, d, _ = ada_w.shape
    nw = c_all.shape[0]
    out = pl.pallas_call(
        _mod_kernel,
        grid=(depth, 6),
        in_specs=[
            pl.BlockSpec((nw, d), lambda l, j: (0, 0)),
            pl.BlockSpec((None, d, d), lambda l, j: (l, 0, j)),
            pl.BlockSpec((None, 1, d), lambda l, j: (l, 0, j)),
        ],
        out_specs=pl.BlockSpec((None, None, nw, d), lambda l, j: (l, j, 0, 0)),
        out_shape=jax.ShapeDtypeStruct((depth, 6, nw, d), F32),
        compiler_params=_cparams(("arbitrary", "arbitrary")),
        name="modulation",
    )(c_all, ada_w, ada_b.reshape(depth, 1, 6 * d))
    return out.reshape(depth * 6 * nw, 1, d)


def _mod_spec(d, layer, chunk, nw, n_batch):
    base = (layer * 6 + chunk) * nw
    return pl.BlockSpec((None, 1, d), lambda b, i: (base + jnp.where(i == 0, n_batch, b), 0, 0))


def _inproj_kernel(x_ref, xprev_ref, xnext_ref, sh_ref, sc_ref, g_ref, w_ref, wg_ref,
                   cos_ref, sa_ref, sb_ref, qg_ref, kg_ref, bd4_ref, bd1_ref, cw_ref, cb_ref, gb_ref, tri_ref,
                   gate_ref, oc_ref, *mixer_refs):
    d_model = x_ref.shape[-1]

    def normed(x):
        return _rms(x, g_ref[...]) * (1.0 + sc_ref[...]) + sh_ref[...]

    xn = normed(x_ref[...])
    xb = xn.astype(BF16)
    ext = jnp.concatenate([normed(xprev_ref[...]), xn, normed(xnext_ref[...])], axis=0).astype(BF16)

    def proj(lhs, start, width):
        return jnp.dot(lhs, w_ref[:, start:start + width], preferred_element_type=F32)

    base = 3 * d_model
    half = C_HEADS * C_DIM
    qab = proj(xb, base + REL_QA, 2 * half)
    kvb = proj(xb, base + REL_KB, 2 * half)
    qkc = proj(ext, base + REL_QC, 2 * half)
    vo = proj(xb, base + REL_VC, 2 * half)
    oc_ref[...] = vo[:, half:].astype(BF16)
    kava = proj(xb, base + REL_KA, 2 * LANES)
    gates = jnp.dot(xb, wg_ref[...], preferred_element_type=F32)
    _prep_math(pl.program_id(1), pl.num_programs(1),
               qab[:, :half], kava, qab[:, half:], kvb[:, :half], kvb[:, half:],
               qkc[SUBLANES:SUBLANES + ROW], qkc[SUBLANES - 1:SUBLANES], qkc[SUBLANES + ROW:SUBLANES + ROW + 1],
               vo[:, :half], gates,
               cos_ref, sa_ref, sb_ref, qg_ref, kg_ref, bd4_ref, bd1_ref, cw_ref, cb_ref, gb_ref, tri_ref,
               *mixer_refs)
    for c in range(6):
        gate_ref[:, c * half:(c + 1) * half] = proj(xb, c * half, half).astype(BF16)


def _inproj(xs, mod, layer, nw, pre_g, w_main, w_gate, tables, qg, kg, bd4, bd1, conv_w, conv_b, gate_b, tri):
    n_batch, ts, d = xs.shape
    n_main = w_main.shape[1]
    nt = ts // ROW
    cos_t, sa_t, sb_t = tables
    row_blocks = ts // SUBLANES
    per_tile = ROW // SUBLANES
    half = C_HEADS * C_DIM

    def const(shape):
        return pl.BlockSpec(shape, lambda b, i: (0,) * len(shape))

    def rows(width):
        return pl.BlockSpec((None, ROW, width), lambda b, i: (b, i, 0))

    def heads(n, width):
        return pl.BlockSpec((None, n, ROW, width), lambda b, i: (b, 0, i, 0))

    def cols(height):
        return pl.BlockSpec((None, height, ROW), lambda b, i: (b, 0, i))

    table = pl.BlockSpec((ROW, LANES), lambda b, i: (i, 0))
    in_specs = [
        rows(d),
        pl.BlockSpec((None, SUBLANES, d), lambda b, i: (b, jnp.maximum(i * per_tile - 1, 0), 0)),
        pl.BlockSpec((None, SUBLANES, d), lambda b, i: (b, jnp.minimum((i + 1) * per_tile, row_blocks - 1), 0)),
        _mod_spec(d, layer, 0, nw, n_batch),
        _mod_spec(d, layer, 1, nw, n_batch),
        const((1, d)),
        pl.BlockSpec((d, n_main), lambda b, i: (0, 0), pipeline_mode=pl.Buffered(1)),
        const((d, LANES)),
        table, table, table,
        const((1, half)), const((1, LANES)), const((half, half)), const((LANES, LANES)),
        const((CONV_W, 2 * half)), const((1, 2 * half)), const((1, LANES)), const((MCHUNK, MCHUNK)),
    ]
    outs = [
        (rows(3 * d), (ts, 3 * d), BF16),
        (rows(half), (ts, half), BF16),
        (heads(A_HEADS, LANES), (A_HEADS, ts, LANES), BF16),
        (rows(LANES), (ts, LANES), BF16),
        (heads(A_KV_HEADS, LANES), (A_KV_HEADS, ts, LANES), BF16),
        (heads(2 * B_HEADS, LANES), (2 * B_HEADS, ts, LANES), BF16),
        (heads(B_HEADS, LANES), (B_HEADS, ts, LANES), BF16),
        (heads(B_HEADS, 2 * LANES), (B_HEADS, ts, 2 * LANES), BF16),
        (cols(half), (half, ts), BF16),
        (rows(half), (ts, half), BF16),
        (cols(half), (half, ts), BF16),
        (rows(3 * LANES), (ts, 3 * LANES), F32),
        (cols(6 * SUBLANES), (6 * SUBLANES, ts), F32),
    ]
    return pl.pallas_call(
        _inproj_kernel,
        grid=(n_batch, nt),
        in_specs=in_specs,
        out_specs=[spec for spec, _, _ in outs],
        out_shape=[jax.ShapeDtypeStruct((n_batch,) + shape, dtype) for _, shape, dtype in outs],
        compiler_params=_cparams(("parallel", "parallel")),
        name="inproj",
    )(xs, xs, xs, mod, mod, pre_g, w_main, w_gate, cos_t, sa_t, sb_t, qg, kg, bd4, bd1, conv_w, conv_b, gate_b, tri)


def _head_mean_sq(x, bd_ref):
    return jnp.dot((x * x).astype(BF16), bd_ref[...], preferred_element_type=F32)


def _prep_math(i, nt, qa, kava, qb, kb, vb, cur, prev_row, next_row, vc, gates,
               cos_ref, sa_ref, sb_ref, qg_ref, kg_ref, bd4_ref, bd1_ref, cw_ref, cb_ref, gb_ref, tri_ref,
               qaz_ref, ka_ref, va_ref, qbz_ref, kbo_ref, vbo_ref, qmt_ref, km_ref, vmt_ref, gcol_ref, grow_ref):
    cos, sin_a, sin_b = cos_ref[...], sa_ref[...], sb_ref[...]

    def rope(x):
        width = x.shape[1]
        reps = width // LANES
        c = jnp.concatenate([cos] * reps, axis=1) if reps > 1 else cos
        a = jnp.concatenate([sin_a] * reps, axis=1) if reps > 1 else sin_a
        b = jnp.concatenate([sin_b] * reps, axis=1) if reps > 1 else sin_b
        return x * c + pltpu.roll(x, width - 16, 1) * a + pltpu.roll(x, 16, 1) * b

    lane = _lane_iota((ROW, LANES))
    ones = jnp.ones((ROW, LANES), BF16)
    scale = HEAD_DIM ** -0.5 * LOG2E

    qa = qa * lax.rsqrt(_head_mean_sq(qa, bd4_ref) + EPS) * qg_ref[...]
    qa = rope(qa) * scale
    heads_per_kv = A_HEADS // A_KV_HEADS
    for h in range(A_HEADS):
        g = h // heads_per_kv
        blk = qa[:, (h // 2) * LANES:(h // 2 + 1) * LANES]
        if h % 2 != g:
            blk = pltpu.roll(blk, HEAD_DIM, 1)
        qaz_ref[h] = jnp.where(lane // HEAD_DIM == g, blk, 0.0).astype(BF16)
    ka = kava[:, :LANES]
    ka = ka * lax.rsqrt(_head_mean_sq(ka, bd1_ref) + EPS) * kg_ref[...]
    ka_ref[...] = rope(ka).astype(BF16)
    va = kava[:, LANES:].astype(BF16)
    for g in range(A_KV_HEADS):
        va_ref[g] = jnp.where(lane // HEAD_DIM == g, va, ones)

    qb = rope(qb) * scale
    kb = rope(kb)
    for h in range(B_HEADS):
        blk = qb[:, h * LANES:(h + 1) * LANES]
        for m in range(2):
            qbz_ref[2 * h + m] = jnp.where(lane // HEAD_DIM == m, blk, 0.0).astype(BF16)
        kbo_ref[h] = kb[:, h * LANES:(h + 1) * LANES].astype(BF16)
        vbo_ref[h, :, :LANES] = vb[:, h * LANES:(h + 1) * LANES].astype(BF16)
        vbo_ref[h, :, LANES:] = ones

    row = _row_iota(cur.shape)
    prev_row = jnp.where(i >= 2, prev_row, 0.0)
    next_row = jnp.where(jnp.logical_and(i >= 1, i < nt - 1), next_row, 0.0)
    up = jnp.where(row == 0, prev_row, pltpu.roll(cur, 1, 0))
    dn = jnp.where(row == ROW - 1, next_row, pltpu.roll(cur, ROW - 1, 0))
    y = up * cw_ref[0:1, :] + cur * cw_ref[1:2, :] + dn * cw_ref[2:3, :] + cb_ref[...]
    y = _silu(y)
    half = C_HEADS * C_DIM
    qmt_ref[...] = y[:, :half].T.astype(BF16)
    km_ref[...] = (y[:, half:] * (C_DIM ** -0.5)).astype(BF16)
    vmt_ref[...] = vc.T.astype(BF16)

    gg = gates + gb_ref[...]
    is_forget = (lane // C_HEADS) % 2 == 1
    gl = jnp.where(is_forget, _log_sigmoid(gg), gg) * LOG2E
    tri = tri_ref[...]
    n_rows = 2 * SUBLANES
    for c in range(ROW // MCHUNK):
        rows = slice(c * MCHUNK, (c + 1) * MCHUNK)
        glc = gl[rows]
        hi = glc.astype(BF16)
        rest = glc - hi.astype(F32)
        mid = rest.astype(BF16)
        low = (rest - mid.astype(F32)).astype(BF16)
        cs = (jnp.dot(tri, hi, preferred_element_type=F32) + jnp.dot(tri, mid, preferred_element_type=F32)
              + jnp.dot(tri, low, preferred_element_type=F32))
        tot = jnp.broadcast_to(jnp.sum(glc, axis=0, keepdims=True), glc.shape)
        for f, val in enumerate((glc, cs, tot)):
            gcol_ref[rows, f * LANES:(f + 1) * LANES] = val
            grow_ref[f * n_rows:(f + 1) * n_rows, rows] = val.T[:n_rows, :]


def _attend_blocks(blocks, n_keys):
    def scores(q, k_ref):
        return lax.dot_general(q, k_ref[:n_keys, :], (((1,), (1,)), ((), ())), preferred_element_type=F32)

    def weighted(s, v_ref):
        p = jnp.exp2(s - jnp.max(s, axis=-1, keepdims=True)).astype(BF16)
        return jnp.dot(p, v_ref[:n_keys, :], preferred_element_type=F32)

    outs = []
    s_cur = scores(blocks[0][0], blocks[0][1])
    for j in range(1, len(blocks)):
        s_next = scores(blocks[j][0], blocks[j][1])
        outs.append(weighted(s_cur, blocks[j - 1][2]))
        s_cur = s_next
    outs.append(weighted(s_cur, blocks[-1][2]))
    return outs


def _per_tile_keys(body, n_all):
    @pl.when(pl.program_id(1) == 0)
    def _():
        body(ROW)

    @pl.when(pl.program_id(1) > 0)
    def _():
        body(n_all)


def _gqa_kernel(q_ref, k_ref, v_ref, o_ref):
    heads_per_kv = A_HEADS // A_KV_HEADS
    lane = _lane_iota((ROW, LANES))

    def body(n_keys):
        blocks = []
        for j in range(A_HEADS // 2):
            q = q_ref[2 * j:2 * j + 2].reshape(2 * ROW, LANES)
            blocks.append((q, k_ref, v_ref.at[(2 * j) // heads_per_kv]))
        for j, o in enumerate(_attend_blocks(blocks, n_keys)):
            g = (2 * j) // heads_per_kv
            den_lane = (1 - g) * HEAD_DIM
            o = o / o[:, den_lane:den_lane + 1]
            even, odd = o[:ROW], o[ROW:]
            even = even if g == 0 else pltpu.roll(even, HEAD_DIM, 1)
            odd = odd if g == 1 else pltpu.roll(odd, HEAD_DIM, 1)
            o_ref[:, j * LANES:(j + 1) * LANES] = jnp.where(lane < HEAD_DIM, even, odd).astype(BF16)

    _per_tile_keys(body, k_ref.shape[0])


def _diff_kernel(q_ref, k_ref, v_ref, lam_ref, g_ref, o_ref, *, lam_init):
    lv = lam_ref[...]
    lam = (jnp.exp(jnp.sum(lv[0:1] * lv[1:2], axis=-1, keepdims=True))
           - jnp.exp(jnp.sum(lv[2:3] * lv[3:4], axis=-1, keepdims=True)) + lam_init)

    def body(n_keys):
        blocks = [(q_ref[2 * h:2 * h + 2].reshape(2 * ROW, LANES), k_ref.at[h], v_ref.at[h]) for h in range(B_HEADS)]
        for h, o in enumerate(_attend_blocks(blocks, n_keys)):
            o = o[:, :LANES] / o[:, LANES:LANES + 1]
            dif = o[:ROW] - lam * o[ROW:]
            o_ref[:, h * LANES:(h + 1) * LANES] = (_rms(dif, g_ref[...]) * (1.0 - lam_init)).astype(BF16)

    _per_tile_keys(body, k_ref.shape[1])


def _gqa(qaz, ka, va):
    n_batch, _, ts, _ = qaz.shape
    nt = ts // ROW
    return pl.pallas_call(
        _gqa_kernel,
        grid=(n_batch, nt),
        in_specs=[
            pl.BlockSpec((None, A_HEADS, ROW, LANES), lambda b, i: (b, 0, i, 0)),
            pl.BlockSpec((None, ts, LANES), lambda b, i: (b, 0, 0)),
            pl.BlockSpec((None, A_KV_HEADS, ts, LANES), lambda b, i: (b, 0, 0, 0)),
        ],
        out_specs=pl.BlockSpec((None, ROW, A_HEADS * HEAD_DIM), lambda b, i: (b, i, 0)),
        out_shape=jax.ShapeDtypeStruct((n_batch, ts, A_HEADS * HEAD_DIM), BF16),
        compiler_params=_cparams(("parallel", "parallel")),
        name="gqa_attention",
    )(qaz, ka, va)


def _diff(qbz, kb, vb, lam_vecs, sub_g, lam_init):
    n_batch, _, ts, _ = qbz.shape
    nt = ts // ROW
    return pl.pallas_call(
        functools.partial(_diff_kernel, lam_init=lam_init),
        grid=(n_batch, nt),
        in_specs=[
            pl.BlockSpec((None, 2 * B_HEADS, ROW, LANES), lambda b, i: (b, 0, i, 0)),
            pl.BlockSpec((None, B_HEADS, ts, LANES), lambda b, i: (b, 0, 0, 0)),
            pl.BlockSpec((None, B_HEADS, ts, 2 * LANES), lambda b, i: (b, 0, 0, 0)),
            pl.BlockSpec((4, HEAD_DIM), lambda b, i: (0, 0)),
            pl.BlockSpec((1, LANES), lambda b, i: (0, 0)),
        ],
        out_specs=pl.BlockSpec((None, ROW, B_HEADS * LANES), lambda b, i: (b, i, 0)),
        out_shape=jax.ShapeDtypeStruct((n_batch, ts, B_HEADS * LANES), BF16),
        compiler_params=_cparams(("parallel", "parallel")),
        name="diff_attention",
    )(qbz, kb, vb, lam_vecs, sub_g)


AUG = 2 * SUBLANES


def _mlstm_kernel(qtf_ref, kf_ref, vtf_ref, gcf_ref, grf_ref, qtb_ref, kb_ref, vtb_ref, gcb_ref, grb_ref,
                  hf_ref, hb_ref, c_ref, m_ref):
    @pl.when(pl.program_id(1) == 0)
    def _():
        c_ref[...] = jnp.zeros_like(c_ref)
        m_ref[...] = jnp.zeros_like(m_ref)

    length = MCHUNK
    key_idx = _row_iota((length, length))
    qry_idx = _lane_iota((length, length))
    ones_rows = jnp.ones((AUG, length), BF16)
    n_rows = 2 * SUBLANES
    n_chain = 2 * C_HEADS
    c_states = [c_ref[ch] for ch in range(n_chain)]
    m_prevs = [m_ref[ch] for ch in range(n_chain)]
    c_news, m_news, h_outs, pending = [], [], [], []

    for direction, (qt_ref, k_ref, vt_ref, gc_ref, gr_ref, h_ref) in enumerate(
            ((qtf_ref, kf_ref, vtf_ref, gcf_ref, grf_ref, hf_ref), (qtb_ref, kb_ref, vtb_ref, gcb_ref, grb_ref, hb_ref))):
        reverse = direction == 1
        gate_c, cs_c, tot_c = gc_ref[:, :LANES], gc_ref[:, LANES:2 * LANES], gc_ref[:, 2 * LANES:]
        p_c = (tot_c - cs_c + gate_c) if reverse else cs_c
        g_c = gate_c - pltpu.roll(p_c, LANES - C_HEADS, 1)
        mask = (key_idx >= qry_idx) if reverse else (key_idx <= qry_idx)
        for hd in range(C_HEADS):
            chain = direction * C_HEADS + hd
            ii = direction * 2 * C_HEADS + hd
            fi = ii + C_HEADS
            i_row = gr_ref[ii:ii + 1, :]
            f_row = gr_ref[fi:fi + 1, :]
            cs_row = gr_ref[n_rows + fi:n_rows + fi + 1, :]
            tot_row = gr_ref[2 * n_rows + fi:2 * n_rows + fi + 1, :]
            p_row = (tot_row - cs_row + f_row) if reverse else cs_row
            m_prev = m_prevs[chain]
            inter = p_row + m_prev
            log_dt = jnp.where(mask, jnp.broadcast_to(g_c[:, ii:ii + 1], (length, length)) + p_row, NEG)
            m_t = jnp.maximum(inter, jnp.max(log_dt, axis=0, keepdims=True))
            d_t = jnp.exp2(log_dt - m_t)
            a_row = jnp.exp2(inter - m_t)
            sl = slice(hd * C_DIM, (hd + 1) * C_DIM)
            k_h, qt_h = k_ref[:, sl], qt_ref[sl, :]
            vt_aug = jnp.concatenate([vt_ref[sl, :], ones_rows], axis=0)
            s_raw = jnp.dot(k_h, qt_h, preferred_element_type=F32)
            c_state = c_states[chain]
            x_t = jnp.dot(c_state.astype(BF16), qt_h, preferred_element_type=F32)
            w_row = tot_row - p_row + i_row
            m_new = jnp.maximum(tot_row + m_prev, jnp.max(w_row, axis=-1, keepdims=True))
            decay = jnp.exp2(tot_row + m_prev - m_new)
            ws = jnp.exp2(w_row - m_new)
            update = jnp.dot((vt_aug.astype(F32) * ws).astype(BF16), k_h, preferred_element_type=F32)
            c_news.append(decay[:, :C_DIM] * c_state + update)
            m_news.append(m_new)
            pending.append((s_raw, d_t, vt_aug, a_row, x_t, m_t))

    for direction in range(2):
        h_parts = []
        for hd in range(C_HEADS):
            s_raw, d_t, vt_aug, a_row, x_t, m_t = pending[direction * C_HEADS + hd]
            y_t = jnp.dot(vt_aug, (s_raw * d_t).astype(BF16), preferred_element_type=F32)
            num_t = a_row * x_t[:C_DIM] + y_t[:C_DIM]
            den = a_row * x_t[C_DIM:C_DIM + 1] + y_t[C_DIM:C_DIM + 1]
            h_t = num_t / jnp.maximum(jnp.abs(den), jnp.exp2(-m_t))
            h_parts.append(h_t.T)
        h_outs.append(jnp.concatenate(h_parts, axis=1))

    hf_ref[...] = h_outs[0]
    hb_ref[...] = h_outs[1]
    for ch in range(n_chain):
        c_ref[ch] = c_news[ch]
        m_ref[ch] = m_news[ch]


def _mlstm(qmt, km, vmt, gcol, grow):
    n_batch, ts, width = km.shape
    nc = ts // MCHUNK
    ctx_chunks = ROW // MCHUNK

    def bwd(j):
        return jnp.where(j < ctx_chunks, ctx_chunks - 1 - j, nc + ctx_chunks - 1 - j)

    def specs(idx):
        return [
            pl.BlockSpec((None, width, MCHUNK), lambda b, j: (b, 0, idx(j))),
            pl.BlockSpec((None, MCHUNK, width), lambda b, j: (b, idx(j), 0)),
            pl.BlockSpec((None, width, MCHUNK), lambda b, j: (b, 0, idx(j))),
            pl.BlockSpec((None, MCHUNK, 3 * LANES), lambda b, j: (b, idx(j), 0)),
            pl.BlockSpec((None, 6 * SUBLANES, MCHUNK), lambda b, j: (b, 0, idx(j))),
        ]

    fwd = lambda j: j
    n_chain = 2 * C_HEADS
    return pl.pallas_call(
        _mlstm_kernel,
        grid=(n_batch, nc),
        in_specs=specs(fwd) + specs(bwd),
        out_specs=[
            pl.BlockSpec((None, MCHUNK, width), lambda b, j: (b, j, 0)),
            pl.BlockSpec((None, MCHUNK, width), lambda b, j: (b, bwd(j), 0)),
        ],
        out_shape=[jax.ShapeDtypeStruct((n_batch, ts, width), F32)] * 2,
        scratch_shapes=[
            pltpu.VMEM((n_chain, C_DIM + AUG, C_DIM), F32),
            pltpu.VMEM((n_chain, 1, MCHUNK), F32),
        ],
        compiler_params=_cparams(("parallel", "arbitrary")),
        name="mlstm",
    )(qmt, km, vmt, gcol, grow, qmt, km, vmt, gcol, grow)


def _mix_kernel(a_ref, d_ref, hf_ref, hb_ref, oc_ref, gate_ref, x_ref, g1_ref, pg_ref, mg_ref,
                wa_ref, wb_ref, wc_ref, wo_ref, o_ref):
    d_model = x_ref.shape[-1]
    hsum = hf_ref[...] + hb_ref[...]
    mg = mg_ref[...]
    m = jnp.concatenate([_rms(hsum[:, hd * C_DIM:(hd + 1) * C_DIM], mg) for hd in range(C_HEADS)], axis=1)
    m = m * _sigmoid(oc_ref[...].astype(F32))
    u = (_sigmoid(gate_ref[:, :d_model].astype(F32))
         * jnp.dot(a_ref[...], wa_ref[...], preferred_element_type=F32)
         + _sigmoid(gate_ref[:, d_model:2 * d_model].astype(F32))
         * jnp.dot(d_ref[...], wb_ref[...], preferred_element_type=F32)
         + _sigmoid(gate_ref[:, 2 * d_model:].astype(F32))
         * jnp.dot(m.astype(BF16), wc_ref[...], preferred_element_type=F32))
    y = jnp.dot(u.astype(BF16), wo_ref[...], preferred_element_type=F32)
    o_ref[...] = x_ref[...] + g1_ref[...] * _rms(y, pg_ref[...])


def _mix(a, dd, hf, hb, out_gate, merge_gate, xs, mod, layer, nw, post_g, mlstm_g, wa, wb, wc, wo):
    n_batch, ts, d = xs.shape
    nt = ts // ROW
    width = a.shape[-1]

    def tile(w):
        return pl.BlockSpec((None, ROW, w), lambda b, i: (b, i, 0))

    def const(shape):
        return pl.BlockSpec(shape, lambda b, i: (0,) * len(shape))

    return pl.pallas_call(
        _mix_kernel,
        grid=(n_batch, nt),
        in_specs=[
            tile(width), tile(width), tile(width), tile(width),
            tile(width), tile(3 * d),
            tile(d),
            _mod_spec(d, layer, 2, nw, n_batch),
            const((1, d)), const((1, C_DIM)),
            const((width, d)), const((width, d)), const((width, d)), const((d, d)),
        ],
        out_specs=tile(d),
        out_shape=jax.ShapeDtypeStruct((n_batch, ts, d), F32),
        compiler_params=_cparams(("parallel", "parallel")),
        name="mix_out",
    )(a, dd, hf, hb, out_gate, merge_gate, xs, mod, post_g, mlstm_g, wa, wb, wc, wo)


def _ffn_kernel(x_ref, sh_ref, sc_ref, g2_ref, pre_ref, post_ref, wg_ref, wu_ref, wd_ref, o_ref):
    x = x_ref[...]
    xb = (_rms(x, pre_ref[...]) * (1.0 + sc_ref[...]) + sh_ref[...]).astype(BF16)
    gate = jnp.dot(xb, wg_ref[...], preferred_element_type=F32)
    up = jnp.dot(xb, wu_ref[...], preferred_element_type=F32)
    z = jnp.dot((_silu(gate) * up).astype(BF16), wd_ref[...], preferred_element_type=F32)
    o_ref[...] = x + g2_ref[...] * _rms(z, post_ref[...])


def _ffn(xs, mod, layer, nw, pre_g, post_g, wg, wu, wd):
    n_batch, ts, d = xs.shape
    nt = ts // ROW
    dff = wg.shape[1]

    def resident(shape):
        return pl.BlockSpec(shape, lambda b, i: (0,) * len(shape), pipeline_mode=pl.Buffered(1))

    tile = pl.BlockSpec((None, ROW, d), lambda b, i: (b, i, 0))
    return pl.pallas_call(
        _ffn_kernel,
        grid=(n_batch, nt),
        in_specs=[
            tile,
            _mod_spec(d, layer, 3, nw, n_batch), _mod_spec(d, layer, 4, nw, n_batch),
            _mod_spec(d, layer, 5, nw, n_batch),
            pl.BlockSpec((1, d), lambda b, i: (0, 0)), pl.BlockSpec((1, d), lambda b, i: (0, 0)),
            resident((d, dff)), resident((d, dff)), resident((dff, d)),
        ],
        out_specs=tile,
        out_shape=jax.ShapeDtypeStruct((n_batch, ts, d), F32),
        compiler_params=_cparams(("parallel", "parallel")),
        name="ffn",
    )(xs, mod, mod, mod, pre_g, post_g, wg, wu, wd)


def _route_kernel(x_ref, sh_ref, sc_ref, pre_ref, wr_ref, br_ref, tri_ref, xn_ref, meta_ref, cnt_ref, carry_ref):
    @pl.when(jnp.logical_and(pl.program_id(0) == 0, pl.program_id(1) == 0))
    def _():
        carry_ref[...] = jnp.zeros_like(carry_ref)

    xn = _rms(x_ref[...], pre_ref[...]) * (1.0 + sc_ref[...]) + sh_ref[...]
    xn_ref[...] = xn
    lane = _lane_iota((ROW, LANES))
    logits = jnp.dot(xn, wr_ref[...], preferred_element_type=F32, precision=HIGHEST) + br_ref[...]
    logits = jnp.where(lane < N_EXPERTS, logits, NEG)
    v1 = jnp.max(logits, axis=-1, keepdims=True)
    i1 = jnp.min(jnp.where(logits == v1, lane, LANES), axis=-1, keepdims=True)
    rest = jnp.where(lane == i1, NEG, logits)
    v2 = jnp.max(rest, axis=-1, keepdims=True)
    i2 = jnp.min(jnp.where(rest == v2, lane, LANES), axis=-1, keepdims=True)
    e2 = jnp.exp(v2 - v1)
    w1 = 1.0 / (1.0 + e2)
    w2 = e2 / (1.0 + e2)
    assigned = jnp.where(lane == i1, 1.0, jnp.where(lane == i2, 1.0, 0.0))
    before = jnp.dot(tri_ref[...], assigned.astype(BF16), preferred_element_type=F32) + carry_ref[0:1, :]
    r1 = jnp.sum(jnp.where(lane == i1, before, 0.0), axis=-1, keepdims=True)
    r2 = jnp.sum(jnp.where(lane == i2, before, 0.0), axis=-1, keepdims=True)
    carry_ref[...] = carry_ref[...] + jnp.sum(assigned, axis=0, keepdims=True)
    cnt_ref[...] = carry_ref[...]
    fields = (i1.astype(F32), i2.astype(F32), w1, w2, r1, r2)
    meta = jnp.zeros((ROW, LANES), F32)
    for f, val in enumerate(fields):
        meta = jnp.where(lane == f, val, meta)
    meta_ref[...] = meta


def _route(xs, mod, layer, nw, pre_g, w_r, b_r):
    n_batch, ts, d = xs.shape
    nt = ts // ROW
    tile = pl.BlockSpec((None, ROW, d), lambda b, i: (b, i, 0))
    t_idx = jnp.arange(ROW)
    tri_strict = (t_idx[:, None] > t_idx[None, :]).astype(BF16)
    return pl.pallas_call(
        _route_kernel,
        grid=(n_batch, nt),
        in_specs=[
            tile, _mod_spec(d, layer, 3, nw, n_batch), _mod_spec(d, layer, 4, nw, n_batch),
            pl.BlockSpec((1, d), lambda b, i: (0, 0)),
            pl.BlockSpec((d, LANES), lambda b, i: (0, 0)),
            pl.BlockSpec((1, LANES), lambda b, i: (0, 0)),
            pl.BlockSpec((ROW, ROW), lambda b, i: (0, 0)),
        ],
        out_specs=[tile, pl.BlockSpec((None, ROW, LANES), lambda b, i: (b, i, 0)),
                   pl.BlockSpec((SUBLANES, LANES), lambda b, i: (0, 0))],
        out_shape=[jax.ShapeDtypeStruct((n_batch, ts, d), F32),
                   jax.ShapeDtypeStruct((n_batch, ts, LANES), F32),
                   jax.ShapeDtypeStruct((SUBLANES, LANES), F32)],
        scratch_shapes=[pltpu.VMEM((SUBLANES, LANES), F32)],
        compiler_params=_cparams(("arbitrary", "arbitrary")),
        name="route",
    )(xs, mod, mod, pre_g, w_r, b_r, tri_strict)


GROUPS = ROW // SUBLANES


def _start_rows(make_copy):
    for g in range(GROUPS):
        for j in range(SUBLANES):
            for slot in range(2):
                make_copy(g, j, slot).start()


def _wait_rows(make_copy):
    def drain(g, carry):
        for j in range(SUBLANES):
            for slot in range(2):
                make_copy(0, 0, slot).wait()
        return carry

    lax.fori_loop(0, GROUPS, drain, 0)


STAGES = 3


def _dispatch_kernel(pos_ref, ends_ref, xn_ref, out_ref, zero_ref, stage_ref, row_sems, in_sems, zero_sem):
    step = pl.program_id(0)
    n_steps = pl.num_programs(0)
    n_tok = pos_ref.shape[0] // 2
    tm = zero_ref.shape[0]
    n_sorted = out_ref.shape[0]

    @pl.when(step == 0)
    def _():
        zero_ref[...] = jnp.zeros_like(zero_ref)

        def fill(row):
            copy = pltpu.make_async_copy(zero_ref, out_ref.at[pl.ds(pl.multiple_of(row, tm), tm)], zero_sem)
            copy.start()
            copy.wait()

        for e in range(N_EXPERTS):
            prev_end = ends_ref[e - 1] if e > 0 else 0

            @pl.when(ends_ref[e] > prev_end)
            def _():
                fill(ends_ref[e] - tm)

        last_end = ends_ref[N_EXPERTS - 1]

        def tail(k, carry):
            fill(last_end + k * tm)
            return carry

        lax.fori_loop(0, (n_sorted - last_end) // tm, tail, 0)

    def tile_in(tile):
        slot = tile % STAGES
        return pltpu.make_async_copy(xn_ref.at[pl.ds(tile * GROUPS, GROUPS)], stage_ref.at[slot], in_sems.at[slot])

    def copies(tile):
        def make_copy(g, j, slot):
            p = pos_ref[slot * n_tok + tile * ROW + g * SUBLANES + j]
            return pltpu.make_async_copy(stage_ref.at[tile % STAGES, g, pl.ds(j, 1)], out_ref.at[pl.ds(p, 1)],
                                         row_sems.at[tile % STAGES])
        return make_copy

    @pl.when(step == 0)
    def _():
        tile_in(step).start()

    @pl.when(step >= STAGES - 1)
    def _():
        _wait_rows(copies(step - (STAGES - 1)))

    @pl.when(step + 1 < n_steps)
    def _():
        tile_in(step + 1).start()

    tile_in(step).wait()
    _start_rows(copies(step))

    @pl.when(step == n_steps - 1)
    def _():
        for back in range(STAGES - 2, -1, -1):
            @pl.when(step - back >= 0)
            def _():
                _wait_rows(copies(step - back))


def _dispatch(pos, ends, xn, n_sorted, tm):
    m_rows, d = xn.shape
    return pl.pallas_call(
        _dispatch_kernel,
        grid_spec=pltpu.PrefetchScalarGridSpec(
            num_scalar_prefetch=2,
            grid=(m_rows // ROW,),
            in_specs=[pl.BlockSpec(memory_space=pl.ANY)],
            out_specs=pl.BlockSpec(memory_space=pl.ANY),
            scratch_shapes=[pltpu.VMEM((tm, d), F32), pltpu.VMEM((STAGES, GROUPS, SUBLANES, d), F32),
                            pltpu.SemaphoreType.DMA((STAGES,)), pltpu.SemaphoreType.DMA((STAGES,)),
                            pltpu.SemaphoreType.DMA(())],
        ),
        out_shape=jax.ShapeDtypeStruct((n_sorted, d), F32),
        compiler_params=_cparams(("arbitrary",)),
        name="dispatch",
    )(pos, ends, xn.reshape(m_rows // SUBLANES, SUBLANES, d))


def _experts_kernel(te_ref, x_ref, wg_ref, wu_ref, wd_ref, y_ref):
    used = te_ref[pl.program_id(0)] < N_EXPERTS

    @pl.when(used)
    def _():
        x = x_ref[...].astype(BF16)
        gate = jnp.dot(x, wg_ref[...], preferred_element_type=F32)
        up = jnp.dot(x, wu_ref[...], preferred_element_type=F32)
        y_ref[...] = jnp.dot((_silu(gate) * up).astype(BF16), wd_ref[...], preferred_element_type=F32)

    @pl.when(jnp.logical_not(used))
    def _():
        y_ref[...] = jnp.zeros_like(y_ref)


def _experts(tile_expert, x_sorted, wg, wu, wd, tm):
    n_sorted, d = x_sorted.shape
    n_e, _, dff = wg.shape

    def weight(shape):
        return pl.BlockSpec((None,) + shape, lambda i, te: (jnp.minimum(te[i], n_e - 1), 0, 0))

    return pl.pallas_call(
        _experts_kernel,
        grid_spec=pltpu.PrefetchScalarGridSpec(
            num_scalar_prefetch=1,
            grid=(n_sorted // tm,),
            in_specs=[pl.BlockSpec((tm, d), lambda i, te: (i, 0)),
                      weight((d, dff)), weight((d, dff)), weight((dff, d))],
            out_specs=pl.BlockSpec((tm, d), lambda i, te: (i, 0)),
        ),
        out_shape=jax.ShapeDtypeStruct((n_sorted, d), F32),
        compiler_params=_cparams(("arbitrary",)),
        name="experts",
    )(tile_expert, x_sorted, wg, wu, wd)


def _combine_kernel(pos_ref, x_ref, meta_ref, g2_ref, post_ref, y_ref, o_ref, buf_ref, sems, *,
                    tiles_per_sample, latent_only):
    step = pl.program_id(0)
    n_steps = pl.num_programs(0)
    n_tok = pos_ref.shape[0] // 2
    d_model = x_ref.shape[-1]

    def wanted(tile):
        return (tile % tiles_per_sample != 0) if latent_only else (tile >= 0)

    def copies(tile):
        def make_copy(g, j, slot):
            p = pos_ref[slot * n_tok + tile * ROW + g * SUBLANES + j]
            return pltpu.make_async_copy(y_ref.at[pl.ds(p, 1)], buf_ref.at[tile % 2, slot, g, pl.ds(j, 1)],
                                         sems.at[tile % 2])
        return make_copy

    @pl.when(jnp.logical_and(step == 0, wanted(step)))
    def _():
        _start_rows(copies(step))

    @pl.when(jnp.logical_and(step + 1 < n_steps, wanted(step + 1)))
    def _():
        _start_rows(copies(step + 1))

    @pl.when(wanted(step))
    def _():
        _wait_rows(copies(step))
        meta = meta_ref[...]
        cur = step % 2
        y1 = buf_ref[cur, 0].reshape(ROW, d_model)
        y2 = buf_ref[cur, 1].reshape(ROW, d_model)
        z = meta[:, 2:3] * y1 + meta[:, 3:4] * y2
        o_ref[...] = x_ref[...] + g2_ref[...] * _rms(z, post_ref[...])


def _combine(pos, xs, meta, y_sorted, mod, layer, nw, post_g, latent_only):
    n_batch, ts, d = xs.shape
    nt = ts // ROW
    m_rows = n_batch * ts
    base = (layer * 6 + 5) * nw
    tile = pl.BlockSpec((ROW, d), lambda i, pos: (i, 0))
    if latent_only:
        out_rows = n_batch * (ts - ROW)
        out_tile = pl.BlockSpec((ROW, d), lambda i, pos: ((i // nt) * (nt - 1) + jnp.maximum(i % nt - 1, 0), 0))
    else:
        out_rows, out_tile = m_rows, tile
    out = pl.pallas_call(
        functools.partial(_combine_kernel, tiles_per_sample=nt, latent_only=latent_only),
        grid_spec=pltpu.PrefetchScalarGridSpec(
            num_scalar_prefetch=1,
            grid=(m_rows // ROW,),
            in_specs=[
                tile,
                pl.BlockSpec((ROW, LANES), lambda i, pos: (i, 0)),
                pl.BlockSpec((None, 1, d), lambda i, pos: (base + jnp.where(i % nt == 0, n_batch, i // nt), 0, 0)),
                pl.BlockSpec((1, d), lambda i, pos: (0, 0)),
                pl.BlockSpec(memory_space=pl.ANY),
            ],
            out_specs=out_tile,
            scratch_shapes=[pltpu.VMEM((2, 2, GROUPS, SUBLANES, d), F32), pltpu.SemaphoreType.DMA((2,))],
        ),
        out_shape=jax.ShapeDtypeStruct((out_rows, d), F32),
        compiler_params=_cparams(("arbitrary",)),
        name="combine",
    )(pos, xs.reshape(m_rows, d), meta.reshape(m_rows, LANES), mod, post_g, y_sorted)
    return out.reshape(n_batch, out_rows // n_batch, d)


def _moe(xs, mod, layer, nw, pre_g, post_g, w_r, b_r, wg, wu, wd, latent_only, tm=512):
    n_batch, ts, d = xs.shape
    m_rows = n_batch * ts
    xn, meta, cnt = _route(xs, mod, layer, nw, pre_g, w_r, b_r)
    m2 = meta.reshape(m_rows, LANES)
    i1, i2 = m2[:, 0].astype(jnp.int32), m2[:, 1].astype(jnp.int32)
    r1, r2 = m2[:, 4].astype(jnp.int32), m2[:, 5].astype(jnp.int32)
    counts = cnt[0, :N_EXPERTS].astype(jnp.int32)
    padded = -(-counts // tm) * tm
    ends = jnp.cumsum(padded)
    start = ends - padded
    pos = jnp.concatenate([start[i1] + r1, start[i2] + r2])
    n_tiles = 2 * m_rows // tm + N_EXPERTS
    tile_row = jnp.arange(n_tiles, dtype=jnp.int32) * tm
    tile_expert = jnp.sum((ends[None, :] <= tile_row[:, None]).astype(jnp.int32), axis=1)
    x_sorted = _dispatch(pos, ends, xn.reshape(m_rows, d), n_tiles * tm, tm)
    y_sorted = _experts(tile_expert, x_sorted, wg, wu, wd, tm)
    return _combine(pos, xs, meta, y_sorted, mod, layer, nw, post_g, latent_only)


def _rope_tables(n_tok, n_ctx):
    n_freq = HEAD_DIM // 4
    pos = jnp.arange(n_tok)
    row = (pos // GRID_W).astype(F32)
    colp = (pos % GRID_W).astype(F32)
    inv = ROPE_THETA ** (-jnp.arange(n_freq, dtype=F32) / n_freq)
    lane = jnp.arange(LANES)
    in_head = lane % HEAD_DIM
    use_col = (in_head // (HEAD_DIM // 2)) == 1
    freq = inv[in_head % n_freq]
    ang = jnp.where(use_col[None, :], colp[:, None], row[:, None]) * freq[None, :]
    lower = (in_head % (HEAD_DIM // 2)) < n_freq
    cos_t = jnp.cos(ang)
    sin_t = jnp.sin(ang)
    sin_a = jnp.where(lower[None, :], -sin_t, 0.0)
    sin_b = jnp.where(lower[None, :], 0.0, sin_t)
    pad = lambda t, v: jnp.concatenate([jnp.full((n_ctx, LANES), v, F32), t], axis=0)
    return pad(cos_t, 1.0), pad(sin_a, 0.0), pad(sin_b, 0.0)


def _block_diag_mean(width):
    idx = jnp.arange(width) // HEAD_DIM
    return jnp.where(idx[:, None] == idx[None, :], 1.0 / HEAD_DIM, 0.0).astype(BF16)


def _pack_w_in(w):
    a_q = A_HEADS * HEAD_DIM
    a_kv = 2 * A_KV_HEADS * HEAD_DIM
    n_gates = 4 * C_HEADS
    gate_start = w.shape[1] - 3 * w.shape[0]
    g_start = gate_start - n_gates
    main = jnp.concatenate([w[:, gate_start:], w[:, :a_q], w[:, a_q + a_kv:g_start], w[:, a_q:a_q + a_kv]], axis=1)
    gates = jnp.pad(w[:, g_start:gate_start], ((0, 0), (0, LANES - n_gates)))
    return main.astype(BF16), gates.astype(BF16)


def kernel(x, c, ctx, c_ctx, ada_w, ada_b, pre_mix_g, post_mix_g, pre_ffn_g, post_ffn_g, w_in, q_norm_g, k_norm_g, lam_q1, lam_k1, lam_q2, lam_k2, diff_norm_g, conv_w, conv_b, mlstm_gate_b, mlstm_norm_g, w_br_attn, w_br_diff, w_br_mlstm, w_out, w_ff_gate, w_ff_up, w_ff_down, w_router, b_router, w_moe_gate, w_moe_up, w_moe_down):
    n_batch, n_tok, d = x.shape
    n_ctx = ctx.shape[1]
    depth = ada_w.shape[0]
    assert n_ctx == ROW and n_tok % ROW == 0 and d == 1024
    ts = n_ctx + n_tok
    nw = -(-(n_batch + 1) // SUBLANES) * SUBLANES

    c_all = jnp.concatenate([c, c_ctx[None, :], jnp.zeros((nw - n_batch - 1, d), F32)], axis=0)
    mod = _modulation(c_all, ada_w, ada_b)
    tables = _rope_tables(n_tok, n_ctx)
    bd4, bd1 = _block_diag_mean(A_HEADS * HEAD_DIM), _block_diag_mean(LANES)
    t_idx = jnp.arange(MCHUNK)
    tri = (t_idx[:, None] >= t_idx[None, :]).astype(BF16)

    xs = jnp.concatenate([ctx, x], axis=1)
    for l in range(depth):
        lam_init = 0.8 - 0.6 * math.exp(-0.3 * l)
        w_main, w_gate = _pack_w_in(w_in[l])
        qg = jnp.tile(q_norm_g[l], A_HEADS)[None, :]
        kg = jnp.tile(k_norm_g[l], A_KV_HEADS)[None, :]
        gate_b = jnp.pad(mlstm_gate_b[l], (0, LANES - 4 * C_HEADS))[None, :]
        merge_gate, out_gate, qaz, ka, va, qbz, kb, vb, qmt, km, vmt, gcol, grow = _inproj(
            xs, mod, l, nw, pre_mix_g[l][None, :], w_main, w_gate,
            tables, qg, kg, bd4, bd1, conv_w[l], conv_b[l][None, :], gate_b, tri)
        a_out = _gqa(qaz, ka, va)
        lam_vecs = jnp.stack([lam_q1[l], lam_k1[l], lam_q2[l], lam_k2[l]], axis=0)
        d_out = _diff(qbz, kb, vb, lam_vecs, diff_norm_g[l][None, :], lam_init)
        hf, hb = _mlstm(qmt, km, vmt, gcol, grow)
        xs = _mix(a_out, d_out, hf, hb, out_gate, merge_gate, xs, mod, l, nw,
                  post_mix_g[l][None, :], mlstm_norm_g[l][None, :],
                  w_br_attn[l].astype(BF16), w_br_diff[l].astype(BF16), w_br_mlstm[l].astype(BF16),
                  w_out[l].astype(BF16))
        j = l // 2
        if l % 2 == 0:
            xs = _ffn(xs, mod, l, nw, pre_ffn_g[l][None, :], post_ffn_g[l][None, :],
                      w_ff_gate[j].astype(BF16), w_ff_up[j].astype(BF16), w_ff_down[j].astype(BF16))
        else:
            w_r = jnp.pad(w_router[j], ((0, 0), (0, LANES - N_EXPERTS)))
            b_r = jnp.pad(b_router[j], (0, LANES - N_EXPERTS))[None, :]
            xs = _moe(xs, mod, l, nw, pre_ffn_g[l][None, :], post_ffn_g[l][None, :], w_r, b_r,
                      w_moe_gate[j].astype(BF16), w_moe_up[j].astype(BF16), w_moe_down[j].astype(BF16),
                      latent_only=l == depth - 1)
    return xs if xs.shape[1] == n_tok else xs[:, n_ctx:, :]
```

```python
import functools
import math

import jax
import jax.numpy as jnp
from jax import lax
from jax.experimental import pallas as pl
from jax.experimental.pallas import tpu as pltpu

F32 = jnp.float32
BF16 = jnp.bfloat16
HIGHEST = lax.Precision.HIGHEST

EPS = 1e-6
HEAD_DIM = 64
A_HEADS = 8
A_KV_HEADS = 2
B_HEADS = 4
C_HEADS = 4
C_DIM = 128
N_EXPERTS = 8
ROPE_THETA = 10000.0
GRID_W = 64
CONV_W = 3

LANES = 128
SUBLANES = 8
ROW = 256
MCHUNK = 256
LOG2E = math.log2(math.e)
NEG = -1e30
VMEM_LIMIT = 56 * 1024 * 1024

OFF_GATE = 0
REL_QA, REL_QB, REL_KB, REL_VB = 0, 512, 1024, 1536
REL_QC, REL_KC, REL_VC, REL_OC = 2048, 2560, 3072, 3584
REL_KA = 4096
REL_END = 4352


def _cparams(sem):
    return pltpu.CompilerParams(dimension_semantics=sem, vmem_limit_bytes=VMEM_LIMIT)


def _rms(x, g):
    y = x * lax.rsqrt(jnp.mean(x * x, axis=-1, keepdims=True) + EPS)
    return y * g


def _sigmoid(x):
    return 1.0 / (1.0 + jnp.exp(-x))


def _silu(x):
    return x * _sigmoid(x)


def _log_sigmoid(x):
    return jnp.minimum(x, 0.0) - jnp.log(1.0 + jnp.exp(-jnp.abs(x)))


def _lane_iota(shape):
    return lax.broadcasted_iota(jnp.int32, shape, len(shape) - 1)


def _row_iota(shape):
    return lax.broadcasted_iota(jnp.int32, shape, len(shape) - 2)


def _mod_kernel(c_ref, w_ref, b_ref, o_ref):
    c = c_ref[...]
    o_ref[...] = jnp.dot(_silu(c), w_ref[...], preferred_element_type=F32, precision=HIGHEST) + b_ref[...]


def _modulation(c_all, ada_w, ada_b):
    depth, d, _ = ada_w.shape
    nw = c_all.shape[0]
    out = pl.pallas_call(
        _mod_kernel,
        grid=(depth, 6),
        in_specs=[
            pl.BlockSpec((nw, d), lambda l, j: (0, 0)),
            pl.BlockSpec((None, d, d), lambda l, j: (l, 0, j)),
            pl.BlockSpec((None, 1, d), lambda l, j: (l, 0, j)),
        ],
        out_specs=pl.BlockSpec((None, None, nw, d), lambda l, j: (l, j, 0, 0)),
        out_shape=jax.ShapeDtypeStruct((depth, 6, nw, d), F32),
        compiler_params=_cparams(("arbitrary", "arbitrary")),
        name="modulation",
    )(c_all, ada_w, ada_b.reshape(depth, 1, 6 * d))
    return out.reshape(depth * 6 * nw, 1, d)


def _mod_spec(d, layer, chunk, nw, n_batch):
    base = (layer * 6 + chunk) * nw
    return pl.BlockSpec((None, 1, d), lambda b, i: (base + jnp.where(i == 0, n_batch, b), 0, 0))


def _inproj_kernel(x_ref, xprev_ref, xnext_ref, sh_ref, sc_ref, g_ref, w_ref, wg_ref,
                   cos_ref, sa_ref, sb_ref, qg_ref, kg_ref, bd4_ref, bd1_ref, cw_ref, cb_ref, gb_ref, tri_ref,
                   gate_ref, oc_ref, *mixer_refs):
    d_model = x_ref.shape[-1]

    def normed(x):
        return _rms(x, g_ref[...]) * (1.0 + sc_ref[...]) + sh_ref[...]

    xn = normed(x_ref[...])
    xb = xn.astype(BF16)
    ext = jnp.concatenate([normed(xprev_ref[...]), xn, normed(xnext_ref[...])], axis=0).astype(BF16)

    def proj(lhs, start, width):
        return jnp.dot(lhs, w_ref[:, start:start + width], preferred_element_type=F32)

    base = 3 * d_model
    half = C_HEADS * C_DIM
    qab = proj(xb, base + REL_QA, 2 * half)
    kvb = proj(xb, base + REL_KB, 2 * half)
    qkc = proj(ext, base + REL_QC, 2 * half)
    vo = proj(xb, base + REL_VC, 2 * half)
    oc_ref[...] = vo[:, half:].astype(BF16)
    kava = proj(xb, base + REL_KA, 2 * LANES)
    gates = jnp.dot(xb, wg_ref[...], preferred_element_type=F32)
    _prep_math(pl.program_id(1), pl.num_programs(1),
               qab[:, :half], kava, qab[:, half:], kvb[:, :half], kvb[:, half:],
               qkc[SUBLANES:SUBLANES + ROW], qkc[SUBLANES - 1:SUBLANES], qkc[SUBLANES + ROW:SUBLANES + ROW + 1],
               vo[:, :half], gates,
               cos_ref, sa_ref, sb_ref, qg_ref, kg_ref, bd4_ref, bd1_ref, cw_ref, cb_ref, gb_ref, tri_ref,
               *mixer_refs)
    for c in range(6):
        gate_ref[:, c * half:(c + 1) * half] = proj(xb, c * half, half).astype(BF16)


def _inproj(xs, mod, layer, nw, pre_g, w_main, w_gate, tables, qg, kg, bd4, bd1, conv_w, conv_b, gate_b, tri):
    n_batch, ts, d = xs.shape
    n_main = w_main.shape[1]
    nt = ts // ROW
    cos_t, sa_t, sb_t = tables
    row_blocks = ts // SUBLANES
    per_tile = ROW // SUBLANES
    half = C_HEADS * C_DIM

    def const(shape):
        return pl.BlockSpec(shape, lambda b, i: (0,) * len(shape))

    def rows(width):
        return pl.BlockSpec((None, ROW, width), lambda b, i: (b, i, 0))

    def heads(n, width):
        return pl.BlockSpec((None, n, ROW, width), lambda b, i: (b, 0, i, 0))

    def cols(height):
        return pl.BlockSpec((None, height, ROW), lambda b, i: (b, 0, i))

    table = pl.BlockSpec((ROW, LANES), lambda b, i: (i, 0))
    in_specs = [
        rows(d),
        pl.BlockSpec((None, SUBLANES, d), lambda b, i: (b, jnp.maximum(i * per_tile - 1, 0), 0)),
        pl.BlockSpec((None, SUBLANES, d), lambda b, i: (b, jnp.minimum((i + 1) * per_tile, row_blocks - 1), 0)),
        _mod_spec(d, layer, 0, nw, n_batch),
        _mod_spec(d, layer, 1, nw, n_batch),
        const((1, d)),
        pl.BlockSpec((d, n_main), lambda b, i: (0, 0), pipeline_mode=pl.Buffered(1)),
        const((d, LANES)),
        table, table, table,
        const((1, half)), const((1, LANES)), const((half, half)), const((LANES, LANES)),
        const((CONV_W, 2 * half)), const((1, 2 * half)), const((1, LANES)), const((MCHUNK, MCHUNK)),
    ]
    outs = [
        (rows(3 * d), (ts, 3 * d), BF16),
        (rows(half), (ts, half), BF16),
        (heads(A_HEADS, LANES), (A_HEADS, ts, LANES), BF16),
        (rows(LANES), (ts, LANES), BF16),
        (heads(A_KV_HEADS, LANES), (A_KV_HEADS, ts, LANES), BF16),
        (heads(2 * B_HEADS, LANES), (2 * B_HEADS, ts, LANES), BF16),
        (heads(B_HEADS, LANES), (B_HEADS, ts, LANES), BF16),
        (heads(B_HEADS, 2 * LANES), (B_HEADS, ts, 2 * LANES), BF16),
        (cols(half), (half, ts), BF16),
        (rows(half), (ts, half), BF16),
        (cols(half), (half, ts), BF16),
        (rows(3 * LANES), (ts, 3 * LANES), F32),
        (cols(6 * SUBLANES), (6 * SUBLANES, ts), F32),
    ]
    return pl.pallas_call(
        _inproj_kernel,
        grid=(n_batch, nt),
        in_specs=in_specs,
        out_specs=[spec for spec, _, _ in outs],
        out_shape=[jax.ShapeDtypeStruct((n_batch,) + shape, dtype) for _, shape, dtype in outs],
        compiler_params=_cparams(("parallel", "parallel")),
        name="inproj",
    )(xs, xs, xs, mod, mod, pre_g, w_main, w_gate, cos_t, sa_t, sb_t, qg, kg, bd4, bd1, conv_w, conv_b, gate_b, tri)


def _head_mean_sq(x, bd_ref):
    return jnp.dot((x * x).astype(BF16), bd_ref[...], preferred_element_type=F32)


def _prep_math(i, nt, qa, kava, qb, kb, vb, cur, prev_row, next_row, vc, gates,
               cos_ref, sa_ref, sb_ref, qg_ref, kg_ref, bd4_ref, bd1_ref, cw_ref, cb_ref, gb_ref, tri_ref,
               qaz_ref, ka_ref, va_ref, qbz_ref, kbo_ref, vbo_ref, qmt_ref, km_ref, vmt_ref, gcol_ref, grow_ref):
    cos, sin_a, sin_b = cos_ref[...], sa_ref[...], sb_ref[...]

    def rope(x):
        width = x.shape[1]
        reps = width // LANES
        c = jnp.concatenate([cos] * reps, axis=1) if reps > 1 else cos
        a = jnp.concatenate([sin_a] * reps, axis=1) if reps > 1 else sin_a
        b = jnp.concatenate([sin_b] * reps, axis=1) if reps > 1 else sin_b
        return x * c + pltpu.roll(x, width - 16, 1) * a + pltpu.roll(x, 16, 1) * b

    lane = _lane_iota((ROW, LANES))
    ones = jnp.ones((ROW, LANES), BF16)
    scale = HEAD_DIM ** -0.5 * LOG2E

    qa = qa * lax.rsqrt(_head_mean_sq(qa, bd4_ref) + EPS) * qg_ref[...]
    qa = rope(qa) * scale
    heads_per_kv = A_HEADS // A_KV_HEADS
    for h in range(A_HEADS):
        g = h // heads_per_kv
        blk = qa[:, (h // 2) * LANES:(h // 2 + 1) * LANES]
        if h % 2 != g:
            blk = pltpu.roll(blk, HEAD_DIM, 1)
        qaz_ref[h] = jnp.where(lane // HEAD_DIM == g, blk, 0.0).astype(BF16)
    ka = kava[:, :LANES]
    ka = ka * lax.rsqrt(_head_mean_sq(ka, bd1_ref) + EPS) * kg_ref[...]
    ka_ref[...] = rope(ka).astype(BF16)
    va = kava[:, LANES:].astype(BF16)
    for g in range(A_KV_HEADS):
        va_ref[g] = jnp.where(lane // HEAD_DIM == g, va, ones)

    qb = rope(qb) * scale
    kb = rope(kb)
    for h in range(B_HEADS):
        blk = qb[:, h * LANES:(h + 1) * LANES]
        for m in range(2):
            qbz_ref[2 * h + m] = jnp.where(lane // HEAD_DIM == m, blk, 0.0).astype(BF16)
        kbo_ref[h] = kb[:, h * LANES:(h + 1) * LANES].astype(BF16)
        vbo_ref[h, :, :LANES] = vb[:, h * LANES:(h + 1) * LANES].astype(BF16)
        vbo_ref[h, :, LANES:] = ones

    row = _row_iota(cur.shape)
    prev_row = jnp.where(i >= 2, prev_row, 0.0)
    next_row = jnp.where(jnp.logical_and(i >= 1, i < nt - 1), next_row, 0.0)
    up = jnp.where(row == 0, prev_row, pltpu.roll(cur, 1, 0))
    dn = jnp.where(row == ROW - 1, next_row, pltpu.roll(cur, ROW - 1, 0))
    y = up * cw_ref[0:1, :] + cur * cw_ref[1:2, :] + dn * cw_ref[2:3, :] + cb_ref[...]
    y = _silu(y)
    half = C_HEADS * C_DIM
    qmt_ref[...] = y[:, :half].T.astype(BF16)
    km_ref[...] = (y[:, half:] * (C_DIM ** -0.5)).astype(BF16)
    vmt_ref[...] = vc.T.astype(BF16)

    gg = gates + gb_ref[...]
    is_forget = (lane // C_HEADS) % 2 == 1
    gl = jnp.where(is_forget, _log_sigmoid(gg), gg) * LOG2E
    tri = tri_ref[...]
    n_rows = 2 * SUBLANES
    for c in range(ROW // MCHUNK):
        rows = slice(c * MCHUNK, (c + 1) * MCHUNK)
        glc = gl[rows]
        hi = glc.astype(BF16)
        rest = glc - hi.astype(F32)
        mid = rest.astype(BF16)
        low = (rest - mid.astype(F32)).astype(BF16)
        cs = (jnp.dot(tri, hi, preferred_element_type=F32) + jnp.dot(tri, mid, preferred_element_type=F32)
              + jnp.dot(tri, low, preferred_element_type=F32))
        tot = jnp.broadcast_to(jnp.sum(glc, axis=0, keepdims=True), glc.shape)
        for f, val in enumerate((glc, cs, tot)):
            gcol_ref[rows, f * LANES:(f + 1) * LANES] = val
            grow_ref[f * n_rows:(f + 1) * n_rows, rows] = val.T[:n_rows, :]


def _attend_blocks(blocks, n_keys):
    def scores(q, k_ref):
        return lax.dot_general(q, k_ref[:n_keys, :], (((1,), (1,)), ((), ())), preferred_element_type=F32)

    def weighted(s, v_ref):
        p = jnp.exp2(s - jnp.max(s, axis=-1, keepdims=True)).astype(BF16)
        return jnp.dot(p, v_ref[:n_keys, :], preferred_element_type=F32)

    outs = []
    s_cur = scores(blocks[0][0], blocks[0][1])
    for j in range(1, len(blocks)):
        s_next = scores(blocks[j][0], blocks[j][1])
        outs.append(weighted(s_cur, blocks[j - 1][2]))
        s_cur = s_next
    outs.append(weighted(s_cur, blocks[-1][2]))
    return outs


def _per_tile_keys(body, n_all):
    @pl.when(pl.program_id(1) == 0)
    def _():
        body(ROW)

    @pl.when(pl.program_id(1) > 0)
    def _():
        body(n_all)


def _gqa_kernel(q_ref, k_ref, v_ref, o_ref):
    heads_per_kv = A_HEADS // A_KV_HEADS
    lane = _lane_iota((ROW, LANES))

    def body(n_keys):
        blocks = []
        for j in range(A_HEADS // 2):
            q = q_ref[2 * j:2 * j + 2].reshape(2 * ROW, LANES)
            blocks.append((q, k_ref, v_ref.at[(2 * j) // heads_per_kv]))
        for j, o in enumerate(_attend_blocks(blocks, n_keys)):
            g = (2 * j) // heads_per_kv
            den_lane = (1 - g) * HEAD_DIM
            o = o / o[:, den_lane:den_lane + 1]
            even, odd = o[:ROW], o[ROW:]
            even = even if g == 0 else pltpu.roll(even, HEAD_DIM, 1)
            odd = odd if g == 1 else pltpu.roll(odd, HEAD_DIM, 1)
            o_ref[:, j * LANES:(j + 1) * LANES] = jnp.where(lane < HEAD_DIM, even, odd).astype(BF16)

    _per_tile_keys(body, k_ref.shape[0])


def _diff_kernel(q_ref, k_ref, v_ref, lam_ref, g_ref, o_ref, *, lam_init):
    lv = lam_ref[...]
    lam = (jnp.exp(jnp.sum(lv[0:1] * lv[1:2], axis=-1, keepdims=True))
           - jnp.exp(jnp.sum(lv[2:3] * lv[3:4], axis=-1, keepdims=True)) + lam_init)

    def body(n_keys):
        blocks = [(q_ref[2 * h:2 * h + 2].reshape(2 * ROW, LANES), k_ref.at[h], v_ref.at[h]) for h in range(B_HEADS)]
        for h, o in enumerate(_attend_blocks(blocks, n_keys)):
            o = o[:, :LANES] / o[:, LANES:LANES + 1]
            dif = o[:ROW] - lam * o[ROW:]
            o_ref[:, h * LANES:(h + 1) * LANES] = (_rms(dif, g_ref[...]) * (1.0 - lam_init)).astype(BF16)

    _per_tile_keys(body, k_ref.shape[1])


def _gqa(qaz, ka, va):
    n_batch, _, ts, _ = qaz.shape
    nt = ts // ROW
    return pl.pallas_call(
        _gqa_kernel,
        grid=(n_batch, nt),
        in_specs=[
            pl.BlockSpec((None, A_HEADS, ROW, LANES), lambda b, i: (b, 0, i, 0)),
            pl.BlockSpec((None, ts, LANES), lambda b, i: (b, 0, 0)),
            pl.BlockSpec((None, A_KV_HEADS, ts, LANES), lambda b, i: (b, 0, 0, 0)),
        ],
        out_specs=pl.BlockSpec((None, ROW, A_HEADS * HEAD_DIM), lambda b, i: (b, i, 0)),
        out_shape=jax.ShapeDtypeStruct((n_batch, ts, A_HEADS * HEAD_DIM), BF16),
        compiler_params=_cparams(("parallel", "parallel")),
        name="gqa_attention",
    )(qaz, ka, va)


def _diff(qbz, kb, vb, lam_vecs, sub_g, lam_init):
    n_batch, _, ts, _ = qbz.shape
    nt = ts // ROW
    return pl.pallas_call(
        functools.partial(_diff_kernel, lam_init=lam_init),
        grid=(n_batch, nt),
        in_specs=[
            pl.BlockSpec((None, 2 * B_HEADS, ROW, LANES), lambda b, i: (b, 0, i, 0)),
            pl.BlockSpec((None, B_HEADS, ts, LANES), lambda b, i: (b, 0, 0, 0)),
            pl.BlockSpec((None, B_HEADS, ts, 2 * LANES), lambda b, i: (b, 0, 0, 0)),
            pl.BlockSpec((4, HEAD_DIM), lambda b, i: (0, 0)),
            pl.BlockSpec((1, LANES), lambda b, i: (0, 0)),
        ],
        out_specs=pl.BlockSpec((None, ROW, B_HEADS * LANES), lambda b, i: (b, i, 0)),
        out_shape=jax.ShapeDtypeStruct((n_batch, ts, B_HEADS * LANES), BF16),
        compiler_params=_cparams(("parallel", "parallel")),
        name="diff_attention",
    )(qbz, kb, vb, lam_vecs, sub_g)


AUG = 2 * SUBLANES


def _mlstm_kernel(qtf_ref, kf_ref, vtf_ref, gcf_ref, grf_ref, qtb_ref, kb_ref, vtb_ref, gcb_ref, grb_ref,
                  hf_ref, hb_ref, c_ref, m_ref):
    @pl.when(pl.program_id(1) == 0)
    def _():
        c_ref[...] = jnp.zeros_like(c_ref)
        m_ref[...] = jnp.zeros_like(m_ref)

    length = MCHUNK
    key_idx = _row_iota((length, length))
    qry_idx = _lane_iota((length, length))
    ones_rows = jnp.ones((AUG, length), BF16)
    n_rows = 2 * SUBLANES
    n_chain = 2 * C_HEADS
    c_states = [c_ref[ch] for ch in range(n_chain)]
    m_prevs = [m_ref[ch] for ch in range(n_chain)]
    c_news, m_news, h_outs, pending = [], [], [], []

    for direction, (qt_ref, k_ref, vt_ref, gc_ref, gr_ref, h_ref) in enumerate(
            ((qtf_ref, kf_ref, vtf_ref, gcf_ref, grf_ref, hf_ref), (qtb_ref, kb_ref, vtb_ref, gcb_ref, grb_ref, hb_ref))):
        reverse = direction == 1
        gate_c, cs_c, tot_c = gc_ref[:, :LANES], gc_ref[:, LANES:2 * LANES], gc_ref[:, 2 * LANES:]
        p_c = (tot_c - cs_c + gate_c) if reverse else cs_c
        g_c = gate_c - pltpu.roll(p_c, LANES - C_HEADS, 1)
        mask = (key_idx >= qry_idx) if reverse else (key_idx <= qry_idx)
        for hd in range(C_HEADS):
            chain = direction * C_HEADS + hd
            ii = direction * 2 * C_HEADS + hd
            fi = ii + C_HEADS
            i_row = gr_ref[ii:ii + 1, :]
            f_row = gr_ref[fi:fi + 1, :]
            cs_row = gr_ref[n_rows + fi:n_rows + fi + 1, :]
            tot_row = gr_ref[2 * n_rows + fi:2 * n_rows + fi + 1, :]
            p_row = (tot_row - cs_row + f_row) if reverse else cs_row
            m_prev = m_prevs[chain]
            inter = p_row + m_prev
            log_dt = jnp.where(mask, jnp.broadcast_to(g_c[:, ii:ii + 1], (length, length)) + p_row, NEG)
            m_t = jnp.maximum(inter, jnp.max(log_dt, axis=0, keepdims=True))
            d_t = jnp.exp2(log_dt - m_t)
            a_row = jnp.exp2(inter - m_t)
            sl = slice(hd * C_DIM, (hd + 1) * C_DIM)
            k_h, qt_h = k_ref[:, sl], qt_ref[sl, :]
            vt_aug = jnp.concatenate([vt_ref[sl, :], ones_rows], axis=0)
            s_raw = jnp.dot(k_h, qt_h, preferred_element_type=F32)
            c_state = c_states[chain]
            x_t = jnp.dot(c_state.astype(BF16), qt_h, preferred_element_type=F32)
            w_row = tot_row - p_row + i_row
            m_new = jnp.maximum(tot_row + m_prev, jnp.max(w_row, axis=-1, keepdims=True))
            decay = jnp.exp2(tot_row + m_prev - m_new)
            ws = jnp.exp2(w_row - m_new)
            update = jnp.dot((vt_aug.astype(F32) * ws).astype(BF16), k_h, preferred_element_type=F32)
            c_news.append(decay[:, :C_DIM] * c_state + update)
            m_news.append(m_new)
            pending.append((s_raw, d_t, vt_aug, a_row, x_t, m_t))

    for direction in range(2):
        h_parts = []
        for hd in range(C_HEADS):
            s_raw, d_t, vt_aug, a_row, x_t, m_t = pending[direction * C_HEADS + hd]
            y_t = jnp.dot(vt_aug, (s_raw * d_t).astype(BF16), preferred_element_type=F32)
            num_t = a_row * x_t[:C_DIM] + y_t[:C_DIM]
            den = a_row * x_t[C_DIM:C_DIM + 1] + y_t[C_DIM:C_DIM + 1]
            h_t = num_t / jnp.maximum(jnp.abs(den), jnp.exp2(-m_t))
            h_parts.append(h_t.T)
        h_outs.append(jnp.concatenate(h_parts, axis=1))

    hf_ref[...] = h_outs[0]
    hb_ref[...] = h_outs[1]
    for ch in range(n_chain):
        c_ref[ch] = c_news[ch]
        m_ref[ch] = m_news[ch]


def _mlstm(qmt, km, vmt, gcol, grow):
    n_batch, ts, width = km.shape
    nc = ts // MCHUNK
    ctx_chunks = ROW // MCHUNK

    def bwd(j):
        return jnp.where(j < ctx_chunks, ctx_chunks - 1 - j, nc + ctx_chunks - 1 - j)

    def specs(idx):
        return [
            pl.BlockSpec((None, width, MCHUNK), lambda b, j: (b, 0, idx(j))),
            pl.BlockSpec((None, MCHUNK, width), lambda b, j: (b, idx(j), 0)),
            pl.BlockSpec((None, width, MCHUNK), lambda b, j: (b, 0, idx(j))),
            pl.BlockSpec((None, MCHUNK, 3 * LANES), lambda b, j: (b, idx(j), 0)),
            pl.BlockSpec((None, 6 * SUBLANES, MCHUNK), lambda b, j: (b, 0, idx(j))),
        ]

    fwd = lambda j: j
    n_chain = 2 * C_HEADS
    return pl.pallas_call(
        _mlstm_kernel,
        grid=(n_batch, nc),
        in_specs=specs(fwd) + specs(bwd),
        out_specs=[
            pl.BlockSpec((None, MCHUNK, width), lambda b, j: (b, j, 0)),
            pl.BlockSpec((None, MCHUNK, width), lambda b, j: (b, bwd(j), 0)),
        ],
        out_shape=[jax.ShapeDtypeStruct((n_batch, ts, width), F32)] * 2,
        scratch_shapes=[
            pltpu.VMEM((n_chain, C_DIM + AUG, C_DIM), F32),
            pltpu.VMEM((n_chain, 1, MCHUNK), F32),
        ],
        compiler_params=_cparams(("parallel", "arbitrary")),
        name="mlstm",
    )(qmt, km, vmt, gcol, grow, qmt, km, vmt, gcol, grow)


def _mix_kernel(a_ref, d_ref, hf_ref, hb_ref, oc_ref, gate_ref, x_ref, g1_ref, pg_ref, mg_ref,
                wa_ref, wb_ref, wc_ref, wo_ref, o_ref):
    d_model = x_ref.shape[-1]
    hsum = hf_ref[...] + hb_ref[...]
    mg = mg_ref[...]
    m = jnp.concatenate([_rms(hsum[:, hd * C_DIM:(hd + 1) * C_DIM], mg) for hd in range(C_HEADS)], axis=1)
    m = m * _sigmoid(oc_ref[...].astype(F32))
    u = (_sigmoid(gate_ref[:, :d_model].astype(F32))
         * jnp.dot(a_ref[...], wa_ref[...], preferred_element_type=F32)
         + _sigmoid(gate_ref[:, d_model:2 * d_model].astype(F32))
         * jnp.dot(d_ref[...], wb_ref[...], preferred_element_type=F32)
         + _sigmoid(gate_ref[:, 2 * d_model:].astype(F32))
         * jnp.dot(m.astype(BF16), wc_ref[...], preferred_element_type=F32))
    y = jnp.dot(u.astype(BF16), wo_ref[...], preferred_element_type=F32)
    o_ref[...] = x_ref[...] + g1_ref[...] * _rms(y, pg_ref[...])


def _mix(a, dd, hf, hb, out_gate, merge_gate, xs, mod, layer, nw, post_g, mlstm_g, wa, wb, wc, wo):
    n_batch, ts, d = xs.shape
    nt = ts // ROW
    width = a.shape[-1]

    def tile(w):
        return pl.BlockSpec((None, ROW, w), lambda b, i: (b, i, 0))

    def const(shape):
        return pl.BlockSpec(shape, lambda b, i: (0,) * len(shape))

    return pl.pallas_call(
        _mix_kernel,
        grid=(n_batch, nt),
        in_specs=[
            tile(width), tile(width), tile(width), tile(width),
            tile(width), tile(3 * d),
            tile(d),
            _mod_spec(d, layer, 2, nw, n_batch),
            const((1, d)), const((1, C_DIM)),
            const((width, d)), const((width, d)), const((width, d)), const((d, d)),
        ],
        out_specs=tile(d),
        out_shape=jax.ShapeDtypeStruct((n_batch, ts, d), F32),
        compiler_params=_cparams(("parallel", "parallel")),
        name="mix_out",
    )(a, dd, hf, hb, out_gate, merge_gate, xs, mod, post_g, mlstm_g, wa, wb, wc, wo)


def _ffn_kernel(x_ref, sh_ref, sc_ref, g2_ref, pre_ref, post_ref, wg_ref, wu_ref, wd_ref, o_ref):
    x = x_ref[...]
    xb = (_rms(x, pre_ref[...]) * (1.0 + sc_ref[...]) + sh_ref[...]).astype(BF16)
    gate = jnp.dot(xb, wg_ref[...], preferred_element_type=F32)
    up = jnp.dot(xb, wu_ref[...], preferred_element_type=F32)
    z = jnp.dot((_silu(gate) * up).astype(BF16), wd_ref[...], preferred_element_type=F32)
    o_ref[...] = x + g2_ref[...] * _rms(z, post_ref[...])


def _ffn(xs, mod, layer, nw, pre_g, post_g, wg, wu, wd):
    n_batch, ts, d = xs.shape
    nt = ts // ROW
    dff = wg.shape[1]

    def resident(shape):
        return pl.BlockSpec(shape, lambda b, i: (0,) * len(shape), pipeline_mode=pl.Buffered(1))

    tile = pl.BlockSpec((None, ROW, d), lambda b, i: (b, i, 0))
    return pl.pallas_call(
        _ffn_kernel,
        grid=(n_batch, nt),
        in_specs=[
            tile,
            _mod_spec(d, layer, 3, nw, n_batch), _mod_spec(d, layer, 4, nw, n_batch),
            _mod_spec(d, layer, 5, nw, n_batch),
            pl.BlockSpec((1, d), lambda b, i: (0, 0)), pl.BlockSpec((1, d), lambda b, i: (0, 0)),
            resident((d, dff)), resident((d, dff)), resident((dff, d)),
        ],
        out_specs=tile,
        out_shape=jax.ShapeDtypeStruct((n_batch, ts, d), F32),
        compiler_params=_cparams(("parallel", "parallel")),
        name="ffn",
    )(xs, mod, mod, mod, pre_g, post_g, wg, wu, wd)


def _route_kernel(x_ref, sh_ref, sc_ref, pre_ref, wr_ref, br_ref, tri_ref, xn_ref, meta_ref, cnt_ref, carry_ref):
    @pl.when(jnp.logical_and(pl.program_id(0) == 0, pl.program_id(1) == 0))
    def _():
        carry_ref[...] = jnp.zeros_like(carry_ref)

    xn = _rms(x_ref[...], pre_ref[...]) * (1.0 + sc_ref[...]) + sh_ref[...]
    xn_ref[...] = xn
    lane = _lane_iota((ROW, LANES))
    logits = jnp.dot(xn, wr_ref[...], preferred_element_type=F32, precision=HIGHEST) + br_ref[...]
    logits = jnp.where(lane < N_EXPERTS, logits, NEG)
    v1 = jnp.max(logits, axis=-1, keepdims=True)
    i1 = jnp.min(jnp.where(logits == v1, lane, LANES), axis=-1, keepdims=True)
    rest = jnp.where(lane == i1, NEG, logits)
    v2 = jnp.max(rest, axis=-1, keepdims=True)
    i2 = jnp.min(jnp.where(rest == v2, lane, LANES), axis=-1, keepdims=True)
    e2 = jnp.exp(v2 - v1)
    w1 = 1.0 / (1.0 + e2)
    w2 = e2 / (1.0 + e2)
    assigned = jnp.where(lane == i1, 1.0, jnp.where(lane == i2, 1.0, 0.0))
    before = jnp.dot(tri_ref[...], assigned.astype(BF16), preferred_element_type=F32) + carry_ref[0:1, :]
    r1 = jnp.sum(jnp.where(lane == i1, before, 0.0), axis=-1, keepdims=True)
    r2 = jnp.sum(jnp.where(lane == i2, before, 0.0), axis=-1, keepdims=True)
    carry_ref[...] = carry_ref[...] + jnp.sum(assigned, axis=0, keepdims=True)
    cnt_ref[...] = carry_ref[...]
    fields = (i1.astype(F32), i2.astype(F32), w1, w2, r1, r2)
    meta = jnp.zeros((ROW, LANES), F32)
    for f, val in enumerate(fields):
        meta = jnp.where(lane == f, val, meta)
    meta_ref[...] = meta


def _route(xs, mod, layer, nw, pre_g, w_r, b_r):
    n_batch, ts, d = xs.shape
    nt = ts // ROW
    tile = pl.BlockSpec((None, ROW, d), lambda b, i: (b, i, 0))
    t_idx = jnp.arange(ROW)
    tri_strict = (t_idx[:, None] > t_idx[None, :]).astype(BF16)
    return pl.pallas_call(
        _route_kernel,
        grid=(n_batch, nt),
        in_specs=[
            tile, _mod_spec(d, layer, 3, nw, n_batch), _mod_spec(d, layer, 4, nw, n_batch),
            pl.BlockSpec((1, d), lambda b, i: (0, 0)),
            pl.BlockSpec((d, LANES), lambda b, i: (0, 0)),
            pl.BlockSpec((1, LANES), lambda b, i: (0, 0)),
            pl.BlockSpec((ROW, ROW), lambda b, i: (0, 0)),
        ],
        out_specs=[tile, pl.BlockSpec((None, ROW, LANES), lambda b, i: (b, i, 0)),
                   pl.BlockSpec((SUBLANES, LANES), lambda b, i: (0, 0))],
        out_shape=[jax.ShapeDtypeStruct((n_batch, ts, d), F32),
                   jax.ShapeDtypeStruct((n_batch, ts, LANES), F32),
                   jax.ShapeDtypeStruct((SUBLANES, LANES), F32)],
        scratch_shapes=[pltpu.VMEM((SUBLANES, LANES), F32)],
        compiler_params=_cparams(("arbitrary", "arbitrary")),
        name="route",
    )(xs, mod, mod, pre_g, w_r, b_r, tri_strict)


GROUPS = ROW // SUBLANES


def _start_rows(make_copy):
    for g in range(GROUPS):
        for j in range(SUBLANES):
            for slot in range(2):
                make_copy(g, j, slot).start(priority=slot)


def _wait_rows(make_copy):
    def drain(g, carry):
        for j in range(SUBLANES):
            for slot in range(2):
                make_copy(0, 0, slot).wait()
        return carry

    lax.fori_loop(0, GROUPS, drain, 0)


STAGES = 3


def _dispatch_kernel(pos_ref, ends_ref, xn_ref, out_ref, zero_ref, stage_ref, row_sems, in_sems, zero_sem):
    step = pl.program_id(0)
    n_steps = pl.num_programs(0)
    n_tok = pos_ref.shape[0] // 2
    tm = zero_ref.shape[0]
    n_sorted = out_ref.shape[0]

    @pl.when(step == 0)
    def _():
        zero_ref[...] = jnp.zeros_like(zero_ref)

        def fill(row):
            copy = pltpu.make_async_copy(zero_ref, out_ref.at[pl.ds(pl.multiple_of(row, tm), tm)], zero_sem)
            copy.start()
            copy.wait()

        for e in range(N_EXPERTS):
            prev_end = ends_ref[e - 1] if e > 0 else 0

            @pl.when(ends_ref[e] > prev_end)
            def _():
                fill(ends_ref[e] - tm)

        last_end = ends_ref[N_EXPERTS - 1]

        def tail(k, carry):
            fill(last_end + k * tm)
            return carry

        lax.fori_loop(0, (n_sorted - last_end) // tm, tail, 0)

    def tile_in(tile):
        slot = tile % STAGES
        return pltpu.make_async_copy(xn_ref.at[pl.ds(tile * GROUPS, GROUPS)], stage_ref.at[slot], in_sems.at[slot])

    def copies(tile):
        def make_copy(g, j, slot):
            p = pos_ref[slot * n_tok + tile * ROW + g * SUBLANES + j]
            return pltpu.make_async_copy(stage_ref.at[tile % STAGES, g, pl.ds(j, 1)], out_ref.at[pl.ds(p, 1)],
                                         row_sems.at[tile % STAGES])
        return make_copy

    @pl.when(step == 0)
    def _():
        tile_in(step).start()

    @pl.when(step >= STAGES - 1)
    def _():
        _wait_rows(copies(step - (STAGES - 1)))

    @pl.when(step + 1 < n_steps)
    def _():
        tile_in(step + 1).start()

    tile_in(step).wait()
    _start_rows(copies(step))

    @pl.when(step == n_steps - 1)
    def _():
        for back in range(STAGES - 2, -1, -1):
            @pl.when(step - back >= 0)
            def _():
                _wait_rows(copies(step - back))


def _dispatch(pos, ends, xn, n_sorted, tm):
    m_rows, d = xn.shape
    return pl.pallas_call(
        _dispatch_kernel,
        grid_spec=pltpu.PrefetchScalarGridSpec(
            num_scalar_prefetch=2,
            grid=(m_rows // ROW,),
            in_specs=[pl.BlockSpec(memory_space=pl.ANY)],
            out_specs=pl.BlockSpec(memory_space=pl.ANY),
            scratch_shapes=[pltpu.VMEM((tm, d), F32), pltpu.VMEM((STAGES, GROUPS, SUBLANES, d), F32),
                            pltpu.SemaphoreType.DMA((STAGES,)), pltpu.SemaphoreType.DMA((STAGES,)),
                            pltpu.SemaphoreType.DMA(())],
        ),
        out_shape=jax.ShapeDtypeStruct((n_sorted, d), F32),
        compiler_params=_cparams(("arbitrary",)),
        name="dispatch",
    )(pos, ends, xn.reshape(m_rows // SUBLANES, SUBLANES, d))


def _experts_kernel(te_ref, x_ref, wg_ref, wu_ref, wd_ref, y_ref):
    used = te_ref[pl.program_id(0)] < N_EXPERTS

    @pl.when(used)
    def _():
        x = x_ref[...].astype(BF16)
        gate = jnp.dot(x, wg_ref[...], preferred_element_type=F32)
        up = jnp.dot(x, wu_ref[...], preferred_element_type=F32)
        y_ref[...] = jnp.dot((_silu(gate) * up).astype(BF16), wd_ref[...], preferred_element_type=F32)

    @pl.when(jnp.logical_not(used))
    def _():
        y_ref[...] = jnp.zeros_like(y_ref)


def _experts(tile_expert, x_sorted, wg, wu, wd, tm):
    n_sorted, d = x_sorted.shape
    n_e, _, dff = wg.shape

    def weight(shape):
        return pl.BlockSpec((None,) + shape, lambda i, te: (jnp.minimum(te[i], n_e - 1), 0, 0))

    return pl.pallas_call(
        _experts_kernel,
        grid_spec=pltpu.PrefetchScalarGridSpec(
            num_scalar_prefetch=1,
            grid=(n_sorted // tm,),
            in_specs=[pl.BlockSpec((tm, d), lambda i, te: (i, 0)),
                      weight((d, dff)), weight((d, dff)), weight((dff, d))],
            out_specs=pl.BlockSpec((tm, d), lambda i, te: (i, 0)),
        ),
        out_shape=jax.ShapeDtypeStruct((n_sorted, d), F32),
        compiler_params=_cparams(("arbitrary",)),
        name="experts",
    )(tile_expert, x_sorted, wg, wu, wd)


def _combine_kernel(pos_ref, x_ref, meta_ref, g2_ref, post_ref, y_ref, o_ref, buf_ref, sems, *,
                    tiles_per_sample, latent_only):
    step = pl.program_id(0)
    n_steps = pl.num_programs(0)
    n_tok = pos_ref.shape[0] // 2
    d_model = x_ref.shape[-1]

    def wanted(tile):
        return (tile % tiles_per_sample != 0) if latent_only else (tile >= 0)

    def copies(tile):
        def make_copy(g, j, slot):
            p = pos_ref[slot * n_tok + tile * ROW + g * SUBLANES + j]
            return pltpu.make_async_copy(y_ref.at[pl.ds(p, 1)], buf_ref.at[tile % 2, slot, g, pl.ds(j, 1)],
                                         sems.at[tile % 2])
        return make_copy

    @pl.when(jnp.logical_and(step == 0, wanted(step)))
    def _():
        _start_rows(copies(step))

    @pl.when(jnp.logical_and(step + 1 < n_steps, wanted(step + 1)))
    def _():
        _start_rows(copies(step + 1))

    @pl.when(wanted(step))
    def _():
        _wait_rows(copies(step))
        meta = meta_ref[...]
        cur = step % 2
        y1 = buf_ref[cur, 0].reshape(ROW, d_model)
        y2 = buf_ref[cur, 1].reshape(ROW, d_model)
        z = meta[:, 2:3] * y1 + meta[:, 3:4] * y2
        o_ref[...] = x_ref[...] + g2_ref[...] * _rms(z, post_ref[...])


def _combine(pos, xs, meta, y_sorted, mod, layer, nw, post_g, latent_only):
    n_batch, ts, d = xs.shape
    nt = ts // ROW
    m_rows = n_batch * ts
    base = (layer * 6 + 5) * nw
    tile = pl.BlockSpec((ROW, d), lambda i, pos: (i, 0))
    if latent_only:
        out_rows = n_batch * (ts - ROW)
        out_tile = pl.BlockSpec((ROW, d), lambda i, pos: ((i // nt) * (nt - 1) + jnp.maximum(i % nt - 1, 0), 0))
    else:
        out_rows, out_tile = m_rows, tile
    out = pl.pallas_call(
        functools.partial(_combine_kernel, tiles_per_sample=nt, latent_only=latent_only),
        grid_spec=pltpu.PrefetchScalarGridSpec(
            num_scalar_prefetch=1,
            grid=(m_rows // ROW,),
            in_specs=[
                tile,
                pl.BlockSpec((ROW, LANES), lambda i, pos: (i, 0)),
                pl.BlockSpec((None, 1, d), lambda i, pos: (base + jnp.where(i % nt == 0, n_batch, i // nt), 0, 0)),
                pl.BlockSpec((1, d), lambda i, pos: (0, 0)),
                pl.BlockSpec(memory_space=pl.ANY),
            ],
            out_specs=out_tile,
            scratch_shapes=[pltpu.VMEM((2, 2, GROUPS, SUBLANES, d), F32), pltpu.SemaphoreType.DMA((2,))],
        ),
        out_shape=jax.ShapeDtypeStruct((out_rows, d), F32),
        compiler_params=_cparams(("arbitrary",)),
        name="combine",
    )(pos, xs.reshape(m_rows, d), meta.reshape(m_rows, LANES), mod, post_g, y_sorted)
    return out.reshape(n_batch, out_rows // n_batch, d)


def _moe(xs, mod, layer, nw, pre_g, post_g, w_r, b_r, wg, wu, wd, latent_only, tm=512):
    n_batch, ts, d = xs.shape
    m_rows = n_batch * ts
    xn, meta, cnt = _route(xs, mod, layer, nw, pre_g, w_r, b_r)
    m2 = meta.reshape(m_rows, LANES)
    i1, i2 = m2[:, 0].astype(jnp.int32), m2[:, 1].astype(jnp.int32)
    r1, r2 = m2[:, 4].astype(jnp.int32), m2[:, 5].astype(jnp.int32)
    counts = cnt[0, :N_EXPERTS].astype(jnp.int32)
    padded = -(-counts // tm) * tm
    ends = jnp.cumsum(padded)
    start = ends - padded
    pos = jnp.concatenate([start[i1] + r1, start[i2] + r2])
    n_tiles = 2 * m_rows // tm + N_EXPERTS
    tile_row = jnp.arange(n_tiles, dtype=jnp.int32) * tm
    tile_expert = jnp.sum((ends[None, :] <= tile_row[:, None]).astype(jnp.int32), axis=1)
    x_sorted = _dispatch(pos, ends, xn.reshape(m_rows, d), n_tiles * tm, tm)
    y_sorted = _experts(tile_expert, x_sorted, wg, wu, wd, tm)
    return _combine(pos, xs, meta, y_sorted, mod, layer, nw, post_g, latent_only)


def _rope_tables(n_tok, n_ctx):
    n_freq = HEAD_DIM // 4
    pos = jnp.arange(n_tok)
    row = (pos // GRID_W).astype(F32)
    colp = (pos % GRID_W).astype(F32)
    inv = ROPE_THETA ** (-jnp.arange(n_freq, dtype=F32) / n_freq)
    lane = jnp.arange(LANES)
    in_head = lane % HEAD_DIM
    use_col = (in_head // (HEAD_DIM // 2)) == 1
    freq = inv[in_head % n_freq]
    ang = jnp.where(use_col[None, :], colp[:, None], row[:, None]) * freq[None, :]
    lower = (in_head % (HEAD_DIM // 2)) < n_freq
    cos_t = jnp.cos(ang)
    sin_t = jnp.sin(ang)
    sin_a = jnp.where(lower[None, :], -sin_t, 0.0)
    sin_b = jnp.where(lower[None, :], 0.0, sin_t)
    pad = lambda t, v: jnp.concatenate([jnp.full((n_ctx, LANES), v, F32), t], axis=0)
    return pad(cos_t, 1.0), pad(sin_a, 0.0), pad(sin_b, 0.0)


def _block_diag_mean(width):
    idx = jnp.arange(width) // HEAD_DIM
    return jnp.where(idx[:, None] == idx[None, :], 1.0 / HEAD_DIM, 0.0).astype(BF16)


def _pack_w_in(w):
    a_q = A_HEADS * HEAD_DIM
    a_kv = 2 * A_KV_HEADS * HEAD_DIM
    n_gates = 4 * C_HEADS
    gate_start = w.shape[1] - 3 * w.shape[0]
    g_start = gate_start - n_gates
    main = jnp.concatenate([w[:, gate_start:], w[:, :a_q], w[:, a_q + a_kv:g_start], w[:, a_q:a_q + a_kv]], axis=1)
    gates = jnp.pad(w[:, g_start:gate_start], ((0, 0), (0, LANES - n_gates)))
    return main.astype(BF16), gates.astype(BF16)


def kernel(x, c, ctx, c_ctx, ada_w, ada_b, pre_mix_g, post_mix_g, pre_ffn_g, post_ffn_g, w_in, q_norm_g, k_norm_g, lam_q1, lam_k1, lam_q2, lam_k2, diff_norm_g, conv_w, conv_b, mlstm_gate_b, mlstm_norm_g, w_br_attn, w_br_diff, w_br_mlstm, w_out, w_ff_gate, w_ff_up, w_ff_down, w_router, b_router, w_moe_gate, w_moe_up, w_moe_down):
    n_batch, n_tok, d = x.shape
    n_ctx = ctx.shape[1]
    depth = ada_w.shape[0]
    assert n_ctx == ROW and n_tok % ROW == 0 and d == 1024
    ts = n_ctx + n_tok
    nw = -(-(n_batch + 1) // SUBLANES) * SUBLANES

    c_all = jnp.concatenate([c, c_ctx[None, :], jnp.zeros((nw - n_batch - 1, d), F32)], axis=0)
    mod = _modulation(c_all, ada_w, ada_b)
    tables = _rope_tables(n_tok, n_ctx)
    bd4, bd1 = _block_diag_mean(A_HEADS * HEAD_DIM), _block_diag_mean(LANES)
    t_idx = jnp.arange(MCHUNK)
    tri = (t_idx[:, None] >= t_idx[None, :]).astype(BF16)

    xs = jnp.concatenate([ctx, x], axis=1)
    for l in range(depth):
        lam_init = 0.8 - 0.6 * math.exp(-0.3 * l)
        w_main, w_gate = _pack_w_in(w_in[l])
        qg = jnp.tile(q_norm_g[l], A_HEADS)[None, :]
        kg = jnp.tile(k_norm_g[l], A_KV_HEADS)[None, :]
        gate_b = jnp.pad(mlstm_gate_b[l], (0, LANES - 4 * C_HEADS))[None, :]
        merge_gate, out_gate, qaz, ka, va, qbz, kb, vb, qmt, km, vmt, gcol, grow = _inproj(
            xs, mod, l, nw, pre_mix_g[l][None, :], w_main, w_gate,
            tables, qg, kg, bd4, bd1, conv_w[l], conv_b[l][None, :], gate_b, tri)
        a_out = _gqa(qaz, ka, va)
        lam_vecs = jnp.stack([lam_q1[l], lam_k1[l], lam_q2[l], lam_k2[l]], axis=0)
        d_out = _diff(qbz, kb, vb, lam_vecs, diff_norm_g[l][None, :], lam_init)
        hf, hb = _mlstm(qmt, km, vmt, gcol, grow)
        xs = _mix(a_out, d_out, hf, hb, out_gate, merge_gate, xs, mod, l, nw,
                  post_mix_g[l][None, :], mlstm_norm_g[l][None, :],
                  w_br_attn[l].astype(BF16), w_br_diff[l].astype(BF16), w_br_mlstm[l].astype(BF16),
                  w_out[l].astype(BF16))
        j = l // 2
        if l % 2 == 0:
            xs = _ffn(xs, mod, l, nw, pre_ffn_g[l][None, :], post_ffn_g[l][None, :],
                      w_ff_gate[j].astype(BF16), w_ff_up[j].astype(BF16), w_ff_down[j].astype(BF16))
        else:
            w_r = jnp.pad(w_router[j], ((0, 0), (0, LANES - N_EXPERTS)))
            b_r = jnp.pad(b_router[j], (0, LANES - N_EXPERTS))[None, :]
            xs = _moe(xs, mod, l, nw, pre_ffn_g[l][None, :], post_ffn_g[l][None, :], w_r, b_r,
                      w_moe_gate[j].astype(BF16), w_moe_up[j].astype(BF16), w_moe_down[j].astype(BF16),
                      latent_only=l == depth - 1)
    return xs if xs.shape[1] == n_tok else xs[:, n_ctx:, :]
```

```python
import functools
import math

import jax
import jax.numpy as jnp
from jax import lax
from jax.experimental import pallas as pl
from jax.experimental.pallas import tpu as pltpu

F32 = jnp.float32
BF16 = jnp.bfloat16
HIGHEST = lax.Precision.HIGHEST

EPS = 1e-6
HEAD_DIM = 64
A_HEADS = 8
A_KV_HEADS = 2
B_HEADS = 4
C_HEADS = 4
C_DIM = 128
N_EXPERTS = 8
ROPE_THETA = 10000.0
GRID_W = 64
CONV_W = 3

LANES = 128
SUBLANES = 8
ROW = 256
MCHUNK = 256
LOG2E = math.log2(math.e)
NEG = -1e30
VMEM_LIMIT = 56 * 1024 * 1024

OFF_GATE = 0
REL_QA, REL_QB, REL_KB, REL_VB = 0, 512, 1024, 1536
REL_QC, REL_KC, REL_VC, REL_OC = 2048, 2560, 3072, 3584
REL_KA = 4096
REL_END = 4352


def _cparams(sem):
    return pltpu.CompilerParams(dimension_semantics=sem, vmem_limit_bytes=VMEM_LIMIT)


def _rms(x, g):
    y = x * lax.rsqrt(jnp.mean(x * x, axis=-1, keepdims=True) + EPS)
    return y * g


def _sigmoid(x):
    return 1.0 / (1.0 + jnp.exp(-x))


def _silu(x):
    return x * _sigmoid(x)


def _log_sigmoid(x):
    return jnp.minimum(x, 0.0) - jnp.log(1.0 + jnp.exp(-jnp.abs(x)))


def _lane_iota(shape):
    return lax.broadcasted_iota(jnp.int32, shape, len(shape) - 1)


def _row_iota(shape):
    return lax.broadcasted_iota(jnp.int32, shape, len(shape) - 2)


def _mod_kernel(c_ref, w_ref, b_ref, o_ref):
    c = c_ref[...]
    o_ref[...] = jnp.dot(_silu(c), w_ref[...], preferred_element_type=F32, precision=HIGHEST) + b_ref[...]


def _modulation(c_all, ada_w, ada_b):
    depth, d, _ = ada_w.shape
    nw = c_all.shape[0]
    out = pl.pallas_call(
        _mod_kernel,
        grid=(depth, 6),
        in_specs=[
            pl.BlockSpec((nw, d), lambda l, j: (0, 0)),
            pl.BlockSpec((None, d, d), lambda l, j: (l, 0, j)),
            pl.BlockSpec((None, 1, d), lambda l, j: (l, 0, j)),
        ],
        out_specs=pl.BlockSpec((None, None, nw, d), lambda l, j: (l, j, 0, 0)),
        out_shape=jax.ShapeDtypeStruct((depth, 6, nw, d), F32),
        compiler_params=_cparams(("arbitrary", "arbitrary")),
        name="modulation",
    )(c_all, ada_w, ada_b.reshape(depth, 1, 6 * d))
    return out.reshape(depth * 6 * nw, 1, d)


def _mod_spec(d, layer, chunk, nw, n_batch):
    base = (layer * 6 + chunk) * nw
    return pl.BlockSpec((None, 1, d), lambda b, i: (base + jnp.where(i == 0, n_batch, b), 0, 0))


def _inproj_kernel(x_ref, xprev_ref, xnext_ref, sh_ref, sc_ref, g_ref, w_ref, wg_ref,
                   cos_ref, sa_ref, sb_ref, qg_ref, kg_ref, bd4_ref, bd1_ref, cw_ref, cb_ref, gb_ref, tri_ref,
                   gate_ref, oc_ref, *mixer_refs):
    d_model = x_ref.shape[-1]

    def normed(x):
        return _rms(x, g_ref[...]) * (1.0 + sc_ref[...]) + sh_ref[...]

    xn = normed(x_ref[...])
    xb = xn.astype(BF16)
    ext = jnp.concatenate([normed(xprev_ref[...]), xn, normed(xnext_ref[...])], axis=0).astype(BF16)

    def proj(lhs, start, width):
        return jnp.dot(lhs, w_ref[:, start:start + width], preferred_element_type=F32)

    base = 3 * d_model
    half = C_HEADS * C_DIM
    qab = proj(xb, base + REL_QA, 2 * half)
    kvb = proj(xb, base + REL_KB, 2 * half)
    qkc = proj(ext, base + REL_QC, 2 * half)
    vo = proj(xb, base + REL_VC, 2 * half)
    oc_ref[...] = vo[:, half:].astype(BF16)
    kava = proj(xb, base + REL_KA, 2 * LANES)
    gates = jnp.dot(xb, wg_ref[...], preferred_element_type=F32)
    _prep_math(pl.program_id(1), pl.num_programs(1),
               qab[:, :half], kava, qab[:, half:], kvb[:, :half], kvb[:, half:],
               qkc[SUBLANES:SUBLANES + ROW], qkc[SUBLANES - 1:SUBLANES], qkc[SUBLANES + ROW:SUBLANES + ROW + 1],
               vo[:, :half], gates,
               cos_ref, sa_ref, sb_ref, qg_ref, kg_ref, bd4_ref, bd1_ref, cw_ref, cb_ref, gb_ref, tri_ref,
               *mixer_refs)
    for c in range(6):
        gate_ref[:, c * half:(c + 1) * half] = proj(xb, c * half, half).astype(BF16)


def _inproj(xs, mod, layer, nw, pre_g, w_main, w_gate, tables, qg, kg, bd4, bd1, conv_w, conv_b, gate_b, tri):
    n_batch, ts, d = xs.shape
    n_main = w_main.shape[1]
    nt = ts // ROW
    cos_t, sa_t, sb_t = tables
    row_blocks = ts // SUBLANES
    per_tile = ROW // SUBLANES
    half = C_HEADS * C_DIM

    def const(shape):
        return pl.BlockSpec(shape, lambda b, i: (0,) * len(shape))

    def rows(width):
        return pl.BlockSpec((None, ROW, width), lambda b, i: (b, i, 0))

    def heads(n, width):
        return pl.BlockSpec((None, n, ROW, width), lambda b, i: (b, 0, i, 0))

    def heads_t(n, height):
        return pl.BlockSpec((None, n, height, ROW), lambda b, i: (b, 0, 0, i))

    def cols(height):
        return pl.BlockSpec((None, height, ROW), lambda b, i: (b, 0, i))

    table = pl.BlockSpec((ROW, LANES), lambda b, i: (i, 0))
    in_specs = [
        rows(d),
        pl.BlockSpec((None, SUBLANES, d), lambda b, i: (b, jnp.maximum(i * per_tile - 1, 0), 0)),
        pl.BlockSpec((None, SUBLANES, d), lambda b, i: (b, jnp.minimum((i + 1) * per_tile, row_blocks - 1), 0)),
        _mod_spec(d, layer, 0, nw, n_batch),
        _mod_spec(d, layer, 1, nw, n_batch),
        const((1, d)),
        pl.BlockSpec((d, n_main), lambda b, i: (0, 0), pipeline_mode=pl.Buffered(1)),
        const((d, LANES)),
        table, table, table,
        const((1, half)), const((1, LANES)), const((half, half)), const((LANES, LANES)),
        const((CONV_W, 2 * half)), const((1, 2 * half)), const((1, LANES)), const((MCHUNK, MCHUNK)),
    ]
    outs = [
        (rows(3 * d), (ts, 3 * d), BF16),
        (rows(half), (ts, half), BF16),
        (heads_t(A_HEADS, LANES), (A_HEADS, LANES, ts), BF16),
        (rows(LANES), (ts, LANES), BF16),
        (heads_t(A_KV_HEADS, HEAD_DIM + AUG), (A_KV_HEADS, HEAD_DIM + AUG, ts), BF16),
        (heads_t(2 * B_HEADS, LANES), (2 * B_HEADS, LANES, ts), BF16),
        (heads(B_HEADS, LANES), (B_HEADS, ts, LANES), BF16),
        (heads_t(B_HEADS, LANES + AUG), (B_HEADS, LANES + AUG, ts), BF16),
        (cols(half), (half, ts), BF16),
        (rows(half), (ts, half), BF16),
        (cols(half), (half, ts), BF16),
        (rows(3 * LANES), (ts, 3 * LANES), F32),
        (cols(6 * SUBLANES), (6 * SUBLANES, ts), F32),
    ]
    return pl.pallas_call(
        _inproj_kernel,
        grid=(n_batch, nt),
        in_specs=in_specs,
        out_specs=[spec for spec, _, _ in outs],
        out_shape=[jax.ShapeDtypeStruct((n_batch,) + shape, dtype) for _, shape, dtype in outs],
        compiler_params=_cparams(("parallel", "parallel")),
        name="inproj",
    )(xs, xs, xs, mod, mod, pre_g, w_main, w_gate, cos_t, sa_t, sb_t, qg, kg, bd4, bd1, conv_w, conv_b, gate_b, tri)


def _head_mean_sq(x, bd_ref):
    return jnp.dot((x * x).astype(BF16), bd_ref[...], preferred_element_type=F32)


def _prep_math(i, nt, qa, kava, qb, kb, vb, cur, prev_row, next_row, vc, gates,
               cos_ref, sa_ref, sb_ref, qg_ref, kg_ref, bd4_ref, bd1_ref, cw_ref, cb_ref, gb_ref, tri_ref,
               qaz_ref, ka_ref, va_ref, qbz_ref, kbo_ref, vbo_ref, qmt_ref, km_ref, vmt_ref, gcol_ref, grow_ref):
    cos, sin_a, sin_b = cos_ref[...], sa_ref[...], sb_ref[...]

    def rope(x):
        width = x.shape[1]
        reps = width // LANES
        c = jnp.concatenate([cos] * reps, axis=1) if reps > 1 else cos
        a = jnp.concatenate([sin_a] * reps, axis=1) if reps > 1 else sin_a
        b = jnp.concatenate([sin_b] * reps, axis=1) if reps > 1 else sin_b
        return x * c + pltpu.roll(x, width - 16, 1) * a + pltpu.roll(x, 16, 1) * b

    lane = _lane_iota((ROW, LANES))
    scale = HEAD_DIM ** -0.5 * LOG2E

    qa = qa * lax.rsqrt(_head_mean_sq(qa, bd4_ref) + EPS) * qg_ref[...]
    qa = rope(qa) * scale
    heads_per_kv = A_HEADS // A_KV_HEADS
    for h in range(A_HEADS):
        g = h // heads_per_kv
        blk = qa[:, (h // 2) * LANES:(h // 2 + 1) * LANES]
        if h % 2 != g:
            blk = pltpu.roll(blk, HEAD_DIM, 1)
        qaz_ref[h] = jnp.where(lane // HEAD_DIM == g, blk, 0.0).T.astype(BF16)
    ka = kava[:, :LANES]
    ka = ka * lax.rsqrt(_head_mean_sq(ka, bd1_ref) + EPS) * kg_ref[...]
    ka_ref[...] = rope(ka).astype(BF16)
    va_t = kava[:, LANES:].T.astype(BF16)
    ones_rows = jnp.ones((AUG, ROW), BF16)
    for g in range(A_KV_HEADS):
        va_ref[g, :HEAD_DIM, :] = va_t[g * HEAD_DIM:(g + 1) * HEAD_DIM]
        va_ref[g, HEAD_DIM:, :] = ones_rows

    qb = rope(qb) * scale
    kb = rope(kb)
    for h in range(B_HEADS):
        blk = qb[:, h * LANES:(h + 1) * LANES]
        for m in range(2):
            qbz_ref[2 * h + m] = jnp.where(lane // HEAD_DIM == m, blk, 0.0).T.astype(BF16)
        kbo_ref[h] = kb[:, h * LANES:(h + 1) * LANES].astype(BF16)
        vbo_ref[h, :LANES, :] = vb[:, h * LANES:(h + 1) * LANES].T.astype(BF16)
        vbo_ref[h, LANES:, :] = ones_rows

    row = _row_iota(cur.shape)
    prev_row = jnp.where(i >= 2, prev_row, 0.0)
    next_row = jnp.where(jnp.logical_and(i >= 1, i < nt - 1), next_row, 0.0)
    up = jnp.where(row == 0, prev_row, pltpu.roll(cur, 1, 0))
    dn = jnp.where(row == ROW - 1, next_row, pltpu.roll(cur, ROW - 1, 0))
    y = up * cw_ref[0:1, :] + cur * cw_ref[1:2, :] + dn * cw_ref[2:3, :] + cb_ref[...]
    y = _silu(y)
    half = C_HEADS * C_DIM
    qmt_ref[...] = y[:, :half].T.astype(BF16)
    km_ref[...] = (y[:, half:] * (C_DIM ** -0.5)).astype(BF16)
    vmt_ref[...] = vc.T.astype(BF16)

    gg = gates + gb_ref[...]
    is_forget = (lane // C_HEADS) % 2 == 1
    gl = jnp.where(is_forget, _log_sigmoid(gg), gg) * LOG2E
    tri = tri_ref[...]
    n_rows = 2 * SUBLANES
    for c in range(ROW // MCHUNK):
        rows = slice(c * MCHUNK, (c + 1) * MCHUNK)
        glc = gl[rows]
        hi = glc.astype(BF16)
        rest = glc - hi.astype(F32)
        mid = rest.astype(BF16)
        low = (rest - mid.astype(F32)).astype(BF16)
        cs = (jnp.dot(tri, hi, preferred_element_type=F32) + jnp.dot(tri, mid, preferred_element_type=F32)
              + jnp.dot(tri, low, preferred_element_type=F32))
        tot = jnp.broadcast_to(jnp.sum(glc, axis=0, keepdims=True), glc.shape)
        for f, val in enumerate((glc, cs, tot)):
            gcol_ref[rows, f * LANES:(f + 1) * LANES] = val
            grow_ref[f * n_rows:(f + 1) * n_rows, rows] = val.T[:n_rows, :]


def _attend_blocks(blocks, n_keys):
    def scores(q_t, k_ref):
        return jnp.dot(k_ref[:n_keys, :], q_t, preferred_element_type=F32)

    def weighted(s_t, vt_ref):
        p_t = jnp.exp2(s_t - jnp.max(s_t, axis=0, keepdims=True)).astype(BF16)
        return jnp.dot(vt_ref[:, :n_keys], p_t, preferred_element_type=F32)

    outs = []
    s_cur = scores(blocks[0][0], blocks[0][1])
    for j in range(1, len(blocks)):
        s_next = scores(blocks[j][0], blocks[j][1])
        outs.append(weighted(s_cur, blocks[j - 1][2]))
        s_cur = s_next
    outs.append(weighted(s_cur, blocks[-1][2]))
    return outs


def _per_tile_keys(body, n_all):
    @pl.when(pl.program_id(1) == 0)
    def _():
        body(ROW)

    @pl.when(pl.program_id(1) > 0)
    def _():
        body(n_all)


def _gqa_kernel(qt_ref, k_ref, vt_ref, o_ref):
    heads_per_kv = A_HEADS // A_KV_HEADS

    def body(n_keys):
        blocks = []
        for j in range(A_HEADS // 2):
            q_t = jnp.concatenate([qt_ref[2 * j], qt_ref[2 * j + 1]], axis=1)
            blocks.append((q_t, k_ref, vt_ref.at[(2 * j) // heads_per_kv]))
        for j, o_t in enumerate(_attend_blocks(blocks, n_keys)):
            o = o_t[:HEAD_DIM] / o_t[HEAD_DIM:HEAD_DIM + 1]
            pair = jnp.concatenate([o[:, :ROW], o[:, ROW:]], axis=0)
            o_ref[:, j * LANES:(j + 1) * LANES] = pair.T.astype(BF16)

    _per_tile_keys(body, k_ref.shape[0])


def _diff_kernel(qt_ref, k_ref, vt_ref, lam_ref, g_ref, o_ref, *, lam_init):
    lv = lam_ref[...]
    lam = (jnp.exp(jnp.sum(lv[0:1] * lv[1:2], axis=-1, keepdims=True))
           - jnp.exp(jnp.sum(lv[2:3] * lv[3:4], axis=-1, keepdims=True)) + lam_init)

    def body(n_keys):
        blocks = [(jnp.concatenate([qt_ref[2 * h], qt_ref[2 * h + 1]], axis=1), k_ref.at[h], vt_ref.at[h])
                  for h in range(B_HEADS)]
        for h, o_t in enumerate(_attend_blocks(blocks, n_keys)):
            o = o_t[:LANES] / o_t[LANES:LANES + 1]
            dif = (o[:, :ROW] - lam * o[:, ROW:]).T
            o_ref[:, h * LANES:(h + 1) * LANES] = (_rms(dif, g_ref[...]) * (1.0 - lam_init)).astype(BF16)

    _per_tile_keys(body, k_ref.shape[1])


def _gqa(qaz_t, ka, va_t):
    n_batch, _, _, ts = qaz_t.shape
    nt = ts // ROW
    return pl.pallas_call(
        _gqa_kernel,
        grid=(n_batch, nt),
        in_specs=[
            pl.BlockSpec((None, A_HEADS, LANES, ROW), lambda b, i: (b, 0, 0, i)),
            pl.BlockSpec((None, ts, LANES), lambda b, i: (b, 0, 0)),
            pl.BlockSpec((None, A_KV_HEADS, HEAD_DIM + AUG, ts), lambda b, i: (b, 0, 0, 0)),
        ],
        out_specs=pl.BlockSpec((None, ROW, A_HEADS * HEAD_DIM), lambda b, i: (b, i, 0)),
        out_shape=jax.ShapeDtypeStruct((n_batch, ts, A_HEADS * HEAD_DIM), BF16),
        compiler_params=_cparams(("parallel", "parallel")),
        name="gqa_attention",
    )(qaz_t, ka, va_t)


def _diff(qbz_t, kb, vb_t, lam_vecs, sub_g, lam_init):
    n_batch, _, _, ts = qbz_t.shape
    nt = ts // ROW
    return pl.pallas_call(
        functools.partial(_diff_kernel, lam_init=lam_init),
        grid=(n_batch, nt),
        in_specs=[
            pl.BlockSpec((None, 2 * B_HEADS, LANES, ROW), lambda b, i: (b, 0, 0, i)),
            pl.BlockSpec((None, B_HEADS, ts, LANES), lambda b, i: (b, 0, 0, 0)),
            pl.BlockSpec((None, B_HEADS, LANES + AUG, ts), lambda b, i: (b, 0, 0, 0)),
            pl.BlockSpec((4, HEAD_DIM), lambda b, i: (0, 0)),
            pl.BlockSpec((1, LANES), lambda b, i: (0, 0)),
        ],
        out_specs=pl.BlockSpec((None, ROW, B_HEADS * LANES), lambda b, i: (b, i, 0)),
        out_shape=jax.ShapeDtypeStruct((n_batch, ts, B_HEADS * LANES), BF16),
        compiler_params=_cparams(("parallel", "parallel")),
        name="diff_attention",
    )(qbz_t, kb, vb_t, lam_vecs, sub_g)


AUG = 2 * SUBLANES


def _mlstm_kernel(qtf_ref, kf_ref, vtf_ref, gcf_ref, grf_ref, qtb_ref, kb_ref, vtb_ref, gcb_ref, grb_ref,
                  hf_ref, hb_ref, c_ref, m_ref):
    @pl.when(pl.program_id(1) == 0)
    def _():
        c_ref[...] = jnp.zeros_like(c_ref)
        m_ref[...] = jnp.zeros_like(m_ref)

    length = MCHUNK
    key_idx = _row_iota((length, length))
    qry_idx = _lane_iota((length, length))
    ones_rows = jnp.ones((AUG, length), BF16)
    n_rows = 2 * SUBLANES
    n_chain = 2 * C_HEADS
    c_states = [c_ref[ch] for ch in range(n_chain)]
    m_prevs = [m_ref[ch] for ch in range(n_chain)]
    c_news, m_news, h_outs, pending = [], [], [], []

    for direction, (qt_ref, k_ref, vt_ref, gc_ref, gr_ref, h_ref) in enumerate(
            ((qtf_ref, kf_ref, vtf_ref, gcf_ref, grf_ref, hf_ref), (qtb_ref, kb_ref, vtb_ref, gcb_ref, grb_ref, hb_ref))):
        reverse = direction == 1
        gate_c, cs_c, tot_c = gc_ref[:, :LANES], gc_ref[:, LANES:2 * LANES], gc_ref[:, 2 * LANES:]
        p_c = (tot_c - cs_c + gate_c) if reverse else cs_c
        g_c = gate_c - pltpu.roll(p_c, LANES - C_HEADS, 1)
        mask = (key_idx >= qry_idx) if reverse else (key_idx <= qry_idx)
        for hd in range(C_HEADS):
            chain = direction * C_HEADS + hd
            ii = direction * 2 * C_HEADS + hd
            fi = ii + C_HEADS
            i_row = gr_ref[ii:ii + 1, :]
            f_row = gr_ref[fi:fi + 1, :]
            cs_row = gr_ref[n_rows + fi:n_rows + fi + 1, :]
            tot_row = gr_ref[2 * n_rows + fi:2 * n_rows + fi + 1, :]
            p_row = (tot_row - cs_row + f_row) if reverse else cs_row
            m_prev = m_prevs[chain]
            inter = p_row + m_prev
            log_dt = jnp.where(mask, jnp.broadcast_to(g_c[:, ii:ii + 1], (length, length)) + p_row, NEG)
            m_t = jnp.maximum(inter, jnp.max(log_dt, axis=0, keepdims=True))
            d_t = jnp.exp2(log_dt - m_t)
            a_row = jnp.exp2(inter - m_t)
            sl = slice(hd * C_DIM, (hd + 1) * C_DIM)
            k_h, qt_h = k_ref[:, sl], qt_ref[sl, :]
            vt_aug = jnp.concatenate([vt_ref[sl, :], ones_rows], axis=0)
            s_raw = jnp.dot(k_h, qt_h, preferred_element_type=F32)
            c_state = c_states[chain]
            x_t = jnp.dot(c_state.astype(BF16), qt_h, preferred_element_type=F32)
            w_row = tot_row - p_row + i_row
            m_new = jnp.maximum(tot_row + m_prev, jnp.max(w_row, axis=-1, keepdims=True))
            decay = jnp.exp2(tot_row + m_prev - m_new)
            ws = jnp.exp2(w_row - m_new)
            update = jnp.dot((vt_aug.astype(F32) * ws).astype(BF16), k_h, preferred_element_type=F32)
            c_news.append(decay[:, :C_DIM] * c_state + update)
            m_news.append(m_new)
            pending.append((s_raw, d_t, vt_aug, a_row, x_t, m_t))

    for direction in range(2):
        h_parts = []
        for hd in range(C_HEADS):
            s_raw, d_t, vt_aug, a_row, x_t, m_t = pending[direction * C_HEADS + hd]
            y_t = jnp.dot(vt_aug, (s_raw * d_t).astype(BF16), preferred_element_type=F32)
            num_t = a_row * x_t[:C_DIM] + y_t[:C_DIM]
            den = a_row * x_t[C_DIM:C_DIM + 1] + y_t[C_DIM:C_DIM + 1]
            h_t = num_t / jnp.maximum(jnp.abs(den), jnp.exp2(-m_t))
            h_parts.append(h_t.T)
        h_outs.append(jnp.concatenate(h_parts, axis=1))

    hf_ref[...] = h_outs[0]
    hb_ref[...] = h_outs[1]
    for ch in range(n_chain):
        c_ref[ch] = c_news[ch]
        m_ref[ch] = m_news[ch]


def _mlstm(qmt, km, vmt, gcol, grow):
    n_batch, ts, width = km.shape
    nc = ts // MCHUNK
    ctx_chunks = ROW // MCHUNK

    def bwd(j):
        return jnp.where(j < ctx_chunks, ctx_chunks - 1 - j, nc + ctx_chunks - 1 - j)

    def specs(idx):
        return [
            pl.BlockSpec((None, width, MCHUNK), lambda b, j: (b, 0, idx(j))),
            pl.BlockSpec((None, MCHUNK, width), lambda b, j: (b, idx(j), 0)),
            pl.BlockSpec((None, width, MCHUNK), lambda b, j: (b, 0, idx(j))),
            pl.BlockSpec((None, MCHUNK, 3 * LANES), lambda b, j: (b, idx(j), 0)),
            pl.BlockSpec((None, 6 * SUBLANES, MCHUNK), lambda b, j: (b, 0, idx(j))),
        ]

    fwd = lambda j: j
    n_chain = 2 * C_HEADS
    return pl.pallas_call(
        _mlstm_kernel,
        grid=(n_batch, nc),
        in_specs=specs(fwd) + specs(bwd),
        out_specs=[
            pl.BlockSpec((None, MCHUNK, width), lambda b, j: (b, j, 0)),
            pl.BlockSpec((None, MCHUNK, width), lambda b, j: (b, bwd(j), 0)),
        ],
        out_shape=[jax.ShapeDtypeStruct((n_batch, ts, width), F32)] * 2,
        scratch_shapes=[
            pltpu.VMEM((n_chain, C_DIM + AUG, C_DIM), F32),
            pltpu.VMEM((n_chain, 1, MCHUNK), F32),
        ],
        compiler_params=_cparams(("parallel", "arbitrary")),
        name="mlstm",
    )(qmt, km, vmt, gcol, grow, qmt, km, vmt, gcol, grow)


def _mix_kernel(a_ref, d_ref, hf_ref, hb_ref, oc_ref, gate_ref, x_ref, g1_ref, pg_ref, mg_ref,
                wa_ref, wb_ref, wc_ref, wo_ref, o_ref):
    d_model = x_ref.shape[-1]
    hsum = hf_ref[...] + hb_ref[...]
    mg = mg_ref[...]
    m = jnp.concatenate([_rms(hsum[:, hd * C_DIM:(hd + 1) * C_DIM], mg) for hd in range(C_HEADS)], axis=1)
    m = m * _sigmoid(oc_ref[...].astype(F32))
    u = (_sigmoid(gate_ref[:, :d_model].astype(F32))
         * jnp.dot(a_ref[...], wa_ref[...], preferred_element_type=F32)
         + _sigmoid(gate_ref[:, d_model:2 * d_model].astype(F32))
         * jnp.dot(d_ref[...], wb_ref[...], preferred_element_type=F32)
         + _sigmoid(gate_ref[:, 2 * d_model:].astype(F32))
         * jnp.dot(m.astype(BF16), wc_ref[...], preferred_element_type=F32))
    y = jnp.dot(u.astype(BF16), wo_ref[...], preferred_element_type=F32)
    o_ref[...] = x_ref[...] + g1_ref[...] * _rms(y, pg_ref[...])


def _mix(a, dd, hf, hb, out_gate, merge_gate, xs, mod, layer, nw, post_g, mlstm_g, wa, wb, wc, wo):
    n_batch, ts, d = xs.shape
    nt = ts // ROW
    width = a.shape[-1]

    def tile(w):
        return pl.BlockSpec((None, ROW, w), lambda b, i: (b, i, 0))

    def const(shape):
        return pl.BlockSpec(shape, lambda b, i: (0,) * len(shape))

    return pl.pallas_call(
        _mix_kernel,
        grid=(n_batch, nt),
        in_specs=[
            tile(width), tile(width), tile(width), tile(width),
            tile(width), tile(3 * d),
            tile(d),
            _mod_spec(d, layer, 2, nw, n_batch),
            const((1, d)), const((1, C_DIM)),
            const((width, d)), const((width, d)), const((width, d)), const((d, d)),
        ],
        out_specs=tile(d),
        out_shape=jax.ShapeDtypeStruct((n_batch, ts, d), F32),
        compiler_params=_cparams(("parallel", "parallel")),
        name="mix_out",
    )(a, dd, hf, hb, out_gate, merge_gate, xs, mod, post_g, mlstm_g, wa, wb, wc, wo)


def _ffn_kernel(x_ref, sh_ref, sc_ref, g2_ref, pre_ref, post_ref, wg_ref, wu_ref, wd_ref, o_ref):
    x = x_ref[...]
    xb = (_rms(x, pre_ref[...]) * (1.0 + sc_ref[...]) + sh_ref[...]).astype(BF16)
    gate = jnp.dot(xb, wg_ref[...], preferred_element_type=F32)
    up = jnp.dot(xb, wu_ref[...], preferred_element_type=F32)
    z = jnp.dot((_silu(gate) * up).astype(BF16), wd_ref[...], preferred_element_type=F32)
    o_ref[...] = x + g2_ref[...] * _rms(z, post_ref[...])


def _ffn(xs, mod, layer, nw, pre_g, post_g, wg, wu, wd):
    n_batch, ts, d = xs.shape
    nt = ts // ROW
    dff = wg.shape[1]

    def resident(shape):
        return pl.BlockSpec(shape, lambda b, i: (0,) * len(shape), pipeline_mode=pl.Buffered(1))

    tile = pl.BlockSpec((None, ROW, d), lambda b, i: (b, i, 0))
    return pl.pallas_call(
        _ffn_kernel,
        grid=(n_batch, nt),
        in_specs=[
            tile,
            _mod_spec(d, layer, 3, nw, n_batch), _mod_spec(d, layer, 4, nw, n_batch),
            _mod_spec(d, layer, 5, nw, n_batch),
            pl.BlockSpec((1, d), lambda b, i: (0, 0)), pl.BlockSpec((1, d), lambda b, i: (0, 0)),
            resident((d, dff)), resident((d, dff)), resident((dff, d)),
        ],
        out_specs=tile,
        out_shape=jax.ShapeDtypeStruct((n_batch, ts, d), F32),
        compiler_params=_cparams(("parallel", "parallel")),
        name="ffn",
    )(xs, mod, mod, mod, pre_g, post_g, wg, wu, wd)


def _route_kernel(x_ref, sh_ref, sc_ref, pre_ref, wr_ref, br_ref, tri_ref, xn_ref, meta_ref, cnt_ref, carry_ref):
    @pl.when(jnp.logical_and(pl.program_id(0) == 0, pl.program_id(1) == 0))
    def _():
        carry_ref[...] = jnp.zeros_like(carry_ref)

    xn = _rms(x_ref[...], pre_ref[...]) * (1.0 + sc_ref[...]) + sh_ref[...]
    xn_ref[...] = xn
    lane = _lane_iota((ROW, LANES))
    logits = jnp.dot(xn, wr_ref[...], preferred_element_type=F32, precision=HIGHEST) + br_ref[...]
    logits = jnp.where(lane < N_EXPERTS, logits, NEG)
    v1 = jnp.max(logits, axis=-1, keepdims=True)
    i1 = jnp.min(jnp.where(logits == v1, lane, LANES), axis=-1, keepdims=True)
    rest = jnp.where(lane == i1, NEG, logits)
    v2 = jnp.max(rest, axis=-1, keepdims=True)
    i2 = jnp.min(jnp.where(rest == v2, lane, LANES), axis=-1, keepdims=True)
    e2 = jnp.exp(v2 - v1)
    w1 = 1.0 / (1.0 + e2)
    w2 = e2 / (1.0 + e2)
    assigned = jnp.where(lane == i1, 1.0, jnp.where(lane == i2, 1.0, 0.0))
    before = jnp.dot(tri_ref[...], assigned.astype(BF16), preferred_element_type=F32) + carry_ref[0:1, :]
    r1 = jnp.sum(jnp.where(lane == i1, before, 0.0), axis=-1, keepdims=True)
    r2 = jnp.sum(jnp.where(lane == i2, before, 0.0), axis=-1, keepdims=True)
    carry_ref[...] = carry_ref[...] + jnp.sum(assigned, axis=0, keepdims=True)
    cnt_ref[...] = carry_ref[...]
    fields = (i1.astype(F32), i2.astype(F32), w1, w2, r1, r2)
    meta = jnp.zeros((ROW, LANES), F32)
    for f, val in enumerate(fields):
        meta = jnp.where(lane == f, val, meta)
    meta_ref[...] = meta


def _route(xs, mod, layer, nw, pre_g, w_r, b_r):
    n_batch, ts, d = xs.shape
    nt = ts // ROW
    tile = pl.BlockSpec((None, ROW, d), lambda b, i: (b, i, 0))
    t_idx = jnp.arange(ROW)
    tri_strict = (t_idx[:, None] > t_idx[None, :]).astype(BF16)
    return pl.pallas_call(
        _route_kernel,
        grid=(n_batch, nt),
        in_specs=[
            tile, _mod_spec(d, layer, 3, nw, n_batch), _mod_spec(d, layer, 4, nw, n_batch),
            pl.BlockSpec((1, d), lambda b, i: (0, 0)),
            pl.BlockSpec((d, LANES), lambda b, i: (0, 0)),
            pl.BlockSpec((1, LANES), lambda b, i: (0, 0)),
            pl.BlockSpec((ROW, ROW), lambda b, i: (0, 0)),
        ],
        out_specs=[tile, pl.BlockSpec((None, ROW, LANES), lambda b, i: (b, i, 0)),
                   pl.BlockSpec((SUBLANES, LANES), lambda b, i: (0, 0))],
        out_shape=[jax.ShapeDtypeStruct((n_batch, ts, d), F32),
                   jax.ShapeDtypeStruct((n_batch, ts, LANES), F32),
                   jax.ShapeDtypeStruct((SUBLANES, LANES), F32)],
        scratch_shapes=[pltpu.VMEM((SUBLANES, LANES), F32)],
        compiler_params=_cparams(("arbitrary", "arbitrary")),
        name="route",
    )(xs, mod, mod, pre_g, w_r, b_r, tri_strict)


GROUPS = ROW // SUBLANES


def _start_rows(make_copy):
    for g in range(GROUPS):
        for j in range(SUBLANES):
            for slot in range(2):
                make_copy(g, j, slot).start(priority=slot)


def _wait_rows(make_copy):
    def drain(g, carry):
        for j in range(SUBLANES):
            for slot in range(2):
                make_copy(0, 0, slot).wait()
        return carry

    lax.fori_loop(0, GROUPS, drain, 0)


STAGES = 3


def _dispatch_kernel(pos_ref, ends_ref, xn_ref, out_ref, zero_ref, stage_ref, row_sems, in_sems, zero_sem):
    step = pl.program_id(0)
    n_steps = pl.num_programs(0)
    n_tok = pos_ref.shape[0] // 2
    tm = zero_ref.shape[0]
    n_sorted = out_ref.shape[0]

    @pl.when(step == 0)
    def _():
        zero_ref[...] = jnp.zeros_like(zero_ref)

        def fill(row):
            copy = pltpu.make_async_copy(zero_ref, out_ref.at[pl.ds(pl.multiple_of(row, tm), tm)], zero_sem)
            copy.start()
            copy.wait()

        for e in range(N_EXPERTS):
            prev_end = ends_ref[e - 1] if e > 0 else 0

            @pl.when(ends_ref[e] > prev_end)
            def _():
                fill(ends_ref[e] - tm)

        last_end = ends_ref[N_EXPERTS - 1]

        def tail(k, carry):
            fill(last_end + k * tm)
            return carry

        lax.fori_loop(0, (n_sorted - last_end) // tm, tail, 0)

    def tile_in(tile):
        slot = tile % STAGES
        return pltpu.make_async_copy(xn_ref.at[pl.ds(tile * GROUPS, GROUPS)], stage_ref.at[slot], in_sems.at[slot])

    def copies(tile):
        def make_copy(g, j, slot):
            p = pos_ref[slot * n_tok + tile * ROW + g * SUBLANES + j]
            return pltpu.make_async_copy(stage_ref.at[tile % STAGES, g, pl.ds(j, 1)], out_ref.at[pl.ds(p, 1)],
                                         row_sems.at[tile % STAGES])
        return make_copy

    @pl.when(step == 0)
    def _():
        tile_in(step).start()

    @pl.when(step >= STAGES - 1)
    def _():
        _wait_rows(copies(step - (STAGES - 1)))

    @pl.when(step + 1 < n_steps)
    def _():
        tile_in(step + 1).start()

    tile_in(step).wait()
    _start_rows(copies(step))

    @pl.when(step == n_steps - 1)
    def _():
        for back in range(STAGES - 2, -1, -1):
            @pl.when(step - back >= 0)
            def _():
                _wait_rows(copies(step - back))


def _dispatch(pos, ends, xn, n_sorted, tm):
    m_rows, d = xn.shape
    return pl.pallas_call(
        _dispatch_kernel,
        grid_spec=pltpu.PrefetchScalarGridSpec(
            num_scalar_prefetch=2,
            grid=(m_rows // ROW,),
            in_specs=[pl.BlockSpec(memory_space=pl.ANY)],
            out_specs=pl.BlockSpec(memory_space=pl.ANY),
            scratch_shapes=[pltpu.VMEM((tm, d), F32), pltpu.VMEM((STAGES, GROUPS, SUBLANES, d), F32),
                            pltpu.SemaphoreType.DMA((STAGES,)), pltpu.SemaphoreType.DMA((STAGES,)),
                            pltpu.SemaphoreType.DMA(())],
        ),
        out_shape=jax.ShapeDtypeStruct((n_sorted, d), F32),
        compiler_params=_cparams(("arbitrary",)),
        name="dispatch",
    )(pos, ends, xn.reshape(m_rows // SUBLANES, SUBLANES, d))


def _experts_kernel(te_ref, x_ref, wg_ref, wu_ref, wd_ref, y_ref):
    used = te_ref[pl.program_id(0)] < N_EXPERTS

    @pl.when(used)
    def _():
        x = x_ref[...].astype(BF16)
        gate = jnp.dot(x, wg_ref[...], preferred_element_type=F32)
        up = jnp.dot(x, wu_ref[...], preferred_element_type=F32)
        y_ref[...] = jnp.dot((_silu(gate) * up).astype(BF16), wd_ref[...], preferred_element_type=F32)

    @pl.when(jnp.logical_not(used))
    def _():
        y_ref[...] = jnp.zeros_like(y_ref)


def _experts(tile_expert, x_sorted, wg, wu, wd, tm):
    n_sorted, d = x_sorted.shape
    n_e, _, dff = wg.shape

    def weight(shape):
        return pl.BlockSpec((None,) + shape, lambda i, te: (jnp.minimum(te[i], n_e - 1), 0, 0))

    return pl.pallas_call(
        _experts_kernel,
        grid_spec=pltpu.PrefetchScalarGridSpec(
            num_scalar_prefetch=1,
            grid=(n_sorted // tm,),
            in_specs=[pl.BlockSpec((tm, d), lambda i, te: (i, 0)),
                      weight((d, dff)), weight((d, dff)), weight((dff, d))],
            out_specs=pl.BlockSpec((tm, d), lambda i, te: (i, 0)),
        ),
        out_shape=jax.ShapeDtypeStruct((n_sorted, d), F32),
        compiler_params=_cparams(("arbitrary",)),
        name="experts",
    )(tile_expert, x_sorted, wg, wu, wd)


def _combine_kernel(pos_ref, x_ref, meta_ref, g2_ref, post_ref, y_ref, o_ref, buf_ref, sems, *,
                    tiles_per_sample, latent_only):
    step = pl.program_id(0)
    n_steps = pl.num_programs(0)
    n_tok = pos_ref.shape[0] // 2
    d_model = x_ref.shape[-1]

    def wanted(tile):
        return (tile % tiles_per_sample != 0) if latent_only else (tile >= 0)

    def copies(tile):
        def make_copy(g, j, slot):
            p = pos_ref[slot * n_tok + tile * ROW + g * SUBLANES + j]
            return pltpu.make_async_copy(y_ref.at[pl.ds(p, 1)], buf_ref.at[tile % 2, slot, g, pl.ds(j, 1)],
                                         sems.at[tile % 2])
        return make_copy

    @pl.when(jnp.logical_and(step == 0, wanted(step)))
    def _():
        _start_rows(copies(step))

    @pl.when(jnp.logical_and(step + 1 < n_steps, wanted(step + 1)))
    def _():
        _start_rows(copies(step + 1))

    @pl.when(wanted(step))
    def _():
        _wait_rows(copies(step))
        meta = meta_ref[...]
        cur = step % 2
        y1 = buf_ref[cur, 0].reshape(ROW, d_model)
        y2 = buf_ref[cur, 1].reshape(ROW, d_model)
        z = meta[:, 2:3] * y1 + meta[:, 3:4] * y2
        o_ref[...] = x_ref[...] + g2_ref[...] * _rms(z, post_ref[...])


def _combine(pos, xs, meta, y_sorted, mod, layer, nw, post_g, latent_only):
    n_batch, ts, d = xs.shape
    nt = ts // ROW
    m_rows = n_batch * ts
    base = (layer * 6 + 5) * nw
    tile = pl.BlockSpec((ROW, d), lambda i, pos: (i, 0))
    if latent_only:
        out_rows = n_batch * (ts - ROW)
        out_tile = pl.BlockSpec((ROW, d), lambda i, pos: ((i // nt) * (nt - 1) + jnp.maximum(i % nt - 1, 0), 0))
    else:
        out_rows, out_tile = m_rows, tile
    out = pl.pallas_call(
        functools.partial(_combine_kernel, tiles_per_sample=nt, latent_only=latent_only),
        grid_spec=pltpu.PrefetchScalarGridSpec(
            num_scalar_prefetch=1,
            grid=(m_rows // ROW,),
            in_specs=[
                tile,
                pl.BlockSpec((ROW, LANES), lambda i, pos: (i, 0)),
                pl.BlockSpec((None, 1, d), lambda i, pos: (base + jnp.where(i % nt == 0, n_batch, i // nt), 0, 0)),
                pl.BlockSpec((1, d), lambda i, pos: (0, 0)),
                pl.BlockSpec(memory_space=pl.ANY),
            ],
            out_specs=out_tile,
            scratch_shapes=[pltpu.VMEM((2, 2, GROUPS, SUBLANES, d), F32), pltpu.SemaphoreType.DMA((2,))],
        ),
        out_shape=jax.ShapeDtypeStruct((out_rows, d), F32),
        compiler_params=_cparams(("arbitrary",)),
        name="combine",
    )(pos, xs.reshape(m_rows, d), meta.reshape(m_rows, LANES), mod, post_g, y_sorted)
    return out.reshape(n_batch, out_rows // n_batch, d)


def _moe(xs, mod, layer, nw, pre_g, post_g, w_r, b_r, wg, wu, wd, latent_only, tm=512):
    n_batch, ts, d = xs.shape
    m_rows = n_batch * ts
    xn, meta, cnt = _route(xs, mod, layer, nw, pre_g, w_r, b_r)
    m2 = meta.reshape(m_rows, LANES)
    i1, i2 = m2[:, 0].astype(jnp.int32), m2[:, 1].astype(jnp.int32)
    r1, r2 = m2[:, 4].astype(jnp.int32), m2[:, 5].astype(jnp.int32)
    counts = cnt[0, :N_EXPERTS].astype(jnp.int32)
    padded = -(-counts // tm) * tm
    ends = jnp.cumsum(padded)
    start = ends - padded
    pos = jnp.concatenate([start[i1] + r1, start[i2] + r2])
    n_tiles = 2 * m_rows // tm + N_EXPERTS
    tile_row = jnp.arange(n_tiles, dtype=jnp.int32) * tm
    tile_expert = jnp.sum((ends[None, :] <= tile_row[:, None]).astype(jnp.int32), axis=1)
    x_sorted = _dispatch(pos, ends, xn.reshape(m_rows, d), n_tiles * tm, tm)
    y_sorted = _experts(tile_expert, x_sorted, wg, wu, wd, tm)
    return _combine(pos, xs, meta, y_sorted, mod, layer, nw, post_g, latent_only)


def _rope_tables(n_tok, n_ctx):
    n_freq = HEAD_DIM // 4
    pos = jnp.arange(n_tok)
    row = (pos // GRID_W).astype(F32)
    colp = (pos % GRID_W).astype(F32)
    inv = ROPE_THETA ** (-jnp.arange(n_freq, dtype=F32) / n_freq)
    lane = jnp.arange(LANES)
    in_head = lane % HEAD_DIM
    use_col = (in_head // (HEAD_DIM // 2)) == 1
    freq = inv[in_head % n_freq]
    ang = jnp.where(use_col[None, :], colp[:, None], row[:, None]) * freq[None, :]
    lower = (in_head % (HEAD_DIM // 2)) < n_freq
    cos_t = jnp.cos(ang)
    sin_t = jnp.sin(ang)
    sin_a = jnp.where(lower[None, :], -sin_t, 0.0)
    sin_b = jnp.where(lower[None, :], 0.0, sin_t)
    pad = lambda t, v: jnp.concatenate([jnp.full((n_ctx, LANES), v, F32), t], axis=0)
    return pad(cos_t, 1.0), pad(sin_a, 0.0), pad(sin_b, 0.0)


def _block_diag_mean(width):
    idx = jnp.arange(width) // HEAD_DIM
    return jnp.where(idx[:, None] == idx[None, :], 1.0 / HEAD_DIM, 0.0).astype(BF16)


def _pack_w_in(w):
    a_q = A_HEADS * HEAD_DIM
    a_kv = 2 * A_KV_HEADS * HEAD_DIM
    n_gates = 4 * C_HEADS
    gate_start = w.shape[1] - 3 * w.shape[0]
    g_start = gate_start - n_gates
    main = jnp.concatenate([w[:, gate_start:], w[:, :a_q], w[:, a_q + a_kv:g_start], w[:, a_q:a_q + a_kv]], axis=1)
    gates = jnp.pad(w[:, g_start:gate_start], ((0, 0), (0, LANES - n_gates)))
    return main.astype(BF16), gates.astype(BF16)


def kernel(x, c, ctx, c_ctx, ada_w, ada_b, pre_mix_g, post_mix_g, pre_ffn_g, post_ffn_g, w_in, q_norm_g, k_norm_g, lam_q1, lam_k1, lam_q2, lam_k2, diff_norm_g, conv_w, conv_b, mlstm_gate_b, mlstm_norm_g, w_br_attn, w_br_diff, w_br_mlstm, w_out, w_ff_gate, w_ff_up, w_ff_down, w_router, b_router, w_moe_gate, w_moe_up, w_moe_down):
    n_batch, n_tok, d = x.shape
    n_ctx = ctx.shape[1]
    depth = ada_w.shape[0]
    assert n_ctx == ROW and n_tok % ROW == 0 and d == 1024
    ts = n_ctx + n_tok
    nw = -(-(n_batch + 1) // SUBLANES) * SUBLANES

    c_all = jnp.concatenate([c, c_ctx[None, :], jnp.zeros((nw - n_batch - 1, d), F32)], axis=0)
    mod = _modulation(c_all, ada_w, ada_b)
    tables = _rope_tables(n_tok, n_ctx)
    bd4, bd1 = _block_diag_mean(A_HEADS * HEAD_DIM), _block_diag_mean(LANES)
    t_idx = jnp.arange(MCHUNK)
    tri = (t_idx[:, None] >= t_idx[None, :]).astype(BF16)

    xs = jnp.concatenate([ctx, x], axis=1)
    for l in range(depth):
        lam_init = 0.8 - 0.6 * math.exp(-0.3 * l)
        w_main, w_gate = _pack_w_in(w_in[l])
        qg = jnp.tile(q_norm_g[l], A_HEADS)[None, :]
        kg = jnp.tile(k_norm_g[l], A_KV_HEADS)[None, :]
        gate_b = jnp.pad(mlstm_gate_b[l], (0, LANES - 4 * C_HEADS))[None, :]
        merge_gate, out_gate, qaz, ka, va, qbz, kb, vb, qmt, km, vmt, gcol, grow = _inproj(
            xs, mod, l, nw, pre_mix_g[l][None, :], w_main, w_gate,
            tables, qg, kg, bd4, bd1, conv_w[l], conv_b[l][None, :], gate_b, tri)
        a_out = _gqa(qaz, ka, va)
        lam_vecs = jnp.stack([lam_q1[l], lam_k1[l], lam_q2[l], lam_k2[l]], axis=0)
        d_out = _diff(qbz, kb, vb, lam_vecs, diff_norm_g[l][None, :], lam_init)
        hf, hb = _mlstm(qmt, km, vmt, gcol, grow)
        xs = _mix(a_out, d_out, hf, hb, out_gate, merge_gate, xs, mod, l, nw,
                  post_mix_g[l][None, :], mlstm_norm_g[l][None, :],
                  w_br_attn[l].astype(BF16), w_br_diff[l].astype(BF16), w_br_mlstm[l].astype(BF16),
                  w_out[l].astype(BF16))
        j = l // 2
        if l % 2 == 0:
            xs = _ffn(xs, mod, l, nw, pre_ffn_g[l][None, :], post_ffn_g[l][None, :],
                      w_ff_gate[j].astype(BF16), w_ff_up[j].astype(BF16), w_ff_down[j].astype(BF16))
        else:
            w_r = jnp.pad(w_router[j], ((0, 0), (0, LANES - N_EXPERTS)))
            b_r = jnp.pad(b_router[j], (0, LANES - N_EXPERTS))[None, :]
            xs = _moe(xs, mod, l, nw, pre_ffn_g[l][None, :], post_ffn_g[l][None, :], w_r, b_r,
                      w_moe_gate[j].astype(BF16), w_moe_up[j].astype(BF16), w_moe_down[j].astype(BF16),
                      latent_only=l == depth - 1)
    return xs if xs.shape[1] == n_tok else xs[:, n_ctx:, :]
```

```python
import functools
import math

import jax
import jax.numpy as jnp
from jax import lax
from jax.experimental import pallas as pl
from jax.experimental.pallas import tpu as pltpu

F32 = jnp.float32
BF16 = jnp.bfloat16
HIGHEST = lax.Precision.HIGHEST

EPS = 1e-6
HEAD_DIM = 64
A_HEADS = 8
A_KV_HEADS = 2
B_HEADS = 4
C_HEADS = 4
C_DIM = 128
N_EXPERTS = 8
ROPE_THETA = 10000.0
GRID_W = 64
CONV_W = 3

LANES = 128
SUBLANES = 8
ROW = 256
MCHUNK = 256
LOG2E = math.log2(math.e)
NEG = -1e30
VMEM_LIMIT = 56 * 1024 * 1024

OFF_GATE = 0
REL_QA, REL_QB, REL_KB, REL_VB = 0, 512, 1024, 1536
REL_QC, REL_KC, REL_VC, REL_OC = 2048, 2560, 3072, 3584
REL_KA = 4096
REL_END = 4352


def _cparams(sem):
    return pltpu.CompilerParams(dimension_semantics=sem, vmem_limit_bytes=VMEM_LIMIT)


def _rms(x, g):
    y = x * lax.rsqrt(jnp.mean(x * x, axis=-1, keepdims=True) + EPS)
    return y * g


def _sigmoid(x):
    return 1.0 / (1.0 + jnp.exp(-x))


def _silu(x):
    return x * _sigmoid(x)


def _log_sigmoid(x):
    return jnp.minimum(x, 0.0) - jnp.log(1.0 + jnp.exp(-jnp.abs(x)))


def _lane_iota(shape):
    return lax.broadcasted_iota(jnp.int32, shape, len(shape) - 1)


def _row_iota(shape):
    return lax.broadcasted_iota(jnp.int32, shape, len(shape) - 2)


def _mod_kernel(c_ref, w_ref, b_ref, o_ref):
    c = c_ref[...]
    o_ref[...] = jnp.dot(_silu(c), w_ref[...], preferred_element_type=F32, precision=HIGHEST) + b_ref[...]


def _modulation(c_all, ada_w, ada_b):
    depth, d, _ = ada_w.shape
    nw = c_all.shape[0]
    out = pl.pallas_call(
        _mod_kernel,
        grid=(depth, 6),
        in_specs=[
            pl.BlockSpec((nw, d), lambda l, j: (0, 0)),
            pl.BlockSpec((None, d, d), lambda l, j: (l, 0, j)),
            pl.BlockSpec((None, 1, d), lambda l, j: (l, 0, j)),
        ],
        out_specs=pl.BlockSpec((None, None, nw, d), lambda l, j: (l, j, 0, 0)),
        out_shape=jax.ShapeDtypeStruct((depth, 6, nw, d), F32),
        compiler_params=_cparams(("arbitrary", "arbitrary")),
        name="modulation",
    )(c_all, ada_w, ada_b.reshape(depth, 1, 6 * d))
    return out.reshape(depth * 6 * nw, 1, d)


def _mod_spec(d, layer, chunk, nw, n_batch):
    base = (layer * 6 + chunk) * nw
    return pl.BlockSpec((None, 1, d), lambda b, i: (base + jnp.where(i == 0, n_batch, b), 0, 0))


def _stream(stream):
    if isinstance(stream, tuple):
        ctx, x = stream
        n_batch, n_tok, d = x.shape
        specs = [pl.BlockSpec((None, ROW, d), lambda b, i: (b, 0, 0)),
                 pl.BlockSpec((None, ROW, d), lambda b, i: (b, jnp.maximum(i - 1, 0), 0))]
        return (n_batch, ctx.shape[1] + n_tok, d), specs, [ctx, x]
    d = stream.shape[-1]
    return stream.shape, [pl.BlockSpec((None, ROW, d), lambda b, i: (b, i, 0))], [stream]


def _stream_tile(refs):
    if len(refs) == 2:
        return jnp.where(pl.program_id(1) == 0, refs[0][...], refs[1][...])
    return refs[0][...]


def _inproj_kernel(*refs, n_stream):
    (xprev_ref, xnext_ref, sh_ref, sc_ref, g_ref, w_ref, wg_ref,
     cos_ref, sa_ref, sb_ref, qg_ref, kg_ref, bd4_ref, bd1_ref, cw_ref, cb_ref, gb_ref, tri_ref,
     gate_ref, oc_ref, *mixer_refs) = refs[n_stream:]
    d_model = xprev_ref.shape[-1]

    def normed(x):
        return _rms(x, g_ref[...]) * (1.0 + sc_ref[...]) + sh_ref[...]

    xn = normed(_stream_tile(refs[:n_stream]))
    xb = xn.astype(BF16)
    ext = jnp.concatenate([normed(xprev_ref[...]), xn, normed(xnext_ref[...])], axis=0).astype(BF16)

    def proj(lhs, start, width):
        return jnp.dot(lhs, w_ref[:, start:start + width], preferred_element_type=F32)

    base = 3 * d_model
    half = C_HEADS * C_DIM
    qab = proj(xb, base + REL_QA, 2 * half)
    kvb = proj(xb, base + REL_KB, 2 * half)
    qkc = proj(ext, base + REL_QC, 2 * half)
    vo = proj(xb, base + REL_VC, 2 * half)
    oc_ref[...] = vo[:, half:].astype(BF16)
    kava = proj(xb, base + REL_KA, 2 * LANES)
    gates = jnp.dot(xb, wg_ref[...], preferred_element_type=F32)
    _prep_math(pl.program_id(1), pl.num_programs(1),
               qab[:, :half], kava, qab[:, half:], kvb[:, :half], kvb[:, half:],
               qkc[SUBLANES:SUBLANES + ROW], qkc[SUBLANES - 1:SUBLANES], qkc[SUBLANES + ROW:SUBLANES + ROW + 1],
               vo[:, :half], gates,
               cos_ref, sa_ref, sb_ref, qg_ref, kg_ref, bd4_ref, bd1_ref, cw_ref, cb_ref, gb_ref, tri_ref,
               *mixer_refs)
    for c in range(6):
        gate_ref[:, c * half:(c + 1) * half] = proj(xb, c * half, half).astype(BF16)


def _inproj(xs, mod, layer, nw, pre_g, w_main, w_gate, tables, qg, kg, bd4, bd1, conv_w, conv_b, gate_b, tri):
    (n_batch, ts, d), stream_specs, stream_arrays = _stream(xs)
    halo_src = stream_arrays[-1]
    tile_shift = len(stream_arrays) - 1
    n_main = w_main.shape[1]
    nt = ts // ROW
    cos_t, sa_t, sb_t = tables
    row_blocks = halo_src.shape[1] // SUBLANES
    per_tile = ROW // SUBLANES
    half = C_HEADS * C_DIM

    def const(shape):
        return pl.BlockSpec(shape, lambda b, i: (0,) * len(shape))

    def rows(width):
        return pl.BlockSpec((None, ROW, width), lambda b, i: (b, i, 0))

    def heads(n, width):
        return pl.BlockSpec((None, n, ROW, width), lambda b, i: (b, 0, i, 0))

    def cols(height):
        return pl.BlockSpec((None, height, ROW), lambda b, i: (b, 0, i))

    table = pl.BlockSpec((ROW, LANES), lambda b, i: (i, 0))
    in_specs = stream_specs + [
        pl.BlockSpec((None, SUBLANES, d), lambda b, i: (b, jnp.maximum((i - tile_shift) * per_tile - 1, 0), 0)),
        pl.BlockSpec((None, SUBLANES, d),
                     lambda b, i: (b, jnp.clip((i + 1 - tile_shift) * per_tile, 0, row_blocks - 1), 0)),
        _mod_spec(d, layer, 0, nw, n_batch),
        _mod_spec(d, layer, 1, nw, n_batch),
        const((1, d)),
        pl.BlockSpec((d, n_main), lambda b, i: (0, 0), pipeline_mode=pl.Buffered(1)),
        const((d, LANES)),
        table, table, table,
        const((1, half)), const((1, LANES)), const((half, half)), const((LANES, LANES)),
        const((CONV_W, 2 * half)), const((1, 2 * half)), const((1, LANES)), const((MCHUNK, MCHUNK)),
    ]
    outs = [
        (rows(3 * d), (ts, 3 * d), BF16),
        (rows(half), (ts, half), BF16),
        (heads(A_HEADS, LANES), (A_HEADS, ts, LANES), BF16),
        (rows(LANES), (ts, LANES), BF16),
        (heads(A_KV_HEADS, LANES), (A_KV_HEADS, ts, LANES), BF16),
        (heads(2 * B_HEADS, LANES), (2 * B_HEADS, ts, LANES), BF16),
        (heads(B_HEADS, LANES), (B_HEADS, ts, LANES), BF16),
        (heads(B_HEADS, 2 * LANES), (B_HEADS, ts, 2 * LANES), BF16),
        (cols(half), (half, ts), BF16),
        (rows(half), (ts, half), BF16),
        (cols(half), (half, ts), BF16),
        (rows(3 * LANES), (ts, 3 * LANES), F32),
        (cols(6 * SUBLANES), (6 * SUBLANES, ts), F32),
    ]
    return pl.pallas_call(
        functools.partial(_inproj_kernel, n_stream=len(stream_arrays)),
        grid=(n_batch, nt),
        in_specs=in_specs,
        out_specs=[spec for spec, _, _ in outs],
        out_shape=[jax.ShapeDtypeStruct((n_batch,) + shape, dtype) for _, shape, dtype in outs],
        compiler_params=_cparams(("parallel", "parallel")),
        name="inproj",
    )(*stream_arrays, halo_src, halo_src, mod, mod, pre_g, w_main, w_gate, cos_t, sa_t, sb_t, qg, kg, bd4, bd1, conv_w, conv_b, gate_b, tri)


def _head_mean_sq(x, bd_ref):
    return jnp.dot((x * x).astype(BF16), bd_ref[...], preferred_element_type=F32)


def _prep_math(i, nt, qa, kava, qb, kb, vb, cur, prev_row, next_row, vc, gates,
               cos_ref, sa_ref, sb_ref, qg_ref, kg_ref, bd4_ref, bd1_ref, cw_ref, cb_ref, gb_ref, tri_ref,
               qaz_ref, ka_ref, va_ref, qbz_ref, kbo_ref, vbo_ref, qmt_ref, km_ref, vmt_ref, gcol_ref, grow_ref):
    cos, sin_a, sin_b = cos_ref[...], sa_ref[...], sb_ref[...]

    def rope(x):
        width = x.shape[1]
        reps = width // LANES
        c = jnp.concatenate([cos] * reps, axis=1) if reps > 1 else cos
        a = jnp.concatenate([sin_a] * reps, axis=1) if reps > 1 else sin_a
        b = jnp.concatenate([sin_b] * reps, axis=1) if reps > 1 else sin_b
        return x * c + pltpu.roll(x, width - 16, 1) * a + pltpu.roll(x, 16, 1) * b

    lane = _lane_iota((ROW, LANES))
    ones = jnp.ones((ROW, LANES), BF16)
    scale = HEAD_DIM ** -0.5 * LOG2E

    qa = qa * lax.rsqrt(_head_mean_sq(qa, bd4_ref) + EPS) * qg_ref[...]
    qa = rope(qa) * scale
    heads_per_kv = A_HEADS // A_KV_HEADS
    for h in range(A_HEADS):
        g = h // heads_per_kv
        blk = qa[:, (h // 2) * LANES:(h // 2 + 1) * LANES]
        if h % 2 != g:
            blk = pltpu.roll(blk, HEAD_DIM, 1)
        qaz_ref[h] = jnp.where(lane // HEAD_DIM == g, blk, 0.0).astype(BF16)
    ka = kava[:, :LANES]
    ka = ka * lax.rsqrt(_head_mean_sq(ka, bd1_ref) + EPS) * kg_ref[...]
    ka_ref[...] = rope(ka).astype(BF16)
    va = kava[:, LANES:].astype(BF16)
    for g in range(A_KV_HEADS):
        va_ref[g] = jnp.where(lane // HEAD_DIM == g, va, ones)

    qb = rope(qb) * scale
    kb = rope(kb)
    for h in range(B_HEADS):
        blk = qb[:, h * LANES:(h + 1) * LANES]
        for m in range(2):
            qbz_ref[2 * h + m] = jnp.where(lane // HEAD_DIM == m, blk, 0.0).astype(BF16)
        kbo_ref[h] = kb[:, h * LANES:(h + 1) * LANES].astype(BF16)
        vbo_ref[h, :, :LANES] = vb[:, h * LANES:(h + 1) * LANES].astype(BF16)
        vbo_ref[h, :, LANES:] = ones

    row = _row_iota(cur.shape)
    prev_row = jnp.where(i >= 2, prev_row, 0.0)
    next_row = jnp.where(jnp.logical_and(i >= 1, i < nt - 1), next_row, 0.0)
    up = jnp.where(row == 0, prev_row, pltpu.roll(cur, 1, 0))
    dn = jnp.where(row == ROW - 1, next_row, pltpu.roll(cur, ROW - 1, 0))
    y = up * cw_ref[0:1, :] + cur * cw_ref[1:2, :] + dn * cw_ref[2:3, :] + cb_ref[...]
    y = _silu(y)
    half = C_HEADS * C_DIM
    qmt_ref[...] = y[:, :half].T.astype(BF16)
    km_ref[...] = (y[:, half:] * (C_DIM ** -0.5)).astype(BF16)
    vmt_ref[...] = vc.T.astype(BF16)

    gg = gates + gb_ref[...]
    is_forget = (lane // C_HEADS) % 2 == 1
    gl = jnp.where(is_forget, _log_sigmoid(gg), gg) * LOG2E
    tri = tri_ref[...]
    n_rows = 2 * SUBLANES
    for c in range(ROW // MCHUNK):
        rows = slice(c * MCHUNK, (c + 1) * MCHUNK)
        glc = gl[rows]
        hi = glc.astype(BF16)
        rest = glc - hi.astype(F32)
        mid = rest.astype(BF16)
        low = (rest - mid.astype(F32)).astype(BF16)
        cs = (jnp.dot(tri, hi, preferred_element_type=F32) + jnp.dot(tri, mid, preferred_element_type=F32)
              + jnp.dot(tri, low, preferred_element_type=F32))
        tot = jnp.broadcast_to(jnp.sum(glc, axis=0, keepdims=True), glc.shape)
        for f, val in enumerate((glc, cs, tot)):
            gcol_ref[rows, f * LANES:(f + 1) * LANES] = val
            grow_ref[f * n_rows:(f + 1) * n_rows, rows] = val.T[:n_rows, :]


def _attend_blocks(blocks, n_keys):
    def scores(q, k_ref):
        return lax.dot_general(q, k_ref[:n_keys, :], (((1,), (1,)), ((), ())), preferred_element_type=F32)

    def weighted(s, v_ref):
        p = jnp.exp2(s - jnp.max(s, axis=-1, keepdims=True)).astype(BF16)
        return jnp.dot(p, v_ref[:n_keys, :], preferred_element_type=F32)

    outs = []
    s_cur = scores(blocks[0][0], blocks[0][1])
    for j in range(1, len(blocks)):
        s_next = scores(blocks[j][0], blocks[j][1])
        outs.append(weighted(s_cur, blocks[j - 1][2]))
        s_cur = s_next
    outs.append(weighted(s_cur, blocks[-1][2]))
    return outs


def _per_tile_keys(body, n_all):
    @pl.when(pl.program_id(1) == 0)
    def _():
        body(ROW)

    @pl.when(pl.program_id(1) > 0)
    def _():
        body(n_all)


def _gqa_kernel(q_ref, k_ref, v_ref, o_ref):
    heads_per_kv = A_HEADS // A_KV_HEADS
    lane = _lane_iota((ROW, LANES))

    def body(n_keys):
        blocks = []
        for j in range(A_HEADS // 2):
            q = q_ref[2 * j:2 * j + 2].reshape(2 * ROW, LANES)
            blocks.append((q, k_ref, v_ref.at[(2 * j) // heads_per_kv]))
        for j, o in enumerate(_attend_blocks(blocks, n_keys)):
            g = (2 * j) // heads_per_kv
            den_lane = (1 - g) * HEAD_DIM
            o = o / o[:, den_lane:den_lane + 1]
            even, odd = o[:ROW], o[ROW:]
            even = even if g == 0 else pltpu.roll(even, HEAD_DIM, 1)
            odd = odd if g == 1 else pltpu.roll(odd, HEAD_DIM, 1)
            o_ref[:, j * LANES:(j + 1) * LANES] = jnp.where(lane < HEAD_DIM, even, odd).astype(BF16)

    _per_tile_keys(body, k_ref.shape[0])


def _diff_kernel(q_ref, k_ref, v_ref, lam_ref, g_ref, o_ref, *, lam_init):
    lv = lam_ref[...]
    lam = (jnp.exp(jnp.sum(lv[0:1] * lv[1:2], axis=-1, keepdims=True))
           - jnp.exp(jnp.sum(lv[2:3] * lv[3:4], axis=-1, keepdims=True)) + lam_init)

    def body(n_keys):
        blocks = [(q_ref[2 * h:2 * h + 2].reshape(2 * ROW, LANES), k_ref.at[h], v_ref.at[h]) for h in range(B_HEADS)]
        for h, o in enumerate(_attend_blocks(blocks, n_keys)):
            o = o[:, :LANES] / o[:, LANES:LANES + 1]
            dif = o[:ROW] - lam * o[ROW:]
            o_ref[:, h * LANES:(h + 1) * LANES] = (_rms(dif, g_ref[...]) * (1.0 - lam_init)).astype(BF16)

    _per_tile_keys(body, k_ref.shape[1])


def _gqa(qaz, ka, va):
    n_batch, _, ts, _ = qaz.shape
    nt = ts // ROW
    return pl.pallas_call(
        _gqa_kernel,
        grid=(n_batch, nt),
        in_specs=[
            pl.BlockSpec((None, A_HEADS, ROW, LANES), lambda b, i: (b, 0, i, 0)),
            pl.BlockSpec((None, ts, LANES), lambda b, i: (b, 0, 0)),
            pl.BlockSpec((None, A_KV_HEADS, ts, LANES), lambda b, i: (b, 0, 0, 0)),
        ],
        out_specs=pl.BlockSpec((None, ROW, A_HEADS * HEAD_DIM), lambda b, i: (b, i, 0)),
        out_shape=jax.ShapeDtypeStruct((n_batch, ts, A_HEADS * HEAD_DIM), BF16),
        compiler_params=_cparams(("parallel", "parallel")),
        name="gqa_attention",
    )(qaz, ka, va)


def _diff(qbz, kb, vb, lam_vecs, sub_g, lam_init):
    n_batch, _, ts, _ = qbz.shape
    nt = ts // ROW
    return pl.pallas_call(
        functools.partial(_diff_kernel, lam_init=lam_init),
        grid=(n_batch, nt),
        in_specs=[
            pl.BlockSpec((None, 2 * B_HEADS, ROW, LANES), lambda b, i: (b, 0, i, 0)),
            pl.BlockSpec((None, B_HEADS, ts, LANES), lambda b, i: (b, 0, 0, 0)),
            pl.BlockSpec((None, B_HEADS, ts, 2 * LANES), lambda b, i: (b, 0, 0, 0)),
            pl.BlockSpec((4, HEAD_DIM), lambda b, i: (0, 0)),
            pl.BlockSpec((1, LANES), lambda b, i: (0, 0)),
        ],
        out_specs=pl.BlockSpec((None, ROW, B_HEADS * LANES), lambda b, i: (b, i, 0)),
        out_shape=jax.ShapeDtypeStruct((n_batch, ts, B_HEADS * LANES), BF16),
        compiler_params=_cparams(("parallel", "parallel")),
        name="diff_attention",
    )(qbz, kb, vb, lam_vecs, sub_g)


AUG = 2 * SUBLANES


def _mlstm_kernel(qtf_ref, kf_ref, vtf_ref, gcf_ref, grf_ref, qtb_ref, kb_ref, vtb_ref, gcb_ref, grb_ref,
                  hf_ref, hb_ref, c_ref, m_ref):
    @pl.when(pl.program_id(1) == 0)
    def _():
        c_ref[...] = jnp.zeros_like(c_ref)
        m_ref[...] = jnp.zeros_like(m_ref)

    length = MCHUNK
    key_idx = _row_iota((length, length))
    qry_idx = _lane_iota((length, length))
    ones_rows = jnp.ones((AUG, length), BF16)
    n_rows = 2 * SUBLANES
    n_chain = 2 * C_HEADS
    c_states = [c_ref[ch] for ch in range(n_chain)]
    m_prevs = [m_ref[ch] for ch in range(n_chain)]
    c_news, m_news, h_outs, pending = [], [], [], []

    for direction, (qt_ref, k_ref, vt_ref, gc_ref, gr_ref, h_ref) in enumerate(
            ((qtf_ref, kf_ref, vtf_ref, gcf_ref, grf_ref, hf_ref), (qtb_ref, kb_ref, vtb_ref, gcb_ref, grb_ref, hb_ref))):
        reverse = direction == 1
        gate_c, cs_c, tot_c = gc_ref[:, :LANES], gc_ref[:, LANES:2 * LANES], gc_ref[:, 2 * LANES:]
        p_c = (tot_c - cs_c + gate_c) if reverse else cs_c
        g_c = gate_c - pltpu.roll(p_c, LANES - C_HEADS, 1)
        mask = (key_idx >= qry_idx) if reverse else (key_idx <= qry_idx)
        for hd in range(C_HEADS):
            chain = direction * C_HEADS + hd
            ii = direction * 2 * C_HEADS + hd
            fi = ii + C_HEADS
            i_row = gr_ref[ii:ii + 1, :]
            f_row = gr_ref[fi:fi + 1, :]
            cs_row = gr_ref[n_rows + fi:n_rows + fi + 1, :]
            tot_row = gr_ref[2 * n_rows + fi:2 * n_rows + fi + 1, :]
            p_row = (tot_row - cs_row + f_row) if reverse else cs_row
            m_prev = m_prevs[chain]
            inter = p_row + m_prev
            log_dt = jnp.where(mask, jnp.broadcast_to(g_c[:, ii:ii + 1], (length, length)) + p_row, NEG)
            m_t = jnp.maximum(inter, jnp.max(log_dt, axis=0, keepdims=True))
            d_t = jnp.exp2(log_dt - m_t)
            a_row = jnp.exp2(inter - m_t)
            sl = slice(hd * C_DIM, (hd + 1) * C_DIM)
            k_h, qt_h = k_ref[:, sl], qt_ref[sl, :]
            vt_aug = jnp.concatenate([vt_ref[sl, :], ones_rows], axis=0)
            s_raw = jnp.dot(k_h, qt_h, preferred_element_type=F32)
            c_state = c_states[chain]
            x_t = jnp.dot(c_state.astype(BF16), qt_h, preferred_element_type=F32)
            w_row = tot_row - p_row + i_row
            m_new = jnp.maximum(tot_row + m_prev, jnp.max(w_row, axis=-1, keepdims=True))
            decay = jnp.exp2(tot_row + m_prev - m_new)
            ws = jnp.exp2(w_row - m_new)
            update = jnp.dot((vt_aug.astype(F32) * ws).astype(BF16), k_h, preferred_element_type=F32)
            c_news.append(decay[:, :C_DIM] * c_state + update)
            m_news.append(m_new)
            pending.append((s_raw, d_t, vt_aug, a_row, x_t, m_t))

    for direction in range(2):
        h_parts = []
        for hd in range(C_HEADS):
            s_raw, d_t, vt_aug, a_row, x_t, m_t = pending[direction * C_HEADS + hd]
            y_t = jnp.dot(vt_aug, (s_raw * d_t).astype(BF16), preferred_element_type=F32)
            num_t = a_row * x_t[:C_DIM] + y_t[:C_DIM]
            den = a_row * x_t[C_DIM:C_DIM + 1] + y_t[C_DIM:C_DIM + 1]
            h_t = num_t / jnp.maximum(jnp.abs(den), jnp.exp2(-m_t))
            h_parts.append(h_t.T)
        h_outs.append(jnp.concatenate(h_parts, axis=1))

    hf_ref[...] = h_outs[0]
    hb_ref[...] = h_outs[1]
    for ch in range(n_chain):
        c_ref[ch] = c_news[ch]
        m_ref[ch] = m_news[ch]


def _mlstm(qmt, km, vmt, gcol, grow):
    n_batch, ts, width = km.shape
    nc = ts // MCHUNK
    ctx_chunks = ROW // MCHUNK

    def bwd(j):
        return jnp.where(j < ctx_chunks, ctx_chunks - 1 - j, nc + ctx_chunks - 1 - j)

    def specs(idx):
        return [
            pl.BlockSpec((None, width, MCHUNK), lambda b, j: (b, 0, idx(j))),
            pl.BlockSpec((None, MCHUNK, width), lambda b, j: (b, idx(j), 0)),
            pl.BlockSpec((None, width, MCHUNK), lambda b, j: (b, 0, idx(j))),
            pl.BlockSpec((None, MCHUNK, 3 * LANES), lambda b, j: (b, idx(j), 0)),
            pl.BlockSpec((None, 6 * SUBLANES, MCHUNK), lambda b, j: (b, 0, idx(j))),
        ]

    fwd = lambda j: j
    n_chain = 2 * C_HEADS
    return pl.pallas_call(
        _mlstm_kernel,
        grid=(n_batch, nc),
        in_specs=specs(fwd) + specs(bwd),
        out_specs=[
            pl.BlockSpec((None, MCHUNK, width), lambda b, j: (b, j, 0)),
            pl.BlockSpec((None, MCHUNK, width), lambda b, j: (b, bwd(j), 0)),
        ],
        out_shape=[jax.ShapeDtypeStruct((n_batch, ts, width), F32)] * 2,
        scratch_shapes=[
            pltpu.VMEM((n_chain, C_DIM + AUG, C_DIM), F32),
            pltpu.VMEM((n_chain, 1, MCHUNK), F32),
        ],
        compiler_params=_cparams(("parallel", "arbitrary")),
        name="mlstm",
    )(qmt, km, vmt, gcol, grow, qmt, km, vmt, gcol, grow)


def _mix_kernel(*refs, n_stream):
    (a_ref, d_ref, hf_ref, hb_ref, oc_ref, gate_ref, g1_ref, pg_ref, mg_ref,
     wa_ref, wb_ref, wc_ref, wo_ref, o_ref) = refs[n_stream:]
    d_model = o_ref.shape[-1]
    hsum = hf_ref[...] + hb_ref[...]
    mg = mg_ref[...]
    m = jnp.concatenate([_rms(hsum[:, hd * C_DIM:(hd + 1) * C_DIM], mg) for hd in range(C_HEADS)], axis=1)
    m = m * _sigmoid(oc_ref[...].astype(F32))
    u = (_sigmoid(gate_ref[:, :d_model].astype(F32))
         * jnp.dot(a_ref[...], wa_ref[...], preferred_element_type=F32)
         + _sigmoid(gate_ref[:, d_model:2 * d_model].astype(F32))
         * jnp.dot(d_ref[...], wb_ref[...], preferred_element_type=F32)
         + _sigmoid(gate_ref[:, 2 * d_model:].astype(F32))
         * jnp.dot(m.astype(BF16), wc_ref[...], preferred_element_type=F32))
    y = jnp.dot(u.astype(BF16), wo_ref[...], preferred_element_type=F32)
    o_ref[...] = _stream_tile(refs[:n_stream]) + g1_ref[...] * _rms(y, pg_ref[...])


def _mix(a, dd, hf, hb, out_gate, merge_gate, xs, mod, layer, nw, post_g, mlstm_g, wa, wb, wc, wo):
    (n_batch, ts, d), stream_specs, stream_arrays = _stream(xs)
    nt = ts // ROW
    width = a.shape[-1]

    def tile(w):
        return pl.BlockSpec((None, ROW, w), lambda b, i: (b, i, 0))

    def const(shape):
        return pl.BlockSpec(shape, lambda b, i: (0,) * len(shape))

    return pl.pallas_call(
        functools.partial(_mix_kernel, n_stream=len(stream_arrays)),
        grid=(n_batch, nt),
        in_specs=stream_specs + [
            tile(width), tile(width), tile(width), tile(width),
            tile(width), tile(3 * d),
            _mod_spec(d, layer, 2, nw, n_batch),
            const((1, d)), const((1, C_DIM)),
            const((width, d)), const((width, d)), const((width, d)), const((d, d)),
        ],
        out_specs=tile(d),
        out_shape=jax.ShapeDtypeStruct((n_batch, ts, d), F32),
        compiler_params=_cparams(("parallel", "parallel")),
        name="mix_out",
    )(*stream_arrays, a, dd, hf, hb, out_gate, merge_gate, mod, post_g, mlstm_g, wa, wb, wc, wo)


def _ffn_kernel(x_ref, sh_ref, sc_ref, g2_ref, pre_ref, post_ref, wg_ref, wu_ref, wd_ref, o_ref):
    x = x_ref[...]
    xb = (_rms(x, pre_ref[...]) * (1.0 + sc_ref[...]) + sh_ref[...]).astype(BF16)
    gate = jnp.dot(xb, wg_ref[...], preferred_element_type=F32)
    up = jnp.dot(xb, wu_ref[...], preferred_element_type=F32)
    z = jnp.dot((_silu(gate) * up).astype(BF16), wd_ref[...], preferred_element_type=F32)
    o_ref[...] = x + g2_ref[...] * _rms(z, post_ref[...])


def _ffn(xs, mod, layer, nw, pre_g, post_g, wg, wu, wd):
    n_batch, ts, d = xs.shape
    nt = ts // ROW
    dff = wg.shape[1]

    def resident(shape):
        return pl.BlockSpec(shape, lambda b, i: (0,) * len(shape), pipeline_mode=pl.Buffered(1))

    tile = pl.BlockSpec((None, ROW, d), lambda b, i: (b, i, 0))
    return pl.pallas_call(
        _ffn_kernel,
        grid=(n_batch, nt),
        in_specs=[
            tile,
            _mod_spec(d, layer, 3, nw, n_batch), _mod_spec(d, layer, 4, nw, n_batch),
            _mod_spec(d, layer, 5, nw, n_batch),
            pl.BlockSpec((1, d), lambda b, i: (0, 0)), pl.BlockSpec((1, d), lambda b, i: (0, 0)),
            resident((d, dff)), resident((d, dff)), resident((dff, d)),
        ],
        out_specs=tile,
        out_shape=jax.ShapeDtypeStruct((n_batch, ts, d), F32),
        compiler_params=_cparams(("parallel", "parallel")),
        name="ffn",
    )(xs, mod, mod, mod, pre_g, post_g, wg, wu, wd)


def _route_kernel(x_ref, sh_ref, sc_ref, pre_ref, wr_ref, br_ref, tri_ref, xn_ref, meta_ref, cnt_ref, carry_ref):
    @pl.when(jnp.logical_and(pl.program_id(0) == 0, pl.program_id(1) == 0))
    def _():
        carry_ref[...] = jnp.zeros_like(carry_ref)

    xn = _rms(x_ref[...], pre_ref[...]) * (1.0 + sc_ref[...]) + sh_ref[...]
    xn_ref[...] = xn
    lane = _lane_iota((ROW, LANES))
    logits = jnp.dot(xn, wr_ref[...], preferred_element_type=F32, precision=HIGHEST) + br_ref[...]
    logits = jnp.where(lane < N_EXPERTS, logits, NEG)
    v1 = jnp.max(logits, axis=-1, keepdims=True)
    i1 = jnp.min(jnp.where(logits == v1, lane, LANES), axis=-1, keepdims=True)
    rest = jnp.where(lane == i1, NEG, logits)
    v2 = jnp.max(rest, axis=-1, keepdims=True)
    i2 = jnp.min(jnp.where(rest == v2, lane, LANES), axis=-1, keepdims=True)
    e2 = jnp.exp(v2 - v1)
    w1 = 1.0 / (1.0 + e2)
    w2 = e2 / (1.0 + e2)
    assigned = jnp.where(lane == i1, 1.0, jnp.where(lane == i2, 1.0, 0.0))
    before = jnp.dot(tri_ref[...], assigned.astype(BF16), preferred_element_type=F32) + carry_ref[0:1, :]
    r1 = jnp.sum(jnp.where(lane == i1, before, 0.0), axis=-1, keepdims=True)
    r2 = jnp.sum(jnp.where(lane == i2, before, 0.0), axis=-1, keepdims=True)
    carry_ref[...] = carry_ref[...] + jnp.sum(assigned, axis=0, keepdims=True)
    cnt_ref[...] = carry_ref[...]
    fields = (i1.astype(F32), i2.astype(F32), w1, w2, r1, r2)
    meta = jnp.zeros((ROW, LANES), F32)
    for f, val in enumerate(fields):
        meta = jnp.where(lane == f, val, meta)
    meta_ref[...] = meta


def _route(xs, mod, layer, nw, pre_g, w_r, b_r):
    n_batch, ts, d = xs.shape
    nt = ts // ROW
    tile = pl.BlockSpec((None, ROW, d), lambda b, i: (b, i, 0))
    t_idx = jnp.arange(ROW)
    tri_strict = (t_idx[:, None] > t_idx[None, :]).astype(BF16)
    return pl.pallas_call(
        _route_kernel,
        grid=(n_batch, nt),
        in_specs=[
            tile, _mod_spec(d, layer, 3, nw, n_batch), _mod_spec(d, layer, 4, nw, n_batch),
            pl.BlockSpec((1, d), lambda b, i: (0, 0)),
            pl.BlockSpec((d, LANES), lambda b, i: (0, 0)),
            pl.BlockSpec((1, LANES), lambda b, i: (0, 0)),
            pl.BlockSpec((ROW, ROW), lambda b, i: (0, 0)),
        ],
        out_specs=[tile, pl.BlockSpec((None, ROW, LANES), lambda b, i: (b, i, 0)),
                   pl.BlockSpec((SUBLANES, LANES), lambda b, i: (0, 0))],
        out_shape=[jax.ShapeDtypeStruct((n_batch, ts, d), F32),
                   jax.ShapeDtypeStruct((n_batch, ts, LANES), F32),
                   jax.ShapeDtypeStruct((SUBLANES, LANES), F32)],
        scratch_shapes=[pltpu.VMEM((SUBLANES, LANES), F32)],
        compiler_params=_cparams(("arbitrary", "arbitrary")),
        name="route",
    )(xs, mod, mod, pre_g, w_r, b_r, tri_strict)


GROUPS = ROW // SUBLANES


def _start_rows(make_copy):
    for g in range(GROUPS):
        for j in range(SUBLANES):
            for slot in range(2):
                make_copy(g, j, slot).start()


def _wait_rows(make_copy):
    def drain(g, carry):
        for j in range(SUBLANES):
            for slot in range(2):
                make_copy(0, 0, slot).wait()
        return carry

    lax.fori_loop(0, GROUPS, drain, 0)


STAGES = 3


def _dispatch_kernel(pos_ref, bounds_ref, xn_ref, out_ref, zero_ref, stage_ref, row_sems, in_sems, zero_sem):
    step = pl.program_id(0)
    n_steps = pl.num_programs(0)
    n_tok = pos_ref.shape[0] // 2
    tm = zero_ref.shape[0]
    n_sorted = out_ref.shape[0]

    @pl.when(step == 0)
    def _():
        zero_ref[...] = jnp.zeros_like(zero_ref)

        def fill(row):
            copy = pltpu.make_async_copy(zero_ref, out_ref.at[pl.ds(pl.multiple_of(row, tm), tm)], zero_sem)
            copy.start()
            copy.wait()

        for e in range(N_EXPERTS):
            end = bounds_ref[N_EXPERTS + e]

            @pl.when(end > bounds_ref[e])
            def _():
                fill(end - tm)

        last_end = bounds_ref[2 * N_EXPERTS - 1]

        def tail(k, carry):
            fill(last_end + k * tm)
            return carry

        lax.fori_loop(0, (n_sorted - last_end) // tm, tail, 0)

    def tile_in(tile):
        slot = tile % STAGES
        return pltpu.make_async_copy(xn_ref.at[pl.ds(tile * GROUPS, GROUPS)], stage_ref.at[slot], in_sems.at[slot])

    def copies(tile):
        def make_copy(g, j, slot):
            p = pos_ref[slot * n_tok + tile * ROW + g * SUBLANES + j]
            return pltpu.make_async_copy(stage_ref.at[tile % STAGES, g, pl.ds(j, 1)], out_ref.at[pl.ds(p, 1)],
                                         row_sems.at[tile % STAGES])
        return make_copy

    @pl.when(step == 0)
    def _():
        tile_in(step).start()

    @pl.when(step >= STAGES - 1)
    def _():
        _wait_rows(copies(step - (STAGES - 1)))

    @pl.when(step + 1 < n_steps)
    def _():
        tile_in(step + 1).start()

    tile_in(step).wait()
    _start_rows(copies(step))

    @pl.when(step == n_steps - 1)
    def _():
        for back in range(STAGES - 2, -1, -1):
            @pl.when(step - back >= 0)
            def _():
                _wait_rows(copies(step - back))


def _dispatch(pos, bounds, xn, n_sorted, tm):
    m_rows, d = xn.shape
    return pl.pallas_call(
        _dispatch_kernel,
        grid_spec=pltpu.PrefetchScalarGridSpec(
            num_scalar_prefetch=2,
            grid=(m_rows // ROW,),
            in_specs=[pl.BlockSpec(memory_space=pl.ANY)],
            out_specs=pl.BlockSpec(memory_space=pl.ANY),
            scratch_shapes=[pltpu.VMEM((tm, d), F32), pltpu.VMEM((STAGES, GROUPS, SUBLANES, d), F32),
                            pltpu.SemaphoreType.DMA((STAGES,)), pltpu.SemaphoreType.DMA((STAGES,)),
                            pltpu.SemaphoreType.DMA(())],
        ),
        out_shape=jax.ShapeDtypeStruct((n_sorted, d), F32),
        compiler_params=_cparams(("arbitrary",)),
        name="dispatch",
    )(pos, bounds, xn.reshape(m_rows // SUBLANES, SUBLANES, d))


def _experts_kernel(te_ref, x_ref, wg_ref, wu_ref, wd_ref, y_ref):
    used = te_ref[pl.program_id(0)] < N_EXPERTS

    @pl.when(used)
    def _():
        x = x_ref[...].astype(BF16)
        gate = jnp.dot(x, wg_ref[...], preferred_element_type=F32)
        up = jnp.dot(x, wu_ref[...], preferred_element_type=F32)
        y_ref[...] = jnp.dot((_silu(gate) * up).astype(BF16), wd_ref[...], preferred_element_type=F32)

    @pl.when(jnp.logical_not(used))
    def _():
        y_ref[...] = jnp.zeros_like(y_ref)


def _experts(tile_expert, x_sorted, wg, wu, wd, tm):
    n_sorted, d = x_sorted.shape
    n_e, _, dff = wg.shape

    def weight(shape):
        return pl.BlockSpec((None,) + shape, lambda i, te: (jnp.minimum(te[i], n_e - 1), 0, 0))

    return pl.pallas_call(
        _experts_kernel,
        grid_spec=pltpu.PrefetchScalarGridSpec(
            num_scalar_prefetch=1,
            grid=(n_sorted // tm,),
            in_specs=[pl.BlockSpec((tm, d), lambda i, te: (i, 0)),
                      weight((d, dff)), weight((d, dff)), weight((dff, d))],
            out_specs=pl.BlockSpec((tm, d), lambda i, te: (i, 0)),
        ),
        out_shape=jax.ShapeDtypeStruct((n_sorted, d), F32),
        compiler_params=_cparams(("arbitrary",)),
        name="experts",
    )(tile_expert, x_sorted, wg, wu, wd)


def _combine_kernel(pos_ref, x_ref, meta_ref, g2_ref, post_ref, y_ref, o_ref, buf_ref, sems, *,
                    tiles_per_sample, latent_only):
    step = pl.program_id(0)
    n_steps = pl.num_programs(0)
    n_tok = pos_ref.shape[0] // 2
    d_model = x_ref.shape[-1]

    def wanted(tile):
        return (tile % tiles_per_sample != 0) if latent_only else (tile >= 0)

    def copies(tile):
        def make_copy(g, j, slot):
            p = pos_ref[slot * n_tok + tile * ROW + g * SUBLANES + j]
            return pltpu.make_async_copy(y_ref.at[pl.ds(p, 1)], buf_ref.at[tile % 2, slot, g, pl.ds(j, 1)],
                                         sems.at[tile % 2])
        return make_copy

    @pl.when(jnp.logical_and(step == 0, wanted(step)))
    def _():
        _start_rows(copies(step))

    @pl.when(jnp.logical_and(step + 1 < n_steps, wanted(step + 1)))
    def _():
        _start_rows(copies(step + 1))

    @pl.when(wanted(step))
    def _():
        _wait_rows(copies(step))
        meta = meta_ref[...]
        cur = step % 2
        y1 = buf_ref[cur, 0].reshape(ROW, d_model)
        y2 = buf_ref[cur, 1].reshape(ROW, d_model)
        z = meta[:, 2:3] * y1 + meta[:, 3:4] * y2
        o_ref[...] = x_ref[...] + g2_ref[...] * _rms(z, post_ref[...])


def _combine(pos, xs, meta, y_sorted, mod, layer, nw, post_g, latent_only):
    n_batch, ts, d = xs.shape
    nt = ts // ROW
    m_rows = n_batch * ts
    base = (layer * 6 + 5) * nw
    tile = pl.BlockSpec((ROW, d), lambda i, pos: (i, 0))
    if latent_only:
        out_rows = n_batch * (ts - ROW)
        out_tile = pl.BlockSpec((ROW, d), lambda i, pos: ((i // nt) * (nt - 1) + jnp.maximum(i % nt - 1, 0), 0))
    else:
        out_rows, out_tile = m_rows, tile
    out = pl.pallas_call(
        functools.partial(_combine_kernel, tiles_per_sample=nt, latent_only=latent_only),
        grid_spec=pltpu.PrefetchScalarGridSpec(
            num_scalar_prefetch=1,
            grid=(m_rows // ROW,),
            in_specs=[
                tile,
                pl.BlockSpec((ROW, LANES), lambda i, pos: (i, 0)),
                pl.BlockSpec((None, 1, d), lambda i, pos: (base + jnp.where(i % nt == 0, n_batch, i // nt), 0, 0)),
                pl.BlockSpec((1, d), lambda i, pos: (0, 0)),
                pl.BlockSpec(memory_space=pl.ANY),
            ],
            out_specs=out_tile,
            scratch_shapes=[pltpu.VMEM((2, 2, GROUPS, SUBLANES, d), F32), pltpu.SemaphoreType.DMA((2,))],
        ),
        out_shape=jax.ShapeDtypeStruct((out_rows, d), F32),
        compiler_params=_cparams(("arbitrary",)),
        name="combine",
    )(pos, xs.reshape(m_rows, d), meta.reshape(m_rows, LANES), mod, post_g, y_sorted)
    return out.reshape(n_batch, out_rows // n_batch, d)


def _moe(xs, mod, layer, nw, pre_g, post_g, w_r, b_r, wg, wu, wd, latent_only, tm=512):
    n_batch, ts, d = xs.shape
    m_rows = n_batch * ts
    xn, meta, cnt = _route(xs, mod, layer, nw, pre_g, w_r, b_r)
    m2 = meta.reshape(m_rows, LANES)
    i1, i2 = m2[:, 0].astype(jnp.int32), m2[:, 1].astype(jnp.int32)
    r1, r2 = m2[:, 4].astype(jnp.int32), m2[:, 5].astype(jnp.int32)
    counts = cnt[0, :N_EXPERTS].astype(jnp.int32)
    padded = -(-counts // tm) * tm
    ends = jnp.cumsum(padded)
    start = ends - padded
    pos = jnp.concatenate([start[i1] + r1, start[i2] + r2])
    bounds = jnp.concatenate([start, ends])
    n_tiles = 2 * m_rows // tm + N_EXPERTS
    tile_row = jnp.arange(n_tiles, dtype=jnp.int32) * tm
    tile_expert = jnp.sum((ends[None, :] <= tile_row[:, None]).astype(jnp.int32), axis=1)
    x_sorted = _dispatch(pos, bounds, xn.reshape(m_rows, d), n_tiles * tm, tm)
    y_sorted = _experts(tile_expert, x_sorted, wg, wu, wd, tm)
    return _combine(pos, xs, meta, y_sorted, mod, layer, nw, post_g, latent_only)


def _rope_tables(n_tok, n_ctx):
    n_freq = HEAD_DIM // 4
    pos = jnp.arange(n_tok)
    row = (pos // GRID_W).astype(F32)
    colp = (pos % GRID_W).astype(F32)
    inv = ROPE_THETA ** (-jnp.arange(n_freq, dtype=F32) / n_freq)
    lane = jnp.arange(LANES)
    in_head = lane % HEAD_DIM
    use_col = (in_head // (HEAD_DIM // 2)) == 1
    freq = inv[in_head % n_freq]
    ang = jnp.where(use_col[None, :], colp[:, None], row[:, None]) * freq[None, :]
    lower = (in_head % (HEAD_DIM // 2)) < n_freq
    cos_t = jnp.cos(ang)
    sin_t = jnp.sin(ang)
    sin_a = jnp.where(lower[None, :], -sin_t, 0.0)
    sin_b = jnp.where(lower[None, :], 0.0, sin_t)
    pad = lambda t, v: jnp.concatenate([jnp.full((n_ctx, LANES), v, F32), t], axis=0)
    return pad(cos_t, 1.0), pad(sin_a, 0.0), pad(sin_b, 0.0)


def _block_diag_mean(width):
    idx = jnp.arange(width) // HEAD_DIM
    return jnp.where(idx[:, None] == idx[None, :], 1.0 / HEAD_DIM, 0.0).astype(BF16)


def _pack_w_in(w):
    a_q = A_HEADS * HEAD_DIM
    a_kv = 2 * A_KV_HEADS * HEAD_DIM
    n_gates = 4 * C_HEADS
    gate_start = w.shape[1] - 3 * w.shape[0]
    g_start = gate_start - n_gates
    main = jnp.concatenate([w[:, gate_start:], w[:, :a_q], w[:, a_q + a_kv:g_start], w[:, a_q:a_q + a_kv]], axis=1)
    gates = jnp.pad(w[:, g_start:gate_start], ((0, 0), (0, LANES - n_gates)))
    return main.astype(BF16), gates.astype(BF16)


def kernel(x, c, ctx, c_ctx, ada_w, ada_b, pre_mix_g, post_mix_g, pre_ffn_g, post_ffn_g, w_in, q_norm_g, k_norm_g, lam_q1, lam_k1, lam_q2, lam_k2, diff_norm_g, conv_w, conv_b, mlstm_gate_b, mlstm_norm_g, w_br_attn, w_br_diff, w_br_mlstm, w_out, w_ff_gate, w_ff_up, w_ff_down, w_router, b_router, w_moe_gate, w_moe_up, w_moe_down):
    n_batch, n_tok, d = x.shape
    n_ctx = ctx.shape[1]
    depth = ada_w.shape[0]
    assert n_ctx == ROW and n_tok % ROW == 0 and d == 1024
    ts = n_ctx + n_tok
    nw = -(-(n_batch + 1) // SUBLANES) * SUBLANES

    c_all = jnp.concatenate([c, c_ctx[None, :], jnp.zeros((nw - n_batch - 1, d), F32)], axis=0)
    mod = _modulation(c_all, ada_w, ada_b)
    tables = _rope_tables(n_tok, n_ctx)
    bd4, bd1 = _block_diag_mean(A_HEADS * HEAD_DIM), _block_diag_mean(LANES)
    t_idx = jnp.arange(MCHUNK)
    tri = (t_idx[:, None] >= t_idx[None, :]).astype(BF16)

    xs = (ctx, x)
    for l in range(depth):
        lam_init = 0.8 - 0.6 * math.exp(-0.3 * l)
        w_main, w_gate = _pack_w_in(w_in[l])
        qg = jnp.tile(q_norm_g[l], A_HEADS)[None, :]
        kg = jnp.tile(k_norm_g[l], A_KV_HEADS)[None, :]
        gate_b = jnp.pad(mlstm_gate_b[l], (0, LANES - 4 * C_HEADS))[None, :]
        merge_gate, out_gate, qaz, ka, va, qbz, kb, vb, qmt, km, vmt, gcol, grow = _inproj(
            xs, mod, l, nw, pre_mix_g[l][None, :], w_main, w_gate,
            tables, qg, kg, bd4, bd1, conv_w[l], conv_b[l][None, :], gate_b, tri)
        a_out = _gqa(qaz, ka, va)
        lam_vecs = jnp.stack([lam_q1[l], lam_k1[l], lam_q2[l], lam_k2[l]], axis=0)
        d_out = _diff(qbz, kb, vb, lam_vecs, diff_norm_g[l][None, :], lam_init)
        hf, hb = _mlstm(qmt, km, vmt, gcol, grow)
        xs = _mix(a_out, d_out, hf, hb, out_gate, merge_gate, xs, mod, l, nw,
                  post_mix_g[l][None, :], mlstm_norm_g[l][None, :],
                  w_br_attn[l].astype(BF16), w_br_diff[l].astype(BF16), w_br_mlstm[l].astype(BF16),
                  w_out[l].astype(BF16))
        j = l // 2
        if l % 2 == 0:
            xs = _ffn(xs, mod, l, nw, pre_ffn_g[l][None, :], post_ffn_g[l][None, :],
                      w_ff_gate[j].astype(BF16), w_ff_up[j].astype(BF16), w_ff_down[j].astype(BF16))
        else:
            w_r = jnp.pad(w_router[j], ((0, 0), (0, LANES - N_EXPERTS)))
            b_r = jnp.pad(b_router[j], (0, LANES - N_EXPERTS))[None, :]
            xs = _moe(xs, mod, l, nw, pre_ffn_g[l][None, :], post_ffn_g[l][None, :], w_r, b_r,
                      w_moe_gate[j].astype(BF16), w_moe_up[j].astype(BF16), w_moe_down[j].astype(BF16),
                      latent_only=l == depth - 1)
    return xs if xs.shape[1] == n_tok else xs[:, n_ctx:, :]
```

```python
import functools
import math

import jax
import jax.numpy as jnp
from jax import lax
from jax.experimental import pallas as pl
from jax.experimental.pallas import tpu as pltpu

F32 = jnp.float32
BF16 = jnp.bfloat16
HIGHEST = lax.Precision.HIGHEST

EPS = 1e-6
HEAD_DIM = 64
A_HEADS = 8
A_KV_HEADS = 2
B_HEADS = 4
C_HEADS = 4
C_DIM = 128
N_EXPERTS = 8
ROPE_THETA = 10000.0
GRID_W = 64
CONV_W = 3

LANES = 128
SUBLANES = 8
ROW = 256
MCHUNK = 256
LOG2E = math.log2(math.e)
NEG = -1e30
VMEM_LIMIT = 56 * 1024 * 1024

OFF_GATE = 0
REL_QA, REL_QB, REL_KB, REL_VB = 0, 512, 1024, 1536
REL_QC, REL_KC, REL_VC, REL_OC = 2048, 2560, 3072, 3584
REL_KA = 4096
REL_END = 4352


def _cparams(sem):
    return pltpu.CompilerParams(dimension_semantics=sem, vmem_limit_bytes=VMEM_LIMIT)


def _rms(x, g):
    y = x * lax.rsqrt(jnp.mean(x * x, axis=-1, keepdims=True) + EPS)
    return y * g


def _sigmoid(x):
    return 1.0 / (1.0 + jnp.exp(-x))


def _silu(x):
    return x * _sigmoid(x)


def _log_sigmoid(x):
    return jnp.minimum(x, 0.0) - jnp.log(1.0 + jnp.exp(-jnp.abs(x)))


def _lane_iota(shape):
    return lax.broadcasted_iota(jnp.int32, shape, len(shape) - 1)


def _row_iota(shape):
    return lax.broadcasted_iota(jnp.int32, shape, len(shape) - 2)


def _mod_kernel(c_ref, w_ref, b_ref, o_ref):
    c = c_ref[...]
    o_ref[...] = jnp.dot(_silu(c), w_ref[...], preferred_element_type=F32, precision=HIGHEST) + b_ref[...]


def _modulation(c_all, ada_w, ada_b):
    depth, d, _ = ada_w.shape
    nw = c_all.shape[0]
    out = pl.pallas_call(
        _mod_kernel,
        grid=(depth, 6),
        in_specs=[
            pl.BlockSpec((nw, d), lambda l, j: (0, 0)),
            pl.BlockSpec((None, d, d), lambda l, j: (l, 0, j)),
            pl.BlockSpec((None, 1, d), lambda l, j: (l, 0, j)),
        ],
        out_specs=pl.BlockSpec((None, None, nw, d), lambda l, j: (l, j, 0, 0)),
        out_shape=jax.ShapeDtypeStruct((depth, 6, nw, d), F32),
        compiler_params=_cparams(("arbitrary", "arbitrary")),
        name="modulation",
    )(c_all, ada_w, ada_b.reshape(depth, 1, 6 * d))
    return out.reshape(depth * 6 * nw, 1, d)


def _mod_spec(d, layer, chunk, nw, n_batch):
    base = (layer * 6 + chunk) * nw
    return pl.BlockSpec((None, 1, d), lambda b, i: (base + jnp.where(i == 0, n_batch, b), 0, 0))


def _stream(stream):
    if isinstance(stream, tuple):
        ctx, x = stream
        n_batch, n_tok, d = x.shape
        specs = [pl.BlockSpec((None, ROW, d), lambda b, i: (b, 0, 0)),
                 pl.BlockSpec((None, ROW, d), lambda b, i: (b, jnp.maximum(i - 1, 0), 0))]
        return (n_batch, ctx.shape[1] + n_tok, d), specs, [ctx, x]
    d = stream.shape[-1]
    return stream.shape, [pl.BlockSpec((None, ROW, d), lambda b, i: (b, i, 0))], [stream]


def _stream_tile(refs):
    if len(refs) == 2:
        return jnp.where(pl.program_id(1) == 0, refs[0][...], refs[1][...])
    return refs[0][...]


def _inproj_kernel(*refs, n_stream):
    (xprev_ref, xnext_ref, sh_ref, sc_ref, g_ref, w_ref, wg_ref,
     cos_ref, sa_ref, sb_ref, qg_ref, kg_ref, bd4_ref, bd1_ref, cw_ref, cb_ref, gb_ref, tri_ref,
     gate_ref, oc_ref, *mixer_refs) = refs[n_stream:]
    d_model = xprev_ref.shape[-1]

    def normed(x):
        return _rms(x, g_ref[...]) * (1.0 + sc_ref[...]) + sh_ref[...]

    xn = normed(_stream_tile(refs[:n_stream]))
    xb = xn.astype(BF16)
    ext = jnp.concatenate([normed(xprev_ref[...]), xn, normed(xnext_ref[...])], axis=0).astype(BF16)

    def proj(lhs, start, width):
        return jnp.dot(lhs, w_ref[:, start:start + width], preferred_element_type=F32)

    base = 3 * d_model
    half = C_HEADS * C_DIM
    qab = proj(xb, base + REL_QA, 2 * half)
    kvb = proj(xb, base + REL_KB, 2 * half)
    qkc = proj(ext, base + REL_QC, 2 * half)
    vo = proj(xb, base + REL_VC, 2 * half)
    oc_ref[...] = vo[:, half:].astype(BF16)
    kava = proj(xb, base + REL_KA, 2 * LANES)
    gates = jnp.dot(xb, wg_ref[...], preferred_element_type=F32)
    _prep_math(pl.program_id(1), pl.num_programs(1),
               qab[:, :half], kava, qab[:, half:], kvb[:, :half], kvb[:, half:],
               qkc[SUBLANES:SUBLANES + ROW], qkc[SUBLANES - 1:SUBLANES], qkc[SUBLANES + ROW:SUBLANES + ROW + 1],
               vo[:, :half], gates,
               cos_ref, sa_ref, sb_ref, qg_ref, kg_ref, bd4_ref, bd1_ref, cw_ref, cb_ref, gb_ref, tri_ref,
               *mixer_refs)
    for c in range(6):
        gate_ref[:, c * half:(c + 1) * half] = proj(xb, c * half, half).astype(BF16)


def _inproj(xs, mod, layer, nw, pre_g, w_main, w_gate, tables, qg, kg, bd4, bd1, conv_w, conv_b, gate_b, tri):
    (n_batch, ts, d), stream_specs, stream_arrays = _stream(xs)
    halo_src = stream_arrays[-1]
    tile_shift = len(stream_arrays) - 1
    n_main = w_main.shape[1]
    nt = ts // ROW
    cos_t, sa_t, sb_t = tables
    row_blocks = halo_src.shape[1] // SUBLANES
    per_tile = ROW // SUBLANES
    half = C_HEADS * C_DIM

    def const(shape):
        return pl.BlockSpec(shape, lambda b, i: (0,) * len(shape))

    def rows(width):
        return pl.BlockSpec((None, ROW, width), lambda b, i: (b, i, 0))

    def heads(n, width):
        return pl.BlockSpec((None, n, ROW, width), lambda b, i: (b, 0, i, 0))

    def cols(height):
        return pl.BlockSpec((None, height, ROW), lambda b, i: (b, 0, i))

    table = pl.BlockSpec((ROW, LANES), lambda b, i: (i, 0))
    in_specs = stream_specs + [
        pl.BlockSpec((None, SUBLANES, d), lambda b, i: (b, jnp.maximum((i - tile_shift) * per_tile - 1, 0), 0)),
        pl.BlockSpec((None, SUBLANES, d),
                     lambda b, i: (b, jnp.clip((i + 1 - tile_shift) * per_tile, 0, row_blocks - 1), 0)),
        _mod_spec(d, layer, 0, nw, n_batch),
        _mod_spec(d, layer, 1, nw, n_batch),
        const((1, d)),
        pl.BlockSpec((d, n_main), lambda b, i: (0, 0), pipeline_mode=pl.Buffered(1)),
        const((d, LANES)),
        table, table, table,
        const((1, half)), const((1, LANES)), const((half, half)), const((LANES, LANES)),
        const((CONV_W, 2 * half)), const((1, 2 * half)), const((1, LANES)), const((MCHUNK, MCHUNK)),
    ]
    outs = [
        (rows(3 * d), (ts, 3 * d), BF16),
        (rows(half), (ts, half), BF16),
        (heads(A_HEADS, LANES), (A_HEADS, ts, LANES), BF16),
        (rows(LANES), (ts, LANES), BF16),
        (heads(A_KV_HEADS, LANES), (A_KV_HEADS, ts, LANES), BF16),
        (heads(2 * B_HEADS, LANES), (2 * B_HEADS, ts, LANES), BF16),
        (heads(B_HEADS, LANES), (B_HEADS, ts, LANES), BF16),
        (heads(B_HEADS, 2 * LANES), (B_HEADS, ts, 2 * LANES), BF16),
        (cols(half), (half, ts), BF16),
        (rows(half), (ts, half), BF16),
        (cols(half), (half, ts), BF16),
        (rows(3 * LANES), (ts, 3 * LANES), F32),
        (cols(6 * SUBLANES), (6 * SUBLANES, ts), F32),
    ]
    return pl.pallas_call(
        functools.partial(_inproj_kernel, n_stream=len(stream_arrays)),
        grid=(n_batch, nt),
        in_specs=in_specs,
        out_specs=[spec for spec, _, _ in outs],
        out_shape=[jax.ShapeDtypeStruct((n_batch,) + shape, dtype) for _, shape, dtype in outs],
        compiler_params=_cparams(("parallel", "parallel")),
        name="inproj",
    )(*stream_arrays, halo_src, halo_src, mod, mod, pre_g, w_main, w_gate, cos_t, sa_t, sb_t, qg, kg, bd4, bd1, conv_w, conv_b, gate_b, tri)


def _head_mean_sq(x, bd_ref):
    return jnp.dot((x * x).astype(BF16), bd_ref[...], preferred_element_type=F32)


def _prep_math(i, nt, qa, kava, qb, kb, vb, cur, prev_row, next_row, vc, gates,
               cos_ref, sa_ref, sb_ref, qg_ref, kg_ref, bd4_ref, bd1_ref, cw_ref, cb_ref, gb_ref, tri_ref,
               qaz_ref, ka_ref, va_ref, qbz_ref, kbo_ref, vbo_ref, qmt_ref, km_ref, vmt_ref, gcol_ref, grow_ref):
    cos, sin_a, sin_b = cos_ref[...], sa_ref[...], sb_ref[...]

    def rope(x):
        width = x.shape[1]
        reps = width // LANES
        c = jnp.concatenate([cos] * reps, axis=1) if reps > 1 else cos
        a = jnp.concatenate([sin_a] * reps, axis=1) if reps > 1 else sin_a
        b = jnp.concatenate([sin_b] * reps, axis=1) if reps > 1 else sin_b
        return x * c + pltpu.roll(x, width - 16, 1) * a + pltpu.roll(x, 16, 1) * b

    lane = _lane_iota((ROW, LANES))
    ones = jnp.ones((ROW, LANES), BF16)
    scale = HEAD_DIM ** -0.5 * LOG2E

    qa = qa * lax.rsqrt(_head_mean_sq(qa, bd4_ref) + EPS) * qg_ref[...]
    qa = rope(qa) * scale
    heads_per_kv = A_HEADS // A_KV_HEADS
    for h in range(A_HEADS):
        g = h // heads_per_kv
        blk = qa[:, (h // 2) * LANES:(h // 2 + 1) * LANES]
        if h % 2 != g:
            blk = pltpu.roll(blk, HEAD_DIM, 1)
        qaz_ref[h] = jnp.where(lane // HEAD_DIM == g, blk, 0.0).astype(BF16)
    ka = kava[:, :LANES]
    ka = ka * lax.rsqrt(_head_mean_sq(ka, bd1_ref) + EPS) * kg_ref[...]
    ka_ref[...] = rope(ka).astype(BF16)
    va = kava[:, LANES:].astype(BF16)
    for g in range(A_KV_HEADS):
        va_ref[g] = jnp.where(lane // HEAD_DIM == g, va, ones)

    qb = rope(qb) * scale
    kb = rope(kb)
    for h in range(B_HEADS):
        blk = qb[:, h * LANES:(h + 1) * LANES]
        for m in range(2):
            qbz_ref[2 * h + m] = jnp.where(lane // HEAD_DIM == m, blk, 0.0).astype(BF16)
        kbo_ref[h] = kb[:, h * LANES:(h + 1) * LANES].astype(BF16)
        vbo_ref[h, :, :LANES] = vb[:, h * LANES:(h + 1) * LANES].astype(BF16)
        vbo_ref[h, :, LANES:] = ones

    row = _row_iota(cur.shape)
    prev_row = jnp.where(i >= 2, prev_row, 0.0)
    next_row = jnp.where(jnp.logical_and(i >= 1, i < nt - 1), next_row, 0.0)
    up = jnp.where(row == 0, prev_row, pltpu.roll(cur, 1, 0))
    dn = jnp.where(row == ROW - 1, next_row, pltpu.roll(cur, ROW - 1, 0))
    y = up * cw_ref[0:1, :] + cur * cw_ref[1:2, :] + dn * cw_ref[2:3, :] + cb_ref[...]
    y = _silu(y)
    half = C_HEADS * C_DIM
    qmt_ref[...] = y[:, :half].T.astype(BF16)
    km_ref[...] = (y[:, half:] * (C_DIM ** -0.5)).astype(BF16)
    vmt_ref[...] = vc.T.astype(BF16)

    gg = gates + gb_ref[...]
    is_forget = (lane // C_HEADS) % 2 == 1
    gl = jnp.where(is_forget, _log_sigmoid(gg), gg) * LOG2E
    tri = tri_ref[...]
    n_rows = 2 * SUBLANES
    for c in range(ROW // MCHUNK):
        rows = slice(c * MCHUNK, (c + 1) * MCHUNK)
        glc = gl[rows]
        hi = glc.astype(BF16)
        rest = glc - hi.astype(F32)
        mid = rest.astype(BF16)
        low = (rest - mid.astype(F32)).astype(BF16)
        cs = (jnp.dot(tri, hi, preferred_element_type=F32) + jnp.dot(tri, mid, preferred_element_type=F32)
              + jnp.dot(tri, low, preferred_element_type=F32))
        tot = jnp.broadcast_to(jnp.sum(glc, axis=0, keepdims=True), glc.shape)
        for f, val in enumerate((glc, cs, tot)):
            gcol_ref[rows, f * LANES:(f + 1) * LANES] = val
            grow_ref[f * n_rows:(f + 1) * n_rows, rows] = val.T[:n_rows, :]


def _attend_blocks(blocks, n_keys):
    def scores(q, k_ref):
        return lax.dot_general(q, k_ref[:n_keys, :], (((1,), (1,)), ((), ())), preferred_element_type=F32)

    def weighted(s, v_ref):
        p = jnp.exp2(s - jnp.max(s, axis=-1, keepdims=True)).astype(BF16)
        return jnp.dot(p, v_ref[:n_keys, :], preferred_element_type=F32)

    outs = []
    s_cur = scores(blocks[0][0], blocks[0][1])
    for j in range(1, len(blocks)):
        s_next = scores(blocks[j][0], blocks[j][1])
        outs.append(weighted(s_cur, blocks[j - 1][2]))
        s_cur = s_next
    outs.append(weighted(s_cur, blocks[-1][2]))
    return outs


def _per_tile_keys(body, n_all):
    @pl.when(pl.program_id(1) == 0)
    def _():
        body(ROW)

    @pl.when(pl.program_id(1) > 0)
    def _():
        body(n_all)


def _gqa_kernel(q_ref, k_ref, v_ref, o_ref):
    heads_per_kv = A_HEADS // A_KV_HEADS
    lane = _lane_iota((ROW, LANES))

    def body(n_keys):
        blocks = []
        for j in range(A_HEADS // 2):
            q = q_ref[2 * j:2 * j + 2].reshape(2 * ROW, LANES)
            blocks.append((q, k_ref, v_ref.at[(2 * j) // heads_per_kv]))
        for j, o in enumerate(_attend_blocks(blocks, n_keys)):
            g = (2 * j) // heads_per_kv
            den_lane = (1 - g) * HEAD_DIM
            o = o / o[:, den_lane:den_lane + 1]
            even, odd = o[:ROW], o[ROW:]
            even = even if g == 0 else pltpu.roll(even, HEAD_DIM, 1)
            odd = odd if g == 1 else pltpu.roll(odd, HEAD_DIM, 1)
            o_ref[:, j * LANES:(j + 1) * LANES] = jnp.where(lane < HEAD_DIM, even, odd).astype(BF16)

    _per_tile_keys(body, k_ref.shape[0])


def _diff_kernel(q_ref, k_ref, v_ref, lam_ref, g_ref, o_ref, *, lam_init):
    lv = lam_ref[...]
    lam = (jnp.exp(jnp.sum(lv[0:1] * lv[1:2], axis=-1, keepdims=True))
           - jnp.exp(jnp.sum(lv[2:3] * lv[3:4], axis=-1, keepdims=True)) + lam_init)

    def body(n_keys):
        blocks = [(q_ref[2 * h:2 * h + 2].reshape(2 * ROW, LANES), k_ref.at[h], v_ref.at[h]) for h in range(B_HEADS)]
        for h, o in enumerate(_attend_blocks(blocks, n_keys)):
            o = o[:, :LANES] / o[:, LANES:LANES + 1]
            dif = o[:ROW] - lam * o[ROW:]
            o_ref[:, h * LANES:(h + 1) * LANES] = (_rms(dif, g_ref[...]) * (1.0 - lam_init)).astype(BF16)

    _per_tile_keys(body, k_ref.shape[1])


def _gqa(qaz, ka, va):
    n_batch, _, ts, _ = qaz.shape
    nt = ts // ROW
    return pl.pallas_call(
        _gqa_kernel,
        grid=(n_batch, nt),
        in_specs=[
            pl.BlockSpec((None, A_HEADS, ROW, LANES), lambda b, i: (b, 0, i, 0)),
            pl.BlockSpec((None, ts, LANES), lambda b, i: (b, 0, 0)),
            pl.BlockSpec((None, A_KV_HEADS, ts, LANES), lambda b, i: (b, 0, 0, 0)),
        ],
        out_specs=pl.BlockSpec((None, ROW, A_HEADS * HEAD_DIM), lambda b, i: (b, i, 0)),
        out_shape=jax.ShapeDtypeStruct((n_batch, ts, A_HEADS * HEAD_DIM), BF16),
        compiler_params=_cparams(("parallel", "parallel")),
        name="gqa_attention",
    )(qaz, ka, va)


def _diff(qbz, kb, vb, lam_vecs, sub_g, lam_init):
    n_batch, _, ts, _ = qbz.shape
    nt = ts // ROW
    return pl.pallas_call(
        functools.partial(_diff_kernel, lam_init=lam_init),
        grid=(n_batch, nt),
        in_specs=[
            pl.BlockSpec((None, 2 * B_HEADS, ROW, LANES), lambda b, i: (b, 0, i, 0)),
            pl.BlockSpec((None, B_HEADS, ts, LANES), lambda b, i: (b, 0, 0, 0)),
            pl.BlockSpec((None, B_HEADS, ts, 2 * LANES), lambda b, i: (b, 0, 0, 0)),
            pl.BlockSpec((4, HEAD_DIM), lambda b, i: (0, 0)),
            pl.BlockSpec((1, LANES), lambda b, i: (0, 0)),
        ],
        out_specs=pl.BlockSpec((None, ROW, B_HEADS * LANES), lambda b, i: (b, i, 0)),
        out_shape=jax.ShapeDtypeStruct((n_batch, ts, B_HEADS * LANES), BF16),
        compiler_params=_cparams(("parallel", "parallel")),
        name="diff_attention",
    )(qbz, kb, vb, lam_vecs, sub_g)


AUG = 2 * SUBLANES


def _mlstm_kernel(qtf_ref, kf_ref, vtf_ref, gcf_ref, grf_ref, qtb_ref, kb_ref, vtb_ref, gcb_ref, grb_ref,
                  hf_ref, hb_ref, c_ref, m_ref):
    @pl.when(pl.program_id(1) == 0)
    def _():
        c_ref[...] = jnp.zeros_like(c_ref)
        m_ref[...] = jnp.zeros_like(m_ref)

    length = MCHUNK
    key_idx = _row_iota((length, length))
    qry_idx = _lane_iota((length, length))
    ones_rows = jnp.ones((AUG, length), BF16)
    n_rows = 2 * SUBLANES
    n_chain = 2 * C_HEADS
    c_states = [c_ref[ch] for ch in range(n_chain)]
    m_prevs = [m_ref[ch] for ch in range(n_chain)]
    c_news, m_news, h_outs, pending = [], [], [], []

    for direction, (qt_ref, k_ref, vt_ref, gc_ref, gr_ref, h_ref) in enumerate(
            ((qtf_ref, kf_ref, vtf_ref, gcf_ref, grf_ref, hf_ref), (qtb_ref, kb_ref, vtb_ref, gcb_ref, grb_ref, hb_ref))):
        reverse = direction == 1
        gate_c, cs_c, tot_c = gc_ref[:, :LANES], gc_ref[:, LANES:2 * LANES], gc_ref[:, 2 * LANES:]
        p_c = (tot_c - cs_c + gate_c) if reverse else cs_c
        g_c = gate_c - pltpu.roll(p_c, LANES - C_HEADS, 1)
        mask = (key_idx >= qry_idx) if reverse else (key_idx <= qry_idx)
        for hd in range(C_HEADS):
            chain = direction * C_HEADS + hd
            ii = direction * 2 * C_HEADS + hd
            fi = ii + C_HEADS
            i_row = gr_ref[ii:ii + 1, :]
            f_row = gr_ref[fi:fi + 1, :]
            cs_row = gr_ref[n_rows + fi:n_rows + fi + 1, :]
            tot_row = gr_ref[2 * n_rows + fi:2 * n_rows + fi + 1, :]
            p_row = (tot_row - cs_row + f_row) if reverse else cs_row
            m_prev = m_prevs[chain]
            inter = p_row + m_prev
            log_dt = jnp.where(mask, jnp.broadcast_to(g_c[:, ii:ii + 1], (length, length)) + p_row, NEG)
            m_t = jnp.maximum(inter, jnp.max(log_dt, axis=0, keepdims=True))
            d_t = jnp.exp2(log_dt - m_t)
            a_row = jnp.exp2(inter - m_t)
            sl = slice(hd * C_DIM, (hd + 1) * C_DIM)
            k_h, qt_h = k_ref[:, sl], qt_ref[sl, :]
            vt_aug = jnp.concatenate([vt_ref[sl, :], ones_rows], axis=0)
            s_raw = jnp.dot(k_h, qt_h, preferred_element_type=F32)
            c_state = c_states[chain]
            x_t = jnp.dot(c_state.astype(BF16), qt_h, preferred_element_type=F32)
            w_row = tot_row - p_row + i_row
            m_new = jnp.maximum(tot_row + m_prev, jnp.max(w_row, axis=-1, keepdims=True))
            decay = jnp.exp2(tot_row + m_prev - m_new)
            ws = jnp.exp2(w_row - m_new)
            update = jnp.dot((vt_aug.astype(F32) * ws).astype(BF16), k_h, preferred_element_type=F32)
            c_news.append(decay[:, :C_DIM] * c_state + update)
            m_news.append(m_new)
            pending.append((s_raw, d_t, vt_aug, a_row, x_t, m_t))

    for direction in range(2):
        h_parts = []
        for hd in range(C_HEADS):
            s_raw, d_t, vt_aug, a_row, x_t, m_t = pending[direction * C_HEADS + hd]
            y_t = jnp.dot(vt_aug, (s_raw * d_t).astype(BF16), preferred_element_type=F32)
            num_t = a_row * x_t[:C_DIM] + y_t[:C_DIM]
            den = a_row * x_t[C_DIM:C_DIM + 1] + y_t[C_DIM:C_DIM + 1]
            h_t = num_t / jnp.maximum(jnp.abs(den), jnp.exp2(-m_t))
            h_parts.append(h_t.T)
        h_outs.append(jnp.concatenate(h_parts, axis=1))

    hf_ref[...] = h_outs[0]
    hb_ref[...] = h_outs[1]
    for ch in range(n_chain):
        c_ref[ch] = c_news[ch]
        m_ref[ch] = m_news[ch]


def _mlstm(qmt, km, vmt, gcol, grow):
    n_batch, ts, width = km.shape
    nc = ts // MCHUNK
    ctx_chunks = ROW // MCHUNK

    def bwd(j):
        return jnp.where(j < ctx_chunks, ctx_chunks - 1 - j, nc + ctx_chunks - 1 - j)

    def specs(idx):
        return [
            pl.BlockSpec((None, width, MCHUNK), lambda b, j: (b, 0, idx(j))),
            pl.BlockSpec((None, MCHUNK, width), lambda b, j: (b, idx(j), 0)),
            pl.BlockSpec((None, width, MCHUNK), lambda b, j: (b, 0, idx(j))),
            pl.BlockSpec((None, MCHUNK, 3 * LANES), lambda b, j: (b, idx(j), 0)),
            pl.BlockSpec((None, 6 * SUBLANES, MCHUNK), lambda b, j: (b, 0, idx(j))),
        ]

    fwd = lambda j: j
    n_chain = 2 * C_HEADS
    return pl.pallas_call(
        _mlstm_kernel,
        grid=(n_batch, nc),
        in_specs=specs(fwd) + specs(bwd),
        out_specs=[
            pl.BlockSpec((None, MCHUNK, width), lambda b, j: (b, j, 0)),
            pl.BlockSpec((None, MCHUNK, width), lambda b, j: (b, bwd(j), 0)),
        ],
        out_shape=[jax.ShapeDtypeStruct((n_batch, ts, width), F32)] * 2,
        scratch_shapes=[
            pltpu.VMEM((n_chain, C_DIM + AUG, C_DIM), F32),
            pltpu.VMEM((n_chain, 1, MCHUNK), F32),
        ],
        compiler_params=_cparams(("parallel", "arbitrary")),
        name="mlstm",
    )(qmt, km, vmt, gcol, grow, qmt, km, vmt, gcol, grow)


def _mix_kernel(*refs, n_stream):
    (a_ref, d_ref, hf_ref, hb_ref, oc_ref, gate_ref, g1_ref, pg_ref, mg_ref,
     wa_ref, wb_ref, wc_ref, wo_ref, o_ref) = refs[n_stream:]
    d_model = o_ref.shape[-1]
    hsum = hf_ref[...] + hb_ref[...]
    mg = mg_ref[...]
    m = jnp.concatenate([_rms(hsum[:, hd * C_DIM:(hd + 1) * C_DIM], mg) for hd in range(C_HEADS)], axis=1)
    m = m * _sigmoid(oc_ref[...].astype(F32))
    u = (_sigmoid(gate_ref[:, :d_model].astype(F32))
         * jnp.dot(a_ref[...], wa_ref[...], preferred_element_type=F32)
         + _sigmoid(gate_ref[:, d_model:2 * d_model].astype(F32))
         * jnp.dot(d_ref[...], wb_ref[...], preferred_element_type=F32)
         + _sigmoid(gate_ref[:, 2 * d_model:].astype(F32))
         * jnp.dot(m.astype(BF16), wc_ref[...], preferred_element_type=F32))
    y = jnp.dot(u.astype(BF16), wo_ref[...], preferred_element_type=F32)
    o_ref[...] = _stream_tile(refs[:n_stream]) + g1_ref[...] * _rms(y, pg_ref[...])


def _mix(a, dd, hf, hb, out_gate, merge_gate, xs, mod, layer, nw, post_g, mlstm_g, wa, wb, wc, wo):
    (n_batch, ts, d), stream_specs, stream_arrays = _stream(xs)
    nt = ts // ROW
    width = a.shape[-1]

    def tile(w):
        return pl.BlockSpec((None, ROW, w), lambda b, i: (b, i, 0))

    def const(shape):
        return pl.BlockSpec(shape, lambda b, i: (0,) * len(shape))

    return pl.pallas_call(
        functools.partial(_mix_kernel, n_stream=len(stream_arrays)),
        grid=(n_batch, nt),
        in_specs=stream_specs + [
            tile(width), tile(width), tile(width), tile(width),
            tile(width), tile(3 * d),
            _mod_spec(d, layer, 2, nw, n_batch),
            const((1, d)), const((1, C_DIM)),
            const((width, d)), const((width, d)), const((width, d)), const((d, d)),
        ],
        out_specs=tile(d),
        out_shape=jax.ShapeDtypeStruct((n_batch, ts, d), F32),
        compiler_params=_cparams(("parallel", "parallel")),
        name="mix_out",
    )(*stream_arrays, a, dd, hf, hb, out_gate, merge_gate, mod, post_g, mlstm_g, wa, wb, wc, wo)


def _ffn_kernel(x_ref, sh_ref, sc_ref, g2_ref, pre_ref, post_ref, wg_ref, wu_ref, wd_ref, o_ref):
    x = x_ref[...]
    xb = (_rms(x, pre_ref[...]) * (1.0 + sc_ref[...]) + sh_ref[...]).astype(BF16)
    gate = jnp.dot(xb, wg_ref[...], preferred_element_type=F32)
    up = jnp.dot(xb, wu_ref[...], preferred_element_type=F32)
    z = jnp.dot((_silu(gate) * up).astype(BF16), wd_ref[...], preferred_element_type=F32)
    o_ref[...] = x + g2_ref[...] * _rms(z, post_ref[...])


def _ffn(xs, mod, layer, nw, pre_g, post_g, wg, wu, wd):
    n_batch, ts, d = xs.shape
    nt = ts // ROW
    dff = wg.shape[1]

    def resident(shape):
        return pl.BlockSpec(shape, lambda b, i: (0,) * len(shape), pipeline_mode=pl.Buffered(1))

    tile = pl.BlockSpec((None, ROW, d), lambda b, i: (b, i, 0))
    return pl.pallas_call(
        _ffn_kernel,
        grid=(n_batch, nt),
        in_specs=[
            tile,
            _mod_spec(d, layer, 3, nw, n_batch), _mod_spec(d, layer, 4, nw, n_batch),
            _mod_spec(d, layer, 5, nw, n_batch),
            pl.BlockSpec((1, d), lambda b, i: (0, 0)), pl.BlockSpec((1, d), lambda b, i: (0, 0)),
            resident((d, dff)), resident((d, dff)), resident((dff, d)),
        ],
        out_specs=tile,
        out_shape=jax.ShapeDtypeStruct((n_batch, ts, d), F32),
        compiler_params=_cparams(("parallel", "parallel")),
        name="ffn",
    )(xs, mod, mod, mod, pre_g, post_g, wg, wu, wd)


def _route_kernel(x_ref, sh_ref, sc_ref, pre_ref, wr_ref, br_ref, tri_ref, xn_ref, meta_ref, slots_ref, cnt_ref,
                  carry_ref):
    @pl.when(jnp.logical_and(pl.program_id(0) == 0, pl.program_id(1) == 0))
    def _():
        carry_ref[...] = jnp.zeros_like(carry_ref)

    xn = _rms(x_ref[...], pre_ref[...]) * (1.0 + sc_ref[...]) + sh_ref[...]
    xn_ref[...] = xn
    lane = _lane_iota((ROW, LANES))
    logits = jnp.dot(xn, wr_ref[...], preferred_element_type=F32, precision=HIGHEST) + br_ref[...]
    logits = jnp.where(lane < N_EXPERTS, logits, NEG)
    v1 = jnp.max(logits, axis=-1, keepdims=True)
    i1 = jnp.min(jnp.where(logits == v1, lane, LANES), axis=-1, keepdims=True)
    rest = jnp.where(lane == i1, NEG, logits)
    v2 = jnp.max(rest, axis=-1, keepdims=True)
    i2 = jnp.min(jnp.where(rest == v2, lane, LANES), axis=-1, keepdims=True)
    e2 = jnp.exp(v2 - v1)
    w1 = 1.0 / (1.0 + e2)
    w2 = e2 / (1.0 + e2)
    assigned = jnp.where(lane == i1, 1.0, jnp.where(lane == i2, 1.0, 0.0))
    before = jnp.dot(tri_ref[...], assigned.astype(BF16), preferred_element_type=F32) + carry_ref[0:1, :]
    r1 = jnp.sum(jnp.where(lane == i1, before, 0.0), axis=-1, keepdims=True)
    r2 = jnp.sum(jnp.where(lane == i2, before, 0.0), axis=-1, keepdims=True)
    carry_ref[...] = carry_ref[...] + jnp.sum(assigned, axis=0, keepdims=True)
    cnt_ref[...] = carry_ref[...]
    fields = (i1.astype(F32), i2.astype(F32), w1, w2, r1, r2)
    meta = jnp.zeros((ROW, LANES), F32)
    for f, val in enumerate(fields):
        meta = jnp.where(lane == f, val, meta)
    meta_ref[...] = meta
    slots_ref[...] = meta.T[:SUBLANES, :]


def _route(xs, mod, layer, nw, pre_g, w_r, b_r):
    n_batch, ts, d = xs.shape
    nt = ts // ROW
    tile = pl.BlockSpec((None, ROW, d), lambda b, i: (b, i, 0))
    t_idx = jnp.arange(ROW)
    tri_strict = (t_idx[:, None] > t_idx[None, :]).astype(BF16)
    return pl.pallas_call(
        _route_kernel,
        grid=(n_batch, nt),
        in_specs=[
            tile, _mod_spec(d, layer, 3, nw, n_batch), _mod_spec(d, layer, 4, nw, n_batch),
            pl.BlockSpec((1, d), lambda b, i: (0, 0)),
            pl.BlockSpec((d, LANES), lambda b, i: (0, 0)),
            pl.BlockSpec((1, LANES), lambda b, i: (0, 0)),
            pl.BlockSpec((ROW, ROW), lambda b, i: (0, 0)),
        ],
        out_specs=[tile, pl.BlockSpec((None, ROW, LANES), lambda b, i: (b, i, 0)),
                   pl.BlockSpec((None, SUBLANES, ROW), lambda b, i: (b, 0, i)),
                   pl.BlockSpec((SUBLANES, LANES), lambda b, i: (0, 0))],
        out_shape=[jax.ShapeDtypeStruct((n_batch, ts, d), F32),
                   jax.ShapeDtypeStruct((n_batch, ts, LANES), F32),
                   jax.ShapeDtypeStruct((n_batch, SUBLANES, ts), F32),
                   jax.ShapeDtypeStruct((SUBLANES, LANES), F32)],
        scratch_shapes=[pltpu.VMEM((SUBLANES, LANES), F32)],
        compiler_params=_cparams(("arbitrary", "arbitrary")),
        name="route",
    )(xs, mod, mod, pre_g, w_r, b_r, tri_strict)


GROUPS = ROW // SUBLANES


def _start_rows(make_copy):
    for g in range(GROUPS):
        for j in range(SUBLANES):
            for slot in range(2):
                make_copy(g, j, slot).start()


def _wait_rows(make_copy):
    def drain(g, carry):
        for j in range(SUBLANES):
            for slot in range(2):
                make_copy(0, 0, slot).wait()
        return carry

    lax.fori_loop(0, GROUPS, drain, 0)


STAGES = 3


def _dispatch_kernel(pos_ref, bounds_ref, xn_ref, out_ref, zero_ref, stage_ref, row_sems, in_sems, zero_sem):
    step = pl.program_id(0)
    n_steps = pl.num_programs(0)
    n_tok = pos_ref.shape[0] // 2
    tm = zero_ref.shape[0]
    n_sorted = out_ref.shape[0]

    @pl.when(step == 0)
    def _():
        zero_ref[...] = jnp.zeros_like(zero_ref)

        def fill(row):
            copy = pltpu.make_async_copy(zero_ref, out_ref.at[pl.ds(pl.multiple_of(row, tm), tm)], zero_sem)
            copy.start()
            copy.wait()

        for e in range(N_EXPERTS):
            end = bounds_ref[N_EXPERTS + e]

            @pl.when(end > bounds_ref[e])
            def _():
                fill(end - tm)

        last_end = bounds_ref[2 * N_EXPERTS - 1]

        def tail(k, carry):
            fill(last_end + k * tm)
            return carry

        lax.fori_loop(0, (n_sorted - last_end) // tm, tail, 0)

    def tile_in(tile):
        slot = tile % STAGES
        return pltpu.make_async_copy(xn_ref.at[pl.ds(tile * GROUPS, GROUPS)], stage_ref.at[slot], in_sems.at[slot])

    def copies(tile):
        def make_copy(g, j, slot):
            p = pos_ref[slot * n_tok + tile * ROW + g * SUBLANES + j]
            return pltpu.make_async_copy(stage_ref.at[tile % STAGES, g, pl.ds(j, 1)], out_ref.at[pl.ds(p, 1)],
                                         row_sems.at[tile % STAGES])
        return make_copy

    @pl.when(step == 0)
    def _():
        tile_in(step).start()

    @pl.when(step >= STAGES - 1)
    def _():
        _wait_rows(copies(step - (STAGES - 1)))

    @pl.when(step + 1 < n_steps)
    def _():
        tile_in(step + 1).start()

    tile_in(step).wait()
    _start_rows(copies(step))

    @pl.when(step == n_steps - 1)
    def _():
        for back in range(STAGES - 2, -1, -1):
            @pl.when(step - back >= 0)
            def _():
                _wait_rows(copies(step - back))


def _dispatch(pos, bounds, xn, n_sorted, tm):
    m_rows, d = xn.shape
    return pl.pallas_call(
        _dispatch_kernel,
        grid_spec=pltpu.PrefetchScalarGridSpec(
            num_scalar_prefetch=2,
            grid=(m_rows // ROW,),
            in_specs=[pl.BlockSpec(memory_space=pl.ANY)],
            out_specs=pl.BlockSpec(memory_space=pl.ANY),
            scratch_shapes=[pltpu.VMEM((tm, d), F32), pltpu.VMEM((STAGES, GROUPS, SUBLANES, d), F32),
                            pltpu.SemaphoreType.DMA((STAGES,)), pltpu.SemaphoreType.DMA((STAGES,)),
                            pltpu.SemaphoreType.DMA(())],
        ),
        out_shape=jax.ShapeDtypeStruct((n_sorted, d), F32),
        compiler_params=_cparams(("arbitrary",)),
        name="dispatch",
    )(pos, bounds, xn.reshape(m_rows // SUBLANES, SUBLANES, d))


def _experts_kernel(te_ref, x_ref, wg_ref, wu_ref, wd_ref, y_ref):
    used = te_ref[pl.program_id(0)] < N_EXPERTS

    @pl.when(used)
    def _():
        x = x_ref[...].astype(BF16)
        gate = jnp.dot(x, wg_ref[...], preferred_element_type=F32)
        up = jnp.dot(x, wu_ref[...], preferred_element_type=F32)
        y_ref[...] = jnp.dot((_silu(gate) * up).astype(BF16), wd_ref[...], preferred_element_type=F32)

    @pl.when(jnp.logical_not(used))
    def _():
        y_ref[...] = jnp.zeros_like(y_ref)


def _experts(tile_expert, x_sorted, wg, wu, wd, tm):
    n_sorted, d = x_sorted.shape
    n_e, _, dff = wg.shape

    def weight(shape):
        return pl.BlockSpec((None,) + shape, lambda i, te: (jnp.minimum(te[i], n_e - 1), 0, 0))

    return pl.pallas_call(
        _experts_kernel,
        grid_spec=pltpu.PrefetchScalarGridSpec(
            num_scalar_prefetch=1,
            grid=(n_sorted // tm,),
            in_specs=[pl.BlockSpec((tm, d), lambda i, te: (i, 0)),
                      weight((d, dff)), weight((d, dff)), weight((dff, d))],
            out_specs=pl.BlockSpec((tm, d), lambda i, te: (i, 0)),
        ),
        out_shape=jax.ShapeDtypeStruct((n_sorted, d), F32),
        compiler_params=_cparams(("arbitrary",)),
        name="experts",
    )(tile_expert, x_sorted, wg, wu, wd)


def _combine_kernel(pos_ref, x_ref, meta_ref, g2_ref, post_ref, y_ref, o_ref, buf_ref, sems, *,
                    tiles_per_sample, latent_only):
    step = pl.program_id(0)
    n_steps = pl.num_programs(0)
    n_tok = pos_ref.shape[0] // 2
    d_model = x_ref.shape[-1]

    def wanted(tile):
        return (tile % tiles_per_sample != 0) if latent_only else (tile >= 0)

    def copies(tile):
        def make_copy(g, j, slot):
            p = pos_ref[slot * n_tok + tile * ROW + g * SUBLANES + j]
            return pltpu.make_async_copy(y_ref.at[pl.ds(p, 1)], buf_ref.at[tile % 2, slot, g, pl.ds(j, 1)],
                                         sems.at[tile % 2])
        return make_copy

    @pl.when(jnp.logical_and(step == 0, wanted(step)))
    def _():
        _start_rows(copies(step))

    @pl.when(jnp.logical_and(step + 1 < n_steps, wanted(step + 1)))
    def _():
        _start_rows(copies(step + 1))

    @pl.when(wanted(step))
    def _():
        _wait_rows(copies(step))
        meta = meta_ref[...]
        cur = step % 2
        y1 = buf_ref[cur, 0].reshape(ROW, d_model)
        y2 = buf_ref[cur, 1].reshape(ROW, d_model)
        z = meta[:, 2:3] * y1 + meta[:, 3:4] * y2
        o_ref[...] = x_ref[...] + g2_ref[...] * _rms(z, post_ref[...])


def _combine(pos, xs, meta, y_sorted, mod, layer, nw, post_g, latent_only):
    n_batch, ts, d = xs.shape
    nt = ts // ROW
    m_rows = n_batch * ts
    base = (layer * 6 + 5) * nw
    tile = pl.BlockSpec((ROW, d), lambda i, pos: (i, 0))
    if latent_only:
        out_rows = n_batch * (ts - ROW)
        out_tile = pl.BlockSpec((ROW, d), lambda i, pos: ((i // nt) * (nt - 1) + jnp.maximum(i % nt - 1, 0), 0))
    else:
        out_rows, out_tile = m_rows, tile
    out = pl.pallas_call(
        functools.partial(_combine_kernel, tiles_per_sample=nt, latent_only=latent_only),
        grid_spec=pltpu.PrefetchScalarGridSpec(
            num_scalar_prefetch=1,
            grid=(m_rows // ROW,),
            in_specs=[
                tile,
                pl.BlockSpec((ROW, LANES), lambda i, pos: (i, 0)),
                pl.BlockSpec((None, 1, d), lambda i, pos: (base + jnp.where(i % nt == 0, n_batch, i // nt), 0, 0)),
                pl.BlockSpec((1, d), lambda i, pos: (0, 0)),
                pl.BlockSpec(memory_space=pl.ANY),
            ],
            out_specs=out_tile,
            scratch_shapes=[pltpu.VMEM((2, 2, GROUPS, SUBLANES, d), F32), pltpu.SemaphoreType.DMA((2,))],
        ),
        out_shape=jax.ShapeDtypeStruct((out_rows, d), F32),
        compiler_params=_cparams(("arbitrary",)),
        name="combine",
    )(pos, xs.reshape(m_rows, d), meta.reshape(m_rows, LANES), mod, post_g, y_sorted)
    return out.reshape(n_batch, out_rows // n_batch, d)


def _moe(xs, mod, layer, nw, pre_g, post_g, w_r, b_r, wg, wu, wd, latent_only, tm=512):
    n_batch, ts, d = xs.shape
    m_rows = n_batch * ts
    xn, meta, slots, cnt = _route(xs, mod, layer, nw, pre_g, w_r, b_r)
    i1, i2, r1, r2 = (slots[:, f, :].reshape(m_rows).astype(jnp.int32) for f in (0, 1, 4, 5))
    counts = cnt[0, :N_EXPERTS].astype(jnp.int32)
    padded = -(-counts // tm) * tm
    ends = jnp.cumsum(padded)
    start = ends - padded
    pos = jnp.concatenate([start[i1] + r1, start[i2] + r2])
    bounds = jnp.concatenate([start, ends])
    n_tiles = 2 * m_rows // tm + N_EXPERTS
    tile_row = jnp.arange(n_tiles, dtype=jnp.int32) * tm
    tile_expert = jnp.sum((ends[None, :] <= tile_row[:, None]).astype(jnp.int32), axis=1)
    x_sorted = _dispatch(pos, bounds, xn.reshape(m_rows, d), n_tiles * tm, tm)
    y_sorted = _experts(tile_expert, x_sorted, wg, wu, wd, tm)
    return _combine(pos, xs, meta, y_sorted, mod, layer, nw, post_g, latent_only)


def _rope_tables(n_tok, n_ctx):
    n_freq = HEAD_DIM // 4
    pos = jnp.arange(n_tok)
    row = (pos // GRID_W).astype(F32)
    colp = (pos % GRID_W).astype(F32)
    inv = ROPE_THETA ** (-jnp.arange(n_freq, dtype=F32) / n_freq)
    lane = jnp.arange(LANES)
    in_head = lane % HEAD_DIM
    use_col = (in_head // (HEAD_DIM // 2)) == 1
    freq = inv[in_head % n_freq]
    ang = jnp.where(use_col[None, :], colp[:, None], row[:, None]) * freq[None, :]
    lower = (in_head % (HEAD_DIM // 2)) < n_freq
    cos_t = jnp.cos(ang)
    sin_t = jnp.sin(ang)
    sin_a = jnp.where(lower[None, :], -sin_t, 0.0)
    sin_b = jnp.where(lower[None, :], 0.0, sin_t)
    pad = lambda t, v: jnp.concatenate([jnp.full((n_ctx, LANES), v, F32), t], axis=0)
    return pad(cos_t, 1.0), pad(sin_a, 0.0), pad(sin_b, 0.0)


def _block_diag_mean(width):
    idx = jnp.arange(width) // HEAD_DIM
    return jnp.where(idx[:, None] == idx[None, :], 1.0 / HEAD_DIM, 0.0).astype(BF16)


def _pack_w_in(w):
    a_q = A_HEADS * HEAD_DIM
    a_kv = 2 * A_KV_HEADS * HEAD_DIM
    n_gates = 4 * C_HEADS
    gate_start = w.shape[1] - 3 * w.shape[0]
    g_start = gate_start - n_gates
    main = jnp.concatenate([w[:, gate_start:], w[:, :a_q], w[:, a_q + a_kv:g_start], w[:, a_q:a_q + a_kv]], axis=1)
    gates = jnp.pad(w[:, g_start:gate_start], ((0, 0), (0, LANES - n_gates)))
    return main.astype(BF16), gates.astype(BF16)


def kernel(x, c, ctx, c_ctx, ada_w, ada_b, pre_mix_g, post_mix_g, pre_ffn_g, post_ffn_g, w_in, q_norm_g, k_norm_g, lam_q1, lam_k1, lam_q2, lam_k2, diff_norm_g, conv_w, conv_b, mlstm_gate_b, mlstm_norm_g, w_br_attn, w_br_diff, w_br_mlstm, w_out, w_ff_gate, w_ff_up, w_ff_down, w_router, b_router, w_moe_gate, w_moe_up, w_moe_down):
    n_batch, n_tok, d = x.shape
    n_ctx = ctx.shape[1]
    depth = ada_w.shape[0]
    assert n_ctx == ROW and n_tok % ROW == 0 and d == 1024
    ts = n_ctx + n_tok
    nw = -(-(n_batch + 1) // SUBLANES) * SUBLANES

    c_all = jnp.concatenate([c, c_ctx[None, :], jnp.zeros((nw - n_batch - 1, d), F32)], axis=0)
    mod = _modulation(c_all, ada_w, ada_b)
    tables = _rope_tables(n_tok, n_ctx)
    bd4, bd1 = _block_diag_mean(A_HEADS * HEAD_DIM), _block_diag_mean(LANES)
    t_idx = jnp.arange(MCHUNK)
    tri = (t_idx[:, None] >= t_idx[None, :]).astype(BF16)

    xs = (ctx, x)
    for l in range(depth):
        lam_init = 0.8 - 0.6 * math.exp(-0.3 * l)
        w_main, w_gate = _pack_w_in(w_in[l])
        qg = jnp.tile(q_norm_g[l], A_HEADS)[None, :]
        kg = jnp.tile(k_norm_g[l], A_KV_HEADS)[None, :]
        gate_b = jnp.pad(mlstm_gate_b[l], (0, LANES - 4 * C_HEADS))[None, :]
        merge_gate, out_gate, qaz, ka, va, qbz, kb, vb, qmt, km, vmt, gcol, grow = _inproj(
            xs, mod, l, nw, pre_mix_g[l][None, :], w_main, w_gate,
            tables, qg, kg, bd4, bd1, conv_w[l], conv_b[l][None, :], gate_b, tri)
        a_out = _gqa(qaz, ka, va)
        lam_vecs = jnp.stack([lam_q1[l], lam_k1[l], lam_q2[l], lam_k2[l]], axis=0)
        d_out = _diff(qbz, kb, vb, lam_vecs, diff_norm_g[l][None, :], lam_init)
        hf, hb = _mlstm(qmt, km, vmt, gcol, grow)
        xs = _mix(a_out, d_out, hf, hb, out_gate, merge_gate, xs, mod, l, nw,
                  post_mix_g[l][None, :], mlstm_norm_g[l][None, :],
                  w_br_attn[l].astype(BF16), w_br_diff[l].astype(BF16), w_br_mlstm[l].astype(BF16),
                  w_out[l].astype(BF16))
        j = l // 2
        if l % 2 == 0:
            xs = _ffn(xs, mod, l, nw, pre_ffn_g[l][None, :], post_ffn_g[l][None, :],
                      w_ff_gate[j].astype(BF16), w_ff_up[j].astype(BF16), w_ff_down[j].astype(BF16))
        else:
            w_r = jnp.pad(w_router[j], ((0, 0), (0, LANES - N_EXPERTS)))
            b_r = jnp.pad(b_router[j], (0, LANES - N_EXPERTS))[None, :]
            xs = _moe(xs, mod, l, nw, pre_ffn_g[l][None, :], post_ffn_g[l][None, :], w_r, b_r,
                      w_moe_gate[j].astype(BF16), w_moe_up[j].astype(BF16), w_moe_down[j].astype(BF16),
                      latent_only=l == depth - 1)
    return xs if xs.shape[1] == n_tok else xs[:, n_ctx:, :]
```

```python
import functools
import math

import jax
import jax.numpy as jnp
import numpy as np
from jax import lax
from jax.experimental import pallas as pl
from jax.experimental.pallas import tpu as pltpu

F32 = jnp.float32
BF16 = jnp.bfloat16
HIGHEST = lax.Precision.HIGHEST

EPS = 1e-6
HEAD_DIM = 64
A_HEADS = 8
A_KV_HEADS = 2
B_HEADS = 4
C_HEADS = 4
C_DIM = 128
N_EXPERTS = 8
ROPE_THETA = 10000.0
GRID_W = 64
CONV_W = 3

LANES = 128
SUBLANES = 8
ROW = 256
MCHUNK = 256
LOG2E = math.log2(math.e)
NEG = -1e30
VMEM_LIMIT = 56 * 1024 * 1024

OFF_GATE = 0
REL_QA, REL_QB, REL_KB, REL_VB = 0, 512, 1024, 1536
REL_QC, REL_KC, REL_VC, REL_OC = 2048, 2560, 3072, 3584
REL_KA = 4096
REL_END = 4352


def _cparams(sem):
    return pltpu.CompilerParams(dimension_semantics=sem, vmem_limit_bytes=VMEM_LIMIT)


def _rms(x, g):
    y = x * lax.rsqrt(jnp.mean(x * x, axis=-1, keepdims=True) + EPS)
    return y * g


def _sigmoid(x):
    return 1.0 / (1.0 + jnp.exp(-x))


def _silu(x):
    return x * _sigmoid(x)


def _log_sigmoid(x):
    return jnp.minimum(x, 0.0) - jnp.log(1.0 + jnp.exp(-jnp.abs(x)))


def _lane_iota(shape):
    return lax.broadcasted_iota(jnp.int32, shape, len(shape) - 1)


def _row_iota(shape):
    return lax.broadcasted_iota(jnp.int32, shape, len(shape) - 2)


def _mod_kernel(c_ref, w_ref, b_ref, o_ref):
    c = c_ref[...]
    o_ref[...] = jnp.dot(_silu(c), w_ref[...], preferred_element_type=F32, precision=HIGHEST) + b_ref[...]


def _modulation(c_all, ada_w, ada_b):
    depth, d, _ = ada_w.shape
    nw = c_all.shape[0]
    out = pl.pallas_call(
        _mod_kernel,
        grid=(depth, 6),
        in_specs=[
            pl.BlockSpec((nw, d), lambda l, j: (0, 0)),
            pl.BlockSpec((None, d, d), lambda l, j: (l, 0, j)),
            pl.BlockSpec((None, 1, d), lambda l, j: (l, 0, j)),
        ],
        out_specs=pl.BlockSpec((None, None, nw, d), lambda l, j: (l, j, 0, 0)),
        out_shape=jax.ShapeDtypeStruct((depth, 6, nw, d), F32),
        compiler_params=_cparams(("arbitrary", "arbitrary")),
        name="modulation",
    )(c_all, ada_w, ada_b.reshape(depth, 1, 6 * d))
    return out.reshape(depth * 6 * nw, 1, d)


def _mod_spec(d, layer, chunk, nw, n_batch):
    base = (layer * 6 + chunk) * nw
    return pl.BlockSpec((None, 1, d), lambda b, i: (base + jnp.where(i == 0, n_batch, b), 0, 0))


def _stream(stream):
    if isinstance(stream, tuple):
        ctx, x = stream
        n_batch, n_tok, d = x.shape
        specs = [pl.BlockSpec((None, ROW, d), lambda b, i: (b, 0, 0)),
                 pl.BlockSpec((None, ROW, d), lambda b, i: (b, jnp.maximum(i - 1, 0), 0))]
        return (n_batch, ctx.shape[1] + n_tok, d), specs, [ctx, x]
    d = stream.shape[-1]
    return stream.shape, [pl.BlockSpec((None, ROW, d), lambda b, i: (b, i, 0))], [stream]


def _stream_tile(refs):
    if len(refs) == 2:
        return jnp.where(pl.program_id(1) == 0, refs[0][...], refs[1][...])
    return refs[0][...]


def _inproj_kernel(*refs, n_stream):
    (xprev_ref, xnext_ref, sh_ref, sc_ref, g_ref, w_ref, wg_ref,
     cos_ref, sa_ref, sb_ref, qg_ref, kg_ref, bd4_ref, bd1_ref, cw_ref, cb_ref, gb_ref, tri_ref,
     gate_ref, oc_ref, *mixer_refs) = refs[n_stream:]
    d_model = xprev_ref.shape[-1]

    def normed(x):
        return _rms(x, g_ref[...]) * (1.0 + sc_ref[...]) + sh_ref[...]

    xn = normed(_stream_tile(refs[:n_stream]))
    xb = xn.astype(BF16)
    ext = jnp.concatenate([normed(xprev_ref[...]), xn, normed(xnext_ref[...])], axis=0).astype(BF16)

    def proj(lhs, start, width):
        return jnp.dot(lhs, w_ref[:, start:start + width], preferred_element_type=F32)

    base = 3 * d_model
    half = C_HEADS * C_DIM
    qab = proj(xb, base + REL_QA, 2 * half)
    kvb = proj(xb, base + REL_KB, 2 * half)
    qkc = proj(ext, base + REL_QC, 2 * half)
    vo = proj(xb, base + REL_VC, 2 * half)
    oc_ref[...] = vo[:, half:].astype(BF16)
    kava = proj(xb, base + REL_KA, 2 * LANES)
    gates = jnp.dot(xb, wg_ref[...], preferred_element_type=F32)
    _prep_math(pl.program_id(1), pl.num_programs(1),
               qab[:, :half], kava, qab[:, half:], kvb[:, :half], kvb[:, half:],
               qkc[SUBLANES:SUBLANES + ROW], qkc[SUBLANES - 1:SUBLANES], qkc[SUBLANES + ROW:SUBLANES + ROW + 1],
               vo[:, :half], gates,
               cos_ref, sa_ref, sb_ref, qg_ref, kg_ref, bd4_ref, bd1_ref, cw_ref, cb_ref, gb_ref, tri_ref,
               *mixer_refs)
    for c in range(6):
        gate_ref[:, c * half:(c + 1) * half] = proj(xb, c * half, half).astype(BF16)


def _inproj(xs, mod, layer, nw, pre_g, w_main, w_gate, tables, qg, kg, bd4, bd1, conv_w, conv_b, gate_b, tri):
    (n_batch, ts, d), stream_specs, stream_arrays = _stream(xs)
    halo_src = stream_arrays[-1]
    tile_shift = len(stream_arrays) - 1
    n_main = w_main.shape[1]
    nt = ts // ROW
    cos_t, sa_t, sb_t = tables
    row_blocks = halo_src.shape[1] // SUBLANES
    per_tile = ROW // SUBLANES
    half = C_HEADS * C_DIM

    def const(shape):
        return pl.BlockSpec(shape, lambda b, i: (0,) * len(shape))

    def rows(width):
        return pl.BlockSpec((None, ROW, width), lambda b, i: (b, i, 0))

    def heads(n, width):
        return pl.BlockSpec((None, n, ROW, width), lambda b, i: (b, 0, i, 0))

    def cols(height):
        return pl.BlockSpec((None, height, ROW), lambda b, i: (b, 0, i))

    table = pl.BlockSpec((ROW, LANES), lambda b, i: (i, 0))
    in_specs = stream_specs + [
        pl.BlockSpec((None, SUBLANES, d), lambda b, i: (b, jnp.maximum((i - tile_shift) * per_tile - 1, 0), 0)),
        pl.BlockSpec((None, SUBLANES, d),
                     lambda b, i: (b, jnp.clip((i + 1 - tile_shift) * per_tile, 0, row_blocks - 1), 0)),
        _mod_spec(d, layer, 0, nw, n_batch),
        _mod_spec(d, layer, 1, nw, n_batch),
        const((1, d)),
        pl.BlockSpec((d, n_main), lambda b, i: (0, 0), pipeline_mode=pl.Buffered(1)),
        const((d, LANES)),
        table, table, table,
        const((1, half)), const((1, LANES)), const((half, half)), const((LANES, LANES)),
        const((CONV_W, 2 * half)), const((1, 2 * half)), const((1, LANES)), const((MCHUNK, MCHUNK)),
    ]
    outs = [
        (rows(3 * d), (ts, 3 * d), BF16),
        (rows(half), (ts, half), BF16),
        (heads(A_HEADS, LANES), (A_HEADS, ts, LANES), BF16),
        (rows(LANES), (ts, LANES), BF16),
        (heads(A_KV_HEADS, LANES), (A_KV_HEADS, ts, LANES), BF16),
        (heads(2 * B_HEADS, LANES), (2 * B_HEADS, ts, LANES), BF16),
        (heads(B_HEADS, LANES), (B_HEADS, ts, LANES), BF16),
        (heads(B_HEADS, 2 * LANES), (B_HEADS, ts, 2 * LANES), BF16),
        (cols(half), (half, ts), BF16),
        (rows(half), (ts, half), BF16),
        (cols(half), (half, ts), BF16),
        (rows(3 * LANES), (ts, 3 * LANES), F32),
        (cols(6 * SUBLANES), (6 * SUBLANES, ts), F32),
    ]
    return pl.pallas_call(
        functools.partial(_inproj_kernel, n_stream=len(stream_arrays)),
        grid=(n_batch, nt),
        in_specs=in_specs,
        out_specs=[spec for spec, _, _ in outs],
        out_shape=[jax.ShapeDtypeStruct((n_batch,) + shape, dtype) for _, shape, dtype in outs],
        compiler_params=_cparams(("parallel", "parallel")),
        name="inproj",
    )(*stream_arrays, halo_src, halo_src, mod, mod, pre_g, w_main, w_gate, cos_t, sa_t, sb_t, qg, kg, bd4, bd1, conv_w, conv_b, gate_b, tri)


def _head_mean_sq(x, bd_ref):
    return jnp.dot((x * x).astype(BF16), bd_ref[...], preferred_element_type=F32)


def _prep_math(i, nt, qa, kava, qb, kb, vb, cur, prev_row, next_row, vc, gates,
               cos_ref, sa_ref, sb_ref, qg_ref, kg_ref, bd4_ref, bd1_ref, cw_ref, cb_ref, gb_ref, tri_ref,
               qaz_ref, ka_ref, va_ref, qbz_ref, kbo_ref, vbo_ref, qmt_ref, km_ref, vmt_ref, gcol_ref, grow_ref):
    cos, sin_a, sin_b = cos_ref[...], sa_ref[...], sb_ref[...]

    def rope(x):
        width = x.shape[1]
        reps = width // LANES
        c = jnp.concatenate([cos] * reps, axis=1) if reps > 1 else cos
        a = jnp.concatenate([sin_a] * reps, axis=1) if reps > 1 else sin_a
        b = jnp.concatenate([sin_b] * reps, axis=1) if reps > 1 else sin_b
        return x * c + pltpu.roll(x, width - 16, 1) * a + pltpu.roll(x, 16, 1) * b

    lane = _lane_iota((ROW, LANES))
    ones = jnp.ones((ROW, LANES), BF16)
    scale = HEAD_DIM ** -0.5 * LOG2E

    qa = qa * lax.rsqrt(_head_mean_sq(qa, bd4_ref) + EPS) * qg_ref[...]
    qa = rope(qa) * scale
    heads_per_kv = A_HEADS // A_KV_HEADS
    for h in range(A_HEADS):
        g = h // heads_per_kv
        blk = qa[:, (h // 2) * LANES:(h // 2 + 1) * LANES]
        if h % 2 != g:
            blk = pltpu.roll(blk, HEAD_DIM, 1)
        qaz_ref[h] = jnp.where(lane // HEAD_DIM == g, blk, 0.0).astype(BF16)
    ka = kava[:, :LANES]
    ka = ka * lax.rsqrt(_head_mean_sq(ka, bd1_ref) + EPS) * kg_ref[...]
    ka_ref[...] = rope(ka).astype(BF16)
    va = kava[:, LANES:].astype(BF16)
    for g in range(A_KV_HEADS):
        va_ref[g] = jnp.where(lane // HEAD_DIM == g, va, ones)

    qb = rope(qb) * scale
    kb = rope(kb)
    for h in range(B_HEADS):
        blk = qb[:, h * LANES:(h + 1) * LANES]
        for m in range(2):
            qbz_ref[2 * h + m] = jnp.where(lane // HEAD_DIM == m, blk, 0.0).astype(BF16)
        kbo_ref[h] = kb[:, h * LANES:(h + 1) * LANES].astype(BF16)
        vbo_ref[h, :, :LANES] = vb[:, h * LANES:(h + 1) * LANES].astype(BF16)
        vbo_ref[h, :, LANES:] = ones

    row = _row_iota(cur.shape)
    prev_row = jnp.where(i >= 2, prev_row, 0.0)
    next_row = jnp.where(jnp.logical_and(i >= 1, i < nt - 1), next_row, 0.0)
    up = jnp.where(row == 0, prev_row, pltpu.roll(cur, 1, 0))
    dn = jnp.where(row == ROW - 1, next_row, pltpu.roll(cur, ROW - 1, 0))
    y = up * cw_ref[0:1, :] + cur * cw_ref[1:2, :] + dn * cw_ref[2:3, :] + cb_ref[...]
    y = _silu(y)
    half = C_HEADS * C_DIM
    qmt_ref[...] = y[:, :half].T.astype(BF16)
    km_ref[...] = (y[:, half:] * (C_DIM ** -0.5)).astype(BF16)
    vmt_ref[...] = vc.T.astype(BF16)

    gg = gates + gb_ref[...]
    is_forget = (lane // C_HEADS) % 2 == 1
    gl = jnp.where(is_forget, _log_sigmoid(gg), gg) * LOG2E
    tri = tri_ref[...]
    n_rows = 2 * SUBLANES
    for c in range(ROW // MCHUNK):
        rows = slice(c * MCHUNK, (c + 1) * MCHUNK)
        glc = gl[rows]
        hi = glc.astype(BF16)
        rest = glc - hi.astype(F32)
        mid = rest.astype(BF16)
        low = (rest - mid.astype(F32)).astype(BF16)
        cs = (jnp.dot(tri, hi, preferred_element_type=F32) + jnp.dot(tri, mid, preferred_element_type=F32)
              + jnp.dot(tri, low, preferred_element_type=F32))
        tot = jnp.broadcast_to(jnp.sum(glc, axis=0, keepdims=True), glc.shape)
        for f, val in enumerate((glc, cs, tot)):
            gcol_ref[rows, f * LANES:(f + 1) * LANES] = val
            grow_ref[f * n_rows:(f + 1) * n_rows, rows] = val.T[:n_rows, :]


def _attend_blocks(blocks, n_keys):
    def scores(q, k_ref):
        return lax.dot_general(q, k_ref[:n_keys, :], (((1,), (1,)), ((), ())), preferred_element_type=F32)

    def weighted(s, v_ref):
        p = jnp.exp2(s - jnp.max(s, axis=-1, keepdims=True)).astype(BF16)
        return jnp.dot(p, v_ref[:n_keys, :], preferred_element_type=F32)

    outs = []
    s_cur = scores(blocks[0][0], blocks[0][1])
    for j in range(1, len(blocks)):
        s_next = scores(blocks[j][0], blocks[j][1])
        outs.append(weighted(s_cur, blocks[j - 1][2]))
        s_cur = s_next
    outs.append(weighted(s_cur, blocks[-1][2]))
    return outs


def _per_tile_keys(body, n_all):
    @pl.when(pl.program_id(1) == 0)
    def _():
        body(ROW)

    @pl.when(pl.program_id(1) > 0)
    def _():
        body(n_all)


def _gqa_kernel(q_ref, k_ref, v_ref, o_ref):
    heads_per_kv = A_HEADS // A_KV_HEADS
    lane = _lane_iota((ROW, LANES))

    def body(n_keys):
        blocks = []
        for j in range(A_HEADS // 2):
            q = q_ref[2 * j:2 * j + 2].reshape(2 * ROW, LANES)
            blocks.append((q, k_ref, v_ref.at[(2 * j) // heads_per_kv]))
        for j, o in enumerate(_attend_blocks(blocks, n_keys)):
            g = (2 * j) // heads_per_kv
            den_lane = (1 - g) * HEAD_DIM
            o = o / o[:, den_lane:den_lane + 1]
            even, odd = o[:ROW], o[ROW:]
            even = even if g == 0 else pltpu.roll(even, HEAD_DIM, 1)
            odd = odd if g == 1 else pltpu.roll(odd, HEAD_DIM, 1)
            o_ref[:, j * LANES:(j + 1) * LANES] = jnp.where(lane < HEAD_DIM, even, odd).astype(BF16)

    _per_tile_keys(body, k_ref.shape[0])


def _diff_kernel(q_ref, k_ref, v_ref, lam_ref, g_ref, o_ref, *, lam_init):
    lv = lam_ref[...]
    lam = (jnp.exp(jnp.sum(lv[0:1] * lv[1:2], axis=-1, keepdims=True))
           - jnp.exp(jnp.sum(lv[2:3] * lv[3:4], axis=-1, keepdims=True)) + lam_init)

    def body(n_keys):
        blocks = [(q_ref[2 * h:2 * h + 2].reshape(2 * ROW, LANES), k_ref.at[h], v_ref.at[h]) for h in range(B_HEADS)]
        for h, o in enumerate(_attend_blocks(blocks, n_keys)):
            o = o[:, :LANES] / o[:, LANES:LANES + 1]
            dif = o[:ROW] - lam * o[ROW:]
            o_ref[:, h * LANES:(h + 1) * LANES] = (_rms(dif, g_ref[...]) * (1.0 - lam_init)).astype(BF16)

    _per_tile_keys(body, k_ref.shape[1])


def _gqa(qaz, ka, va):
    n_batch, _, ts, _ = qaz.shape
    nt = ts // ROW
    return pl.pallas_call(
        _gqa_kernel,
        grid=(n_batch, nt),
        in_specs=[
            pl.BlockSpec((None, A_HEADS, ROW, LANES), lambda b, i: (b, 0, i, 0)),
            pl.BlockSpec((None, ts, LANES), lambda b, i: (b, 0, 0)),
            pl.BlockSpec((None, A_KV_HEADS, ts, LANES), lambda b, i: (b, 0, 0, 0)),
        ],
        out_specs=pl.BlockSpec((None, ROW, A_HEADS * HEAD_DIM), lambda b, i: (b, i, 0)),
        out_shape=jax.ShapeDtypeStruct((n_batch, ts, A_HEADS * HEAD_DIM), BF16),
        compiler_params=_cparams(("parallel", "parallel")),
        name="gqa_attention",
    )(qaz, ka, va)


def _diff(qbz, kb, vb, lam_vecs, sub_g, lam_init):
    n_batch, _, ts, _ = qbz.shape
    nt = ts // ROW
    return pl.pallas_call(
        functools.partial(_diff_kernel, lam_init=lam_init),
        grid=(n_batch, nt),
        in_specs=[
            pl.BlockSpec((None, 2 * B_HEADS, ROW, LANES), lambda b, i: (b, 0, i, 0)),
            pl.BlockSpec((None, B_HEADS, ts, LANES), lambda b, i: (b, 0, 0, 0)),
            pl.BlockSpec((None, B_HEADS, ts, 2 * LANES), lambda b, i: (b, 0, 0, 0)),
            pl.BlockSpec((4, HEAD_DIM), lambda b, i: (0, 0)),
            pl.BlockSpec((1, LANES), lambda b, i: (0, 0)),
        ],
        out_specs=pl.BlockSpec((None, ROW, B_HEADS * LANES), lambda b, i: (b, i, 0)),
        out_shape=jax.ShapeDtypeStruct((n_batch, ts, B_HEADS * LANES), BF16),
        compiler_params=_cparams(("parallel", "parallel")),
        name="diff_attention",
    )(qbz, kb, vb, lam_vecs, sub_g)


MBATCH = 2
AUG = 2 * SUBLANES


def _mlstm_kernel(qtf_ref, kf_ref, vtf_ref, gcf_ref, grf_ref, qtb_ref, kb_ref, vtb_ref, gcb_ref, grb_ref,
                  hf_ref, hb_ref, c_ref, m_ref):
    @pl.when(pl.program_id(1) == 0)
    def _():
        c_ref[...] = jnp.zeros_like(c_ref)
        m_ref[...] = jnp.zeros_like(m_ref)

    length = MCHUNK
    key_idx = _row_iota((length, length))
    qry_idx = _lane_iota((length, length))
    ones_rows = jnp.ones((AUG, length), BF16)
    n_rows = 2 * SUBLANES
    n_chain = MBATCH * 2 * C_HEADS
    c_states = [c_ref[ch] for ch in range(n_chain)]
    m_prevs = [m_ref[ch] for ch in range(n_chain)]
    c_news, m_news, pending = [], [], []

    lanes = ((qtf_ref, kf_ref, vtf_ref, gcf_ref, grf_ref), (qtb_ref, kb_ref, vtb_ref, gcb_ref, grb_ref))
    for sample, direction in [(sm, dr) for sm in range(MBATCH) for dr in range(2)]:
        qt_ref, k_ref, vt_ref, gc_ref, gr_ref = (r.at[sample] for r in lanes[direction])
        reverse = direction == 1
        gate_c, cs_c, tot_c = gc_ref[:, :LANES], gc_ref[:, LANES:2 * LANES], gc_ref[:, 2 * LANES:]
        p_c = (tot_c - cs_c + gate_c) if reverse else cs_c
        g_c = gate_c - pltpu.roll(p_c, LANES - C_HEADS, 1)
        mask = (key_idx >= qry_idx) if reverse else (key_idx <= qry_idx)
        for hd in range(C_HEADS):
            chain = (sample * 2 + direction) * C_HEADS + hd
            ii = direction * 2 * C_HEADS + hd
            fi = ii + C_HEADS
            i_row = gr_ref[ii:ii + 1, :]
            f_row = gr_ref[fi:fi + 1, :]
            cs_row = gr_ref[n_rows + fi:n_rows + fi + 1, :]
            tot_row = gr_ref[2 * n_rows + fi:2 * n_rows + fi + 1, :]
            p_row = (tot_row - cs_row + f_row) if reverse else cs_row
            m_prev = m_prevs[chain]
            inter = p_row + m_prev
            log_dt = jnp.where(mask, jnp.broadcast_to(g_c[:, ii:ii + 1], (length, length)) + p_row, NEG)
            m_t = jnp.maximum(inter, jnp.max(log_dt, axis=0, keepdims=True))
            d_t = jnp.exp2(log_dt - m_t)
            a_row = jnp.exp2(inter - m_t)
            sl = slice(hd * C_DIM, (hd + 1) * C_DIM)
            k_h, qt_h = k_ref[:, sl], qt_ref[sl, :]
            vt_aug = jnp.concatenate([vt_ref[sl, :], ones_rows], axis=0)
            s_raw = jnp.dot(k_h, qt_h, preferred_element_type=F32)
            c_state = c_states[chain]
            x_t = jnp.dot(c_state.astype(BF16), qt_h, preferred_element_type=F32)
            w_row = tot_row - p_row + i_row
            m_new = jnp.maximum(tot_row + m_prev, jnp.max(w_row, axis=-1, keepdims=True))
            decay = jnp.exp2(tot_row + m_prev - m_new)
            ws = jnp.exp2(w_row - m_new)
            update = jnp.dot((vt_aug.astype(F32) * ws).astype(BF16), k_h, preferred_element_type=F32)
            c_news.append(decay[:, :C_DIM] * c_state + update)
            m_news.append(m_new)
            pending.append((s_raw, d_t, vt_aug, a_row, x_t, m_t))

    for sample, direction in [(sm, dr) for sm in range(MBATCH) for dr in range(2)]:
        h_parts = []
        for hd in range(C_HEADS):
            s_raw, d_t, vt_aug, a_row, x_t, m_t = pending[(sample * 2 + direction) * C_HEADS + hd]
            y_t = jnp.dot(vt_aug, (s_raw * d_t).astype(BF16), preferred_element_type=F32)
            num_t = a_row * x_t[:C_DIM] + y_t[:C_DIM]
            den = a_row * x_t[C_DIM:C_DIM + 1] + y_t[C_DIM:C_DIM + 1]
            h_t = num_t / jnp.maximum(jnp.abs(den), jnp.exp2(-m_t))
            h_parts.append(h_t.T)
        (hb_ref if direction else hf_ref)[sample] = jnp.concatenate(h_parts, axis=1)

    for ch in range(n_chain):
        c_ref[ch] = c_news[ch]
        m_ref[ch] = m_news[ch]


def _mlstm(qmt, km, vmt, gcol, grow):
    n_batch, ts, width = km.shape
    nc = ts // MCHUNK
    ctx_chunks = ROW // MCHUNK

    def bwd(j):
        return jnp.where(j < ctx_chunks, ctx_chunks - 1 - j, nc + ctx_chunks - 1 - j)

    def specs(idx):
        return [
            pl.BlockSpec((MBATCH, width, MCHUNK), lambda b, j: (b, 0, idx(j))),
            pl.BlockSpec((MBATCH, MCHUNK, width), lambda b, j: (b, idx(j), 0)),
            pl.BlockSpec((MBATCH, width, MCHUNK), lambda b, j: (b, 0, idx(j))),
            pl.BlockSpec((MBATCH, MCHUNK, 3 * LANES), lambda b, j: (b, idx(j), 0)),
            pl.BlockSpec((MBATCH, 6 * SUBLANES, MCHUNK), lambda b, j: (b, 0, idx(j))),
        ]

    fwd = lambda j: j
    assert n_batch % MBATCH == 0
    n_chain = MBATCH * 2 * C_HEADS
    return pl.pallas_call(
        _mlstm_kernel,
        grid=(n_batch // MBATCH, nc),
        in_specs=specs(fwd) + specs(bwd),
        out_specs=[
            pl.BlockSpec((MBATCH, MCHUNK, width), lambda b, j: (b, j, 0)),
            pl.BlockSpec((MBATCH, MCHUNK, width), lambda b, j: (b, bwd(j), 0)),
        ],
        out_shape=[jax.ShapeDtypeStruct((n_batch, ts, width), F32)] * 2,
        scratch_shapes=[
            pltpu.VMEM((n_chain, C_DIM + AUG, C_DIM), F32),
            pltpu.VMEM((n_chain, 1, MCHUNK), F32),
        ],
        compiler_params=_cparams(("parallel", "arbitrary")),
        name="mlstm",
    )(qmt, km, vmt, gcol, grow, qmt, km, vmt, gcol, grow)


def _mix_kernel(*refs, n_stream):
    (a_ref, d_ref, hf_ref, hb_ref, oc_ref, gate_ref, g1_ref, pg_ref, mg_ref,
     wa_ref, wb_ref, wc_ref, wo_ref, o_ref) = refs[n_stream:]
    d_model = o_ref.shape[-1]
    hsum = hf_ref[...] + hb_ref[...]
    mg = mg_ref[...]
    m = jnp.concatenate([_rms(hsum[:, hd * C_DIM:(hd + 1) * C_DIM], mg) for hd in range(C_HEADS)], axis=1)
    m = m * _sigmoid(oc_ref[...].astype(F32))
    u = (_sigmoid(gate_ref[:, :d_model].astype(F32))
         * jnp.dot(a_ref[...], wa_ref[...], preferred_element_type=F32)
         + _sigmoid(gate_ref[:, d_model:2 * d_model].astype(F32))
         * jnp.dot(d_ref[...], wb_ref[...], preferred_element_type=F32)
         + _sigmoid(gate_ref[:, 2 * d_model:].astype(F32))
         * jnp.dot(m.astype(BF16), wc_ref[...], preferred_element_type=F32))
    y = jnp.dot(u.astype(BF16), wo_ref[...], preferred_element_type=F32)
    o_ref[...] = _stream_tile(refs[:n_stream]) + g1_ref[...] * _rms(y, pg_ref[...])


def _mix(a, dd, hf, hb, out_gate, merge_gate, xs, mod, layer, nw, post_g, mlstm_g, wa, wb, wc, wo):
    (n_batch, ts, d), stream_specs, stream_arrays = _stream(xs)
    nt = ts // ROW
    width = a.shape[-1]

    def tile(w):
        return pl.BlockSpec((None, ROW, w), lambda b, i: (b, i, 0))

    def const(shape):
        return pl.BlockSpec(shape, lambda b, i: (0,) * len(shape))

    return pl.pallas_call(
        functools.partial(_mix_kernel, n_stream=len(stream_arrays)),
        grid=(n_batch, nt),
        in_specs=stream_specs + [
            tile(width), tile(width), tile(width), tile(width),
            tile(width), tile(3 * d),
            _mod_spec(d, layer, 2, nw, n_batch),
            const((1, d)), const((1, C_DIM)),
            const((width, d)), const((width, d)), const((width, d)), const((d, d)),
        ],
        out_specs=tile(d),
        out_shape=jax.ShapeDtypeStruct((n_batch, ts, d), F32),
        compiler_params=_cparams(("parallel", "parallel")),
        name="mix_out",
    )(*stream_arrays, a, dd, hf, hb, out_gate, merge_gate, mod, post_g, mlstm_g, wa, wb, wc, wo)


def _ffn_kernel(x_ref, sh_ref, sc_ref, g2_ref, pre_ref, post_ref, wg_ref, wu_ref, wd_ref, o_ref):
    x = x_ref[...]
    xb = (_rms(x, pre_ref[...]) * (1.0 + sc_ref[...]) + sh_ref[...]).astype(BF16)
    gate = jnp.dot(xb, wg_ref[...], preferred_element_type=F32)
    up = jnp.dot(xb, wu_ref[...], preferred_element_type=F32)
    z = jnp.dot((_silu(gate) * up).astype(BF16), wd_ref[...], preferred_element_type=F32)
    o_ref[...] = x + g2_ref[...] * _rms(z, post_ref[...])


def _ffn(xs, mod, layer, nw, pre_g, post_g, wg, wu, wd):
    n_batch, ts, d = xs.shape
    nt = ts // ROW
    dff = wg.shape[1]

    def resident(shape):
        return pl.BlockSpec(shape, lambda b, i: (0,) * len(shape), pipeline_mode=pl.Buffered(1))

    tile = pl.BlockSpec((None, ROW, d), lambda b, i: (b, i, 0))
    return pl.pallas_call(
        _ffn_kernel,
        grid=(n_batch, nt),
        in_specs=[
            tile,
            _mod_spec(d, layer, 3, nw, n_batch), _mod_spec(d, layer, 4, nw, n_batch),
            _mod_spec(d, layer, 5, nw, n_batch),
            pl.BlockSpec((1, d), lambda b, i: (0, 0)), pl.BlockSpec((1, d), lambda b, i: (0, 0)),
            resident((d, dff)), resident((d, dff)), resident((dff, d)),
        ],
        out_specs=tile,
        out_shape=jax.ShapeDtypeStruct((n_batch, ts, d), F32),
        compiler_params=_cparams(("parallel", "parallel")),
        name="ffn",
    )(xs, mod, mod, mod, pre_g, post_g, wg, wu, wd)


def _route_kernel(x_ref, sh_ref, sc_ref, pre_ref, wr_ref, br_ref, tri_ref, xn_ref, meta_ref, slots_ref, cnt_ref,
                  carry_ref):
    @pl.when(jnp.logical_and(pl.program_id(0) == 0, pl.program_id(1) == 0))
    def _():
        carry_ref[...] = jnp.zeros_like(carry_ref)

    xn = _rms(x_ref[...], pre_ref[...]) * (1.0 + sc_ref[...]) + sh_ref[...]
    xn_ref[...] = xn
    lane = _lane_iota((ROW, LANES))
    logits = jnp.dot(xn, wr_ref[...], preferred_element_type=F32, precision=HIGHEST) + br_ref[...]
    logits = jnp.where(lane < N_EXPERTS, logits, NEG)
    v1 = jnp.max(logits, axis=-1, keepdims=True)
    i1 = jnp.min(jnp.where(logits == v1, lane, LANES), axis=-1, keepdims=True)
    rest = jnp.where(lane == i1, NEG, logits)
    v2 = jnp.max(rest, axis=-1, keepdims=True)
    i2 = jnp.min(jnp.where(rest == v2, lane, LANES), axis=-1, keepdims=True)
    e2 = jnp.exp(v2 - v1)
    w1 = 1.0 / (1.0 + e2)
    w2 = e2 / (1.0 + e2)
    assigned = jnp.where(lane == i1, 1.0, jnp.where(lane == i2, 1.0, 0.0))
    before = jnp.dot(tri_ref[...], assigned.astype(BF16), preferred_element_type=F32) + carry_ref[0:1, :]
    r1 = jnp.sum(jnp.where(lane == i1, before, 0.0), axis=-1, keepdims=True)
    r2 = jnp.sum(jnp.where(lane == i2, before, 0.0), axis=-1, keepdims=True)
    carry_ref[...] = carry_ref[...] + jnp.sum(assigned, axis=0, keepdims=True)
    cnt_ref[...] = carry_ref[...]
    fields = (i1.astype(F32), i2.astype(F32), w1, w2, r1, r2)
    meta = jnp.zeros((ROW, LANES), F32)
    for f, val in enumerate(fields):
        meta = jnp.where(lane == f, val, meta)
    meta_ref[...] = meta
    slots_ref[...] = meta.T[:SUBLANES, :]


def _route(xs, mod, layer, nw, pre_g, w_r, b_r):
    n_batch, ts, d = xs.shape
    nt = ts // ROW
    tile = pl.BlockSpec((None, ROW, d), lambda b, i: (b, i, 0))
    t_idx = jnp.arange(ROW)
    tri_strict = (t_idx[:, None] > t_idx[None, :]).astype(BF16)
    return pl.pallas_call(
        _route_kernel,
        grid=(n_batch, nt),
        in_specs=[
            tile, _mod_spec(d, layer, 3, nw, n_batch), _mod_spec(d, layer, 4, nw, n_batch),
            pl.BlockSpec((1, d), lambda b, i: (0, 0)),
            pl.BlockSpec((d, LANES), lambda b, i: (0, 0)),
            pl.BlockSpec((1, LANES), lambda b, i: (0, 0)),
            pl.BlockSpec((ROW, ROW), lambda b, i: (0, 0)),
        ],
        out_specs=[tile, pl.BlockSpec((None, ROW, LANES), lambda b, i: (b, i, 0)),
                   pl.BlockSpec((None, SUBLANES, ROW), lambda b, i: (b, 0, i)),
                   pl.BlockSpec((SUBLANES, LANES), lambda b, i: (0, 0))],
        out_shape=[jax.ShapeDtypeStruct((n_batch, ts, d), F32),
                   jax.ShapeDtypeStruct((n_batch, ts, LANES), F32),
                   jax.ShapeDtypeStruct((n_batch, SUBLANES, ts), F32),
                   jax.ShapeDtypeStruct((SUBLANES, LANES), F32)],
        scratch_shapes=[pltpu.VMEM((SUBLANES, LANES), F32)],
        compiler_params=_cparams(("arbitrary", "arbitrary")),
        name="route",
    )(xs, mod, mod, pre_g, w_r, b_r, tri_strict)


GROUPS = ROW // SUBLANES


def _start_rows(make_copy):
    for g in range(GROUPS):
        for j in range(SUBLANES):
            for slot in range(2):
                make_copy(g, j, slot).start()


def _wait_rows(make_copy):
    def drain(g, carry):
        for j in range(SUBLANES):
            for slot in range(2):
                make_copy(0, 0, slot).wait()
        return carry

    lax.fori_loop(0, GROUPS, drain, 0)


STAGES = 3


def _dispatch_kernel(pos_ref, bounds_ref, xn_ref, out_ref, zero_ref, stage_ref, row_sems, in_sems, zero_sem):
    step = pl.program_id(0)
    n_steps = pl.num_programs(0)
    n_tok = pos_ref.shape[0] // 2
    tm = zero_ref.shape[0]
    n_sorted = out_ref.shape[0]

    @pl.when(step == 0)
    def _():
        zero_ref[...] = jnp.zeros_like(zero_ref)

        def fill(row):
            copy = pltpu.make_async_copy(zero_ref, out_ref.at[pl.ds(pl.multiple_of(row, tm), tm)], zero_sem)
            copy.start()
            copy.wait()

        for e in range(N_EXPERTS):
            end = bounds_ref[N_EXPERTS + e]

            @pl.when(end > bounds_ref[e])
            def _():
                fill(end - tm)

        last_end = bounds_ref[2 * N_EXPERTS - 1]

        def tail(k, carry):
            fill(last_end + k * tm)
            return carry

        lax.fori_loop(0, (n_sorted - last_end) // tm, tail, 0)

    def tile_in(tile):
        slot = tile % STAGES
        return pltpu.make_async_copy(xn_ref.at[pl.ds(tile * GROUPS, GROUPS)], stage_ref.at[slot], in_sems.at[slot])

    def copies(tile):
        def make_copy(g, j, slot):
            p = pos_ref[slot * n_tok + tile * ROW + g * SUBLANES + j]
            return pltpu.make_async_copy(stage_ref.at[tile % STAGES, g, pl.ds(j, 1)], out_ref.at[pl.ds(p, 1)],
                                         row_sems.at[tile % STAGES])
        return make_copy

    @pl.when(step == 0)
    def _():
        tile_in(step).start()

    @pl.when(step >= STAGES - 1)
    def _():
        _wait_rows(copies(step - (STAGES - 1)))

    @pl.when(step + 1 < n_steps)
    def _():
        tile_in(step + 1).start()

    tile_in(step).wait()
    _start_rows(copies(step))

    @pl.when(step == n_steps - 1)
    def _():
        for back in range(STAGES - 2, -1, -1):
            @pl.when(step - back >= 0)
            def _():
                _wait_rows(copies(step - back))


def _dispatch(pos, bounds, xn, n_sorted, tm):
    m_rows, d = xn.shape
    return pl.pallas_call(
        _dispatch_kernel,
        grid_spec=pltpu.PrefetchScalarGridSpec(
            num_scalar_prefetch=2,
            grid=(m_rows // ROW,),
            in_specs=[pl.BlockSpec(memory_space=pl.ANY)],
            out_specs=pl.BlockSpec(memory_space=pl.ANY),
            scratch_shapes=[pltpu.VMEM((tm, d), F32), pltpu.VMEM((STAGES, GROUPS, SUBLANES, d), F32),
                            pltpu.SemaphoreType.DMA((STAGES,)), pltpu.SemaphoreType.DMA((STAGES,)),
                            pltpu.SemaphoreType.DMA(())],
        ),
        out_shape=jax.ShapeDtypeStruct((n_sorted, d), F32),
        compiler_params=_cparams(("arbitrary",)),
        name="dispatch",
    )(pos, bounds, xn.reshape(m_rows // SUBLANES, SUBLANES, d))


def _experts_kernel(te_ref, x_ref, wg_ref, wu_ref, wd_ref, y_ref):
    used = te_ref[pl.program_id(0)] < N_EXPERTS

    @pl.when(used)
    def _():
        x = x_ref[...].astype(BF16)
        gate = jnp.dot(x, wg_ref[...], preferred_element_type=F32)
        up = jnp.dot(x, wu_ref[...], preferred_element_type=F32)
        y_ref[...] = jnp.dot((_silu(gate) * up).astype(BF16), wd_ref[...], preferred_element_type=F32)

    @pl.when(jnp.logical_not(used))
    def _():
        y_ref[...] = jnp.zeros_like(y_ref)


def _experts(tile_expert, x_sorted, wg, wu, wd, tm):
    n_sorted, d = x_sorted.shape
    n_e, _, dff = wg.shape

    def weight(shape):
        return pl.BlockSpec((None,) + shape, lambda i, te: (jnp.minimum(te[i], n_e - 1), 0, 0))

    return pl.pallas_call(
        _experts_kernel,
        grid_spec=pltpu.PrefetchScalarGridSpec(
            num_scalar_prefetch=1,
            grid=(n_sorted // tm,),
            in_specs=[pl.BlockSpec((tm, d), lambda i, te: (i, 0)),
                      weight((d, dff)), weight((d, dff)), weight((dff, d))],
            out_specs=pl.BlockSpec((tm, d), lambda i, te: (i, 0)),
        ),
        out_shape=jax.ShapeDtypeStruct((n_sorted, d), F32),
        compiler_params=_cparams(("arbitrary",)),
        name="experts",
    )(tile_expert, x_sorted, wg, wu, wd)


def _combine_kernel(pos_ref, x_ref, meta_ref, g2_ref, post_ref, y_ref, o_ref, buf_ref, sems, *,
                    tiles_per_sample, latent_only):
    step = pl.program_id(0)
    n_steps = pl.num_programs(0)
    n_tok = pos_ref.shape[0] // 2
    d_model = x_ref.shape[-1]

    def wanted(tile):
        return (tile % tiles_per_sample != 0) if latent_only else (tile >= 0)

    def copies(tile):
        def make_copy(g, j, slot):
            p = pos_ref[slot * n_tok + tile * ROW + g * SUBLANES + j]
            return pltpu.make_async_copy(y_ref.at[pl.ds(p, 1)], buf_ref.at[tile % 2, slot, g, pl.ds(j, 1)],
                                         sems.at[tile % 2])
        return make_copy

    @pl.when(jnp.logical_and(step == 0, wanted(step)))
    def _():
        _start_rows(copies(step))

    @pl.when(jnp.logical_and(step + 1 < n_steps, wanted(step + 1)))
    def _():
        _start_rows(copies(step + 1))

    @pl.when(wanted(step))
    def _():
        _wait_rows(copies(step))
        meta = meta_ref[...]
        cur = step % 2
        y1 = buf_ref[cur, 0].reshape(ROW, d_model)
        y2 = buf_ref[cur, 1].reshape(ROW, d_model)
        z = meta[:, 2:3] * y1 + meta[:, 3:4] * y2
        o_ref[...] = x_ref[...] + g2_ref[...] * _rms(z, post_ref[...])


def _combine(pos, xs, meta, y_sorted, mod, layer, nw, post_g, latent_only):
    n_batch, ts, d = xs.shape
    nt = ts // ROW
    m_rows = n_batch * ts
    base = (layer * 6 + 5) * nw
    tile = pl.BlockSpec((ROW, d), lambda i, pos: (i, 0))
    if latent_only:
        out_rows = n_batch * (ts - ROW)
        out_tile = pl.BlockSpec((ROW, d), lambda i, pos: ((i // nt) * (nt - 1) + jnp.maximum(i % nt - 1, 0), 0))
    else:
        out_rows, out_tile = m_rows, tile
    out = pl.pallas_call(
        functools.partial(_combine_kernel, tiles_per_sample=nt, latent_only=latent_only),
        grid_spec=pltpu.PrefetchScalarGridSpec(
            num_scalar_prefetch=1,
            grid=(m_rows // ROW,),
            in_specs=[
                tile,
                pl.BlockSpec((ROW, LANES), lambda i, pos: (i, 0)),
                pl.BlockSpec((None, 1, d), lambda i, pos: (base + jnp.where(i % nt == 0, n_batch, i // nt), 0, 0)),
                pl.BlockSpec((1, d), lambda i, pos: (0, 0)),
                pl.BlockSpec(memory_space=pl.ANY),
            ],
            out_specs=out_tile,
            scratch_shapes=[pltpu.VMEM((2, 2, GROUPS, SUBLANES, d), F32), pltpu.SemaphoreType.DMA((2,))],
        ),
        out_shape=jax.ShapeDtypeStruct((out_rows, d), F32),
        compiler_params=_cparams(("arbitrary",)),
        name="combine",
    )(pos, xs.reshape(m_rows, d), meta.reshape(m_rows, LANES), mod, post_g, y_sorted)
    return out.reshape(n_batch, out_rows // n_batch, d)


def _moe(xs, mod, layer, nw, pre_g, post_g, w_r, b_r, wg, wu, wd, latent_only, tm=512):
    n_batch, ts, d = xs.shape
    m_rows = n_batch * ts
    xn, meta, slots, cnt = _route(xs, mod, layer, nw, pre_g, w_r, b_r)
    i1, i2, r1, r2 = (slots[:, f, :].reshape(m_rows).astype(jnp.int32) for f in (0, 1, 4, 5))
    counts = cnt[0, :N_EXPERTS].astype(jnp.int32)
    padded = -(-counts // tm) * tm
    ends = jnp.cumsum(padded)
    start = ends - padded
    pos = jnp.concatenate([start[i1] + r1, start[i2] + r2])
    bounds = jnp.concatenate([start, ends])
    n_tiles = 2 * m_rows // tm + N_EXPERTS
    tile_row = jnp.arange(n_tiles, dtype=jnp.int32) * tm
    tile_expert = jnp.sum((ends[None, :] <= tile_row[:, None]).astype(jnp.int32), axis=1)
    x_sorted = _dispatch(pos, bounds, xn.reshape(m_rows, d), n_tiles * tm, tm)
    y_sorted = _experts(tile_expert, x_sorted, wg, wu, wd, tm)
    return _combine(pos, xs, meta, y_sorted, mod, layer, nw, post_g, latent_only)


def _rope_tables(n_tok, n_ctx):
    f32 = np.float32
    n_freq = HEAD_DIM // 4
    pos = np.arange(n_tok)
    row = (pos // GRID_W).astype(f32)
    colp = (pos % GRID_W).astype(f32)
    inv = (f32(ROPE_THETA) ** (-np.arange(n_freq, dtype=f32) / f32(n_freq))).astype(f32)
    lane = np.arange(LANES)
    in_head = lane % HEAD_DIM
    use_col = (in_head // (HEAD_DIM // 2)) == 1
    freq = inv[in_head % n_freq]
    ang = (np.where(use_col[None, :], colp[:, None], row[:, None]) * freq[None, :]).astype(f32)
    lower = (in_head % (HEAD_DIM // 2)) < n_freq
    cos_t = np.cos(ang).astype(f32)
    sin_t = np.sin(ang).astype(f32)
    sin_a = np.where(lower[None, :], -sin_t, f32(0.0)).astype(f32)
    sin_b = np.where(lower[None, :], f32(0.0), sin_t).astype(f32)
    pad = lambda t, v: np.concatenate([np.full((n_ctx, LANES), v, f32), t], axis=0)
    return jnp.asarray(pad(cos_t, 1.0)), jnp.asarray(pad(sin_a, 0.0)), jnp.asarray(pad(sin_b, 0.0))


def _block_diag_mean(width):
    idx = jnp.arange(width) // HEAD_DIM
    return jnp.where(idx[:, None] == idx[None, :], 1.0 / HEAD_DIM, 0.0).astype(BF16)


def _pack_w_in(w):
    a_q = A_HEADS * HEAD_DIM
    a_kv = 2 * A_KV_HEADS * HEAD_DIM
    n_gates = 4 * C_HEADS
    gate_start = w.shape[1] - 3 * w.shape[0]
    g_start = gate_start - n_gates
    main = jnp.concatenate([w[:, gate_start:], w[:, :a_q], w[:, a_q + a_kv:g_start], w[:, a_q:a_q + a_kv]], axis=1)
    gates = jnp.pad(w[:, g_start:gate_start], ((0, 0), (0, LANES - n_gates)))
    return main.astype(BF16), gates.astype(BF16)


def kernel(x, c, ctx, c_ctx, ada_w, ada_b, pre_mix_g, post_mix_g, pre_ffn_g, post_ffn_g, w_in, q_norm_g, k_norm_g, lam_q1, lam_k1, lam_q2, lam_k2, diff_norm_g, conv_w, conv_b, mlstm_gate_b, mlstm_norm_g, w_br_attn, w_br_diff, w_br_mlstm, w_out, w_ff_gate, w_ff_up, w_ff_down, w_router, b_router, w_moe_gate, w_moe_up, w_moe_down):
    n_batch, n_tok, d = x.shape
    n_ctx = ctx.shape[1]
    depth = ada_w.shape[0]
    assert n_ctx == ROW and n_tok % ROW == 0 and d == 1024
    ts = n_ctx + n_tok
    nw = -(-(n_batch + 1) // SUBLANES) * SUBLANES

    c_all = jnp.concatenate([c, c_ctx[None, :], jnp.zeros((nw - n_batch - 1, d), F32)], axis=0)
    mod = _modulation(c_all, ada_w, ada_b)
    tables = _rope_tables(n_tok, n_ctx)
    bd4, bd1 = _block_diag_mean(A_HEADS * HEAD_DIM), _block_diag_mean(LANES)
    t_idx = jnp.arange(MCHUNK)
    tri = (t_idx[:, None] >= t_idx[None, :]).astype(BF16)

    xs = (ctx, x)
    for l in range(depth):
        lam_init = 0.8 - 0.6 * math.exp(-0.3 * l)
        w_main, w_gate = _pack_w_in(w_in[l])
        qg = jnp.tile(q_norm_g[l], A_HEADS)[None, :]
        kg = jnp.tile(k_norm_g[l], A_KV_HEADS)[None, :]
        gate_b = jnp.pad(mlstm_gate_b[l], (0, LANES - 4 * C_HEADS))[None, :]
        merge_gate, out_gate, qaz, ka, va, qbz, kb, vb, qmt, km, vmt, gcol, grow = _inproj(
            xs, mod, l, nw, pre_mix_g[l][None, :], w_main, w_gate,
            tables, qg, kg, bd4, bd1, conv_w[l], conv_b[l][None, :], gate_b, tri)
        a_out = _gqa(qaz, ka, va)
        lam_vecs = jnp.stack([lam_q1[l], lam_k1[l], lam_q2[l], lam_k2[l]], axis=0)
        d_out = _diff(qbz, kb, vb, lam_vecs, diff_norm_g[l][None, :], lam_init)
        hf, hb = _mlstm(qmt, km, vmt, gcol, grow)
        xs = _mix(a_out, d_out, hf, hb, out_gate, merge_gate, xs, mod, l, nw,
                  post_mix_g[l][None, :], mlstm_norm_g[l][None, :],
                  w_br_attn[l].astype(BF16), w_br_diff[l].astype(BF16), w_br_mlstm[l].astype(BF16),
                  w_out[l].astype(BF16))
        j = l // 2
        if l % 2 == 0:
            xs = _ffn(xs, mod, l, nw, pre_ffn_g[l][None, :], post_ffn_g[l][None, :],
                      w_ff_gate[j].astype(BF16), w_ff_up[j].astype(BF16), w_ff_down[j].astype(BF16))
        else:
            w_r = jnp.pad(w_router[j], ((0, 0), (0, LANES - N_EXPERTS)))
            b_r = jnp.pad(b_router[j], (0, LANES - N_EXPERTS))[None, :]
            xs = _moe(xs, mod, l, nw, pre_ffn_g[l][None, :], post_ffn_g[l][None, :], w_r, b_r,
                      w_moe_gate[j].astype(BF16), w_moe_up[j].astype(BF16), w_moe_down[j].astype(BF16),
                      latent_only=l == depth - 1)
    return xs if xs.shape[1] == n_tok else xs[:, n_ctx:, :]
```

```python
import functools
import math

import jax
import jax.numpy as jnp
import numpy as np
from jax import lax
from jax.experimental import pallas as pl
from jax.experimental.pallas import tpu as pltpu

F32 = jnp.float32
BF16 = jnp.bfloat16
HIGHEST = lax.Precision.HIGHEST

EPS = 1e-6
HEAD_DIM = 64
A_HEADS = 8
A_KV_HEADS = 2
B_HEADS = 4
C_HEADS = 4
C_DIM = 128
N_EXPERTS = 8
ROPE_THETA = 10000.0
GRID_W = 64
CONV_W = 3

LANES = 128
SUBLANES = 8
ROW = 256
MCHUNK = 256
LOG2E = math.log2(math.e)
NEG = -1e30
VMEM_LIMIT = 56 * 1024 * 1024

OFF_GATE = 0
REL_QA, REL_QB, REL_KB, REL_VB = 0, 512, 1024, 1536
REL_QC, REL_KC, REL_VC, REL_OC = 2048, 2560, 3072, 3584
REL_KA = 4096
REL_END = 4352


def _cparams(sem):
    return pltpu.CompilerParams(dimension_semantics=sem, vmem_limit_bytes=VMEM_LIMIT)


def _rms(x, g):
    y = x * lax.rsqrt(jnp.mean(x * x, axis=-1, keepdims=True) + EPS)
    return y * g


def _sigmoid(x):
    return 1.0 / (1.0 + jnp.exp(-x))


def _silu(x):
    return x * _sigmoid(x)


def _log_sigmoid(x):
    return jnp.minimum(x, 0.0) - jnp.log(1.0 + jnp.exp(-jnp.abs(x)))


def _lane_iota(shape):
    return lax.broadcasted_iota(jnp.int32, shape, len(shape) - 1)


def _row_iota(shape):
    return lax.broadcasted_iota(jnp.int32, shape, len(shape) - 2)


def _mod_kernel(c_ref, w_ref, b_ref, o_ref):
    c = c_ref[...]
    o_ref[...] = jnp.dot(_silu(c), w_ref[...], preferred_element_type=F32, precision=HIGHEST) + b_ref[...]


def _modulation(c_all, ada_w, ada_b):
    depth, d, _ = ada_w.shape
    nw = c_all.shape[0]
    out = pl.pallas_call(
        _mod_kernel,
        grid=(depth, 6),
        in_specs=[
            pl.BlockSpec((nw, d), lambda l, j: (0, 0)),
            pl.BlockSpec((None, d, d), lambda l, j: (l, 0, j)),
            pl.BlockSpec((None, 1, d), lambda l, j: (l, 0, j)),
        ],
        out_specs=pl.BlockSpec((None, None, nw, d), lambda l, j: (l, j, 0, 0)),
        out_shape=jax.ShapeDtypeStruct((depth, 6, nw, d), F32),
        compiler_params=_cparams(("arbitrary", "arbitrary")),
        name="modulation",
    )(c_all, ada_w, ada_b.reshape(depth, 1, 6 * d))
    return out.reshape(depth * 6 * nw, 1, d)


def _mod_spec(d, layer, chunk, nw, n_batch):
    base = (layer * 6 + chunk) * nw
    return pl.BlockSpec((None, 1, d), lambda b, i: (base + jnp.where(i == 0, n_batch, b), 0, 0))


def _stream(stream):
    if isinstance(stream, tuple):
        ctx, x = stream
        n_batch, n_tok, d = x.shape
        specs = [pl.BlockSpec((None, ROW, d), lambda b, i: (b, 0, 0)),
                 pl.BlockSpec((None, ROW, d), lambda b, i: (b, jnp.maximum(i - 1, 0), 0))]
        return (n_batch, ctx.shape[1] + n_tok, d), specs, [ctx, x]
    d = stream.shape[-1]
    return stream.shape, [pl.BlockSpec((None, ROW, d), lambda b, i: (b, i, 0))], [stream]


def _stream_tile(refs):
    if len(refs) == 2:
        return jnp.where(pl.program_id(1) == 0, refs[0][...], refs[1][...])
    return refs[0][...]


def _inproj_kernel(*refs, n_stream):
    (xprev_ref, xnext_ref, sh_ref, sc_ref, g_ref, w_ref, wg_ref,
     cos_ref, sa_ref, sb_ref, qg_ref, kg_ref, bd4_ref, bd1_ref, cw_ref, cb_ref, gb_ref, tri_ref,
     gate_ref, oc_ref, *mixer_refs) = refs[n_stream:]
    d_model = xprev_ref.shape[-1]

    def normed(x):
        return _rms(x, g_ref[...]) * (1.0 + sc_ref[...]) + sh_ref[...]

    xn = normed(_stream_tile(refs[:n_stream]))
    xb = xn.astype(BF16)
    ext = jnp.concatenate([normed(xprev_ref[...]), xn, normed(xnext_ref[...])], axis=0).astype(BF16)

    def proj(lhs, start, width):
        return jnp.dot(lhs, w_ref[:, start:start + width], preferred_element_type=F32)

    base = 3 * d_model
    half = C_HEADS * C_DIM
    qab = proj(xb, base + REL_QA, 2 * half)
    kvb = proj(xb, base + REL_KB, 2 * half)
    qkc = proj(ext, base + REL_QC, 2 * half)
    vo = proj(xb, base + REL_VC, 2 * half)
    oc_ref[...] = vo[:, half:].astype(BF16)
    kava = proj(xb, base + REL_KA, 2 * LANES)
    gates = jnp.dot(xb, wg_ref[...], preferred_element_type=F32)
    _prep_math(pl.program_id(1), pl.num_programs(1),
               qab[:, :half], kava, qab[:, half:], kvb[:, :half], kvb[:, half:],
               qkc[SUBLANES:SUBLANES + ROW], qkc[SUBLANES - 1:SUBLANES], qkc[SUBLANES + ROW:SUBLANES + ROW + 1],
               vo[:, :half], gates,
               cos_ref, sa_ref, sb_ref, qg_ref, kg_ref, bd4_ref, bd1_ref, cw_ref, cb_ref, gb_ref, tri_ref,
               *mixer_refs)
    for c in range(6):
        gate_ref[:, c * half:(c + 1) * half] = proj(xb, c * half, half).astype(BF16)


def _inproj(xs, mod, layer, nw, pre_g, w_main, w_gate, tables, qg, kg, bd4, bd1, conv_w, conv_b, gate_b, tri):
    (n_batch, ts, d), stream_specs, stream_arrays = _stream(xs)
    halo_src = stream_arrays[-1]
    tile_shift = len(stream_arrays) - 1
    n_main = w_main.shape[1]
    nt = ts // ROW
    cos_t, sa_t, sb_t = tables
    row_blocks = halo_src.shape[1] // SUBLANES
    per_tile = ROW // SUBLANES
    half = C_HEADS * C_DIM

    def const(shape):
        return pl.BlockSpec(shape, lambda b, i: (0,) * len(shape))

    def rows(width):
        return pl.BlockSpec((None, ROW, width), lambda b, i: (b, i, 0))

    def heads(n, width):
        return pl.BlockSpec((None, n, ROW, width), lambda b, i: (b, 0, i, 0))

    def cols(height):
        return pl.BlockSpec((None, height, ROW), lambda b, i: (b, 0, i))

    table = pl.BlockSpec((ROW, LANES), lambda b, i: (i, 0))
    in_specs = stream_specs + [
        pl.BlockSpec((None, SUBLANES, d), lambda b, i: (b, jnp.maximum((i - tile_shift) * per_tile - 1, 0), 0)),
        pl.BlockSpec((None, SUBLANES, d),
                     lambda b, i: (b, jnp.clip((i + 1 - tile_shift) * per_tile, 0, row_blocks - 1), 0)),
        _mod_spec(d, layer, 0, nw, n_batch),
        _mod_spec(d, layer, 1, nw, n_batch),
        const((1, d)),
        pl.BlockSpec((d, n_main), lambda b, i: (0, 0), pipeline_mode=pl.Buffered(1)),
        const((d, LANES)),
        table, table, table,
        const((1, half)), const((1, LANES)), const((half, half)), const((LANES, LANES)),
        const((CONV_W, 2 * half)), const((1, 2 * half)), const((1, LANES)), const((MCHUNK, MCHUNK)),
    ]
    outs = [
        (rows(3 * d), (ts, 3 * d), BF16),
        (rows(half), (ts, half), BF16),
        (heads(A_HEADS, LANES), (A_HEADS, ts, LANES), BF16),
        (rows(LANES), (ts, LANES), BF16),
        (heads(A_KV_HEADS, LANES), (A_KV_HEADS, ts, LANES), BF16),
        (heads(2 * B_HEADS, LANES), (2 * B_HEADS, ts, LANES), BF16),
        (heads(B_HEADS, LANES), (B_HEADS, ts, LANES), BF16),
        (heads(B_HEADS, 2 * LANES), (B_HEADS, ts, 2 * LANES), BF16),
        (cols(half), (half, ts), BF16),
        (rows(half), (ts, half), BF16),
        (cols(half), (half, ts), BF16),
        (rows(3 * LANES), (ts, 3 * LANES), F32),
        (cols(6 * SUBLANES), (6 * SUBLANES, ts), F32),
    ]
    return pl.pallas_call(
        functools.partial(_inproj_kernel, n_stream=len(stream_arrays)),
        grid=(n_batch, nt),
        in_specs=in_specs,
        out_specs=[spec for spec, _, _ in outs],
        out_shape=[jax.ShapeDtypeStruct((n_batch,) + shape, dtype) for _, shape, dtype in outs],
        compiler_params=_cparams(("parallel", "parallel")),
        name="inproj",
    )(*stream_arrays, halo_src, halo_src, mod, mod, pre_g, w_main, w_gate, cos_t, sa_t, sb_t, qg, kg, bd4, bd1, conv_w, conv_b, gate_b, tri)


def _head_mean_sq(x, bd_ref):
    return jnp.dot((x * x).astype(BF16), bd_ref[...], preferred_element_type=F32)


def _prep_math(i, nt, qa, kava, qb, kb, vb, cur, prev_row, next_row, vc, gates,
               cos_ref, sa_ref, sb_ref, qg_ref, kg_ref, bd4_ref, bd1_ref, cw_ref, cb_ref, gb_ref, tri_ref,
               qaz_ref, ka_ref, va_ref, qbz_ref, kbo_ref, vbo_ref, qmt_ref, km_ref, vmt_ref, gcol_ref, grow_ref):
    cos, sin_a, sin_b = cos_ref[...], sa_ref[...], sb_ref[...]

    def rope(x):
        width = x.shape[1]
        reps = width // LANES
        c = jnp.concatenate([cos] * reps, axis=1) if reps > 1 else cos
        a = jnp.concatenate([sin_a] * reps, axis=1) if reps > 1 else sin_a
        b = jnp.concatenate([sin_b] * reps, axis=1) if reps > 1 else sin_b
        return x * c + pltpu.roll(x, width - 16, 1) * a + pltpu.roll(x, 16, 1) * b

    lane = _lane_iota((ROW, LANES))
    ones = jnp.ones((ROW, LANES), BF16)
    scale = HEAD_DIM ** -0.5 * LOG2E

    qa = qa * lax.rsqrt(_head_mean_sq(qa, bd4_ref) + EPS) * qg_ref[...]
    qa = rope(qa) * scale
    heads_per_kv = A_HEADS // A_KV_HEADS
    for h in range(A_HEADS):
        g = h // heads_per_kv
        blk = qa[:, (h // 2) * LANES:(h // 2 + 1) * LANES]
        if h % 2 != g:
            blk = pltpu.roll(blk, HEAD_DIM, 1)
        qaz_ref[h] = jnp.where(lane // HEAD_DIM == g, blk, 0.0).astype(BF16)
    ka = kava[:, :LANES]
    ka = ka * lax.rsqrt(_head_mean_sq(ka, bd1_ref) + EPS) * kg_ref[...]
    ka_ref[...] = rope(ka).astype(BF16)
    va = kava[:, LANES:].astype(BF16)
    for g in range(A_KV_HEADS):
        va_ref[g] = jnp.where(lane // HEAD_DIM == g, va, ones)

    qb = rope(qb) * scale
    kb = rope(kb)
    for h in range(B_HEADS):
        blk = qb[:, h * LANES:(h + 1) * LANES]
        for m in range(2):
            qbz_ref[2 * h + m] = jnp.where(lane // HEAD_DIM == m, blk, 0.0).astype(BF16)
        kbo_ref[h] = kb[:, h * LANES:(h + 1) * LANES].astype(BF16)
        vbo_ref[h, :, :LANES] = vb[:, h * LANES:(h + 1) * LANES].astype(BF16)
        vbo_ref[h, :, LANES:] = ones

    row = _row_iota(cur.shape)
    prev_row = jnp.where(i >= 2, prev_row, 0.0)
    next_row = jnp.where(jnp.logical_and(i >= 1, i < nt - 1), next_row, 0.0)
    up = jnp.where(row == 0, prev_row, pltpu.roll(cur, 1, 0))
    dn = jnp.where(row == ROW - 1, next_row, pltpu.roll(cur, ROW - 1, 0))
    y = up * cw_ref[0:1, :] + cur * cw_ref[1:2, :] + dn * cw_ref[2:3, :] + cb_ref[...]
    y = _silu(y)
    half = C_HEADS * C_DIM
    qmt_ref[...] = y[:, :half].T.astype(BF16)
    km_ref[...] = (y[:, half:] * (C_DIM ** -0.5)).astype(BF16)
    vmt_ref[...] = vc.T.astype(BF16)

    gg = gates + gb_ref[...]
    is_forget = (lane // C_HEADS) % 2 == 1
    gl = jnp.where(is_forget, _log_sigmoid(gg), gg) * LOG2E
    tri = tri_ref[...]
    n_rows = 2 * SUBLANES
    for c in range(ROW // MCHUNK):
        rows = slice(c * MCHUNK, (c + 1) * MCHUNK)
        glc = gl[rows]
        hi = glc.astype(BF16)
        rest = glc - hi.astype(F32)
        mid = rest.astype(BF16)
        low = (rest - mid.astype(F32)).astype(BF16)
        cs = (jnp.dot(tri, hi, preferred_element_type=F32) + jnp.dot(tri, mid, preferred_element_type=F32)
              + jnp.dot(tri, low, preferred_element_type=F32))
        tot = jnp.broadcast_to(jnp.sum(glc, axis=0, keepdims=True), glc.shape)
        for f, val in enumerate((glc, cs, tot)):
            gcol_ref[rows, f * LANES:(f + 1) * LANES] = val
            grow_ref[f * n_rows:(f + 1) * n_rows, rows] = val.T[:n_rows, :]


def _attend_blocks(blocks, n_keys):
    def scores(q, k_ref):
        return lax.dot_general(q, k_ref[:n_keys, :], (((1,), (1,)), ((), ())), preferred_element_type=F32)

    def weighted(s, v_ref):
        p = jnp.exp2(s - jnp.max(s, axis=-1, keepdims=True)).astype(BF16)
        return jnp.dot(p, v_ref[:n_keys, :], preferred_element_type=F32)

    outs = []
    s_cur = scores(blocks[0][0], blocks[0][1])
    for j in range(1, len(blocks)):
        s_next = scores(blocks[j][0], blocks[j][1])
        outs.append(weighted(s_cur, blocks[j - 1][2]))
        s_cur = s_next
    outs.append(weighted(s_cur, blocks[-1][2]))
    return outs


def _per_tile_keys(body, n_all):
    @pl.when(pl.program_id(1) == 0)
    def _():
        body(ROW)

    @pl.when(pl.program_id(1) > 0)
    def _():
        body(n_all)


def _gqa_kernel(q_ref, k_ref, v_ref, o_ref):
    heads_per_kv = A_HEADS // A_KV_HEADS
    lane = _lane_iota((ROW, LANES))

    def body(n_keys):
        blocks = []
        for j in range(A_HEADS // 2):
            q = q_ref[2 * j:2 * j + 2].reshape(2 * ROW, LANES)
            blocks.append((q, k_ref, v_ref.at[(2 * j) // heads_per_kv]))
        for j, o in enumerate(_attend_blocks(blocks, n_keys)):
            g = (2 * j) // heads_per_kv
            den_lane = (1 - g) * HEAD_DIM
            o = o / o[:, den_lane:den_lane + 1]
            even, odd = o[:ROW], o[ROW:]
            even = even if g == 0 else pltpu.roll(even, HEAD_DIM, 1)
            odd = odd if g == 1 else pltpu.roll(odd, HEAD_DIM, 1)
            o_ref[:, j * LANES:(j + 1) * LANES] = jnp.where(lane < HEAD_DIM, even, odd).astype(BF16)

    _per_tile_keys(body, k_ref.shape[0])


def _diff_kernel(q_ref, k_ref, v_ref, lam_ref, g_ref, o_ref, *, lam_init):
    lv = lam_ref[...]
    lam = (jnp.exp(jnp.sum(lv[0:1] * lv[1:2], axis=-1, keepdims=True))
           - jnp.exp(jnp.sum(lv[2:3] * lv[3:4], axis=-1, keepdims=True)) + lam_init)

    def body(n_keys):
        blocks = [(q_ref[2 * h:2 * h + 2].reshape(2 * ROW, LANES), k_ref.at[h], v_ref.at[h]) for h in range(B_HEADS)]
        for h, o in enumerate(_attend_blocks(blocks, n_keys)):
            o = o[:, :LANES] / o[:, LANES:LANES + 1]
            dif = o[:ROW] - lam * o[ROW:]
            o_ref[:, h * LANES:(h + 1) * LANES] = (_rms(dif, g_ref[...]) * (1.0 - lam_init)).astype(BF16)

    _per_tile_keys(body, k_ref.shape[1])


def _gqa(qaz, ka, va):
    n_batch, _, ts, _ = qaz.shape
    nt = ts // ROW
    return pl.pallas_call(
        _gqa_kernel,
        grid=(n_batch, nt),
        in_specs=[
            pl.BlockSpec((None, A_HEADS, ROW, LANES), lambda b, i: (b, 0, i, 0)),
            pl.BlockSpec((None, ts, LANES), lambda b, i: (b, 0, 0)),
            pl.BlockSpec((None, A_KV_HEADS, ts, LANES), lambda b, i: (b, 0, 0, 0)),
        ],
        out_specs=pl.BlockSpec((None, ROW, A_HEADS * HEAD_DIM), lambda b, i: (b, i, 0)),
        out_shape=jax.ShapeDtypeStruct((n_batch, ts, A_HEADS * HEAD_DIM), BF16),
        compiler_params=_cparams(("parallel", "parallel")),
        name="gqa_attention",
    )(qaz, ka, va)


def _diff(qbz, kb, vb, lam_vecs, sub_g, lam_init):
    n_batch, _, ts, _ = qbz.shape
    nt = ts // ROW
    return pl.pallas_call(
        functools.partial(_diff_kernel, lam_init=lam_init),
        grid=(n_batch, nt),
        in_specs=[
            pl.BlockSpec((None, 2 * B_HEADS, ROW, LANES), lambda b, i: (b, 0, i, 0)),
            pl.BlockSpec((None, B_HEADS, ts, LANES), lambda b, i: (b, 0, 0, 0)),
            pl.BlockSpec((None, B_HEADS, ts, 2 * LANES), lambda b, i: (b, 0, 0, 0)),
            pl.BlockSpec((4, HEAD_DIM), lambda b, i: (0, 0)),
            pl.BlockSpec((1, LANES), lambda b, i: (0, 0)),
        ],
        out_specs=pl.BlockSpec((None, ROW, B_HEADS * LANES), lambda b, i: (b, i, 0)),
        out_shape=jax.ShapeDtypeStruct((n_batch, ts, B_HEADS * LANES), BF16),
        compiler_params=_cparams(("parallel", "parallel")),
        name="diff_attention",
    )(qbz, kb, vb, lam_vecs, sub_g)


MBATCH = 4
AUG = 2 * SUBLANES


def _mlstm_kernel(qtf_ref, kf_ref, vtf_ref, gcf_ref, grf_ref, qtb_ref, kb_ref, vtb_ref, gcb_ref, grb_ref,
                  hf_ref, hb_ref, c_ref, m_ref):
    @pl.when(pl.program_id(1) == 0)
    def _():
        c_ref[...] = jnp.zeros_like(c_ref)
        m_ref[...] = jnp.zeros_like(m_ref)

    length = MCHUNK
    key_idx = _row_iota((length, length))
    qry_idx = _lane_iota((length, length))
    ones_rows = jnp.ones((AUG, length), BF16)
    n_rows = 2 * SUBLANES
    n_samples = qtf_ref.shape[0]
    n_chain = n_samples * 2 * C_HEADS
    c_states = [c_ref[ch] for ch in range(n_chain)]
    m_prevs = [m_ref[ch] for ch in range(n_chain)]
    c_news, m_news, pending = [], [], []

    lanes = ((qtf_ref, kf_ref, vtf_ref, gcf_ref, grf_ref), (qtb_ref, kb_ref, vtb_ref, gcb_ref, grb_ref))
    for sample, direction in [(sm, dr) for sm in range(n_samples) for dr in range(2)]:
        qt_ref, k_ref, vt_ref, gc_ref, gr_ref = (r.at[sample] for r in lanes[direction])
        reverse = direction == 1
        gate_c, cs_c, tot_c = gc_ref[:, :LANES], gc_ref[:, LANES:2 * LANES], gc_ref[:, 2 * LANES:]
        p_c = (tot_c - cs_c + gate_c) if reverse else cs_c
        g_c = gate_c - pltpu.roll(p_c, LANES - C_HEADS, 1)
        mask = (key_idx >= qry_idx) if reverse else (key_idx <= qry_idx)
        for hd in range(C_HEADS):
            chain = (sample * 2 + direction) * C_HEADS + hd
            ii = direction * 2 * C_HEADS + hd
            fi = ii + C_HEADS
            i_row = gr_ref[ii:ii + 1, :]
            f_row = gr_ref[fi:fi + 1, :]
            cs_row = gr_ref[n_rows + fi:n_rows + fi + 1, :]
            tot_row = gr_ref[2 * n_rows + fi:2 * n_rows + fi + 1, :]
            p_row = (tot_row - cs_row + f_row) if reverse else cs_row
            m_prev = m_prevs[chain]
            inter = p_row + m_prev
            log_dt = jnp.where(mask, jnp.broadcast_to(g_c[:, ii:ii + 1], (length, length)) + p_row, NEG)
            m_t = jnp.maximum(inter, jnp.max(log_dt, axis=0, keepdims=True))
            d_t = jnp.exp2(log_dt - m_t)
            a_row = jnp.exp2(inter - m_t)
            sl = slice(hd * C_DIM, (hd + 1) * C_DIM)
            k_h, qt_h = k_ref[:, sl], qt_ref[sl, :]
            vt_aug = jnp.concatenate([vt_ref[sl, :], ones_rows], axis=0)
            s_raw = jnp.dot(k_h, qt_h, preferred_element_type=F32)
            c_state = c_states[chain]
            x_t = jnp.dot(c_state.astype(BF16), qt_h, preferred_element_type=F32)
            w_row = tot_row - p_row + i_row
            m_new = jnp.maximum(tot_row + m_prev, jnp.max(w_row, axis=-1, keepdims=True))
            decay = jnp.exp2(tot_row + m_prev - m_new)
            ws = jnp.exp2(w_row - m_new)
            update = jnp.dot((vt_aug.astype(F32) * ws).astype(BF16), k_h, preferred_element_type=F32)
            c_news.append(decay[:, :C_DIM] * c_state + update)
            m_news.append(m_new)
            pending.append((s_raw, d_t, vt_aug, a_row, x_t, m_t))

    for sample, direction in [(sm, dr) for sm in range(n_samples) for dr in range(2)]:
        h_parts = []
        for hd in range(C_HEADS):
            s_raw, d_t, vt_aug, a_row, x_t, m_t = pending[(sample * 2 + direction) * C_HEADS + hd]
            y_t = jnp.dot(vt_aug, (s_raw * d_t).astype(BF16), preferred_element_type=F32)
            num_t = a_row * x_t[:C_DIM] + y_t[:C_DIM]
            den = a_row * x_t[C_DIM:C_DIM + 1] + y_t[C_DIM:C_DIM + 1]
            h_t = num_t / jnp.maximum(jnp.abs(den), jnp.exp2(-m_t))
            h_parts.append(h_t.T)
        (hb_ref if direction else hf_ref)[sample] = jnp.concatenate(h_parts, axis=1)

    for ch in range(n_chain):
        c_ref[ch] = c_news[ch]
        m_ref[ch] = m_news[ch]


def _mlstm(qmt, km, vmt, gcol, grow):
    n_batch, ts, width = km.shape
    nc = ts // MCHUNK
    ctx_chunks = ROW // MCHUNK
    mbatch = math.gcd(n_batch, MBATCH)

    def bwd(j):
        return jnp.where(j < ctx_chunks, ctx_chunks - 1 - j, nc + ctx_chunks - 1 - j)

    def specs(idx):
        return [
            pl.BlockSpec((mbatch, width, MCHUNK), lambda b, j: (b, 0, idx(j))),
            pl.BlockSpec((mbatch, MCHUNK, width), lambda b, j: (b, idx(j), 0)),
            pl.BlockSpec((mbatch, width, MCHUNK), lambda b, j: (b, 0, idx(j))),
            pl.BlockSpec((mbatch, MCHUNK, 3 * LANES), lambda b, j: (b, idx(j), 0)),
            pl.BlockSpec((mbatch, 6 * SUBLANES, MCHUNK), lambda b, j: (b, 0, idx(j))),
        ]

    fwd = lambda j: j
    n_chain = mbatch * 2 * C_HEADS
    return pl.pallas_call(
        _mlstm_kernel,
        grid=(n_batch // mbatch, nc),
        in_specs=specs(fwd) + specs(bwd),
        out_specs=[
            pl.BlockSpec((mbatch, MCHUNK, width), lambda b, j: (b, j, 0)),
            pl.BlockSpec((mbatch, MCHUNK, width), lambda b, j: (b, bwd(j), 0)),
        ],
        out_shape=[jax.ShapeDtypeStruct((n_batch, ts, width), F32)] * 2,
        scratch_shapes=[
            pltpu.VMEM((n_chain, C_DIM + AUG, C_DIM), F32),
            pltpu.VMEM((n_chain, 1, MCHUNK), F32),
        ],
        compiler_params=_cparams(("parallel", "arbitrary")),
        name="mlstm",
    )(qmt, km, vmt, gcol, grow, qmt, km, vmt, gcol, grow)


def _mix_kernel(*refs, n_stream):
    (a_ref, d_ref, hf_ref, hb_ref, oc_ref, gate_ref, g1_ref, pg_ref, mg_ref,
     wa_ref, wb_ref, wc_ref, wo_ref, o_ref) = refs[n_stream:]
    d_model = o_ref.shape[-1]
    hsum = hf_ref[...] + hb_ref[...]
    mg = mg_ref[...]
    m = jnp.concatenate([_rms(hsum[:, hd * C_DIM:(hd + 1) * C_DIM], mg) for hd in range(C_HEADS)], axis=1)
    m = m * _sigmoid(oc_ref[...].astype(F32))
    u = (_sigmoid(gate_ref[:, :d_model].astype(F32))
         * jnp.dot(a_ref[...], wa_ref[...], preferred_element_type=F32)
         + _sigmoid(gate_ref[:, d_model:2 * d_model].astype(F32))
         * jnp.dot(d_ref[...], wb_ref[...], preferred_element_type=F32)
         + _sigmoid(gate_ref[:, 2 * d_model:].astype(F32))
         * jnp.dot(m.astype(BF16), wc_ref[...], preferred_element_type=F32))
    y = jnp.dot(u.astype(BF16), wo_ref[...], preferred_element_type=F32)
    o_ref[...] = _stream_tile(refs[:n_stream]) + g1_ref[...] * _rms(y, pg_ref[...])


def _mix(a, dd, hf, hb, out_gate, merge_gate, xs, mod, layer, nw, post_g, mlstm_g, wa, wb, wc, wo):
    (n_batch, ts, d), stream_specs, stream_arrays = _stream(xs)
    nt = ts // ROW
    width = a.shape[-1]

    def tile(w):
        return pl.BlockSpec((None, ROW, w), lambda b, i: (b, i, 0))

    def const(shape):
        return pl.BlockSpec(shape, lambda b, i: (0,) * len(shape))

    return pl.pallas_call(
        functools.partial(_mix_kernel, n_stream=len(stream_arrays)),
        grid=(n_batch, nt),
        in_specs=stream_specs + [
            tile(width), tile(width), tile(width), tile(width),
            tile(width), tile(3 * d),
            _mod_spec(d, layer, 2, nw, n_batch),
            const((1, d)), const((1, C_DIM)),
            const((width, d)), const((width, d)), const((width, d)), const((d, d)),
        ],
        out_specs=tile(d),
        out_shape=jax.ShapeDtypeStruct((n_batch, ts, d), F32),
        compiler_params=_cparams(("parallel", "parallel")),
        name="mix_out",
    )(*stream_arrays, a, dd, hf, hb, out_gate, merge_gate, mod, post_g, mlstm_g, wa, wb, wc, wo)


def _ffn_kernel(x_ref, sh_ref, sc_ref, g2_ref, pre_ref, post_ref, wg_ref, wu_ref, wd_ref, o_ref):
    x = x_ref[...]
    xb = (_rms(x, pre_ref[...]) * (1.0 + sc_ref[...]) + sh_ref[...]).astype(BF16)
    gate = jnp.dot(xb, wg_ref[...], preferred_element_type=F32)
    up = jnp.dot(xb, wu_ref[...], preferred_element_type=F32)
    z = jnp.dot((_silu(gate) * up).astype(BF16), wd_ref[...], preferred_element_type=F32)
    o_ref[...] = x + g2_ref[...] * _rms(z, post_ref[...])


def _ffn(xs, mod, layer, nw, pre_g, post_g, wg, wu, wd):
    n_batch, ts, d = xs.shape
    nt = ts // ROW
    dff = wg.shape[1]

    def resident(shape):
        return pl.BlockSpec(shape, lambda b, i: (0,) * len(shape), pipeline_mode=pl.Buffered(1))

    tile = pl.BlockSpec((None, ROW, d), lambda b, i: (b, i, 0))
    return pl.pallas_call(
        _ffn_kernel,
        grid=(n_batch, nt),
        in_specs=[
            tile,
            _mod_spec(d, layer, 3, nw, n_batch), _mod_spec(d, layer, 4, nw, n_batch),
            _mod_spec(d, layer, 5, nw, n_batch),
            pl.BlockSpec((1, d), lambda b, i: (0, 0)), pl.BlockSpec((1, d), lambda b, i: (0, 0)),
            resident((d, dff)), resident((d, dff)), resident((dff, d)),
        ],
        out_specs=tile,
        out_shape=jax.ShapeDtypeStruct((n_batch, ts, d), F32),
        compiler_params=_cparams(("parallel", "parallel")),
        name="ffn",
    )(xs, mod, mod, mod, pre_g, post_g, wg, wu, wd)


def _route_kernel(x_ref, sh_ref, sc_ref, pre_ref, wr_ref, br_ref, tri_ref, xn_ref, meta_ref, slots_ref, cnt_ref,
                  carry_ref):
    @pl.when(jnp.logical_and(pl.program_id(0) == 0, pl.program_id(1) == 0))
    def _():
        carry_ref[...] = jnp.zeros_like(carry_ref)

    xn = _rms(x_ref[...], pre_ref[...]) * (1.0 + sc_ref[...]) + sh_ref[...]
    xn_ref[...] = xn
    lane = _lane_iota((ROW, LANES))
    logits = jnp.dot(xn, wr_ref[...], preferred_element_type=F32, precision=HIGHEST) + br_ref[...]
    logits = jnp.where(lane < N_EXPERTS, logits, NEG)
    v1 = jnp.max(logits, axis=-1, keepdims=True)
    i1 = jnp.min(jnp.where(logits == v1, lane, LANES), axis=-1, keepdims=True)
    rest = jnp.where(lane == i1, NEG, logits)
    v2 = jnp.max(rest, axis=-1, keepdims=True)
    i2 = jnp.min(jnp.where(rest == v2, lane, LANES), axis=-1, keepdims=True)
    e2 = jnp.exp(v2 - v1)
    w1 = 1.0 / (1.0 + e2)
    w2 = e2 / (1.0 + e2)
    assigned = jnp.where(lane == i1, 1.0, jnp.where(lane == i2, 1.0, 0.0))
    before = jnp.dot(tri_ref[...], assigned.astype(BF16), preferred_element_type=F32) + carry_ref[0:1, :]
    r1 = jnp.sum(jnp.where(lane == i1, before, 0.0), axis=-1, keepdims=True)
    r2 = jnp.sum(jnp.where(lane == i2, before, 0.0), axis=-1, keepdims=True)
    carry_ref[...] = carry_ref[...] + jnp.sum(assigned, axis=0, keepdims=True)
    cnt_ref[...] = carry_ref[...]
    fields = (i1.astype(F32), i2.astype(F32), w1, w2, r1, r2)
    meta = jnp.zeros((ROW, LANES), F32)
    for f, val in enumerate(fields):
        meta = jnp.where(lane == f, val, meta)
    meta_ref[...] = meta
    slots_ref[...] = meta.T[:SUBLANES, :]


def _route(xs, mod, layer, nw, pre_g, w_r, b_r):
    n_batch, ts, d = xs.shape
    nt = ts // ROW
    tile = pl.BlockSpec((None, ROW, d), lambda b, i: (b, i, 0))
    t_idx = jnp.arange(ROW)
    tri_strict = (t_idx[:, None] > t_idx[None, :]).astype(BF16)
    return pl.pallas_call(
        _route_kernel,
        grid=(n_batch, nt),
        in_specs=[
            tile, _mod_spec(d, layer, 3, nw, n_batch), _mod_spec(d, layer, 4, nw, n_batch),
            pl.BlockSpec((1, d), lambda b, i: (0, 0)),
            pl.BlockSpec((d, LANES), lambda b, i: (0, 0)),
            pl.BlockSpec((1, LANES), lambda b, i: (0, 0)),
            pl.BlockSpec((ROW, ROW), lambda b, i: (0, 0)),
        ],
        out_specs=[tile, pl.BlockSpec((None, ROW, LANES), lambda b, i: (b, i, 0)),
                   pl.BlockSpec((None, SUBLANES, ROW), lambda b, i: (b, 0, i)),
                   pl.BlockSpec((SUBLANES, LANES), lambda b, i: (0, 0))],
        out_shape=[jax.ShapeDtypeStruct((n_batch, ts, d), F32),
                   jax.ShapeDtypeStruct((n_batch, ts, LANES), F32),
                   jax.ShapeDtypeStruct((n_batch, SUBLANES, ts), F32),
                   jax.ShapeDtypeStruct((SUBLANES, LANES), F32)],
        scratch_shapes=[pltpu.VMEM((SUBLANES, LANES), F32)],
        compiler_params=_cparams(("arbitrary", "arbitrary")),
        name="route",
    )(xs, mod, mod, pre_g, w_r, b_r, tri_strict)


GROUPS = ROW // SUBLANES


def _start_rows(make_copy):
    for g in range(GROUPS):
        for j in range(SUBLANES):
            for slot in range(2):
                make_copy(g, j, slot).start()


def _wait_rows(make_copy):
    def drain(g, carry):
        for j in range(SUBLANES):
            for slot in range(2):
                make_copy(0, 0, slot).wait()
        return carry

    lax.fori_loop(0, GROUPS, drain, 0)


STAGES = 3


def _dispatch_kernel(pos_ref, bounds_ref, xn_ref, out_ref, zero_ref, stage_ref, row_sems, in_sems, zero_sem):
    step = pl.program_id(0)
    n_steps = pl.num_programs(0)
    n_tok = pos_ref.shape[0] // 2
    tm = zero_ref.shape[0]
    n_sorted = out_ref.shape[0]

    @pl.when(step == 0)
    def _():
        zero_ref[...] = jnp.zeros_like(zero_ref)

        def fill(row):
            copy = pltpu.make_async_copy(zero_ref, out_ref.at[pl.ds(pl.multiple_of(row, tm), tm)], zero_sem)
            copy.start()
            copy.wait()

        for e in range(N_EXPERTS):
            end = bounds_ref[N_EXPERTS + e]

            @pl.when(end > bounds_ref[e])
            def _():
                fill(end - tm)

        last_end = bounds_ref[2 * N_EXPERTS - 1]

        def tail(k, carry):
            fill(last_end + k * tm)
            return carry

        lax.fori_loop(0, (n_sorted - last_end) // tm, tail, 0)

    def tile_in(tile):
        slot = tile % STAGES
        return pltpu.make_async_copy(xn_ref.at[pl.ds(tile * GROUPS, GROUPS)], stage_ref.at[slot], in_sems.at[slot])

    def copies(tile):
        def make_copy(g, j, slot):
            p = pos_ref[slot * n_tok + tile * ROW + g * SUBLANES + j]
            return pltpu.make_async_copy(stage_ref.at[tile % STAGES, g, pl.ds(j, 1)], out_ref.at[pl.ds(p, 1)],
                                         row_sems.at[tile % STAGES])
        return make_copy

    @pl.when(step == 0)
    def _():
        tile_in(step).start()

    @pl.when(step >= STAGES - 1)
    def _():
        _wait_rows(copies(step - (STAGES - 1)))

    @pl.when(step + 1 < n_steps)
    def _():
        tile_in(step + 1).start()

    tile_in(step).wait()
    _start_rows(copies(step))

    @pl.when(step == n_steps - 1)
    def _():
        for back in range(STAGES - 2, -1, -1):
            @pl.when(step - back >= 0)
            def _():
                _wait_rows(copies(step - back))


def _dispatch(pos, bounds, xn, n_sorted, tm):
    m_rows, d = xn.shape
    return pl.pallas_call(
        _dispatch_kernel,
        grid_spec=pltpu.PrefetchScalarGridSpec(
            num_scalar_prefetch=2,
            grid=(m_rows // ROW,),
            in_specs=[pl.BlockSpec(memory_space=pl.ANY)],
            out_specs=pl.BlockSpec(memory_space=pl.ANY),
            scratch_shapes=[pltpu.VMEM((tm, d), F32), pltpu.VMEM((STAGES, GROUPS, SUBLANES, d), F32),
                            pltpu.SemaphoreType.DMA((STAGES,)), pltpu.SemaphoreType.DMA((STAGES,)),
                            pltpu.SemaphoreType.DMA(())],
        ),
        out_shape=jax.ShapeDtypeStruct((n_sorted, d), F32),
        compiler_params=_cparams(("arbitrary",)),
        name="dispatch",
    )(pos, bounds, xn.reshape(m_rows // SUBLANES, SUBLANES, d))


def _experts_kernel(te_ref, x_ref, wg_ref, wu_ref, wd_ref, y_ref):
    used = te_ref[pl.program_id(0)] < N_EXPERTS

    @pl.when(used)
    def _():
        x = x_ref[...].astype(BF16)
        gate = jnp.dot(x, wg_ref[...], preferred_element_type=F32)
        up = jnp.dot(x, wu_ref[...], preferred_element_type=F32)
        y_ref[...] = jnp.dot((_silu(gate) * up).astype(BF16), wd_ref[...], preferred_element_type=F32)

    @pl.when(jnp.logical_not(used))
    def _():
        y_ref[...] = jnp.zeros_like(y_ref)


def _experts(tile_expert, x_sorted, wg, wu, wd, tm):
    n_sorted, d = x_sorted.shape
    n_e, _, dff = wg.shape

    def weight(shape):
        return pl.BlockSpec((None,) + shape, lambda i, te: (jnp.minimum(te[i], n_e - 1), 0, 0))

    return pl.pallas_call(
        _experts_kernel,
        grid_spec=pltpu.PrefetchScalarGridSpec(
            num_scalar_prefetch=1,
            grid=(n_sorted // tm,),
            in_specs=[pl.BlockSpec((tm, d), lambda i, te: (i, 0)),
                      weight((d, dff)), weight((d, dff)), weight((dff, d))],
            out_specs=pl.BlockSpec((tm, d), lambda i, te: (i, 0)),
        ),
        out_shape=jax.ShapeDtypeStruct((n_sorted, d), F32),
        compiler_params=_cparams(("arbitrary",)),
        name="experts",
    )(tile_expert, x_sorted, wg, wu, wd)


def _combine_kernel(pos_ref, x_ref, meta_ref, g2_ref, post_ref, y_ref, o_ref, buf_ref, sems, *,
                    tiles_per_sample, latent_only):
    step = pl.program_id(0)
    n_steps = pl.num_programs(0)
    n_tok = pos_ref.shape[0] // 2
    d_model = x_ref.shape[-1]

    def wanted(tile):
        return (tile % tiles_per_sample != 0) if latent_only else (tile >= 0)

    def copies(tile):
        def make_copy(g, j, slot):
            p = pos_ref[slot * n_tok + tile * ROW + g * SUBLANES + j]
            return pltpu.make_async_copy(y_ref.at[pl.ds(p, 1)], buf_ref.at[tile % 2, slot, g, pl.ds(j, 1)],
                                         sems.at[tile % 2])
        return make_copy

    @pl.when(jnp.logical_and(step == 0, wanted(step)))
    def _():
        _start_rows(copies(step))

    @pl.when(jnp.logical_and(step + 1 < n_steps, wanted(step + 1)))
    def _():
        _start_rows(copies(step + 1))

    @pl.when(wanted(step))
    def _():
        _wait_rows(copies(step))
        meta = meta_ref[...]
        cur = step % 2
        y1 = buf_ref[cur, 0].reshape(ROW, d_model)
        y2 = buf_ref[cur, 1].reshape(ROW, d_model)
        z = meta[:, 2:3] * y1 + meta[:, 3:4] * y2
        o_ref[...] = x_ref[...] + g2_ref[...] * _rms(z, post_ref[...])


def _combine(pos, xs, meta, y_sorted, mod, layer, nw, post_g, latent_only):
    n_batch, ts, d = xs.shape
    nt = ts // ROW
    m_rows = n_batch * ts
    base = (layer * 6 + 5) * nw
    tile = pl.BlockSpec((ROW, d), lambda i, pos: (i, 0))
    if latent_only:
        out_rows = n_batch * (ts - ROW)
        out_tile = pl.BlockSpec((ROW, d), lambda i, pos: ((i // nt) * (nt - 1) + jnp.maximum(i % nt - 1, 0), 0))
    else:
        out_rows, out_tile = m_rows, tile
    out = pl.pallas_call(
        functools.partial(_combine_kernel, tiles_per_sample=nt, latent_only=latent_only),
        grid_spec=pltpu.PrefetchScalarGridSpec(
            num_scalar_prefetch=1,
            grid=(m_rows // ROW,),
            in_specs=[
                tile,
                pl.BlockSpec((ROW, LANES), lambda i, pos: (i, 0)),
                pl.BlockSpec((None, 1, d), lambda i, pos: (base + jnp.where(i % nt == 0, n_batch, i // nt), 0, 0)),
                pl.BlockSpec((1, d), lambda i, pos: (0, 0)),
                pl.BlockSpec(memory_space=pl.ANY),
            ],
            out_specs=out_tile,
            scratch_shapes=[pltpu.VMEM((2, 2, GROUPS, SUBLANES, d), F32), pltpu.SemaphoreType.DMA((2,))],
        ),
        out_shape=jax.ShapeDtypeStruct((out_rows, d), F32),
        compiler_params=_cparams(("arbitrary",)),
        name="combine",
    )(pos, xs.reshape(m_rows, d), meta.reshape(m_rows, LANES), mod, post_g, y_sorted)
    return out.reshape(n_batch, out_rows // n_batch, d)


def _moe(xs, mod, layer, nw, pre_g, post_g, w_r, b_r, wg, wu, wd, latent_only, tm=512):
    n_batch, ts, d = xs.shape
    m_rows = n_batch * ts
    xn, meta, slots, cnt = _route(xs, mod, layer, nw, pre_g, w_r, b_r)
    i1, i2, r1, r2 = (slots[:, f, :].reshape(m_rows).astype(jnp.int32) for f in (0, 1, 4, 5))
    counts = cnt[0, :N_EXPERTS].astype(jnp.int32)
    padded = -(-counts // tm) * tm
    ends = jnp.cumsum(padded)
    start = ends - padded
    pos = jnp.concatenate([start[i1] + r1, start[i2] + r2])
    bounds = jnp.concatenate([start, ends])
    n_tiles = 2 * m_rows // tm + N_EXPERTS
    tile_row = jnp.arange(n_tiles, dtype=jnp.int32) * tm
    tile_expert = jnp.sum((ends[None, :] <= tile_row[:, None]).astype(jnp.int32), axis=1)
    x_sorted = _dispatch(pos, bounds, xn.reshape(m_rows, d), n_tiles * tm, tm)
    y_sorted = _experts(tile_expert, x_sorted, wg, wu, wd, tm)
    return _combine(pos, xs, meta, y_sorted, mod, layer, nw, post_g, latent_only)


def _rope_tables(n_tok, n_ctx):
    f32 = np.float32
    n_freq = HEAD_DIM // 4
    pos = np.arange(n_tok)
    row = (pos // GRID_W).astype(f32)
    colp = (pos % GRID_W).astype(f32)
    inv = (f32(ROPE_THETA) ** (-np.arange(n_freq, dtype=f32) / f32(n_freq))).astype(f32)
    lane = np.arange(LANES)
    in_head = lane % HEAD_DIM
    use_col = (in_head // (HEAD_DIM // 2)) == 1
    freq = inv[in_head % n_freq]
    ang = (np.where(use_col[None, :], colp[:, None], row[:, None]) * freq[None, :]).astype(f32)
    lower = (in_head % (HEAD_DIM // 2)) < n_freq
    cos_t = np.cos(ang).astype(f32)
    sin_t = np.sin(ang).astype(f32)
    sin_a = np.where(lower[None, :], -sin_t, f32(0.0)).astype(f32)
    sin_b = np.where(lower[None, :], f32(0.0), sin_t).astype(f32)
    pad = lambda t, v: np.concatenate([np.full((n_ctx, LANES), v, f32), t], axis=0)
    return jnp.asarray(pad(cos_t, 1.0)), jnp.asarray(pad(sin_a, 0.0)), jnp.asarray(pad(sin_b, 0.0))


def _block_diag_mean(width):
    idx = jnp.arange(width) // HEAD_DIM
    return jnp.where(idx[:, None] == idx[None, :], 1.0 / HEAD_DIM, 0.0).astype(BF16)


def _pack_w_in(w):
    a_q = A_HEADS * HEAD_DIM
    a_kv = 2 * A_KV_HEADS * HEAD_DIM
    n_gates = 4 * C_HEADS
    gate_start = w.shape[1] - 3 * w.shape[0]
    g_start = gate_start - n_gates
    main = jnp.concatenate([w[:, gate_start:], w[:, :a_q], w[:, a_q + a_kv:g_start], w[:, a_q:a_q + a_kv]], axis=1)
    gates = jnp.pad(w[:, g_start:gate_start], ((0, 0), (0, LANES - n_gates)))
    return main.astype(BF16), gates.astype(BF16)


def kernel(x, c, ctx, c_ctx, ada_w, ada_b, pre_mix_g, post_mix_g, pre_ffn_g, post_ffn_g, w_in, q_norm_g, k_norm_g, lam_q1, lam_k1, lam_q2, lam_k2, diff_norm_g, conv_w, conv_b, mlstm_gate_b, mlstm_norm_g, w_br_attn, w_br_diff, w_br_mlstm, w_out, w_ff_gate, w_ff_up, w_ff_down, w_router, b_router, w_moe_gate, w_moe_up, w_moe_down):
    n_batch, n_tok, d = x.shape
    n_ctx = ctx.shape[1]
    depth = ada_w.shape[0]
    assert n_ctx == ROW and n_tok % ROW == 0 and d == 1024
    ts = n_ctx + n_tok
    nw = -(-(n_batch + 1) // SUBLANES) * SUBLANES

    c_all = jnp.concatenate([c, c_ctx[None, :], jnp.zeros((nw - n_batch - 1, d), F32)], axis=0)
    mod = _modulation(c_all, ada_w, ada_b)
    tables = _rope_tables(n_tok, n_ctx)
    bd4, bd1 = _block_diag_mean(A_HEADS * HEAD_DIM), _block_diag_mean(LANES)
    t_idx = jnp.arange(MCHUNK)
    tri = (t_idx[:, None] >= t_idx[None, :]).astype(BF16)

    xs = (ctx, x)
    for l in range(depth):
        lam_init = 0.8 - 0.6 * math.exp(-0.3 * l)
        w_main, w_gate = _pack_w_in(w_in[l])
        qg = jnp.tile(q_norm_g[l], A_HEADS)[None, :]
        kg = jnp.tile(k_norm_g[l], A_KV_HEADS)[None, :]
        gate_b = jnp.pad(mlstm_gate_b[l], (0, LANES - 4 * C_HEADS))[None, :]
        merge_gate, out_gate, qaz, ka, va, qbz, kb, vb, qmt, km, vmt, gcol, grow = _inproj(
            xs, mod, l, nw, pre_mix_g[l][None, :], w_main, w_gate,
            tables, qg, kg, bd4, bd1, conv_w[l], conv_b[l][None, :], gate_b, tri)
        a_out = _gqa(qaz, ka, va)
        lam_vecs = jnp.stack([lam_q1[l], lam_k1[l], lam_q2[l], lam_k2[l]], axis=0)
        d_out = _diff(qbz, kb, vb, lam_vecs, diff_norm_g[l][None, :], lam_init)
        hf, hb = _mlstm(qmt, km, vmt, gcol, grow)
        xs = _mix(a_out, d_out, hf, hb, out_gate, merge_gate, xs, mod, l, nw,
                  post_mix_g[l][None, :], mlstm_norm_g[l][None, :],
                  w_br_attn[l].astype(BF16), w_br_diff[l].astype(BF16), w_br_mlstm[l].astype(BF16),
                  w_out[l].astype(BF16))
        j = l // 2
        if l % 2 == 0:
            xs = _ffn(xs, mod, l, nw, pre_ffn_g[l][None, :], post_ffn_g[l][None, :],
                      w_ff_gate[j].astype(BF16), w_ff_up[j].astype(BF16), w_ff_down[j].astype(BF16))
        else:
            w_r = jnp.pad(w_router[j], ((0, 0), (0, LANES - N_EXPERTS)))
            b_r = jnp.pad(b_router[j], (0, LANES - N_EXPERTS))[None, :]
            xs = _moe(xs, mod, l, nw, pre_ffn_g[l][None, :], post_ffn_g[l][None, :], w_r, b_r,
                      w_moe_gate[j].astype(BF16), w_moe_up[j].astype(BF16), w_moe_down[j].astype(BF16),
                      latent_only=l == depth - 1)
    return xs if xs.shape[1] == n_tok else xs[:, n_ctx:, :]
```

```python
import functools
import math

import jax
import jax.numpy as jnp
import numpy as np
from jax import lax
from jax.experimental import pallas as pl
from jax.experimental.pallas import tpu as pltpu

F32 = jnp.float32
BF16 = jnp.bfloat16
HIGHEST = lax.Precision.HIGHEST

EPS = 1e-6
HEAD_DIM = 64
A_HEADS = 8
A_KV_HEADS = 2
B_HEADS = 4
C_HEADS = 4
C_DIM = 128
N_EXPERTS = 8
ROPE_THETA = 10000.0
GRID_W = 64
CONV_W = 3

LANES = 128
SUBLANES = 8
ROW = 256
MCHUNK = 256
LOG2E = math.log2(math.e)
NEG = -1e30
V7X_VMEM_BYTES = 64 * 1024 * 1024
VMEM_LIMIT = V7X_VMEM_BYTES - 8 * 1024 * 1024

REL_QA, REL_KB, REL_QC, REL_VC, REL_KA = 0, 1024, 2048, 3072, 4096


def _cparams(sem):
    return pltpu.CompilerParams(dimension_semantics=sem, vmem_limit_bytes=VMEM_LIMIT)


def _rms(x, g):
    y = x * lax.rsqrt(jnp.mean(x * x, axis=-1, keepdims=True) + EPS)
    return y * g


def _sigmoid(x):
    return 1.0 / (1.0 + jnp.exp(-x))


def _silu(x):
    return x * _sigmoid(x)


def _log_sigmoid(x):
    return jnp.minimum(x, 0.0) - jnp.log(1.0 + jnp.exp(-jnp.abs(x)))


def _lane_iota(shape):
    return lax.broadcasted_iota(jnp.int32, shape, len(shape) - 1)


def _row_iota(shape):
    return lax.broadcasted_iota(jnp.int32, shape, len(shape) - 2)


def _mod_kernel(c_ref, w_ref, b_ref, o_ref):
    c = c_ref[...]
    o_ref[...] = jnp.dot(_silu(c), w_ref[...], preferred_element_type=F32, precision=HIGHEST) + b_ref[...]


def _modulation(c_all, ada_w, ada_b):
    depth, d, _ = ada_w.shape
    nw = c_all.shape[0]
    out = pl.pallas_call(
        _mod_kernel,
        grid=(depth, 6),
        in_specs=[
            pl.BlockSpec((nw, d), lambda l, j: (0, 0)),
            pl.BlockSpec((None, d, d), lambda l, j: (l, 0, j)),
            pl.BlockSpec((None, 1, d), lambda l, j: (l, 0, j)),
        ],
        out_specs=pl.BlockSpec((None, None, nw, d), lambda l, j: (l, j, 0, 0)),
        out_shape=jax.ShapeDtypeStruct((depth, 6, nw, d), F32),
        compiler_params=_cparams(("arbitrary", "arbitrary")),
        name="modulation",
    )(c_all, ada_w, ada_b.reshape(depth, 1, 6 * d))
    return out.reshape(depth * 6 * nw, 1, d)


def _mod_spec(d, layer, chunk, nw, n_batch):
    base = (layer * 6 + chunk) * nw
    return pl.BlockSpec((None, 1, d), lambda b, i: (base + jnp.where(i == 0, n_batch, b), 0, 0))


def _stream(stream):
    if isinstance(stream, tuple):
        ctx, x = stream
        n_batch, n_tok, d = x.shape
        specs = [pl.BlockSpec((None, ROW, d), lambda b, i: (b, 0, 0)),
                 pl.BlockSpec((None, ROW, d), lambda b, i: (b, jnp.maximum(i - 1, 0), 0))]
        return (n_batch, ctx.shape[1] + n_tok, d), specs, [ctx, x]
    d = stream.shape[-1]
    return stream.shape, [pl.BlockSpec((None, ROW, d), lambda b, i: (b, i, 0))], [stream]


def _stream_tile(refs):
    if len(refs) == 2:
        return jnp.where(pl.program_id(1) == 0, refs[0][...], refs[1][...])
    return refs[0][...]


def _inproj_kernel(*refs, n_stream):
    (xprev_ref, xnext_ref, sh_ref, sc_ref, g_ref, w_ref, wg_ref,
     cos_ref, sa_ref, sb_ref, qg_ref, kg_ref, bd4_ref, bd1_ref, cw_ref, cb_ref, gb_ref, tri_ref,
     gate_ref, oc_ref, *mixer_refs) = refs[n_stream:]
    d_model = xprev_ref.shape[-1]

    def normed(x):
        return _rms(x, g_ref[...]) * (1.0 + sc_ref[...]) + sh_ref[...]

    xn = normed(_stream_tile(refs[:n_stream]))
    xb = xn.astype(BF16)
    ext = jnp.concatenate([normed(xprev_ref[...]), xn, normed(xnext_ref[...])], axis=0).astype(BF16)

    def proj(lhs, start, width):
        return jnp.dot(lhs, w_ref[:, start:start + width], preferred_element_type=F32)

    base = 3 * d_model
    half = C_HEADS * C_DIM
    qab = proj(xb, base + REL_QA, 2 * half)
    kvb = proj(xb, base + REL_KB, 2 * half)
    qkc = proj(ext, base + REL_QC, 2 * half)
    vo = proj(xb, base + REL_VC, 2 * half)
    oc_ref[...] = vo[:, half:].astype(BF16)
    kava = proj(xb, base + REL_KA, 2 * LANES)
    gates = jnp.dot(xb, wg_ref[...], preferred_element_type=F32)
    _prep_math(pl.program_id(1), pl.num_programs(1),
               qab[:, :half], kava, qab[:, half:], kvb[:, :half], kvb[:, half:],
               qkc[SUBLANES:SUBLANES + ROW], qkc[SUBLANES - 1:SUBLANES], qkc[SUBLANES + ROW:SUBLANES + ROW + 1],
               vo[:, :half], gates,
               cos_ref, sa_ref, sb_ref, qg_ref, kg_ref, bd4_ref, bd1_ref, cw_ref, cb_ref, gb_ref, tri_ref,
               *mixer_refs)
    for c in range(6):
        gate_ref[:, c * half:(c + 1) * half] = proj(xb, c * half, half).astype(BF16)


def _inproj(xs, mod, layer, nw, pre_g, w_main, w_gate, tables, qg, kg, bd4, bd1, conv_w, conv_b, gate_b, tri):
    (n_batch, ts, d), stream_specs, stream_arrays = _stream(xs)
    halo_src = stream_arrays[-1]
    tile_shift = len(stream_arrays) - 1
    n_main = w_main.shape[1]
    nt = ts // ROW
    cos_t, sa_t, sb_t = tables
    row_blocks = halo_src.shape[1] // SUBLANES
    per_tile = ROW // SUBLANES
    half = C_HEADS * C_DIM

    def const(shape):
        return pl.BlockSpec(shape, lambda b, i: (0,) * len(shape))

    def rows(width):
        return pl.BlockSpec((None, ROW, width), lambda b, i: (b, i, 0))

    def heads(n, width):
        return pl.BlockSpec((None, n, ROW, width), lambda b, i: (b, 0, i, 0))

    def cols(height):
        return pl.BlockSpec((None, height, ROW), lambda b, i: (b, 0, i))

    table = pl.BlockSpec((ROW, LANES), lambda b, i: (i, 0))
    in_specs = stream_specs + [
        pl.BlockSpec((None, SUBLANES, d), lambda b, i: (b, jnp.maximum((i - tile_shift) * per_tile - 1, 0), 0)),
        pl.BlockSpec((None, SUBLANES, d),
                     lambda b, i: (b, jnp.clip((i + 1 - tile_shift) * per_tile, 0, row_blocks - 1), 0)),
        _mod_spec(d, layer, 0, nw, n_batch),
        _mod_spec(d, layer, 1, nw, n_batch),
        const((1, d)),
        pl.BlockSpec((d, n_main), lambda b, i: (0, 0), pipeline_mode=pl.Buffered(1)),
        const((d, LANES)),
        table, table, table,
        const((1, half)), const((1, LANES)), const((half, half)), const((LANES, LANES)),
        const((CONV_W, 2 * half)), const((1, 2 * half)), const((1, LANES)), const((MCHUNK, MCHUNK)),
    ]
    outs = [
        (rows(3 * d), (ts, 3 * d), BF16),
        (rows(half), (ts, half), BF16),
        (heads(A_HEADS, LANES), (A_HEADS, ts, LANES), BF16),
        (rows(LANES), (ts, LANES), BF16),
        (heads(A_KV_HEADS, LANES), (A_KV_HEADS, ts, LANES), BF16),
        (heads(2 * B_HEADS, LANES), (2 * B_HEADS, ts, LANES), BF16),
        (heads(B_HEADS, LANES), (B_HEADS, ts, LANES), BF16),
        (heads(B_HEADS, 2 * LANES), (B_HEADS, ts, 2 * LANES), BF16),
        (cols(half), (half, ts), BF16),
        (rows(half), (ts, half), BF16),
        (cols(half), (half, ts), BF16),
        (rows(3 * LANES), (ts, 3 * LANES), F32),
        (cols(6 * SUBLANES), (6 * SUBLANES, ts), F32),
    ]
    return pl.pallas_call(
        functools.partial(_inproj_kernel, n_stream=len(stream_arrays)),
        grid=(n_batch, nt),
        in_specs=in_specs,
        out_specs=[spec for spec, _, _ in outs],
        out_shape=[jax.ShapeDtypeStruct((n_batch,) + shape, dtype) for _, shape, dtype in outs],
        compiler_params=_cparams(("parallel", "parallel")),
        name="inproj",
    )(*stream_arrays, halo_src, halo_src, mod, mod, pre_g, w_main, w_gate, cos_t, sa_t, sb_t, qg, kg, bd4, bd1, conv_w, conv_b, gate_b, tri)


def _head_mean_sq(x, bd_ref):
    return jnp.dot((x * x).astype(BF16), bd_ref[...], preferred_element_type=F32)


def _prep_math(i, nt, qa, kava, qb, kb, vb, cur, prev_row, next_row, vc, gates,
               cos_ref, sa_ref, sb_ref, qg_ref, kg_ref, bd4_ref, bd1_ref, cw_ref, cb_ref, gb_ref, tri_ref,
               qaz_ref, ka_ref, va_ref, qbz_ref, kbo_ref, vbo_ref, qmt_ref, km_ref, vmt_ref, gcol_ref, grow_ref):
    cos, sin_a, sin_b = cos_ref[...], sa_ref[...], sb_ref[...]

    def rope(x):
        width = x.shape[1]
        reps = width // LANES
        c = jnp.concatenate([cos] * reps, axis=1) if reps > 1 else cos
        a = jnp.concatenate([sin_a] * reps, axis=1) if reps > 1 else sin_a
        b = jnp.concatenate([sin_b] * reps, axis=1) if reps > 1 else sin_b
        return x * c + pltpu.roll(x, width - 16, 1) * a + pltpu.roll(x, 16, 1) * b

    lane = _lane_iota((ROW, LANES))
    ones = jnp.ones((ROW, LANES), BF16)
    scale = HEAD_DIM ** -0.5 * LOG2E

    qa = qa * lax.rsqrt(_head_mean_sq(qa, bd4_ref) + EPS) * qg_ref[...]
    qa = rope(qa) * scale
    heads_per_kv = A_HEADS // A_KV_HEADS
    for h in range(A_HEADS):
        g = h // heads_per_kv
        blk = qa[:, (h // 2) * LANES:(h // 2 + 1) * LANES]
        if h % 2 != g:
            blk = pltpu.roll(blk, HEAD_DIM, 1)
        qaz_ref[h] = jnp.where(lane // HEAD_DIM == g, blk, 0.0).astype(BF16)
    ka = kava[:, :LANES]
    ka = ka * lax.rsqrt(_head_mean_sq(ka, bd1_ref) + EPS) * kg_ref[...]
    ka_ref[...] = rope(ka).astype(BF16)
    va = kava[:, LANES:].astype(BF16)
    for g in range(A_KV_HEADS):
        va_ref[g] = jnp.where(lane // HEAD_DIM == g, va, ones)

    qb = rope(qb) * scale
    kb = rope(kb)
    for h in range(B_HEADS):
        blk = qb[:, h * LANES:(h + 1) * LANES]
        for m in range(2):
            qbz_ref[2 * h + m] = jnp.where(lane // HEAD_DIM == m, blk, 0.0).astype(BF16)
        kbo_ref[h] = kb[:, h * LANES:(h + 1) * LANES].astype(BF16)
        vbo_ref[h, :, :LANES] = vb[:, h * LANES:(h + 1) * LANES].astype(BF16)
        vbo_ref[h, :, LANES:] = ones

    row = _row_iota(cur.shape)
    prev_row = jnp.where(i >= 2, prev_row, 0.0)
    next_row = jnp.where(jnp.logical_and(i >= 1, i < nt - 1), next_row, 0.0)
    up = jnp.where(row == 0, prev_row, pltpu.roll(cur, 1, 0))
    dn = jnp.where(row == ROW - 1, next_row, pltpu.roll(cur, ROW - 1, 0))
    y = up * cw_ref[0:1, :] + cur * cw_ref[1:2, :] + dn * cw_ref[2:3, :] + cb_ref[...]
    y = _silu(y)
    half = C_HEADS * C_DIM
    qmt_ref[...] = y[:, :half].T.astype(BF16)
    km_ref[...] = (y[:, half:] * (C_DIM ** -0.5)).astype(BF16)
    vmt_ref[...] = vc.T.astype(BF16)

    gg = gates + gb_ref[...]
    is_forget = (lane // C_HEADS) % 2 == 1
    gl = jnp.where(is_forget, _log_sigmoid(gg), gg) * LOG2E
    tri = tri_ref[...]
    n_rows = 2 * SUBLANES
    for c in range(ROW // MCHUNK):
        rows = slice(c * MCHUNK, (c + 1) * MCHUNK)
        glc = gl[rows]
        hi = glc.astype(BF16)
        rest = glc - hi.astype(F32)
        mid = rest.astype(BF16)
        low = (rest - mid.astype(F32)).astype(BF16)
        cs = (jnp.dot(tri, hi, preferred_element_type=F32) + jnp.dot(tri, mid, preferred_element_type=F32)
              + jnp.dot(tri, low, preferred_element_type=F32))
        tot = jnp.broadcast_to(jnp.sum(glc, axis=0, keepdims=True), glc.shape)
        for f, val in enumerate((glc, cs, tot)):
            gcol_ref[rows, f * LANES:(f + 1) * LANES] = val
            grow_ref[f * n_rows:(f + 1) * n_rows, rows] = val.T[:n_rows, :]


def _attend_blocks(blocks, n_keys):
    def scores(q, k_ref):
        return lax.dot_general(q, k_ref[:n_keys, :], (((1,), (1,)), ((), ())), preferred_element_type=F32)

    def weighted(s, v_ref):
        p = jnp.exp2(s - jnp.max(s, axis=-1, keepdims=True)).astype(BF16)
        return jnp.dot(p, v_ref[:n_keys, :], preferred_element_type=F32)

    outs = []
    s_cur = scores(blocks[0][0], blocks[0][1])
    for j in range(1, len(blocks)):
        s_next = scores(blocks[j][0], blocks[j][1])
        outs.append(weighted(s_cur, blocks[j - 1][2]))
        s_cur = s_next
    outs.append(weighted(s_cur, blocks[-1][2]))
    return outs


def _per_tile_keys(body, n_all):
    @pl.when(pl.program_id(1) == 0)
    def _():
        body(ROW)

    @pl.when(pl.program_id(1) > 0)
    def _():
        body(n_all)


def _gqa_kernel(q_ref, k_ref, v_ref, o_ref):
    heads_per_kv = A_HEADS // A_KV_HEADS
    lane = _lane_iota((ROW, LANES))

    def body(n_keys):
        blocks = []
        for j in range(A_HEADS // 2):
            q = q_ref[2 * j:2 * j + 2].reshape(2 * ROW, LANES)
            blocks.append((q, k_ref, v_ref.at[(2 * j) // heads_per_kv]))
        for j, o in enumerate(_attend_blocks(blocks, n_keys)):
            g = (2 * j) // heads_per_kv
            den_lane = (1 - g) * HEAD_DIM
            o = o / o[:, den_lane:den_lane + 1]
            even, odd = o[:ROW], o[ROW:]
            even = even if g == 0 else pltpu.roll(even, HEAD_DIM, 1)
            odd = odd if g == 1 else pltpu.roll(odd, HEAD_DIM, 1)
            o_ref[:, j * LANES:(j + 1) * LANES] = jnp.where(lane < HEAD_DIM, even, odd).astype(BF16)

    _per_tile_keys(body, k_ref.shape[0])


def _diff_kernel(q_ref, k_ref, v_ref, lam_ref, g_ref, o_ref, *, lam_init):
    lv = lam_ref[...]
    lam = (jnp.exp(jnp.sum(lv[0:1] * lv[1:2], axis=-1, keepdims=True))
           - jnp.exp(jnp.sum(lv[2:3] * lv[3:4], axis=-1, keepdims=True)) + lam_init)

    def body(n_keys):
        blocks = [(q_ref[2 * h:2 * h + 2].reshape(2 * ROW, LANES), k_ref.at[h], v_ref.at[h]) for h in range(B_HEADS)]
        for h, o in enumerate(_attend_blocks(blocks, n_keys)):
            o = o[:, :LANES] / o[:, LANES:LANES + 1]
            dif = o[:ROW] - lam * o[ROW:]
            o_ref[:, h * LANES:(h + 1) * LANES] = (_rms(dif, g_ref[...]) * (1.0 - lam_init)).astype(BF16)

    _per_tile_keys(body, k_ref.shape[1])


def _gqa(qaz, ka, va):
    n_batch, _, ts, _ = qaz.shape
    nt = ts // ROW
    return pl.pallas_call(
        _gqa_kernel,
        grid=(n_batch, nt),
        in_specs=[
            pl.BlockSpec((None, A_HEADS, ROW, LANES), lambda b, i: (b, 0, i, 0)),
            pl.BlockSpec((None, ts, LANES), lambda b, i: (b, 0, 0)),
            pl.BlockSpec((None, A_KV_HEADS, ts, LANES), lambda b, i: (b, 0, 0, 0)),
        ],
        out_specs=pl.BlockSpec((None, ROW, A_HEADS * HEAD_DIM), lambda b, i: (b, i, 0)),
        out_shape=jax.ShapeDtypeStruct((n_batch, ts, A_HEADS * HEAD_DIM), BF16),
        compiler_params=_cparams(("parallel", "parallel")),
        name="gqa_attention",
    )(qaz, ka, va)


def _diff(qbz, kb, vb, lam_vecs, sub_g, lam_init):
    n_batch, _, ts, _ = qbz.shape
    nt = ts // ROW
    return pl.pallas_call(
        functools.partial(_diff_kernel, lam_init=lam_init),
        grid=(n_batch, nt),
        in_specs=[
            pl.BlockSpec((None, 2 * B_HEADS, ROW, LANES), lambda b, i: (b, 0, i, 0)),
            pl.BlockSpec((None, B_HEADS, ts, LANES), lambda b, i: (b, 0, 0, 0)),
            pl.BlockSpec((None, B_HEADS, ts, 2 * LANES), lambda b, i: (b, 0, 0, 0)),
            pl.BlockSpec((4, HEAD_DIM), lambda b, i: (0, 0)),
            pl.BlockSpec((1, LANES), lambda b, i: (0, 0)),
        ],
        out_specs=pl.BlockSpec((None, ROW, B_HEADS * LANES), lambda b, i: (b, i, 0)),
        out_shape=jax.ShapeDtypeStruct((n_batch, ts, B_HEADS * LANES), BF16),
        compiler_params=_cparams(("parallel", "parallel")),
        name="diff_attention",
    )(qbz, kb, vb, lam_vecs, sub_g)


MBATCH = 4
AUG = 2 * SUBLANES


def _mlstm_kernel(qtf_ref, kf_ref, vtf_ref, gcf_ref, grf_ref, qtb_ref, kb_ref, vtb_ref, gcb_ref, grb_ref,
                  hf_ref, hb_ref, c_ref, m_ref):
    @pl.when(pl.program_id(1) == 0)
    def _():
        c_ref[...] = jnp.zeros_like(c_ref)
        m_ref[...] = jnp.zeros_like(m_ref)

    length = MCHUNK
    key_idx = _row_iota((length, length))
    qry_idx = _lane_iota((length, length))
    ones_rows = jnp.ones((AUG, length), BF16)
    n_rows = 2 * SUBLANES
    n_samples = qtf_ref.shape[0]
    n_chain = n_samples * 2 * C_HEADS
    c_states = [c_ref[ch] for ch in range(n_chain)]
    m_prevs = [m_ref[ch] for ch in range(n_chain)]
    c_news, m_news, pending = [], [], []

    lanes = ((qtf_ref, kf_ref, vtf_ref, gcf_ref, grf_ref), (qtb_ref, kb_ref, vtb_ref, gcb_ref, grb_ref))
    for sample, direction in [(sm, dr) for sm in range(n_samples) for dr in range(2)]:
        qt_ref, k_ref, vt_ref, gc_ref, gr_ref = (r.at[sample] for r in lanes[direction])
        reverse = direction == 1
        gate_c, cs_c, tot_c = gc_ref[:, :LANES], gc_ref[:, LANES:2 * LANES], gc_ref[:, 2 * LANES:]
        p_c = (tot_c - cs_c + gate_c) if reverse else cs_c
        g_c = gate_c - pltpu.roll(p_c, LANES - C_HEADS, 1)
        mask = (key_idx >= qry_idx) if reverse else (key_idx <= qry_idx)
        for hd in range(C_HEADS):
            chain = (sample * 2 + direction) * C_HEADS + hd
            ii = direction * 2 * C_HEADS + hd
            fi = ii + C_HEADS
            i_row = gr_ref[ii:ii + 1, :]
            f_row = gr_ref[fi:fi + 1, :]
            cs_row = gr_ref[n_rows + fi:n_rows + fi + 1, :]
            tot_row = gr_ref[2 * n_rows + fi:2 * n_rows + fi + 1, :]
            p_row = (tot_row - cs_row + f_row) if reverse else cs_row
            m_prev = m_prevs[chain]
            inter = p_row + m_prev
            log_dt = jnp.where(mask, jnp.broadcast_to(g_c[:, ii:ii + 1], (length, length)) + p_row, NEG)
            m_t = jnp.maximum(inter, jnp.max(log_dt, axis=0, keepdims=True))
            d_t = jnp.exp2(log_dt - m_t)
            a_row = jnp.exp2(inter - m_t)
            sl = slice(hd * C_DIM, (hd + 1) * C_DIM)
            k_h, qt_h = k_ref[:, sl], qt_ref[sl, :]
            vt_aug = jnp.concatenate([vt_ref[sl, :], ones_rows], axis=0)
            s_raw = jnp.dot(k_h, qt_h, preferred_element_type=F32)
            c_state = c_states[chain]
            x_t = jnp.dot(c_state.astype(BF16), qt_h, preferred_element_type=F32)
            w_row = tot_row - p_row + i_row
            m_new = jnp.maximum(tot_row + m_prev, jnp.max(w_row, axis=-1, keepdims=True))
            decay = jnp.exp2(tot_row + m_prev - m_new)
            ws = jnp.exp2(w_row - m_new)
            update = jnp.dot((vt_aug.astype(F32) * ws).astype(BF16), k_h, preferred_element_type=F32)
            c_news.append(decay[:, :C_DIM] * c_state + update)
            m_news.append(m_new)
            pending.append((s_raw, d_t, vt_aug, a_row, x_t, m_t))

    for sample, direction in [(sm, dr) for sm in range(n_samples) for dr in range(2)]:
        h_parts = []
        for hd in range(C_HEADS):
            s_raw, d_t, vt_aug, a_row, x_t, m_t = pending[(sample * 2 + direction) * C_HEADS + hd]
            y_t = jnp.dot(vt_aug, (s_raw * d_t).astype(BF16), preferred_element_type=F32)
            num_t = a_row * x_t[:C_DIM] + y_t[:C_DIM]
            den = a_row * x_t[C_DIM:C_DIM + 1] + y_t[C_DIM:C_DIM + 1]
            h_t = num_t / jnp.maximum(jnp.abs(den), jnp.exp2(-m_t))
            h_parts.append(h_t.T)
        (hb_ref if direction else hf_ref)[sample] = jnp.concatenate(h_parts, axis=1)

    for ch in range(n_chain):
        c_ref[ch] = c_news[ch]
        m_ref[ch] = m_news[ch]


def _mlstm(qmt, km, vmt, gcol, grow):
    n_batch, ts, width = km.shape
    nc = ts // MCHUNK
    ctx_chunks = ROW // MCHUNK
    mbatch = math.gcd(n_batch, MBATCH)

    def bwd(j):
        return jnp.where(j < ctx_chunks, ctx_chunks - 1 - j, nc + ctx_chunks - 1 - j)

    def specs(idx):
        return [
            pl.BlockSpec((mbatch, width, MCHUNK), lambda b, j: (b, 0, idx(j))),
            pl.BlockSpec((mbatch, MCHUNK, width), lambda b, j: (b, idx(j), 0)),
            pl.BlockSpec((mbatch, width, MCHUNK), lambda b, j: (b, 0, idx(j))),
            pl.BlockSpec((mbatch, MCHUNK, 3 * LANES), lambda b, j: (b, idx(j), 0)),
            pl.BlockSpec((mbatch, 6 * SUBLANES, MCHUNK), lambda b, j: (b, 0, idx(j))),
        ]

    fwd = lambda j: j
    n_chain = mbatch * 2 * C_HEADS
    return pl.pallas_call(
        _mlstm_kernel,
        grid=(n_batch // mbatch, nc),
        in_specs=specs(fwd) + specs(bwd),
        out_specs=[
            pl.BlockSpec((mbatch, MCHUNK, width), lambda b, j: (b, j, 0)),
            pl.BlockSpec((mbatch, MCHUNK, width), lambda b, j: (b, bwd(j), 0)),
        ],
        out_shape=[jax.ShapeDtypeStruct((n_batch, ts, width), F32)] * 2,
        scratch_shapes=[
            pltpu.VMEM((n_chain, C_DIM + AUG, C_DIM), F32),
            pltpu.VMEM((n_chain, 1, MCHUNK), F32),
        ],
        compiler_params=_cparams(("parallel", "arbitrary")),
        name="mlstm",
    )(qmt, km, vmt, gcol, grow, qmt, km, vmt, gcol, grow)


def _mix_kernel(*refs, n_stream):
    (a_ref, d_ref, hf_ref, hb_ref, oc_ref, gate_ref, g1_ref, pg_ref, mg_ref,
     wa_ref, wb_ref, wc_ref, wo_ref, o_ref) = refs[n_stream:]
    d_model = o_ref.shape[-1]
    hsum = hf_ref[...] + hb_ref[...]
    mg = mg_ref[...]
    m = jnp.concatenate([_rms(hsum[:, hd * C_DIM:(hd + 1) * C_DIM], mg) for hd in range(C_HEADS)], axis=1)
    m = m * _sigmoid(oc_ref[...].astype(F32))
    u = (_sigmoid(gate_ref[:, :d_model].astype(F32))
         * jnp.dot(a_ref[...], wa_ref[...], preferred_element_type=F32)
         + _sigmoid(gate_ref[:, d_model:2 * d_model].astype(F32))
         * jnp.dot(d_ref[...], wb_ref[...], preferred_element_type=F32)
         + _sigmoid(gate_ref[:, 2 * d_model:].astype(F32))
         * jnp.dot(m.astype(BF16), wc_ref[...], preferred_element_type=F32))
    y = jnp.dot(u.astype(BF16), wo_ref[...], preferred_element_type=F32)
    o_ref[...] = _stream_tile(refs[:n_stream]) + g1_ref[...] * _rms(y, pg_ref[...])


def _mix(a, dd, hf, hb, out_gate, merge_gate, xs, mod, layer, nw, post_g, mlstm_g, wa, wb, wc, wo):
    (n_batch, ts, d), stream_specs, stream_arrays = _stream(xs)
    nt = ts // ROW
    width = a.shape[-1]

    def tile(w):
        return pl.BlockSpec((None, ROW, w), lambda b, i: (b, i, 0))

    def const(shape):
        return pl.BlockSpec(shape, lambda b, i: (0,) * len(shape))

    return pl.pallas_call(
        functools.partial(_mix_kernel, n_stream=len(stream_arrays)),
        grid=(n_batch, nt),
        in_specs=stream_specs + [
            tile(width), tile(width), tile(width), tile(width),
            tile(width), tile(3 * d),
            _mod_spec(d, layer, 2, nw, n_batch),
            const((1, d)), const((1, C_DIM)),
            const((width, d)), const((width, d)), const((width, d)), const((d, d)),
        ],
        out_specs=tile(d),
        out_shape=jax.ShapeDtypeStruct((n_batch, ts, d), F32),
        compiler_params=_cparams(("parallel", "parallel")),
        name="mix_out",
    )(*stream_arrays, a, dd, hf, hb, out_gate, merge_gate, mod, post_g, mlstm_g, wa, wb, wc, wo)


def _ffn_kernel(x_ref, sh_ref, sc_ref, g2_ref, pre_ref, post_ref, wg_ref, wu_ref, wd_ref, o_ref):
    x = x_ref[...]
    xb = (_rms(x, pre_ref[...]) * (1.0 + sc_ref[...]) + sh_ref[...]).astype(BF16)
    gate = jnp.dot(xb, wg_ref[...], preferred_element_type=F32)
    up = jnp.dot(xb, wu_ref[...], preferred_element_type=F32)
    z = jnp.dot((_silu(gate) * up).astype(BF16), wd_ref[...], preferred_element_type=F32)
    o_ref[...] = x + g2_ref[...] * _rms(z, post_ref[...])


def _ffn(xs, mod, layer, nw, pre_g, post_g, wg, wu, wd):
    n_batch, ts, d = xs.shape
    nt = ts // ROW
    dff = wg.shape[1]

    def resident(shape):
        return pl.BlockSpec(shape, lambda b, i: (0,) * len(shape), pipeline_mode=pl.Buffered(1))

    tile = pl.BlockSpec((None, ROW, d), lambda b, i: (b, i, 0))
    return pl.pallas_call(
        _ffn_kernel,
        grid=(n_batch, nt),
        in_specs=[
            tile,
            _mod_spec(d, layer, 3, nw, n_batch), _mod_spec(d, layer, 4, nw, n_batch),
            _mod_spec(d, layer, 5, nw, n_batch),
            pl.BlockSpec((1, d), lambda b, i: (0, 0)), pl.BlockSpec((1, d), lambda b, i: (0, 0)),
            resident((d, dff)), resident((d, dff)), resident((dff, d)),
        ],
        out_specs=tile,
        out_shape=jax.ShapeDtypeStruct((n_batch, ts, d), F32),
        compiler_params=_cparams(("parallel", "parallel")),
        name="ffn",
    )(xs, mod, mod, mod, pre_g, post_g, wg, wu, wd)


def _route_kernel(x_ref, sh_ref, sc_ref, pre_ref, wr_ref, br_ref, tri_ref, xn_ref, meta_ref, slots_ref, cnt_ref,
                  carry_ref):
    @pl.when(jnp.logical_and(pl.program_id(0) == 0, pl.program_id(1) == 0))
    def _():
        carry_ref[...] = jnp.zeros_like(carry_ref)

    xn = _rms(x_ref[...], pre_ref[...]) * (1.0 + sc_ref[...]) + sh_ref[...]
    xn_ref[...] = xn
    lane = _lane_iota((ROW, LANES))
    logits = jnp.dot(xn, wr_ref[...], preferred_element_type=F32, precision=HIGHEST) + br_ref[...]
    logits = jnp.where(lane < N_EXPERTS, logits, NEG)
    v1 = jnp.max(logits, axis=-1, keepdims=True)
    i1 = jnp.min(jnp.where(logits == v1, lane, LANES), axis=-1, keepdims=True)
    rest = jnp.where(lane == i1, NEG, logits)
    v2 = jnp.max(rest, axis=-1, keepdims=True)
    i2 = jnp.min(jnp.where(rest == v2, lane, LANES), axis=-1, keepdims=True)
    e2 = jnp.exp(v2 - v1)
    w1 = 1.0 / (1.0 + e2)
    w2 = e2 / (1.0 + e2)
    assigned = jnp.where(lane == i1, 1.0, jnp.where(lane == i2, 1.0, 0.0))
    before = jnp.dot(tri_ref[...], assigned.astype(BF16), preferred_element_type=F32) + carry_ref[0:1, :]
    r1 = jnp.sum(jnp.where(lane == i1, before, 0.0), axis=-1, keepdims=True)
    r2 = jnp.sum(jnp.where(lane == i2, before, 0.0), axis=-1, keepdims=True)
    carry_ref[...] = carry_ref[...] + jnp.sum(assigned, axis=0, keepdims=True)
    cnt_ref[...] = carry_ref[...]
    fields = (i1.astype(F32), i2.astype(F32), w1, w2, r1, r2)
    meta = jnp.zeros((ROW, LANES), F32)
    for f, val in enumerate(fields):
        meta = jnp.where(lane == f, val, meta)
    meta_ref[...] = meta
    slots_ref[...] = meta.T[:SUBLANES, :]


def _route(xs, mod, layer, nw, pre_g, w_r, b_r):
    n_batch, ts, d = xs.shape
    nt = ts // ROW
    tile = pl.BlockSpec((None, ROW, d), lambda b, i: (b, i, 0))
    t_idx = jnp.arange(ROW)
    tri_strict = (t_idx[:, None] > t_idx[None, :]).astype(BF16)
    return pl.pallas_call(
        _route_kernel,
        grid=(n_batch, nt),
        in_specs=[
            tile, _mod_spec(d, layer, 3, nw, n_batch), _mod_spec(d, layer, 4, nw, n_batch),
            pl.BlockSpec((1, d), lambda b, i: (0, 0)),
            pl.BlockSpec((d, LANES), lambda b, i: (0, 0)),
            pl.BlockSpec((1, LANES), lambda b, i: (0, 0)),
            pl.BlockSpec((ROW, ROW), lambda b, i: (0, 0)),
        ],
        out_specs=[tile, pl.BlockSpec((None, ROW, LANES), lambda b, i: (b, i, 0)),
                   pl.BlockSpec((None, SUBLANES, ROW), lambda b, i: (b, 0, i)),
                   pl.BlockSpec((SUBLANES, LANES), lambda b, i: (0, 0))],
        out_shape=[jax.ShapeDtypeStruct((n_batch, ts, d), F32),
                   jax.ShapeDtypeStruct((n_batch, ts, LANES), F32),
                   jax.ShapeDtypeStruct((n_batch, SUBLANES, ts), F32),
                   jax.ShapeDtypeStruct((SUBLANES, LANES), F32)],
        scratch_shapes=[pltpu.VMEM((SUBLANES, LANES), F32)],
        compiler_params=_cparams(("arbitrary", "arbitrary")),
        name="route",
    )(xs, mod, mod, pre_g, w_r, b_r, tri_strict)


GROUPS = ROW // SUBLANES


def _start_rows(make_copy):
    for g in range(GROUPS):
        for j in range(SUBLANES):
            for slot in range(2):
                make_copy(g, j, slot).start()


def _wait_rows(make_copy):
    def drain(g, carry):
        for j in range(SUBLANES):
            for slot in range(2):
                make_copy(0, 0, slot).wait()
        return carry

    lax.fori_loop(0, GROUPS, drain, 0)


STAGES = 3


def _dispatch_kernel(pos_ref, bounds_ref, xn_ref, out_ref, zero_ref, stage_ref, row_sems, in_sems, zero_sem):
    step = pl.program_id(0)
    n_steps = pl.num_programs(0)
    n_tok = pos_ref.shape[0] // 2
    tm = zero_ref.shape[0]
    n_sorted = out_ref.shape[0]

    @pl.when(step == 0)
    def _():
        zero_ref[...] = jnp.zeros_like(zero_ref)

        def fill(row):
            copy = pltpu.make_async_copy(zero_ref, out_ref.at[pl.ds(pl.multiple_of(row, tm), tm)], zero_sem)
            copy.start()
            copy.wait()

        for e in range(N_EXPERTS):
            end = bounds_ref[N_EXPERTS + e]

            @pl.when(end > bounds_ref[e])
            def _():
                fill(end - tm)

        last_end = bounds_ref[2 * N_EXPERTS - 1]

        def tail(k, carry):
            fill(last_end + k * tm)
            return carry

        lax.fori_loop(0, (n_sorted - last_end) // tm, tail, 0)

    def tile_in(tile):
        slot = tile % STAGES
        return pltpu.make_async_copy(xn_ref.at[pl.ds(tile * GROUPS, GROUPS)], stage_ref.at[slot], in_sems.at[slot])

    def copies(tile):
        def make_copy(g, j, slot):
            p = pos_ref[slot * n_tok + tile * ROW + g * SUBLANES + j]
            return pltpu.make_async_copy(stage_ref.at[tile % STAGES, g, pl.ds(j, 1)], out_ref.at[pl.ds(p, 1)],
                                         row_sems.at[tile % STAGES])
        return make_copy

    @pl.when(step == 0)
    def _():
        tile_in(step).start()

    @pl.when(step >= STAGES - 1)
    def _():
        _wait_rows(copies(step - (STAGES - 1)))

    @pl.when(step + 1 < n_steps)
    def _():
        tile_in(step + 1).start()

    tile_in(step).wait()
    _start_rows(copies(step))

    @pl.when(step == n_steps - 1)
    def _():
        for back in range(STAGES - 2, -1, -1):
            @pl.when(step - back >= 0)
            def _():
                _wait_rows(copies(step - back))


def _dispatch(pos, bounds, xn, n_sorted, tm):
    m_rows, d = xn.shape
    return pl.pallas_call(
        _dispatch_kernel,
        grid_spec=pltpu.PrefetchScalarGridSpec(
            num_scalar_prefetch=2,
            grid=(m_rows // ROW,),
            in_specs=[pl.BlockSpec(memory_space=pl.ANY)],
            out_specs=pl.BlockSpec(memory_space=pl.ANY),
            scratch_shapes=[pltpu.VMEM((tm, d), F32), pltpu.VMEM((STAGES, GROUPS, SUBLANES, d), F32),
                            pltpu.SemaphoreType.DMA((STAGES,)), pltpu.SemaphoreType.DMA((STAGES,)),
                            pltpu.SemaphoreType.DMA(())],
        ),
        out_shape=jax.ShapeDtypeStruct((n_sorted, d), F32),
        compiler_params=_cparams(("arbitrary",)),
        name="dispatch",
    )(pos, bounds, xn.reshape(m_rows // SUBLANES, SUBLANES, d))


def _experts_kernel(te_ref, x_ref, wg_ref, wu_ref, wd_ref, y_ref):
    used = te_ref[pl.program_id(0)] < N_EXPERTS

    @pl.when(used)
    def _():
        x = x_ref[...].astype(BF16)
        gate = jnp.dot(x, wg_ref[...], preferred_element_type=F32)
        up = jnp.dot(x, wu_ref[...], preferred_element_type=F32)
        y_ref[...] = jnp.dot((_silu(gate) * up).astype(BF16), wd_ref[...], preferred_element_type=F32)

    @pl.when(jnp.logical_not(used))
    def _():
        y_ref[...] = jnp.zeros_like(y_ref)


def _experts(tile_expert, x_sorted, wg, wu, wd, tm):
    n_sorted, d = x_sorted.shape
    n_e, _, dff = wg.shape

    def weight(shape):
        return pl.BlockSpec((None,) + shape, lambda i, te: (jnp.minimum(te[i], n_e - 1), 0, 0))

    return pl.pallas_call(
        _experts_kernel,
        grid_spec=pltpu.PrefetchScalarGridSpec(
            num_scalar_prefetch=1,
            grid=(n_sorted // tm,),
            in_specs=[pl.BlockSpec((tm, d), lambda i, te: (i, 0)),
                      weight((d, dff)), weight((d, dff)), weight((dff, d))],
            out_specs=pl.BlockSpec((tm, d), lambda i, te: (i, 0)),
        ),
        out_shape=jax.ShapeDtypeStruct((n_sorted, d), F32),
        compiler_params=_cparams(("arbitrary",)),
        name="experts",
    )(tile_expert, x_sorted, wg, wu, wd)


def _combine_kernel(pos_ref, x_ref, meta_ref, g2_ref, post_ref, y_ref, o_ref, buf_ref, sems, *,
                    tiles_per_sample, latent_only):
    step = pl.program_id(0)
    n_steps = pl.num_programs(0)
    n_tok = pos_ref.shape[0] // 2
    d_model = x_ref.shape[-1]

    def wanted(tile):
        return (tile % tiles_per_sample != 0) if latent_only else (tile >= 0)

    def copies(tile):
        def make_copy(g, j, slot):
            p = pos_ref[slot * n_tok + tile * ROW + g * SUBLANES + j]
            return pltpu.make_async_copy(y_ref.at[pl.ds(p, 1)], buf_ref.at[tile % 2, slot, g, pl.ds(j, 1)],
                                         sems.at[tile % 2])
        return make_copy

    @pl.when(jnp.logical_and(step == 0, wanted(step)))
    def _():
        _start_rows(copies(step))

    @pl.when(jnp.logical_and(step + 1 < n_steps, wanted(step + 1)))
    def _():
        _start_rows(copies(step + 1))

    @pl.when(wanted(step))
    def _():
        _wait_rows(copies(step))
        meta = meta_ref[...]
        cur = step % 2
        y1 = buf_ref[cur, 0].reshape(ROW, d_model)
        y2 = buf_ref[cur, 1].reshape(ROW, d_model)
        z = meta[:, 2:3] * y1 + meta[:, 3:4] * y2
        o_ref[...] = x_ref[...] + g2_ref[...] * _rms(z, post_ref[...])


def _combine(pos, xs, meta, y_sorted, mod, layer, nw, post_g, latent_only):
    n_batch, ts, d = xs.shape
    nt = ts // ROW
    m_rows = n_batch * ts
    base = (layer * 6 + 5) * nw
    tile = pl.BlockSpec((ROW, d), lambda i, pos: (i, 0))
    if latent_only:
        out_rows = n_batch * (ts - ROW)
        out_tile = pl.BlockSpec((ROW, d), lambda i, pos: ((i // nt) * (nt - 1) + jnp.maximum(i % nt - 1, 0), 0))
    else:
        out_rows, out_tile = m_rows, tile
    out = pl.pallas_call(
        functools.partial(_combine_kernel, tiles_per_sample=nt, latent_only=latent_only),
        grid_spec=pltpu.PrefetchScalarGridSpec(
            num_scalar_prefetch=1,
            grid=(m_rows // ROW,),
            in_specs=[
                tile,
                pl.BlockSpec((ROW, LANES), lambda i, pos: (i, 0)),
                pl.BlockSpec((None, 1, d), lambda i, pos: (base + jnp.where(i % nt == 0, n_batch, i // nt), 0, 0)),
                pl.BlockSpec((1, d), lambda i, pos: (0, 0)),
                pl.BlockSpec(memory_space=pl.ANY),
            ],
            out_specs=out_tile,
            scratch_shapes=[pltpu.VMEM((2, 2, GROUPS, SUBLANES, d), F32), pltpu.SemaphoreType.DMA((2,))],
        ),
        out_shape=jax.ShapeDtypeStruct((out_rows, d), F32),
        compiler_params=_cparams(("arbitrary",)),
        name="combine",
    )(pos, xs.reshape(m_rows, d), meta.reshape(m_rows, LANES), mod, post_g, y_sorted)
    return out.reshape(n_batch, out_rows // n_batch, d)


def _moe(xs, mod, layer, nw, pre_g, post_g, w_r, b_r, wg, wu, wd, latent_only, tm=512):
    n_batch, ts, d = xs.shape
    m_rows = n_batch * ts
    xn, meta, slots, cnt = _route(xs, mod, layer, nw, pre_g, w_r, b_r)
    i1, i2, r1, r2 = (slots[:, f, :].reshape(m_rows).astype(jnp.int32) for f in (0, 1, 4, 5))
    counts = cnt[0, :N_EXPERTS].astype(jnp.int32)
    padded = -(-counts // tm) * tm
    ends = jnp.cumsum(padded)
    start = ends - padded
    pos = jnp.concatenate([start[i1] + r1, start[i2] + r2])
    bounds = jnp.concatenate([start, ends])
    n_tiles = 2 * m_rows // tm + N_EXPERTS
    tile_row = jnp.arange(n_tiles, dtype=jnp.int32) * tm
    tile_expert = jnp.sum((ends[None, :] <= tile_row[:, None]).astype(jnp.int32), axis=1)
    x_sorted = _dispatch(pos, bounds, xn.reshape(m_rows, d), n_tiles * tm, tm)
    y_sorted = _experts(tile_expert, x_sorted, wg, wu, wd, tm)
    return _combine(pos, xs, meta, y_sorted, mod, layer, nw, post_g, latent_only)


def _rope_tables(n_tok, n_ctx):
    f32 = np.float32
    n_freq = HEAD_DIM // 4
    pos = np.arange(n_tok)
    row = (pos // GRID_W).astype(f32)
    colp = (pos % GRID_W).astype(f32)
    inv = (f32(ROPE_THETA) ** (-np.arange(n_freq, dtype=f32) / f32(n_freq))).astype(f32)
    lane = np.arange(LANES)
    in_head = lane % HEAD_DIM
    use_col = (in_head // (HEAD_DIM // 2)) == 1
    freq = inv[in_head % n_freq]
    ang = (np.where(use_col[None, :], colp[:, None], row[:, None]) * freq[None, :]).astype(f32)
    lower = (in_head % (HEAD_DIM // 2)) < n_freq
    cos_t = np.cos(ang).astype(f32)
    sin_t = np.sin(ang).astype(f32)
    sin_a = np.where(lower[None, :], -sin_t, f32(0.0)).astype(f32)
    sin_b = np.where(lower[None, :], f32(0.0), sin_t).astype(f32)
    pad = lambda t, v: np.concatenate([np.full((n_ctx, LANES), v, f32), t], axis=0)
    return jnp.asarray(pad(cos_t, 1.0)), jnp.asarray(pad(sin_a, 0.0)), jnp.asarray(pad(sin_b, 0.0))


def _block_diag_mean(width):
    idx = jnp.arange(width) // HEAD_DIM
    return jnp.where(idx[:, None] == idx[None, :], 1.0 / HEAD_DIM, 0.0).astype(BF16)


def _pack_w_in(w):
    a_q = A_HEADS * HEAD_DIM
    a_kv = 2 * A_KV_HEADS * HEAD_DIM
    n_gates = 4 * C_HEADS
    gate_start = w.shape[1] - 3 * w.shape[0]
    g_start = gate_start - n_gates
    main = jnp.concatenate([w[:, gate_start:], w[:, :a_q], w[:, a_q + a_kv:g_start], w[:, a_q:a_q + a_kv]], axis=1)
    gates = jnp.pad(w[:, g_start:gate_start], ((0, 0), (0, LANES - n_gates)))
    return main.astype(BF16), gates.astype(BF16)


def kernel(x, c, ctx, c_ctx, ada_w, ada_b, pre_mix_g, post_mix_g, pre_ffn_g, post_ffn_g, w_in, q_norm_g, k_norm_g, lam_q1, lam_k1, lam_q2, lam_k2, diff_norm_g, conv_w, conv_b, mlstm_gate_b, mlstm_norm_g, w_br_attn, w_br_diff, w_br_mlstm, w_out, w_ff_gate, w_ff_up, w_ff_down, w_router, b_router, w_moe_gate, w_moe_up, w_moe_down):
    n_batch, n_tok, d = x.shape
    n_ctx = ctx.shape[1]
    depth = ada_w.shape[0]
    assert n_ctx == ROW and n_tok % ROW == 0 and d == 1024
    ts = n_ctx + n_tok
    nw = -(-(n_batch + 1) // SUBLANES) * SUBLANES

    c_all = jnp.concatenate([c, c_ctx[None, :], jnp.zeros((nw - n_batch - 1, d), F32)], axis=0)
    mod = _modulation(c_all, ada_w, ada_b)
    tables = _rope_tables(n_tok, n_ctx)
    bd4, bd1 = _block_diag_mean(A_HEADS * HEAD_DIM), _block_diag_mean(LANES)
    t_idx = jnp.arange(MCHUNK)
    tri = (t_idx[:, None] >= t_idx[None, :]).astype(BF16)

    xs = (ctx, x)
    for l in range(depth):
        lam_init = 0.8 - 0.6 * math.exp(-0.3 * l)
        w_main, w_gate = _pack_w_in(w_in[l])
        qg = jnp.tile(q_norm_g[l], A_HEADS)[None, :]
        kg = jnp.tile(k_norm_g[l], A_KV_HEADS)[None, :]
        gate_b = jnp.pad(mlstm_gate_b[l], (0, LANES - 4 * C_HEADS))[None, :]
        merge_gate, out_gate, qaz, ka, va, qbz, kb, vb, qmt, km, vmt, gcol, grow = _inproj(
            xs, mod, l, nw, pre_mix_g[l][None, :], w_main, w_gate,
            tables, qg, kg, bd4, bd1, conv_w[l], conv_b[l][None, :], gate_b, tri)
        a_out = _gqa(qaz, ka, va)
        lam_vecs = jnp.stack([lam_q1[l], lam_k1[l], lam_q2[l], lam_k2[l]], axis=0)
        d_out = _diff(qbz, kb, vb, lam_vecs, diff_norm_g[l][None, :], lam_init)
        hf, hb = _mlstm(qmt, km, vmt, gcol, grow)
        xs = _mix(a_out, d_out, hf, hb, out_gate, merge_gate, xs, mod, l, nw,
                  post_mix_g[l][None, :], mlstm_norm_g[l][None, :],
                  w_br_attn[l].astype(BF16), w_br_diff[l].astype(BF16), w_br_mlstm[l].astype(BF16),
                  w_out[l].astype(BF16))
        j = l // 2
        if l % 2 == 0:
            xs = _ffn(xs, mod, l, nw, pre_ffn_g[l][None, :], post_ffn_g[l][None, :],
                      w_ff_gate[j].astype(BF16), w_ff_up[j].astype(BF16), w_ff_down[j].astype(BF16))
        else:
            w_r = jnp.pad(w_router[j], ((0, 0), (0, LANES - N_EXPERTS)))
            b_r = jnp.pad(b_router[j], (0, LANES - N_EXPERTS))[None, :]
            xs = _moe(xs, mod, l, nw, pre_ffn_g[l][None, :], post_ffn_g[l][None, :], w_r, b_r,
                      w_moe_gate[j].astype(BF16), w_moe_up[j].astype(BF16), w_moe_down[j].astype(BF16),
                      latent_only=l == depth - 1)
    return xs if xs.shape[1] == n_tok else xs[:, n_ctx:, :]
```

```python
import functools
import math

import jax
import jax.numpy as jnp
import numpy as np
from jax import lax
from jax.experimental import pallas as pl
from jax.experimental.pallas import tpu as pltpu

F32 = jnp.float32
BF16 = jnp.bfloat16
HIGHEST = lax.Precision.HIGHEST

EPS = 1e-6
HEAD_DIM = 64
A_HEADS = 8
A_KV_HEADS = 2
B_HEADS = 4
C_HEADS = 4
C_DIM = 128
N_EXPERTS = 8
ROPE_THETA = 10000.0
GRID_W = 64
CONV_W = 3

LANES = 128
SUBLANES = 8
ROW = 256
MCHUNK = 256
LOG2E = math.log2(math.e)
NEG = -1e30
V7X_VMEM_BYTES = 64 * 1024 * 1024
VMEM_LIMIT = V7X_VMEM_BYTES - 8 * 1024 * 1024

REL_QA, REL_KB, REL_QC, REL_VC, REL_KA = 0, 1024, 2048, 3072, 4096


def _cparams(sem):
    return pltpu.CompilerParams(dimension_semantics=sem, vmem_limit_bytes=VMEM_LIMIT)


def _rms(x, g):
    y = x * lax.rsqrt(jnp.mean(x * x, axis=-1, keepdims=True) + EPS)
    return y * g


def _sigmoid(x):
    return 1.0 / (1.0 + jnp.exp(-x))


def _silu(x):
    return x * _sigmoid(x)


def _log_sigmoid(x):
    return jnp.minimum(x, 0.0) - jnp.log(1.0 + jnp.exp(-jnp.abs(x)))


def _lane_iota(shape):
    return lax.broadcasted_iota(jnp.int32, shape, len(shape) - 1)


def _row_iota(shape):
    return lax.broadcasted_iota(jnp.int32, shape, len(shape) - 2)


def _mod_kernel(c_ref, w_ref, b_ref, o_ref):
    c = c_ref[...]
    o_ref[...] = jnp.dot(_silu(c), w_ref[...], preferred_element_type=F32, precision=HIGHEST) + b_ref[...]


def _modulation(c_all, ada_w, ada_b):
    depth, d, _ = ada_w.shape
    nw = c_all.shape[0]
    out = pl.pallas_call(
        _mod_kernel,
        grid=(depth, 6),
        in_specs=[
            pl.BlockSpec((nw, d), lambda l, j: (0, 0)),
            pl.BlockSpec((None, d, d), lambda l, j: (l, 0, j)),
            pl.BlockSpec((None, 1, d), lambda l, j: (l, 0, j)),
        ],
        out_specs=pl.BlockSpec((None, None, nw, d), lambda l, j: (l, j, 0, 0)),
        out_shape=jax.ShapeDtypeStruct((depth, 6, nw, d), F32),
        compiler_params=_cparams(("arbitrary", "arbitrary")),
        name="modulation",
    )(c_all, ada_w, ada_b.reshape(depth, 1, 6 * d))
    return out.reshape(depth * 6 * nw, 1, d)


def _mod_spec(d, layer, chunk, nw, n_batch):
    base = (layer * 6 + chunk) * nw
    return pl.BlockSpec((None, 1, d), lambda b, i: (base + jnp.where(i == 0, n_batch, b), 0, 0))


def _stream(stream):
    if isinstance(stream, tuple):
        ctx, x = stream
        n_batch, n_tok, d = x.shape
        specs = [pl.BlockSpec((None, ROW, d), lambda b, i: (b, 0, 0)),
                 pl.BlockSpec((None, ROW, d), lambda b, i: (b, jnp.maximum(i - 1, 0), 0))]
        return (n_batch, ctx.shape[1] + n_tok, d), specs, [ctx, x]
    d = stream.shape[-1]
    return stream.shape, [pl.BlockSpec((None, ROW, d), lambda b, i: (b, i, 0))], [stream]


def _stream_tile(refs):
    if len(refs) == 2:
        return jnp.where(pl.program_id(1) == 0, refs[0][...], refs[1][...])
    return refs[0][...]


def _inproj_kernel(*refs, n_stream):
    (xprev_ref, xnext_ref, sh_ref, sc_ref, g_ref, w_ref, wg_ref,
     cos_ref, sa_ref, sb_ref, qg_ref, kg_ref, bd4_ref, bd1_ref, cw_ref, cb_ref, gb_ref, tri_ref,
     gate_ref, oc_ref, *mixer_refs) = refs[n_stream:]
    d_model = xprev_ref.shape[-1]

    def normed(x):
        return _rms(x, g_ref[...]) * (1.0 + sc_ref[...]) + sh_ref[...]

    xn = normed(_stream_tile(refs[:n_stream]))
    xb = xn.astype(BF16)
    ext = jnp.concatenate([normed(xprev_ref[...]), xn, normed(xnext_ref[...])], axis=0).astype(BF16)

    def proj(lhs, start, width):
        return jnp.dot(lhs, w_ref[:, start:start + width], preferred_element_type=F32)

    base = 3 * d_model
    half = C_HEADS * C_DIM
    qab = proj(xb, base + REL_QA, 2 * half)
    kvb = proj(xb, base + REL_KB, 2 * half)
    qkc = proj(ext, base + REL_QC, 2 * half)
    vo = proj(xb, base + REL_VC, 2 * half)
    oc_ref[...] = vo[:, half:].astype(BF16)
    kava = proj(xb, base + REL_KA, 2 * LANES)
    gates = jnp.dot(xb, wg_ref[...], preferred_element_type=F32)
    _prep_math(pl.program_id(1), pl.num_programs(1),
               qab[:, :half], kava, qab[:, half:], kvb[:, :half], kvb[:, half:],
               qkc[SUBLANES:SUBLANES + ROW], qkc[SUBLANES - 1:SUBLANES], qkc[SUBLANES + ROW:SUBLANES + ROW + 1],
               vo[:, :half], gates,
               cos_ref, sa_ref, sb_ref, qg_ref, kg_ref, bd4_ref, bd1_ref, cw_ref, cb_ref, gb_ref, tri_ref,
               *mixer_refs)
    for c in range(6):
        gate_ref[:, c * half:(c + 1) * half] = proj(xb, c * half, half).astype(BF16)


def _inproj(xs, mod, layer, nw, pre_g, w_main, w_gate, tables, qg, kg, bd4, bd1, conv_w, conv_b, gate_b, tri):
    (n_batch, ts, d), stream_specs, stream_arrays = _stream(xs)
    halo_src = stream_arrays[-1]
    tile_shift = len(stream_arrays) - 1
    n_main = w_main.shape[1]
    nt = ts // ROW
    cos_t, sa_t, sb_t = tables
    row_blocks = halo_src.shape[1] // SUBLANES
    per_tile = ROW // SUBLANES
    half = C_HEADS * C_DIM

    def const(shape):
        return pl.BlockSpec(shape, lambda b, i: (0,) * len(shape))

    def rows(width):
        return pl.BlockSpec((None, ROW, width), lambda b, i: (b, i, 0))

    def heads(n, width):
        return pl.BlockSpec((None, n, ROW, width), lambda b, i: (b, 0, i, 0))

    def cols(height):
        return pl.BlockSpec((None, height, ROW), lambda b, i: (b, 0, i))

    table = pl.BlockSpec((ROW, LANES), lambda b, i: (i, 0))
    in_specs = stream_specs + [
        pl.BlockSpec((None, SUBLANES, d), lambda b, i: (b, jnp.maximum((i - tile_shift) * per_tile - 1, 0), 0)),
        pl.BlockSpec((None, SUBLANES, d),
                     lambda b, i: (b, jnp.clip((i + 1 - tile_shift) * per_tile, 0, row_blocks - 1), 0)),
        _mod_spec(d, layer, 0, nw, n_batch),
        _mod_spec(d, layer, 1, nw, n_batch),
        const((1, d)),
        pl.BlockSpec((d, n_main), lambda b, i: (0, 0), pipeline_mode=pl.Buffered(1)),
        const((d, LANES)),
        table, table, table,
        const((1, half)), const((1, LANES)), const((half, half)), const((LANES, LANES)),
        const((CONV_W, 2 * half)), const((1, 2 * half)), const((1, LANES)), const((MCHUNK, MCHUNK)),
    ]
    outs = [
        (rows(3 * d), (ts, 3 * d), BF16),
        (rows(half), (ts, half), BF16),
        (heads(A_HEADS, LANES), (A_HEADS, ts, LANES), BF16),
        (rows(LANES), (ts, LANES), BF16),
        (heads(A_KV_HEADS, LANES), (A_KV_HEADS, ts, LANES), BF16),
        (heads(2 * B_HEADS, LANES), (2 * B_HEADS, ts, LANES), BF16),
        (heads(B_HEADS, LANES), (B_HEADS, ts, LANES), BF16),
        (heads(B_HEADS, 2 * LANES), (B_HEADS, ts, 2 * LANES), BF16),
        (cols(half), (half, ts), BF16),
        (rows(half), (ts, half), BF16),
        (cols(half), (half, ts), BF16),
        (rows(3 * LANES), (ts, 3 * LANES), F32),
        (cols(6 * SUBLANES), (6 * SUBLANES, ts), F32),
    ]
    return pl.pallas_call(
        functools.partial(_inproj_kernel, n_stream=len(stream_arrays)),
        grid=(n_batch, nt),
        in_specs=in_specs,
        out_specs=[spec for spec, _, _ in outs],
        out_shape=[jax.ShapeDtypeStruct((n_batch,) + shape, dtype) for _, shape, dtype in outs],
        compiler_params=_cparams(("parallel", "parallel")),
        name="inproj",
    )(*stream_arrays, halo_src, halo_src, mod, mod, pre_g, w_main, w_gate, cos_t, sa_t, sb_t, qg, kg, bd4, bd1, conv_w, conv_b, gate_b, tri)


def _head_mean_sq(x, bd_ref):
    return jnp.dot((x * x).astype(BF16), bd_ref[...], preferred_element_type=F32)


def _prep_math(i, nt, qa, kava, qb, kb, vb, cur, prev_row, next_row, vc, gates,
               cos_ref, sa_ref, sb_ref, qg_ref, kg_ref, bd4_ref, bd1_ref, cw_ref, cb_ref, gb_ref, tri_ref,
               qaz_ref, ka_ref, va_ref, qbz_ref, kbo_ref, vbo_ref, qmt_ref, km_ref, vmt_ref, gcol_ref, grow_ref):
    cos, sin_a, sin_b = cos_ref[...], sa_ref[...], sb_ref[...]

    def rope(x):
        width = x.shape[1]
        reps = width // LANES
        c = jnp.concatenate([cos] * reps, axis=1) if reps > 1 else cos
        a = jnp.concatenate([sin_a] * reps, axis=1) if reps > 1 else sin_a
        b = jnp.concatenate([sin_b] * reps, axis=1) if reps > 1 else sin_b
        return x * c + pltpu.roll(x, width - 16, 1) * a + pltpu.roll(x, 16, 1) * b

    lane = _lane_iota((ROW, LANES))
    ones = jnp.ones((ROW, LANES), BF16)
    scale = HEAD_DIM ** -0.5 * LOG2E

    qa = qa * lax.rsqrt(_head_mean_sq(qa, bd4_ref) + EPS) * qg_ref[...]
    qa = rope(qa) * scale
    heads_per_kv = A_HEADS // A_KV_HEADS
    for h in range(A_HEADS):
        g = h // heads_per_kv
        blk = qa[:, (h // 2) * LANES:(h // 2 + 1) * LANES]
        if h % 2 != g:
            blk = pltpu.roll(blk, HEAD_DIM, 1)
        qaz_ref[h] = jnp.where(lane // HEAD_DIM == g, blk, 0.0).astype(BF16)
    ka = kava[:, :LANES]
    ka = ka * lax.rsqrt(_head_mean_sq(ka, bd1_ref) + EPS) * kg_ref[...]
    ka_ref[...] = rope(ka).astype(BF16)
    va = kava[:, LANES:].astype(BF16)
    for g in range(A_KV_HEADS):
        va_ref[g] = jnp.where(lane // HEAD_DIM == g, va, ones)

    qb = rope(qb) * scale
    kb = rope(kb)
    for h in range(B_HEADS):
        blk = qb[:, h * LANES:(h + 1) * LANES]
        for m in range(2):
            qbz_ref[2 * h + m] = jnp.where(lane // HEAD_DIM == m, blk, 0.0).astype(BF16)
        kbo_ref[h] = kb[:, h * LANES:(h + 1) * LANES].astype(BF16)
        vbo_ref[h, :, :LANES] = vb[:, h * LANES:(h + 1) * LANES].astype(BF16)
        vbo_ref[h, :, LANES:] = ones

    row = _row_iota(cur.shape)
    prev_row = jnp.where(i >= 2, prev_row, 0.0)
    next_row = jnp.where(jnp.logical_and(i >= 1, i < nt - 1), next_row, 0.0)
    up = jnp.where(row == 0, prev_row, pltpu.roll(cur, 1, 0))
    dn = jnp.where(row == ROW - 1, next_row, pltpu.roll(cur, ROW - 1, 0))
    y = up * cw_ref[0:1, :] + cur * cw_ref[1:2, :] + dn * cw_ref[2:3, :] + cb_ref[...]
    y = _silu(y)
    half = C_HEADS * C_DIM
    qmt_ref[...] = y[:, :half].T.astype(BF16)
    km_ref[...] = (y[:, half:] * (C_DIM ** -0.5)).astype(BF16)
    vmt_ref[...] = vc.T.astype(BF16)

    gg = gates + gb_ref[...]
    is_forget = (lane // C_HEADS) % 2 == 1
    gl = jnp.where(is_forget, _log_sigmoid(gg), gg) * LOG2E
    tri = tri_ref[...]
    n_rows = 2 * SUBLANES
    for c in range(ROW // MCHUNK):
        rows = slice(c * MCHUNK, (c + 1) * MCHUNK)
        glc = gl[rows]
        hi = glc.astype(BF16)
        rest = glc - hi.astype(F32)
        mid = rest.astype(BF16)
        low = (rest - mid.astype(F32)).astype(BF16)
        cs = (jnp.dot(tri, hi, preferred_element_type=F32) + jnp.dot(tri, mid, preferred_element_type=F32)
              + jnp.dot(tri, low, preferred_element_type=F32))
        tot = jnp.broadcast_to(jnp.sum(glc, axis=0, keepdims=True), glc.shape)
        for f, val in enumerate((glc, cs, tot)):
            gcol_ref[rows, f * LANES:(f + 1) * LANES] = val
            grow_ref[f * n_rows:(f + 1) * n_rows, rows] = val.T[:n_rows, :]


def _attend_blocks(blocks, n_keys):
    def scores(q, k_ref):
        return lax.dot_general(q, k_ref[:n_keys, :], (((1,), (1,)), ((), ())), preferred_element_type=F32)

    def weighted(s, v_ref):
        p = jnp.exp2(s - jnp.max(s, axis=-1, keepdims=True)).astype(BF16)
        return jnp.dot(p, v_ref[:n_keys, :], preferred_element_type=F32)

    outs = []
    s_cur = scores(blocks[0][0], blocks[0][1])
    for j in range(1, len(blocks)):
        s_next = scores(blocks[j][0], blocks[j][1])
        outs.append(weighted(s_cur, blocks[j - 1][2]))
        s_cur = s_next
    outs.append(weighted(s_cur, blocks[-1][2]))
    return outs


def _per_tile_keys(body, n_all):
    @pl.when(pl.program_id(1) == 0)
    def _():
        body(ROW)

    @pl.when(pl.program_id(1) > 0)
    def _():
        body(n_all)


def _gqa_kernel(q_ref, k_ref, v_ref, o_ref):
    heads_per_kv = A_HEADS // A_KV_HEADS
    lane = _lane_iota((ROW, LANES))

    def body(n_keys):
        blocks = []
        for j in range(A_HEADS // 2):
            q = q_ref[2 * j:2 * j + 2].reshape(2 * ROW, LANES)
            blocks.append((q, k_ref, v_ref.at[(2 * j) // heads_per_kv]))
        for j, o in enumerate(_attend_blocks(blocks, n_keys)):
            g = (2 * j) // heads_per_kv
            den_lane = (1 - g) * HEAD_DIM
            o = o / o[:, den_lane:den_lane + 1]
            even, odd = o[:ROW], o[ROW:]
            even = even if g == 0 else pltpu.roll(even, HEAD_DIM, 1)
            odd = odd if g == 1 else pltpu.roll(odd, HEAD_DIM, 1)
            o_ref[:, j * LANES:(j + 1) * LANES] = jnp.where(lane < HEAD_DIM, even, odd).astype(BF16)

    _per_tile_keys(body, k_ref.shape[0])


def _diff_kernel(q_ref, k_ref, v_ref, lam_ref, g_ref, o_ref, *, lam_init):
    lv = lam_ref[...]
    lam = (jnp.exp(jnp.sum(lv[0:1] * lv[1:2], axis=-1, keepdims=True))
           - jnp.exp(jnp.sum(lv[2:3] * lv[3:4], axis=-1, keepdims=True)) + lam_init)

    def body(n_keys):
        blocks = [(q_ref[2 * h:2 * h + 2].reshape(2 * ROW, LANES), k_ref.at[h], v_ref.at[h]) for h in range(B_HEADS)]
        for h, o in enumerate(_attend_blocks(blocks, n_keys)):
            o = o[:, :LANES] / o[:, LANES:LANES + 1]
            dif = o[:ROW] - lam * o[ROW:]
            o_ref[:, h * LANES:(h + 1) * LANES] = (_rms(dif, g_ref[...]) * (1.0 - lam_init)).astype(BF16)

    _per_tile_keys(body, k_ref.shape[1])


def _gqa(qaz, ka, va):
    n_batch, _, ts, _ = qaz.shape
    nt = ts // ROW
    return pl.pallas_call(
        _gqa_kernel,
        grid=(n_batch, nt),
        in_specs=[
            pl.BlockSpec((None, A_HEADS, ROW, LANES), lambda b, i: (b, 0, i, 0)),
            pl.BlockSpec((None, ts, LANES), lambda b, i: (b, 0, 0)),
            pl.BlockSpec((None, A_KV_HEADS, ts, LANES), lambda b, i: (b, 0, 0, 0)),
        ],
        out_specs=pl.BlockSpec((None, ROW, A_HEADS * HEAD_DIM), lambda b, i: (b, i, 0)),
        out_shape=jax.ShapeDtypeStruct((n_batch, ts, A_HEADS * HEAD_DIM), BF16),
        compiler_params=_cparams(("parallel", "parallel")),
        name="gqa_attention",
    )(qaz, ka, va)


def _diff(qbz, kb, vb, lam_vecs, sub_g, lam_init):
    n_batch, _, ts, _ = qbz.shape
    nt = ts // ROW
    return pl.pallas_call(
        functools.partial(_diff_kernel, lam_init=lam_init),
        grid=(n_batch, nt),
        in_specs=[
            pl.BlockSpec((None, 2 * B_HEADS, ROW, LANES), lambda b, i: (b, 0, i, 0)),
            pl.BlockSpec((None, B_HEADS, ts, LANES), lambda b, i: (b, 0, 0, 0)),
            pl.BlockSpec((None, B_HEADS, ts, 2 * LANES), lambda b, i: (b, 0, 0, 0)),
            pl.BlockSpec((4, HEAD_DIM), lambda b, i: (0, 0)),
            pl.BlockSpec((1, LANES), lambda b, i: (0, 0)),
        ],
        out_specs=pl.BlockSpec((None, ROW, B_HEADS * LANES), lambda b, i: (b, i, 0)),
        out_shape=jax.ShapeDtypeStruct((n_batch, ts, B_HEADS * LANES), BF16),
        compiler_params=_cparams(("parallel", "parallel")),
        name="diff_attention",
    )(qbz, kb, vb, lam_vecs, sub_g)


MBATCH = 4
AUG = 2 * SUBLANES


def _mlstm_kernel(qtf_ref, kf_ref, vtf_ref, gcf_ref, grf_ref, qtb_ref, kb_ref, vtb_ref, gcb_ref, grb_ref,
                  hf_ref, hb_ref, c_ref, m_ref):
    @pl.when(pl.program_id(1) == 0)
    def _():
        c_ref[...] = jnp.zeros_like(c_ref)
        m_ref[...] = jnp.zeros_like(m_ref)

    length = MCHUNK
    key_idx = _row_iota((length, length))
    qry_idx = _lane_iota((length, length))
    ones_rows = jnp.ones((AUG, length), BF16)
    n_rows = 2 * SUBLANES
    n_samples = qtf_ref.shape[0]
    n_chain = n_samples * 2 * C_HEADS
    c_states = [c_ref[ch] for ch in range(n_chain)]
    m_prevs = [m_ref[ch] for ch in range(n_chain)]
    c_news, m_news, pending = [], [], []

    lanes = ((qtf_ref, kf_ref, vtf_ref, gcf_ref, grf_ref), (qtb_ref, kb_ref, vtb_ref, gcb_ref, grb_ref))
    for sample, direction in [(sm, dr) for sm in range(n_samples) for dr in range(2)]:
        qt_ref, k_ref, vt_ref, gc_ref, gr_ref = (r.at[sample] for r in lanes[direction])
        reverse = direction == 1
        gate_c, cs_c, tot_c = gc_ref[:, :LANES], gc_ref[:, LANES:2 * LANES], gc_ref[:, 2 * LANES:]
        p_c = (tot_c - cs_c + gate_c) if reverse else cs_c
        g_c = gate_c - pltpu.roll(p_c, LANES - C_HEADS, 1)
        mask = (key_idx >= qry_idx) if reverse else (key_idx <= qry_idx)
        for hd in range(C_HEADS):
            chain = (sample * 2 + direction) * C_HEADS + hd
            ii = direction * 2 * C_HEADS + hd
            fi = ii + C_HEADS
            i_row = gr_ref[ii:ii + 1, :]
            f_row = gr_ref[fi:fi + 1, :]
            cs_row = gr_ref[n_rows + fi:n_rows + fi + 1, :]
            tot_row = gr_ref[2 * n_rows + fi:2 * n_rows + fi + 1, :]
            p_row = (tot_row - cs_row + f_row) if reverse else cs_row
            m_prev = m_prevs[chain]
            inter = p_row + m_prev
            log_dt = jnp.where(mask, jnp.broadcast_to(g_c[:, ii:ii + 1], (length, length)) + p_row, NEG)
            m_t = jnp.maximum(inter, jnp.max(log_dt, axis=0, keepdims=True))
            d_t = jnp.exp2(log_dt - m_t)
            a_row = jnp.exp2(inter - m_t)
            sl = slice(hd * C_DIM, (hd + 1) * C_DIM)
            k_h, qt_h = k_ref[:, sl], qt_ref[sl, :]
            vt_aug = jnp.concatenate([vt_ref[sl, :], ones_rows], axis=0)
            s_raw = jnp.dot(k_h, qt_h, preferred_element_type=F32)
            c_state = c_states[chain]
            x_t = jnp.dot(c_state.astype(BF16), qt_h, preferred_element_type=F32)
            w_row = tot_row - p_row + i_row
            m_new = jnp.maximum(tot_row + m_prev, jnp.max(w_row, axis=-1, keepdims=True))
            decay = jnp.exp2(tot_row + m_prev - m_new)
            ws = jnp.exp2(w_row - m_new)
            update = jnp.dot((vt_aug.astype(F32) * ws).astype(BF16), k_h, preferred_element_type=F32)
            c_news.append(decay[:, :C_DIM] * c_state + update)
            m_news.append(m_new)
            pending.append((s_raw, d_t, vt_aug, a_row, x_t, m_t))

    for sample, direction in [(sm, dr) for sm in range(n_samples) for dr in range(2)]:
        h_parts = []
        for hd in range(C_HEADS):
            s_raw, d_t, vt_aug, a_row, x_t, m_t = pending[(sample * 2 + direction) * C_HEADS + hd]
            y_t = jnp.dot(vt_aug, (s_raw * d_t).astype(BF16), preferred_element_type=F32)
            num_t = a_row * x_t[:C_DIM] + y_t[:C_DIM]
            den = a_row * x_t[C_DIM:C_DIM + 1] + y_t[C_DIM:C_DIM + 1]
            h_t = num_t / jnp.maximum(jnp.abs(den), jnp.exp2(-m_t))
            h_parts.append(h_t.T)
        (hb_ref if direction else hf_ref)[sample] = jnp.concatenate(h_parts, axis=1)

    for ch in range(n_chain):
        c_ref[ch] = c_news[ch]
        m_ref[ch] = m_news[ch]


def _mlstm(qmt, km, vmt, gcol, grow):
    n_batch, ts, width = km.shape
    nc = ts // MCHUNK
    ctx_chunks = ROW // MCHUNK
    mbatch = math.gcd(n_batch, MBATCH)

    def bwd(j):
        return jnp.where(j < ctx_chunks, ctx_chunks - 1 - j, nc + ctx_chunks - 1 - j)

    def specs(idx):
        return [
            pl.BlockSpec((mbatch, width, MCHUNK), lambda b, j: (b, 0, idx(j))),
            pl.BlockSpec((mbatch, MCHUNK, width), lambda b, j: (b, idx(j), 0)),
            pl.BlockSpec((mbatch, width, MCHUNK), lambda b, j: (b, 0, idx(j))),
            pl.BlockSpec((mbatch, MCHUNK, 3 * LANES), lambda b, j: (b, idx(j), 0)),
            pl.BlockSpec((mbatch, 6 * SUBLANES, MCHUNK), lambda b, j: (b, 0, idx(j))),
        ]

    fwd = lambda j: j
    n_chain = mbatch * 2 * C_HEADS
    return pl.pallas_call(
        _mlstm_kernel,
        grid=(n_batch // mbatch, nc),
        in_specs=specs(fwd) + specs(bwd),
        out_specs=[
            pl.BlockSpec((mbatch, MCHUNK, width), lambda b, j: (b, j, 0)),
            pl.BlockSpec((mbatch, MCHUNK, width), lambda b, j: (b, bwd(j), 0)),
        ],
        out_shape=[jax.ShapeDtypeStruct((n_batch, ts, width), F32)] * 2,
        scratch_shapes=[
            pltpu.VMEM((n_chain, C_DIM + AUG, C_DIM), F32),
            pltpu.VMEM((n_chain, 1, MCHUNK), F32),
        ],
        compiler_params=_cparams(("parallel", "arbitrary")),
        name="mlstm",
    )(qmt, km, vmt, gcol, grow, qmt, km, vmt, gcol, grow)


def _mix_kernel(*refs, n_stream):
    (a_ref, d_ref, hf_ref, hb_ref, oc_ref, gate_ref, g1_ref, pg_ref, mg_ref,
     wa_ref, wb_ref, wc_ref, wo_ref, o_ref) = refs[n_stream:]
    d_model = o_ref.shape[-1]
    hsum = hf_ref[...] + hb_ref[...]
    mg = mg_ref[...]
    m = jnp.concatenate([_rms(hsum[:, hd * C_DIM:(hd + 1) * C_DIM], mg) for hd in range(C_HEADS)], axis=1)
    m = m * _sigmoid(oc_ref[...].astype(F32))
    u = (_sigmoid(gate_ref[:, :d_model].astype(F32))
         * jnp.dot(a_ref[...], wa_ref[...], preferred_element_type=F32)
         + _sigmoid(gate_ref[:, d_model:2 * d_model].astype(F32))
         * jnp.dot(d_ref[...], wb_ref[...], preferred_element_type=F32)
         + _sigmoid(gate_ref[:, 2 * d_model:].astype(F32))
         * jnp.dot(m.astype(BF16), wc_ref[...], preferred_element_type=F32))
    y = jnp.dot(u.astype(BF16), wo_ref[...], preferred_element_type=F32)
    o_ref[...] = _stream_tile(refs[:n_stream]) + g1_ref[...] * _rms(y, pg_ref[...])


def _mix(a, dd, hf, hb, out_gate, merge_gate, xs, mod, layer, nw, post_g, mlstm_g, wa, wb, wc, wo):
    (n_batch, ts, d), stream_specs, stream_arrays = _stream(xs)
    nt = ts // ROW
    width = a.shape[-1]

    def tile(w):
        return pl.BlockSpec((None, ROW, w), lambda b, i: (b, i, 0))

    def const(shape):
        return pl.BlockSpec(shape, lambda b, i: (0,) * len(shape))

    return pl.pallas_call(
        functools.partial(_mix_kernel, n_stream=len(stream_arrays)),
        grid=(n_batch, nt),
        in_specs=stream_specs + [
            tile(width), tile(width), tile(width), tile(width),
            tile(width), tile(3 * d),
            _mod_spec(d, layer, 2, nw, n_batch),
            const((1, d)), const((1, C_DIM)),
            const((width, d)), const((width, d)), const((width, d)), const((d, d)),
        ],
        out_specs=tile(d),
        out_shape=jax.ShapeDtypeStruct((n_batch, ts, d), F32),
        compiler_params=_cparams(("parallel", "parallel")),
        name="mix_out",
    )(*stream_arrays, a, dd, hf, hb, out_gate, merge_gate, mod, post_g, mlstm_g, wa, wb, wc, wo)


def _flat_mod_spec(d, layer, chunk, nw, n_batch, nt, n_sub, sub):
    base = (layer * 6 + chunk) * nw

    def index(k):
        t = n_sub * k + sub
        return (base + jnp.where(t % nt == 0, n_batch, t // nt), 0, 0)
    return pl.BlockSpec((None, 1, d), index)


def _ffn_kernel(x_ref, *refs, n_sub):
    mods = refs[:3 * n_sub]
    pre_ref, post_ref, wg_ref, wu_ref, wd_ref, o_ref = refs[3 * n_sub:]
    tiles = [x_ref[s * ROW:(s + 1) * ROW] for s in range(n_sub)]
    xn = [_rms(x, pre_ref[...]) * (1.0 + mods[3 * s + 1][...]) + mods[3 * s][...] for s, x in enumerate(tiles)]
    xb = jnp.concatenate(xn, axis=0).astype(BF16)
    gate = jnp.dot(xb, wg_ref[...], preferred_element_type=F32)
    up = jnp.dot(xb, wu_ref[...], preferred_element_type=F32)
    z = jnp.dot((_silu(gate) * up).astype(BF16), wd_ref[...], preferred_element_type=F32)
    for s, x in enumerate(tiles):
        o_ref[s * ROW:(s + 1) * ROW] = x + mods[3 * s + 2][...] * _rms(z[s * ROW:(s + 1) * ROW], post_ref[...])


def _ffn(xs, mod, layer, nw, pre_g, post_g, wg, wu, wd):
    n_batch, ts, d = xs.shape
    nt = ts // ROW
    n_tiles = n_batch * nt
    n_sub = 2 if n_tiles % 2 == 0 else 1
    dff = wg.shape[1]

    def resident(shape):
        return pl.BlockSpec(shape, lambda k: (0,) * len(shape), pipeline_mode=pl.Buffered(1))

    tile = pl.BlockSpec((n_sub * ROW, d), lambda k: (k, 0))
    mod_specs = [_flat_mod_spec(d, layer, chunk, nw, n_batch, nt, n_sub, sub)
                 for sub in range(n_sub) for chunk in (3, 4, 5)]
    out = pl.pallas_call(
        functools.partial(_ffn_kernel, n_sub=n_sub),
        grid=(n_tiles // n_sub,),
        in_specs=[tile] + mod_specs + [
            pl.BlockSpec((1, d), lambda k: (0, 0)), pl.BlockSpec((1, d), lambda k: (0, 0)),
            resident((d, dff)), resident((d, dff)), resident((dff, d)),
        ],
        out_specs=tile,
        out_shape=jax.ShapeDtypeStruct((n_batch * ts, d), F32),
        compiler_params=_cparams(("parallel",)),
        name="ffn",
    )(xs.reshape(n_batch * ts, d), *([mod] * (3 * n_sub)), pre_g, post_g, wg, wu, wd)
    return out.reshape(n_batch, ts, d)


def _route_kernel(x_ref, *refs, n_sub):
    mods = refs[:2 * n_sub]
    pre_ref, wr_ref, br_ref, tri_ref, xn_ref, meta_ref, slots_ref, cnt_ref, carry_ref = refs[2 * n_sub:]

    @pl.when(pl.program_id(0) == 0)
    def _():
        carry_ref[...] = jnp.zeros_like(carry_ref)

    lane = _lane_iota((ROW, LANES))
    routed = []
    for s in range(n_sub):
        rows = slice(s * ROW, (s + 1) * ROW)
        xn = _rms(x_ref[rows], pre_ref[...]) * (1.0 + mods[2 * s + 1][...]) + mods[2 * s][...]
        xn_ref[rows] = xn
        logits = jnp.dot(xn, wr_ref[...], preferred_element_type=F32, precision=HIGHEST) + br_ref[...]
        logits = jnp.where(lane < N_EXPERTS, logits, NEG)
        v1 = jnp.max(logits, axis=-1, keepdims=True)
        i1 = jnp.min(jnp.where(logits == v1, lane, LANES), axis=-1, keepdims=True)
        rest = jnp.where(lane == i1, NEG, logits)
        v2 = jnp.max(rest, axis=-1, keepdims=True)
        i2 = jnp.min(jnp.where(rest == v2, lane, LANES), axis=-1, keepdims=True)
        e2 = jnp.exp(v2 - v1)
        w1 = 1.0 / (1.0 + e2)
        w2 = e2 / (1.0 + e2)
        assigned = jnp.where(lane == i1, 1.0, jnp.where(lane == i2, 1.0, 0.0))
        local = jnp.dot(tri_ref[...], assigned.astype(BF16), preferred_element_type=F32)
        routed.append((i1, i2, w1, w2, assigned, local))

    carry = carry_ref[...]
    for s, (i1, i2, w1, w2, assigned, local) in enumerate(routed):
        rows = slice(s * ROW, (s + 1) * ROW)
        before = local + carry[0:1, :]
        r1 = jnp.sum(jnp.where(lane == i1, before, 0.0), axis=-1, keepdims=True)
        r2 = jnp.sum(jnp.where(lane == i2, before, 0.0), axis=-1, keepdims=True)
        carry = carry + jnp.sum(assigned, axis=0, keepdims=True)
        fields = (i1.astype(F32), i2.astype(F32), w1, w2, r1, r2)
        meta = jnp.zeros((ROW, LANES), F32)
        for f, val in enumerate(fields):
            meta = jnp.where(lane == f, val, meta)
        meta_ref[rows] = meta
        slots_ref[:, rows] = meta.T[:SUBLANES, :]
    carry_ref[...] = carry
    cnt_ref[...] = carry


def _route(xs, mod, layer, nw, pre_g, w_r, b_r):
    n_batch, ts, d = xs.shape
    nt = ts // ROW
    m_rows = n_batch * ts
    n_tiles = n_batch * nt
    n_sub = 2 if n_tiles % 2 == 0 else 1
    t_idx = jnp.arange(ROW)
    tri_strict = (t_idx[:, None] > t_idx[None, :]).astype(BF16)

    def const(shape):
        return pl.BlockSpec(shape, lambda k: (0,) * len(shape))

    rows = n_sub * ROW
    return pl.pallas_call(
        functools.partial(_route_kernel, n_sub=n_sub),
        grid=(n_tiles // n_sub,),
        in_specs=[pl.BlockSpec((rows, d), lambda k: (k, 0))]
        + [_flat_mod_spec(d, layer, chunk, nw, n_batch, nt, n_sub, sub) for sub in range(n_sub) for chunk in (3, 4)]
        + [const((1, d)), const((d, LANES)), const((1, LANES)), const((ROW, ROW))],
        out_specs=[pl.BlockSpec((rows, d), lambda k: (k, 0)),
                   pl.BlockSpec((rows, LANES), lambda k: (k, 0)),
                   pl.BlockSpec((SUBLANES, rows), lambda k: (0, k)),
                   const((SUBLANES, LANES))],
        out_shape=[jax.ShapeDtypeStruct((m_rows, d), F32),
                   jax.ShapeDtypeStruct((m_rows, LANES), F32),
                   jax.ShapeDtypeStruct((SUBLANES, m_rows), F32),
                   jax.ShapeDtypeStruct((SUBLANES, LANES), F32)],
        scratch_shapes=[pltpu.VMEM((SUBLANES, LANES), F32)],
        compiler_params=_cparams(("arbitrary",)),
        name="route",
    )(xs.reshape(m_rows, d), *([mod] * (2 * n_sub)), pre_g, w_r, b_r, tri_strict)


GROUPS = ROW // SUBLANES


def _start_rows(make_copy):
    for g in range(GROUPS):
        for j in range(SUBLANES):
            for slot in range(2):
                make_copy(g, j, slot).start()


def _wait_rows(make_copy):
    def drain(g, carry):
        for j in range(SUBLANES):
            for slot in range(2):
                make_copy(0, 0, slot).wait()
        return carry

    lax.fori_loop(0, GROUPS, drain, 0)


STAGES = 3


def _dispatch_kernel(pos_ref, bounds_ref, xn_ref, out_ref, zero_ref, stage_ref, row_sems, in_sems, zero_sem):
    step = pl.program_id(0)
    n_steps = pl.num_programs(0)
    n_tok = pos_ref.shape[0] // 2
    tm = zero_ref.shape[0]
    n_sorted = out_ref.shape[0]

    @pl.when(step == 0)
    def _():
        zero_ref[...] = jnp.zeros_like(zero_ref)

        def fill(row):
            copy = pltpu.make_async_copy(zero_ref, out_ref.at[pl.ds(pl.multiple_of(row, tm), tm)], zero_sem)
            copy.start()
            copy.wait()

        for e in range(N_EXPERTS):
            end = bounds_ref[N_EXPERTS + e]

            @pl.when(end > bounds_ref[e])
            def _():
                fill(end - tm)

        last_end = bounds_ref[2 * N_EXPERTS - 1]

        def tail(k, carry):
            fill(last_end + k * tm)
            return carry

        lax.fori_loop(0, (n_sorted - last_end) // tm, tail, 0)

    def tile_in(tile):
        slot = tile % STAGES
        return pltpu.make_async_copy(xn_ref.at[pl.ds(tile * GROUPS, GROUPS)], stage_ref.at[slot], in_sems.at[slot])

    def copies(tile):
        def make_copy(g, j, slot):
            p = pos_ref[slot * n_tok + tile * ROW + g * SUBLANES + j]
            return pltpu.make_async_copy(stage_ref.at[tile % STAGES, g, pl.ds(j, 1)], out_ref.at[pl.ds(p, 1)],
                                         row_sems.at[tile % STAGES])
        return make_copy

    @pl.when(step == 0)
    def _():
        tile_in(step).start()

    @pl.when(step >= STAGES - 1)
    def _():
        _wait_rows(copies(step - (STAGES - 1)))

    @pl.when(step + 1 < n_steps)
    def _():
        tile_in(step + 1).start()

    tile_in(step).wait()
    _start_rows(copies(step))

    @pl.when(step == n_steps - 1)
    def _():
        for back in range(STAGES - 2, -1, -1):
            @pl.when(step - back >= 0)
            def _():
                _wait_rows(copies(step - back))


def _dispatch(pos, bounds, xn, n_sorted, tm):
    m_rows, d = xn.shape
    return pl.pallas_call(
        _dispatch_kernel,
        grid_spec=pltpu.PrefetchScalarGridSpec(
            num_scalar_prefetch=2,
            grid=(m_rows // ROW,),
            in_specs=[pl.BlockSpec(memory_space=pl.ANY)],
            out_specs=pl.BlockSpec(memory_space=pl.ANY),
            scratch_shapes=[pltpu.VMEM((tm, d), F32), pltpu.VMEM((STAGES, GROUPS, SUBLANES, d), F32),
                            pltpu.SemaphoreType.DMA((STAGES,)), pltpu.SemaphoreType.DMA((STAGES,)),
                            pltpu.SemaphoreType.DMA(())],
        ),
        out_shape=jax.ShapeDtypeStruct((n_sorted, d), F32),
        compiler_params=_cparams(("arbitrary",)),
        name="dispatch",
    )(pos, bounds, xn.reshape(m_rows // SUBLANES, SUBLANES, d))


def _experts_kernel(te_ref, x_ref, wg_ref, wu_ref, wd_ref, y_ref):
    used = te_ref[pl.program_id(0)] < N_EXPERTS

    @pl.when(used)
    def _():
        x = x_ref[...].astype(BF16)
        gate = jnp.dot(x, wg_ref[...], preferred_element_type=F32)
        up = jnp.dot(x, wu_ref[...], preferred_element_type=F32)
        y_ref[...] = jnp.dot((_silu(gate) * up).astype(BF16), wd_ref[...], preferred_element_type=F32)

    @pl.when(jnp.logical_not(used))
    def _():
        y_ref[...] = jnp.zeros_like(y_ref)


def _experts(tile_expert, x_sorted, wg, wu, wd, tm):
    n_sorted, d = x_sorted.shape
    n_e, _, dff = wg.shape

    def weight(shape):
        return pl.BlockSpec((None,) + shape, lambda i, te: (jnp.minimum(te[i], n_e - 1), 0, 0))

    return pl.pallas_call(
        _experts_kernel,
        grid_spec=pltpu.PrefetchScalarGridSpec(
            num_scalar_prefetch=1,
            grid=(n_sorted // tm,),
            in_specs=[pl.BlockSpec((tm, d), lambda i, te: (i, 0)),
                      weight((d, dff)), weight((d, dff)), weight((dff, d))],
            out_specs=pl.BlockSpec((tm, d), lambda i, te: (i, 0)),
        ),
        out_shape=jax.ShapeDtypeStruct((n_sorted, d), F32),
        compiler_params=_cparams(("arbitrary",)),
        name="experts",
    )(tile_expert, x_sorted, wg, wu, wd)


def _combine_kernel(pos_ref, x_ref, meta_ref, g2_ref, post_ref, y_ref, o_ref, buf_ref, sems, *,
                    tiles_per_sample, latent_only):
    step = pl.program_id(0)
    n_steps = pl.num_programs(0)
    n_tok = pos_ref.shape[0] // 2
    d_model = x_ref.shape[-1]

    def wanted(tile):
        return (tile % tiles_per_sample != 0) if latent_only else (tile >= 0)

    def copies(tile):
        def make_copy(g, j, slot):
            p = pos_ref[slot * n_tok + tile * ROW + g * SUBLANES + j]
            return pltpu.make_async_copy(y_ref.at[pl.ds(p, 1)], buf_ref.at[tile % 2, slot, g, pl.ds(j, 1)],
                                         sems.at[tile % 2])
        return make_copy

    @pl.when(jnp.logical_and(step == 0, wanted(step)))
    def _():
        _start_rows(copies(step))

    @pl.when(jnp.logical_and(step + 1 < n_steps, wanted(step + 1)))
    def _():
        _start_rows(copies(step + 1))

    @pl.when(wanted(step))
    def _():
        _wait_rows(copies(step))
        meta = meta_ref[...]
        cur = step % 2
        y1 = buf_ref[cur, 0].reshape(ROW, d_model)
        y2 = buf_ref[cur, 1].reshape(ROW, d_model)
        z = meta[:, 2:3] * y1 + meta[:, 3:4] * y2
        o_ref[...] = x_ref[...] + g2_ref[...] * _rms(z, post_ref[...])


def _combine(pos, xs, meta, y_sorted, mod, layer, nw, post_g, latent_only):
    n_batch, ts, d = xs.shape
    nt = ts // ROW
    m_rows = n_batch * ts
    base = (layer * 6 + 5) * nw
    tile = pl.BlockSpec((ROW, d), lambda i, pos: (i, 0))
    if latent_only:
        out_rows = n_batch * (ts - ROW)
        out_tile = pl.BlockSpec((ROW, d), lambda i, pos: ((i // nt) * (nt - 1) + jnp.maximum(i % nt - 1, 0), 0))
    else:
        out_rows, out_tile = m_rows, tile
    out = pl.pallas_call(
        functools.partial(_combine_kernel, tiles_per_sample=nt, latent_only=latent_only),
        grid_spec=pltpu.PrefetchScalarGridSpec(
            num_scalar_prefetch=1,
            grid=(m_rows // ROW,),
            in_specs=[
                tile,
                pl.BlockSpec((ROW, LANES), lambda i, pos: (i, 0)),
                pl.BlockSpec((None, 1, d), lambda i, pos: (base + jnp.where(i % nt == 0, n_batch, i // nt), 0, 0)),
                pl.BlockSpec((1, d), lambda i, pos: (0, 0)),
                pl.BlockSpec(memory_space=pl.ANY),
            ],
            out_specs=out_tile,
            scratch_shapes=[pltpu.VMEM((2, 2, GROUPS, SUBLANES, d), F32), pltpu.SemaphoreType.DMA((2,))],
        ),
        out_shape=jax.ShapeDtypeStruct((out_rows, d), F32),
        compiler_params=_cparams(("arbitrary",)),
        name="combine",
    )(pos, xs.reshape(m_rows, d), meta.reshape(m_rows, LANES), mod, post_g, y_sorted)
    return out.reshape(n_batch, out_rows // n_batch, d)


def _moe(xs, mod, layer, nw, pre_g, post_g, w_r, b_r, wg, wu, wd, latent_only, tm=512):
    n_batch, ts, d = xs.shape
    m_rows = n_batch * ts
    xn, meta, slots, cnt = _route(xs, mod, layer, nw, pre_g, w_r, b_r)
    i1, i2, r1, r2 = (slots[f].astype(jnp.int32) for f in (0, 1, 4, 5))
    counts = cnt[0, :N_EXPERTS].astype(jnp.int32)
    padded = -(-counts // tm) * tm
    ends = jnp.cumsum(padded)
    start = ends - padded
    pos = jnp.concatenate([start[i1] + r1, start[i2] + r2])
    bounds = jnp.concatenate([start, ends])
    n_tiles = 2 * m_rows // tm + N_EXPERTS
    tile_row = jnp.arange(n_tiles, dtype=jnp.int32) * tm
    tile_expert = jnp.sum((ends[None, :] <= tile_row[:, None]).astype(jnp.int32), axis=1)
    x_sorted = _dispatch(pos, bounds, xn.reshape(m_rows, d), n_tiles * tm, tm)
    y_sorted = _experts(tile_expert, x_sorted, wg, wu, wd, tm)
    return _combine(pos, xs, meta, y_sorted, mod, layer, nw, post_g, latent_only)


def _rope_tables(n_tok, n_ctx):
    f32 = np.float32
    n_freq = HEAD_DIM // 4
    pos = np.arange(n_tok)
    row = (pos // GRID_W).astype(f32)
    colp = (pos % GRID_W).astype(f32)
    inv = (f32(ROPE_THETA) ** (-np.arange(n_freq, dtype=f32) / f32(n_freq))).astype(f32)
    lane = np.arange(LANES)
    in_head = lane % HEAD_DIM
    use_col = (in_head // (HEAD_DIM // 2)) == 1
    freq = inv[in_head % n_freq]
    ang = (np.where(use_col[None, :], colp[:, None], row[:, None]) * freq[None, :]).astype(f32)
    lower = (in_head % (HEAD_DIM // 2)) < n_freq
    cos_t = np.cos(ang).astype(f32)
    sin_t = np.sin(ang).astype(f32)
    sin_a = np.where(lower[None, :], -sin_t, f32(0.0)).astype(f32)
    sin_b = np.where(lower[None, :], f32(0.0), sin_t).astype(f32)
    pad = lambda t, v: np.concatenate([np.full((n_ctx, LANES), v, f32), t], axis=0)
    return jnp.asarray(pad(cos_t, 1.0)), jnp.asarray(pad(sin_a, 0.0)), jnp.asarray(pad(sin_b, 0.0))


def _block_diag_mean(width):
    idx = jnp.arange(width) // HEAD_DIM
    return jnp.where(idx[:, None] == idx[None, :], 1.0 / HEAD_DIM, 0.0).astype(BF16)


def _pack_w_in(w):
    a_q = A_HEADS * HEAD_DIM
    a_kv = 2 * A_KV_HEADS * HEAD_DIM
    n_gates = 4 * C_HEADS
    gate_start = w.shape[1] - 3 * w.shape[0]
    g_start = gate_start - n_gates
    main = jnp.concatenate([w[:, gate_start:], w[:, :a_q], w[:, a_q + a_kv:g_start], w[:, a_q:a_q + a_kv]], axis=1)
    gates = jnp.pad(w[:, g_start:gate_start], ((0, 0), (0, LANES - n_gates)))
    return main.astype(BF16), gates.astype(BF16)


def kernel(x, c, ctx, c_ctx, ada_w, ada_b, pre_mix_g, post_mix_g, pre_ffn_g, post_ffn_g, w_in, q_norm_g, k_norm_g, lam_q1, lam_k1, lam_q2, lam_k2, diff_norm_g, conv_w, conv_b, mlstm_gate_b, mlstm_norm_g, w_br_attn, w_br_diff, w_br_mlstm, w_out, w_ff_gate, w_ff_up, w_ff_down, w_router, b_router, w_moe_gate, w_moe_up, w_moe_down):
    n_batch, n_tok, d = x.shape
    n_ctx = ctx.shape[1]
    depth = ada_w.shape[0]
    assert n_ctx == ROW and n_tok % ROW == 0 and d == 1024
    ts = n_ctx + n_tok
    nw = -(-(n_batch + 1) // SUBLANES) * SUBLANES

    c_all = jnp.concatenate([c, c_ctx[None, :], jnp.zeros((nw - n_batch - 1, d), F32)], axis=0)
    mod = _modulation(c_all, ada_w, ada_b)
    tables = _rope_tables(n_tok, n_ctx)
    bd4, bd1 = _block_diag_mean(A_HEADS * HEAD_DIM), _block_diag_mean(LANES)
    t_idx = jnp.arange(MCHUNK)
    tri = (t_idx[:, None] >= t_idx[None, :]).astype(BF16)

    xs = (ctx, x)
    for l in range(depth):
        lam_init = 0.8 - 0.6 * math.exp(-0.3 * l)
        w_main, w_gate = _pack_w_in(w_in[l])
        qg = jnp.tile(q_norm_g[l], A_HEADS)[None, :]
        kg = jnp.tile(k_norm_g[l], A_KV_HEADS)[None, :]
        gate_b = jnp.pad(mlstm_gate_b[l], (0, LANES - 4 * C_HEADS))[None, :]
        merge_gate, out_gate, qaz, ka, va, qbz, kb, vb, qmt, km, vmt, gcol, grow = _inproj(
            xs, mod, l, nw, pre_mix_g[l][None, :], w_main, w_gate,
            tables, qg, kg, bd4, bd1, conv_w[l], conv_b[l][None, :], gate_b, tri)
        a_out = _gqa(qaz, ka, va)
        lam_vecs = jnp.stack([lam_q1[l], lam_k1[l], lam_q2[l], lam_k2[l]], axis=0)
        d_out = _diff(qbz, kb, vb, lam_vecs, diff_norm_g[l][None, :], lam_init)
        hf, hb = _mlstm(qmt, km, vmt, gcol, grow)
        xs = _mix(a_out, d_out, hf, hb, out_gate, merge_gate, xs, mod, l, nw,
                  post_mix_g[l][None, :], mlstm_norm_g[l][None, :],
                  w_br_attn[l].astype(BF16), w_br_diff[l].astype(BF16), w_br_mlstm[l].astype(BF16),
                  w_out[l].astype(BF16))
        j = l // 2
        if l % 2 == 0:
            xs = _ffn(xs, mod, l, nw, pre_ffn_g[l][None, :], post_ffn_g[l][None, :],
                      w_ff_gate[j].astype(BF16), w_ff_up[j].astype(BF16), w_ff_down[j].astype(BF16))
        else:
            w_r = jnp.pad(w_router[j], ((0, 0), (0, LANES - N_EXPERTS)))
            b_r = jnp.pad(b_router[j], (0, LANES - N_EXPERTS))[None, :]
            xs = _moe(xs, mod, l, nw, pre_ffn_g[l][None, :], post_ffn_g[l][None, :], w_r, b_r,
                      w_moe_gate[j].astype(BF16), w_moe_up[j].astype(BF16), w_moe_down[j].astype(BF16),
                      latent_only=l == depth - 1)
    return xs if xs.shape[1] == n_tok else xs[:, n_ctx:, :]
```

```python
import functools
import math

import jax
import jax.numpy as jnp
import numpy as np
from jax import lax
from jax.experimental import pallas as pl
from jax.experimental.pallas import tpu as pltpu

F32 = jnp.float32
BF16 = jnp.bfloat16
HIGHEST = lax.Precision.HIGHEST

EPS = 1e-6
HEAD_DIM = 64
A_HEADS = 8
A_KV_HEADS = 2
B_HEADS = 4
C_HEADS = 4
C_DIM = 128
N_EXPERTS = 8
ROPE_THETA = 10000.0
GRID_W = 64
CONV_W = 3

LANES = 128
SUBLANES = 8
ROW = 256
MCHUNK = 256
LOG2E = math.log2(math.e)
NEG = -1e30
V7X_VMEM_BYTES = 64 * 1024 * 1024
VMEM_LIMIT = V7X_VMEM_BYTES - 8 * 1024 * 1024

REL_QA, REL_KB, REL_QC, REL_VC, REL_KA = 0, 1024, 2048, 3072, 4096


def _cparams(sem):
    return pltpu.CompilerParams(dimension_semantics=sem, vmem_limit_bytes=VMEM_LIMIT)


def _rms(x, g):
    y = x * lax.rsqrt(jnp.mean(x * x, axis=-1, keepdims=True) + EPS)
    return y * g


def _sigmoid(x):
    return 1.0 / (1.0 + jnp.exp(-x))


def _silu(x):
    return x * _sigmoid(x)


def _log_sigmoid(x):
    return jnp.minimum(x, 0.0) - jnp.log(1.0 + jnp.exp(-jnp.abs(x)))


def _lane_iota(shape):
    return lax.broadcasted_iota(jnp.int32, shape, len(shape) - 1)


def _row_iota(shape):
    return lax.broadcasted_iota(jnp.int32, shape, len(shape) - 2)


def _mod_kernel(c_ref, w_ref, b_ref, o_ref):
    c = c_ref[...]
    o_ref[...] = jnp.dot(_silu(c), w_ref[...], preferred_element_type=F32, precision=HIGHEST) + b_ref[...]


def _modulation(c_all, ada_w, ada_b):
    depth, d, _ = ada_w.shape
    nw = c_all.shape[0]
    out = pl.pallas_call(
        _mod_kernel,
        grid=(depth, 6),
        in_specs=[
            pl.BlockSpec((nw, d), lambda l, j: (0, 0)),
            pl.BlockSpec((None, d, d), lambda l, j: (l, 0, j)),
            pl.BlockSpec((None, 1, d), lambda l, j: (l, 0, j)),
        ],
        out_specs=pl.BlockSpec((None, None, nw, d), lambda l, j: (l, j, 0, 0)),
        out_shape=jax.ShapeDtypeStruct((depth, 6, nw, d), F32),
        compiler_params=_cparams(("arbitrary", "arbitrary")),
        name="modulation",
    )(c_all, ada_w, ada_b.reshape(depth, 1, 6 * d))
    return out.reshape(depth * 6 * nw, 1, d)


def _mod_spec(d, layer, chunk, nw, n_batch):
    base = (layer * 6 + chunk) * nw
    return pl.BlockSpec((None, 1, d), lambda b, i: (base + jnp.where(i == 0, n_batch, b), 0, 0))


def _stream(stream):
    if isinstance(stream, tuple):
        ctx, x = stream
        n_batch, n_tok, d = x.shape
        specs = [pl.BlockSpec((None, ROW, d), lambda b, i: (b, 0, 0)),
                 pl.BlockSpec((None, ROW, d), lambda b, i: (b, jnp.maximum(i - 1, 0), 0))]
        return (n_batch, ctx.shape[1] + n_tok, d), specs, [ctx, x]
    d = stream.shape[-1]
    return stream.shape, [pl.BlockSpec((None, ROW, d), lambda b, i: (b, i, 0))], [stream]


def _stream_tile(refs):
    if len(refs) == 2:
        return jnp.where(pl.program_id(1) == 0, refs[0][...], refs[1][...])
    return refs[0][...]


def _inproj_kernel(*refs, n_stream):
    (xprev_ref, xnext_ref, sh_ref, sc_ref, g_ref, w_ref, wg_ref,
     cos_ref, sa_ref, sb_ref, qg_ref, kg_ref, bd4_ref, bd1_ref, cw_ref, cb_ref, gb_ref, tri_ref,
     gate_ref, oc_ref, *mixer_refs) = refs[n_stream:]
    d_model = xprev_ref.shape[-1]

    def normed(x):
        return _rms(x, g_ref[...]) * (1.0 + sc_ref[...]) + sh_ref[...]

    xn = normed(_stream_tile(refs[:n_stream]))
    xb = xn.astype(BF16)
    ext = jnp.concatenate([normed(xprev_ref[...]), xn, normed(xnext_ref[...])], axis=0).astype(BF16)

    def proj(lhs, start, width):
        return jnp.dot(lhs, w_ref[:, start:start + width], preferred_element_type=F32)

    base = 3 * d_model
    half = C_HEADS * C_DIM
    qab = proj(xb, base + REL_QA, 2 * half)
    kvb = proj(xb, base + REL_KB, 2 * half)
    qkc = proj(ext, base + REL_QC, 2 * half)
    vo = proj(xb, base + REL_VC, 2 * half)
    oc_ref[...] = _sigmoid(vo[:, half:]).astype(BF16)
    kava = proj(xb, base + REL_KA, 2 * LANES)
    gates = jnp.dot(xb, wg_ref[...], preferred_element_type=F32)
    _prep_math(pl.program_id(1), pl.num_programs(1),
               qab[:, :half], kava, qab[:, half:], kvb[:, :half], kvb[:, half:],
               qkc[SUBLANES:SUBLANES + ROW], qkc[SUBLANES - 1:SUBLANES], qkc[SUBLANES + ROW:SUBLANES + ROW + 1],
               vo[:, :half], gates,
               cos_ref, sa_ref, sb_ref, qg_ref, kg_ref, bd4_ref, bd1_ref, cw_ref, cb_ref, gb_ref, tri_ref,
               *mixer_refs)
    for c in range(6):
        gate_ref[:, c * half:(c + 1) * half] = _sigmoid(proj(xb, c * half, half)).astype(BF16)


def _inproj(xs, mod, layer, nw, pre_g, w_main, w_gate, tables, qg, kg, bd4, bd1, conv_w, conv_b, gate_b, tri):
    (n_batch, ts, d), stream_specs, stream_arrays = _stream(xs)
    halo_src = stream_arrays[-1]
    tile_shift = len(stream_arrays) - 1
    n_main = w_main.shape[1]
    nt = ts // ROW
    cos_t, sa_t, sb_t = tables
    row_blocks = halo_src.shape[1] // SUBLANES
    per_tile = ROW // SUBLANES
    half = C_HEADS * C_DIM

    def const(shape):
        return pl.BlockSpec(shape, lambda b, i: (0,) * len(shape))

    def rows(width):
        return pl.BlockSpec((None, ROW, width), lambda b, i: (b, i, 0))

    def heads(n, width):
        return pl.BlockSpec((None, n, ROW, width), lambda b, i: (b, 0, i, 0))

    def cols(height):
        return pl.BlockSpec((None, height, ROW), lambda b, i: (b, 0, i))

    table = pl.BlockSpec((ROW, LANES), lambda b, i: (i, 0))
    in_specs = stream_specs + [
        pl.BlockSpec((None, SUBLANES, d), lambda b, i: (b, jnp.maximum((i - tile_shift) * per_tile - 1, 0), 0)),
        pl.BlockSpec((None, SUBLANES, d),
                     lambda b, i: (b, jnp.clip((i + 1 - tile_shift) * per_tile, 0, row_blocks - 1), 0)),
        _mod_spec(d, layer, 0, nw, n_batch),
        _mod_spec(d, layer, 1, nw, n_batch),
        const((1, d)),
        pl.BlockSpec((d, n_main), lambda b, i: (0, 0), pipeline_mode=pl.Buffered(1)),
        const((d, LANES)),
        table, table, table,
        const((1, half)), const((1, LANES)), const((half, half)), const((LANES, LANES)),
        const((CONV_W, 2 * half)), const((1, 2 * half)), const((1, LANES)), const((MCHUNK, MCHUNK)),
    ]
    outs = [
        (rows(3 * d), (ts, 3 * d), BF16),
        (rows(half), (ts, half), BF16),
        (heads(A_HEADS, LANES), (A_HEADS, ts, LANES), BF16),
        (rows(LANES), (ts, LANES), BF16),
        (heads(A_KV_HEADS, LANES), (A_KV_HEADS, ts, LANES), BF16),
        (heads(2 * B_HEADS, LANES), (2 * B_HEADS, ts, LANES), BF16),
        (heads(B_HEADS, LANES), (B_HEADS, ts, LANES), BF16),
        (heads(B_HEADS, 2 * LANES), (B_HEADS, ts, 2 * LANES), BF16),
        (cols(half), (half, ts), BF16),
        (rows(half), (ts, half), BF16),
        (cols(half), (half, ts), BF16),
        (rows(3 * LANES), (ts, 3 * LANES), F32),
        (cols(6 * SUBLANES), (6 * SUBLANES, ts), F32),
    ]
    return pl.pallas_call(
        functools.partial(_inproj_kernel, n_stream=len(stream_arrays)),
        grid=(n_batch, nt),
        in_specs=in_specs,
        out_specs=[spec for spec, _, _ in outs],
        out_shape=[jax.ShapeDtypeStruct((n_batch,) + shape, dtype) for _, shape, dtype in outs],
        compiler_params=_cparams(("parallel", "parallel")),
        name="inproj",
    )(*stream_arrays, halo_src, halo_src, mod, mod, pre_g, w_main, w_gate, cos_t, sa_t, sb_t, qg, kg, bd4, bd1, conv_w, conv_b, gate_b, tri)


def _head_mean_sq(x, bd_ref):
    return jnp.dot((x * x).astype(BF16), bd_ref[...], preferred_element_type=F32)


def _prep_math(i, nt, qa, kava, qb, kb, vb, cur, prev_row, next_row, vc, gates,
               cos_ref, sa_ref, sb_ref, qg_ref, kg_ref, bd4_ref, bd1_ref, cw_ref, cb_ref, gb_ref, tri_ref,
               qaz_ref, ka_ref, va_ref, qbz_ref, kbo_ref, vbo_ref, qmt_ref, km_ref, vmt_ref, gcol_ref, grow_ref):
    cos, sin_a, sin_b = cos_ref[...], sa_ref[...], sb_ref[...]

    def rope(x):
        width = x.shape[1]
        reps = width // LANES
        c = jnp.concatenate([cos] * reps, axis=1) if reps > 1 else cos
        a = jnp.concatenate([sin_a] * reps, axis=1) if reps > 1 else sin_a
        b = jnp.concatenate([sin_b] * reps, axis=1) if reps > 1 else sin_b
        return x * c + pltpu.roll(x, width - 16, 1) * a + pltpu.roll(x, 16, 1) * b

    lane = _lane_iota((ROW, LANES))
    ones = jnp.ones((ROW, LANES), BF16)
    scale = HEAD_DIM ** -0.5 * LOG2E

    qa = qa * lax.rsqrt(_head_mean_sq(qa, bd4_ref) + EPS) * qg_ref[...]
    qa = rope(qa) * scale
    heads_per_kv = A_HEADS // A_KV_HEADS
    for h in range(A_HEADS):
        g = h // heads_per_kv
        blk = qa[:, (h // 2) * LANES:(h // 2 + 1) * LANES]
        if h % 2 != g:
            blk = pltpu.roll(blk, HEAD_DIM, 1)
        qaz_ref[h] = jnp.where(lane // HEAD_DIM == g, blk, 0.0).astype(BF16)
    ka = kava[:, :LANES]
    ka = ka * lax.rsqrt(_head_mean_sq(ka, bd1_ref) + EPS) * kg_ref[...]
    ka_ref[...] = rope(ka).astype(BF16)
    va = kava[:, LANES:].astype(BF16)
    for g in range(A_KV_HEADS):
        va_ref[g] = jnp.where(lane // HEAD_DIM == g, va, ones)

    qb = rope(qb) * scale
    kb = rope(kb)
    for h in range(B_HEADS):
        blk = qb[:, h * LANES:(h + 1) * LANES]
        for m in range(2):
            qbz_ref[2 * h + m] = jnp.where(lane // HEAD_DIM == m, blk, 0.0).astype(BF16)
        kbo_ref[h] = kb[:, h * LANES:(h + 1) * LANES].astype(BF16)
        vbo_ref[h, :, :LANES] = vb[:, h * LANES:(h + 1) * LANES].astype(BF16)
        vbo_ref[h, :, LANES:] = ones

    row = _row_iota(cur.shape)
    prev_row = jnp.where(i >= 2, prev_row, 0.0)
    next_row = jnp.where(jnp.logical_and(i >= 1, i < nt - 1), next_row, 0.0)
    up = jnp.where(row == 0, prev_row, pltpu.roll(cur, 1, 0))
    dn = jnp.where(row == ROW - 1, next_row, pltpu.roll(cur, ROW - 1, 0))
    y = up * cw_ref[0:1, :] + cur * cw_ref[1:2, :] + dn * cw_ref[2:3, :] + cb_ref[...]
    y = _silu(y)
    half = C_HEADS * C_DIM
    qmt_ref[...] = y[:, :half].T.astype(BF16)
    km_ref[...] = (y[:, half:] * (C_DIM ** -0.5)).astype(BF16)
    vmt_ref[...] = vc.T.astype(BF16)

    gg = gates + gb_ref[...]
    is_forget = (lane // C_HEADS) % 2 == 1
    gl = jnp.where(is_forget, _log_sigmoid(gg), gg) * LOG2E
    tri = tri_ref[...]
    n_rows = 2 * SUBLANES
    for c in range(ROW // MCHUNK):
        rows = slice(c * MCHUNK, (c + 1) * MCHUNK)
        glc = gl[rows]
        hi = glc.astype(BF16)
        rest = glc - hi.astype(F32)
        mid = rest.astype(BF16)
        low = (rest - mid.astype(F32)).astype(BF16)
        cs = (jnp.dot(tri, hi, preferred_element_type=F32) + jnp.dot(tri, mid, preferred_element_type=F32)
              + jnp.dot(tri, low, preferred_element_type=F32))
        tot = jnp.broadcast_to(jnp.sum(glc, axis=0, keepdims=True), glc.shape)
        for f, val in enumerate((glc, cs, tot)):
            gcol_ref[rows, f * LANES:(f + 1) * LANES] = val
            grow_ref[f * n_rows:(f + 1) * n_rows, rows] = val.T[:n_rows, :]


def _attend_blocks(blocks, n_keys):
    def scores(q, k_ref):
        return lax.dot_general(q, k_ref[:n_keys, :], (((1,), (1,)), ((), ())), preferred_element_type=F32)

    def weighted(s, v_ref):
        p = jnp.exp2(s - jnp.max(s, axis=-1, keepdims=True)).astype(BF16)
        return jnp.dot(p, v_ref[:n_keys, :], preferred_element_type=F32)

    outs = []
    s_cur = scores(blocks[0][0], blocks[0][1])
    for j in range(1, len(blocks)):
        s_next = scores(blocks[j][0], blocks[j][1])
        outs.append(weighted(s_cur, blocks[j - 1][2]))
        s_cur = s_next
    outs.append(weighted(s_cur, blocks[-1][2]))
    return outs


def _per_tile_keys(body, n_all):
    @pl.when(pl.program_id(1) == 0)
    def _():
        body(ROW)

    @pl.when(pl.program_id(1) > 0)
    def _():
        body(n_all)


def _gqa_kernel(q_ref, k_ref, v_ref, o_ref):
    heads_per_kv = A_HEADS // A_KV_HEADS
    lane = _lane_iota((ROW, LANES))

    def body(n_keys):
        blocks = []
        for j in range(A_HEADS // 2):
            q = q_ref[2 * j:2 * j + 2].reshape(2 * ROW, LANES)
            blocks.append((q, k_ref, v_ref.at[(2 * j) // heads_per_kv]))
        for j, o in enumerate(_attend_blocks(blocks, n_keys)):
            g = (2 * j) // heads_per_kv
            den_lane = (1 - g) * HEAD_DIM
            o = o / o[:, den_lane:den_lane + 1]
            even, odd = o[:ROW], o[ROW:]
            even = even if g == 0 else pltpu.roll(even, HEAD_DIM, 1)
            odd = odd if g == 1 else pltpu.roll(odd, HEAD_DIM, 1)
            o_ref[:, j * LANES:(j + 1) * LANES] = jnp.where(lane < HEAD_DIM, even, odd).astype(BF16)

    _per_tile_keys(body, k_ref.shape[0])


def _diff_kernel(q_ref, k_ref, v_ref, lam_ref, g_ref, o_ref, *, lam_init):
    lv = lam_ref[...]
    lam = (jnp.exp(jnp.sum(lv[0:1] * lv[1:2], axis=-1, keepdims=True))
           - jnp.exp(jnp.sum(lv[2:3] * lv[3:4], axis=-1, keepdims=True)) + lam_init)

    def body(n_keys):
        blocks = [(q_ref[2 * h:2 * h + 2].reshape(2 * ROW, LANES), k_ref.at[h], v_ref.at[h]) for h in range(B_HEADS)]
        for h, o in enumerate(_attend_blocks(blocks, n_keys)):
            o = o[:, :LANES] / o[:, LANES:LANES + 1]
            dif = o[:ROW] - lam * o[ROW:]
            o_ref[:, h * LANES:(h + 1) * LANES] = (_rms(dif, g_ref[...]) * (1.0 - lam_init)).astype(BF16)

    _per_tile_keys(body, k_ref.shape[1])


def _gqa(qaz, ka, va):
    n_batch, _, ts, _ = qaz.shape
    nt = ts // ROW
    return pl.pallas_call(
        _gqa_kernel,
        grid=(n_batch, nt),
        in_specs=[
            pl.BlockSpec((None, A_HEADS, ROW, LANES), lambda b, i: (b, 0, i, 0)),
            pl.BlockSpec((None, ts, LANES), lambda b, i: (b, 0, 0)),
            pl.BlockSpec((None, A_KV_HEADS, ts, LANES), lambda b, i: (b, 0, 0, 0)),
        ],
        out_specs=pl.BlockSpec((None, ROW, A_HEADS * HEAD_DIM), lambda b, i: (b, i, 0)),
        out_shape=jax.ShapeDtypeStruct((n_batch, ts, A_HEADS * HEAD_DIM), BF16),
        compiler_params=_cparams(("parallel", "parallel")),
        name="gqa_attention",
    )(qaz, ka, va)


def _diff(qbz, kb, vb, lam_vecs, sub_g, lam_init):
    n_batch, _, ts, _ = qbz.shape
    nt = ts // ROW
    return pl.pallas_call(
        functools.partial(_diff_kernel, lam_init=lam_init),
        grid=(n_batch, nt),
        in_specs=[
            pl.BlockSpec((None, 2 * B_HEADS, ROW, LANES), lambda b, i: (b, 0, i, 0)),
            pl.BlockSpec((None, B_HEADS, ts, LANES), lambda b, i: (b, 0, 0, 0)),
            pl.BlockSpec((None, B_HEADS, ts, 2 * LANES), lambda b, i: (b, 0, 0, 0)),
            pl.BlockSpec((4, HEAD_DIM), lambda b, i: (0, 0)),
            pl.BlockSpec((1, LANES), lambda b, i: (0, 0)),
        ],
        out_specs=pl.BlockSpec((None, ROW, B_HEADS * LANES), lambda b, i: (b, i, 0)),
        out_shape=jax.ShapeDtypeStruct((n_batch, ts, B_HEADS * LANES), BF16),
        compiler_params=_cparams(("parallel", "parallel")),
        name="diff_attention",
    )(qbz, kb, vb, lam_vecs, sub_g)


MBATCH = 4
AUG = 2 * SUBLANES


def _mlstm_kernel(qtf_ref, kf_ref, vtf_ref, gcf_ref, grf_ref, qtb_ref, kb_ref, vtb_ref, gcb_ref, grb_ref,
                  hf_ref, hb_ref, c_ref, m_ref):
    @pl.when(pl.program_id(1) == 0)
    def _():
        c_ref[...] = jnp.zeros_like(c_ref)
        m_ref[...] = jnp.zeros_like(m_ref)

    length = MCHUNK
    key_idx = _row_iota((length, length))
    qry_idx = _lane_iota((length, length))
    ones_rows = jnp.ones((AUG, length), BF16)
    n_rows = 2 * SUBLANES
    n_samples = qtf_ref.shape[0]
    n_chain = n_samples * 2 * C_HEADS
    c_states = [c_ref[ch] for ch in range(n_chain)]
    m_prevs = [m_ref[ch] for ch in range(n_chain)]
    c_news, m_news, pending = [], [], []

    lanes = ((qtf_ref, kf_ref, vtf_ref, gcf_ref, grf_ref), (qtb_ref, kb_ref, vtb_ref, gcb_ref, grb_ref))
    for sample, direction in [(sm, dr) for sm in range(n_samples) for dr in range(2)]:
        qt_ref, k_ref, vt_ref, gc_ref, gr_ref = (r.at[sample] for r in lanes[direction])
        reverse = direction == 1
        gate_c, cs_c, tot_c = gc_ref[:, :LANES], gc_ref[:, LANES:2 * LANES], gc_ref[:, 2 * LANES:]
        p_c = (tot_c - cs_c + gate_c) if reverse else cs_c
        g_c = gate_c - pltpu.roll(p_c, LANES - C_HEADS, 1)
        mask = (key_idx >= qry_idx) if reverse else (key_idx <= qry_idx)
        for hd in range(C_HEADS):
            chain = (sample * 2 + direction) * C_HEADS + hd
            ii = direction * 2 * C_HEADS + hd
            fi = ii + C_HEADS
            i_row = gr_ref[ii:ii + 1, :]
            f_row = gr_ref[fi:fi + 1, :]
            cs_row = gr_ref[n_rows + fi:n_rows + fi + 1, :]
            tot_row = gr_ref[2 * n_rows + fi:2 * n_rows + fi + 1, :]
            p_row = (tot_row - cs_row + f_row) if reverse else cs_row
            m_prev = m_prevs[chain]
            inter = p_row + m_prev
            log_dt = jnp.where(mask, jnp.broadcast_to(g_c[:, ii:ii + 1], (length, length)) + p_row, NEG)
            m_t = jnp.maximum(inter, jnp.max(log_dt, axis=0, keepdims=True))
            d_t = jnp.exp2(log_dt - m_t)
            a_row = jnp.exp2(inter - m_t)
            sl = slice(hd * C_DIM, (hd + 1) * C_DIM)
            k_h, qt_h = k_ref[:, sl], qt_ref[sl, :]
            vt_aug = jnp.concatenate([vt_ref[sl, :], ones_rows], axis=0)
            s_raw = jnp.dot(k_h, qt_h, preferred_element_type=F32)
            c_state = c_states[chain]
            x_t = jnp.dot(c_state.astype(BF16), qt_h, preferred_element_type=F32)
            w_row = tot_row - p_row + i_row
            m_new = jnp.maximum(tot_row + m_prev, jnp.max(w_row, axis=-1, keepdims=True))
            decay = jnp.exp2(tot_row + m_prev - m_new)
            ws = jnp.exp2(w_row - m_new)
            update = jnp.dot((vt_aug.astype(F32) * ws).astype(BF16), k_h, preferred_element_type=F32)
            c_news.append(decay[:, :C_DIM] * c_state + update)
            m_news.append(m_new)
            pending.append((s_raw, d_t, vt_aug, a_row, x_t, m_t))

    for sample, direction in [(sm, dr) for sm in range(n_samples) for dr in range(2)]:
        h_parts = []
        for hd in range(C_HEADS):
            s_raw, d_t, vt_aug, a_row, x_t, m_t = pending[(sample * 2 + direction) * C_HEADS + hd]
            y_t = jnp.dot(vt_aug, (s_raw * d_t).astype(BF16), preferred_element_type=F32)
            num_t = a_row * x_t[:C_DIM] + y_t[:C_DIM]
            den = a_row * x_t[C_DIM:C_DIM + 1] + y_t[C_DIM:C_DIM + 1]
            h_t = num_t / jnp.maximum(jnp.abs(den), jnp.exp2(-m_t))
            h_parts.append(h_t.T)
        (hb_ref if direction else hf_ref)[sample] = jnp.concatenate(h_parts, axis=1)

    for ch in range(n_chain):
        c_ref[ch] = c_news[ch]
        m_ref[ch] = m_news[ch]


def _mlstm(qmt, km, vmt, gcol, grow):
    n_batch, ts, width = km.shape
    nc = ts // MCHUNK
    ctx_chunks = ROW // MCHUNK
    mbatch = math.gcd(n_batch, MBATCH)

    def bwd(j):
        return jnp.where(j < ctx_chunks, ctx_chunks - 1 - j, nc + ctx_chunks - 1 - j)

    def specs(idx):
        return [
            pl.BlockSpec((mbatch, width, MCHUNK), lambda b, j: (b, 0, idx(j))),
            pl.BlockSpec((mbatch, MCHUNK, width), lambda b, j: (b, idx(j), 0)),
            pl.BlockSpec((mbatch, width, MCHUNK), lambda b, j: (b, 0, idx(j))),
            pl.BlockSpec((mbatch, MCHUNK, 3 * LANES), lambda b, j: (b, idx(j), 0)),
            pl.BlockSpec((mbatch, 6 * SUBLANES, MCHUNK), lambda b, j: (b, 0, idx(j))),
        ]

    fwd = lambda j: j
    n_chain = mbatch * 2 * C_HEADS
    return pl.pallas_call(
        _mlstm_kernel,
        grid=(n_batch // mbatch, nc),
        in_specs=specs(fwd) + specs(bwd),
        out_specs=[
            pl.BlockSpec((mbatch, MCHUNK, width), lambda b, j: (b, j, 0)),
            pl.BlockSpec((mbatch, MCHUNK, width), lambda b, j: (b, bwd(j), 0)),
        ],
        out_shape=[jax.ShapeDtypeStruct((n_batch, ts, width), F32)] * 2,
        scratch_shapes=[
            pltpu.VMEM((n_chain, C_DIM + AUG, C_DIM), F32),
            pltpu.VMEM((n_chain, 1, MCHUNK), F32),
        ],
        compiler_params=_cparams(("parallel", "arbitrary")),
        name="mlstm",
    )(qmt, km, vmt, gcol, grow, qmt, km, vmt, gcol, grow)


def _mix_kernel(*refs, n_stream):
    (a_ref, d_ref, hf_ref, hb_ref, oc_ref, gate_ref, g1_ref, pg_ref, mg_ref,
     wa_ref, wb_ref, wc_ref, wo_ref, o_ref) = refs[n_stream:]
    d_model = o_ref.shape[-1]
    hsum = hf_ref[...] + hb_ref[...]
    mg = mg_ref[...]
    m = jnp.concatenate([_rms(hsum[:, hd * C_DIM:(hd + 1) * C_DIM], mg) for hd in range(C_HEADS)], axis=1)
    m = m * oc_ref[...].astype(F32)
    u = (gate_ref[:, :d_model].astype(F32)
         * jnp.dot(a_ref[...], wa_ref[...], preferred_element_type=F32)
         + gate_ref[:, d_model:2 * d_model].astype(F32)
         * jnp.dot(d_ref[...], wb_ref[...], preferred_element_type=F32)
         + gate_ref[:, 2 * d_model:].astype(F32)
         * jnp.dot(m.astype(BF16), wc_ref[...], preferred_element_type=F32))
    y = jnp.dot(u.astype(BF16), wo_ref[...], preferred_element_type=F32)
    o_ref[...] = _stream_tile(refs[:n_stream]) + g1_ref[...] * _rms(y, pg_ref[...])


def _mix(a, dd, hf, hb, out_gate, merge_gate, xs, mod, layer, nw, post_g, mlstm_g, wa, wb, wc, wo):
    (n_batch, ts, d), stream_specs, stream_arrays = _stream(xs)
    nt = ts // ROW
    width = a.shape[-1]

    def tile(w):
        return pl.BlockSpec((None, ROW, w), lambda b, i: (b, i, 0))

    def const(shape):
        return pl.BlockSpec(shape, lambda b, i: (0,) * len(shape))

    return pl.pallas_call(
        functools.partial(_mix_kernel, n_stream=len(stream_arrays)),
        grid=(n_batch, nt),
        in_specs=stream_specs + [
            tile(width), tile(width), tile(width), tile(width),
            tile(width), tile(3 * d),
            _mod_spec(d, layer, 2, nw, n_batch),
            const((1, d)), const((1, C_DIM)),
            const((width, d)), const((width, d)), const((width, d)), const((d, d)),
        ],
        out_specs=tile(d),
        out_shape=jax.ShapeDtypeStruct((n_batch, ts, d), F32),
        compiler_params=_cparams(("parallel", "parallel")),
        name="mix_out",
    )(*stream_arrays, a, dd, hf, hb, out_gate, merge_gate, mod, post_g, mlstm_g, wa, wb, wc, wo)


def _flat_mod_spec(d, layer, chunk, nw, n_batch, nt, n_sub, sub):
    base = (layer * 6 + chunk) * nw

    def index(k):
        t = n_sub * k + sub
        return (base + jnp.where(t % nt == 0, n_batch, t // nt), 0, 0)
    return pl.BlockSpec((None, 1, d), index)


def _ffn_kernel(x_ref, *refs, n_sub):
    mods = refs[:3 * n_sub]
    pre_ref, post_ref, wg_ref, wu_ref, wd_ref, o_ref = refs[3 * n_sub:]
    tiles = [x_ref[s * ROW:(s + 1) * ROW] for s in range(n_sub)]
    xn = [_rms(x, pre_ref[...]) * (1.0 + mods[3 * s + 1][...]) + mods[3 * s][...] for s, x in enumerate(tiles)]
    xb = jnp.concatenate(xn, axis=0).astype(BF16)
    gate = jnp.dot(xb, wg_ref[...], preferred_element_type=F32)
    up = jnp.dot(xb, wu_ref[...], preferred_element_type=F32)
    z = jnp.dot((_silu(gate) * up).astype(BF16), wd_ref[...], preferred_element_type=F32)
    for s, x in enumerate(tiles):
        o_ref[s * ROW:(s + 1) * ROW] = x + mods[3 * s + 2][...] * _rms(z[s * ROW:(s + 1) * ROW], post_ref[...])


def _ffn(xs, mod, layer, nw, pre_g, post_g, wg, wu, wd):
    n_batch, ts, d = xs.shape
    nt = ts // ROW
    n_tiles = n_batch * nt
    n_sub = 2 if n_tiles % 2 == 0 else 1
    dff = wg.shape[1]

    def resident(shape):
        return pl.BlockSpec(shape, lambda k: (0,) * len(shape), pipeline_mode=pl.Buffered(1))

    tile = pl.BlockSpec((n_sub * ROW, d), lambda k: (k, 0))
    mod_specs = [_flat_mod_spec(d, layer, chunk, nw, n_batch, nt, n_sub, sub)
                 for sub in range(n_sub) for chunk in (3, 4, 5)]
    out = pl.pallas_call(
        functools.partial(_ffn_kernel, n_sub=n_sub),
        grid=(n_tiles // n_sub,),
        in_specs=[tile] + mod_specs + [
            pl.BlockSpec((1, d), lambda k: (0, 0)), pl.BlockSpec((1, d), lambda k: (0, 0)),
            resident((d, dff)), resident((d, dff)), resident((dff, d)),
        ],
        out_specs=tile,
        out_shape=jax.ShapeDtypeStruct((n_batch * ts, d), F32),
        compiler_params=_cparams(("parallel",)),
        name="ffn",
    )(xs.reshape(n_batch * ts, d), *([mod] * (3 * n_sub)), pre_g, post_g, wg, wu, wd)
    return out.reshape(n_batch, ts, d)


def _route_kernel(x_ref, *refs, n_sub):
    mods = refs[:2 * n_sub]
    pre_ref, wr_ref, br_ref, tri_ref, xn_ref, meta_ref, slots_ref, cnt_ref, carry_ref = refs[2 * n_sub:]

    @pl.when(pl.program_id(0) == 0)
    def _():
        carry_ref[...] = jnp.zeros_like(carry_ref)

    lane = _lane_iota((ROW, LANES))
    routed = []
    for s in range(n_sub):
        rows = slice(s * ROW, (s + 1) * ROW)
        xn = _rms(x_ref[rows], pre_ref[...]) * (1.0 + mods[2 * s + 1][...]) + mods[2 * s][...]
        xn_ref[rows] = xn
        logits = jnp.dot(xn, wr_ref[...], preferred_element_type=F32, precision=HIGHEST) + br_ref[...]
        logits = jnp.where(lane < N_EXPERTS, logits, NEG)
        v1 = jnp.max(logits, axis=-1, keepdims=True)
        i1 = jnp.min(jnp.where(logits == v1, lane, LANES), axis=-1, keepdims=True)
        rest = jnp.where(lane == i1, NEG, logits)
        v2 = jnp.max(rest, axis=-1, keepdims=True)
        i2 = jnp.min(jnp.where(rest == v2, lane, LANES), axis=-1, keepdims=True)
        e2 = jnp.exp(v2 - v1)
        w1 = 1.0 / (1.0 + e2)
        w2 = e2 / (1.0 + e2)
        assigned = jnp.where(lane == i1, 1.0, jnp.where(lane == i2, 1.0, 0.0))
        local = jnp.dot(tri_ref[...], assigned.astype(BF16), preferred_element_type=F32)
        routed.append((i1, i2, w1, w2, assigned, local))

    carry = carry_ref[...]
    for s, (i1, i2, w1, w2, assigned, local) in enumerate(routed):
        rows = slice(s * ROW, (s + 1) * ROW)
        before = local + carry[0:1, :]
        r1 = jnp.sum(jnp.where(lane == i1, before, 0.0), axis=-1, keepdims=True)
        r2 = jnp.sum(jnp.where(lane == i2, before, 0.0), axis=-1, keepdims=True)
        carry = carry + jnp.sum(assigned, axis=0, keepdims=True)
        fields = (i1.astype(F32), i2.astype(F32), w1, w2, r1, r2)
        meta = jnp.zeros((ROW, LANES), F32)
        for f, val in enumerate(fields):
            meta = jnp.where(lane == f, val, meta)
        meta_ref[rows] = meta
        slots_ref[:, rows] = meta.T[:SUBLANES, :]
    carry_ref[...] = carry
    cnt_ref[...] = carry


def _route(xs, mod, layer, nw, pre_g, w_r, b_r):
    n_batch, ts, d = xs.shape
    nt = ts // ROW
    m_rows = n_batch * ts
    n_tiles = n_batch * nt
    n_sub = 2 if n_tiles % 2 == 0 else 1
    t_idx = jnp.arange(ROW)
    tri_strict = (t_idx[:, None] > t_idx[None, :]).astype(BF16)

    def const(shape):
        return pl.BlockSpec(shape, lambda k: (0,) * len(shape))

    rows = n_sub * ROW
    return pl.pallas_call(
        functools.partial(_route_kernel, n_sub=n_sub),
        grid=(n_tiles // n_sub,),
        in_specs=[pl.BlockSpec((rows, d), lambda k: (k, 0))]
        + [_flat_mod_spec(d, layer, chunk, nw, n_batch, nt, n_sub, sub) for sub in range(n_sub) for chunk in (3, 4)]
        + [const((1, d)), const((d, LANES)), const((1, LANES)), const((ROW, ROW))],
        out_specs=[pl.BlockSpec((rows, d), lambda k: (k, 0)),
                   pl.BlockSpec((rows, LANES), lambda k: (k, 0)),
                   pl.BlockSpec((SUBLANES, rows), lambda k: (0, k)),
                   const((SUBLANES, LANES))],
        out_shape=[jax.ShapeDtypeStruct((m_rows, d), F32),
                   jax.ShapeDtypeStruct((m_rows, LANES), F32),
                   jax.ShapeDtypeStruct((SUBLANES, m_rows), F32),
                   jax.ShapeDtypeStruct((SUBLANES, LANES), F32)],
        scratch_shapes=[pltpu.VMEM((SUBLANES, LANES), F32)],
        compiler_params=_cparams(("arbitrary",)),
        name="route",
    )(xs.reshape(m_rows, d), *([mod] * (2 * n_sub)), pre_g, w_r, b_r, tri_strict)


GROUPS = ROW // SUBLANES


def _start_rows(make_copy):
    for g in range(GROUPS):
        for j in range(SUBLANES):
            for slot in range(2):
                make_copy(g, j, slot).start()


def _wait_rows(make_copy):
    def drain(g, carry):
        for j in range(SUBLANES):
            for slot in range(2):
                make_copy(0, 0, slot).wait()
        return carry

    lax.fori_loop(0, GROUPS, drain, 0)


STAGES = 3


def _dispatch_kernel(pos_ref, bounds_ref, xn_ref, out_ref, zero_ref, stage_ref, row_sems, in_sems, zero_sem):
    step = pl.program_id(0)
    n_steps = pl.num_programs(0)
    n_tok = pos_ref.shape[0] // 2
    tm = zero_ref.shape[0]
    n_sorted = out_ref.shape[0]

    @pl.when(step == 0)
    def _():
        zero_ref[...] = jnp.zeros_like(zero_ref)

        def fill(row):
            copy = pltpu.make_async_copy(zero_ref, out_ref.at[pl.ds(pl.multiple_of(row, tm), tm)], zero_sem)
            copy.start()
            copy.wait()

        for e in range(N_EXPERTS):
            end = bounds_ref[N_EXPERTS + e]

            @pl.when(end > bounds_ref[e])
            def _():
                fill(end - tm)

        last_end = bounds_ref[2 * N_EXPERTS - 1]

        def tail(k, carry):
            fill(last_end + k * tm)
            return carry

        lax.fori_loop(0, (n_sorted - last_end) // tm, tail, 0)

    def tile_in(tile):
        slot = tile % STAGES
        return pltpu.make_async_copy(xn_ref.at[pl.ds(tile * GROUPS, GROUPS)], stage_ref.at[slot], in_sems.at[slot])

    def copies(tile):
        def make_copy(g, j, slot):
            p = pos_ref[slot * n_tok + tile * ROW + g * SUBLANES + j]
            return pltpu.make_async_copy(stage_ref.at[tile % STAGES, g, pl.ds(j, 1)], out_ref.at[pl.ds(p, 1)],
                                         row_sems.at[tile % STAGES])
        return make_copy

    @pl.when(step == 0)
    def _():
        tile_in(step).start()

    @pl.when(step >= STAGES - 1)
    def _():
        _wait_rows(copies(step - (STAGES - 1)))

    @pl.when(step + 1 < n_steps)
    def _():
        tile_in(step + 1).start()

    tile_in(step).wait()
    _start_rows(copies(step))

    @pl.when(step == n_steps - 1)
    def _():
        for back in range(STAGES - 2, -1, -1):
            @pl.when(step - back >= 0)
            def _():
                _wait_rows(copies(step - back))


def _dispatch(pos, bounds, xn, n_sorted, tm):
    m_rows, d = xn.shape
    return pl.pallas_call(
        _dispatch_kernel,
        grid_spec=pltpu.PrefetchScalarGridSpec(
            num_scalar_prefetch=2,
            grid=(m_rows // ROW,),
            in_specs=[pl.BlockSpec(memory_space=pl.ANY)],
            out_specs=pl.BlockSpec(memory_space=pl.ANY),
            scratch_shapes=[pltpu.VMEM((tm, d), F32), pltpu.VMEM((STAGES, GROUPS, SUBLANES, d), F32),
                            pltpu.SemaphoreType.DMA((STAGES,)), pltpu.SemaphoreType.DMA((STAGES,)),
                            pltpu.SemaphoreType.DMA(())],
        ),
        out_shape=jax.ShapeDtypeStruct((n_sorted, d), F32),
        compiler_params=_cparams(("arbitrary",)),
        name="dispatch",
    )(pos, bounds, xn.reshape(m_rows // SUBLANES, SUBLANES, d))


def _experts_kernel(te_ref, x_ref, wg_ref, wu_ref, wd_ref, y_ref):
    used = te_ref[pl.program_id(0)] < N_EXPERTS

    @pl.when(used)
    def _():
        x = x_ref[...].astype(BF16)
        gate = jnp.dot(x, wg_ref[...], preferred_element_type=F32)
        up = jnp.dot(x, wu_ref[...], preferred_element_type=F32)
        y_ref[...] = jnp.dot((_silu(gate) * up).astype(BF16), wd_ref[...], preferred_element_type=F32)

    @pl.when(jnp.logical_not(used))
    def _():
        y_ref[...] = jnp.zeros_like(y_ref)


def _experts(tile_expert, x_sorted, wg, wu, wd, tm):
    n_sorted, d = x_sorted.shape
    n_e, _, dff = wg.shape

    def weight(shape):
        return pl.BlockSpec((None,) + shape, lambda i, te: (jnp.minimum(te[i], n_e - 1), 0, 0))

    return pl.pallas_call(
        _experts_kernel,
        grid_spec=pltpu.PrefetchScalarGridSpec(
            num_scalar_prefetch=1,
            grid=(n_sorted // tm,),
            in_specs=[pl.BlockSpec((tm, d), lambda i, te: (i, 0)),
                      weight((d, dff)), weight((d, dff)), weight((dff, d))],
            out_specs=pl.BlockSpec((tm, d), lambda i, te: (i, 0)),
        ),
        out_shape=jax.ShapeDtypeStruct((n_sorted, d), F32),
        compiler_params=_cparams(("arbitrary",)),
        name="experts",
    )(tile_expert, x_sorted, wg, wu, wd)


def _combine_kernel(pos_ref, x_ref, meta_ref, g2_ref, post_ref, y_ref, o_ref, buf_ref, sems, *,
                    tiles_per_sample, latent_only):
    step = pl.program_id(0)
    n_steps = pl.num_programs(0)
    n_tok = pos_ref.shape[0] // 2
    d_model = x_ref.shape[-1]

    def wanted(tile):
        return (tile % tiles_per_sample != 0) if latent_only else (tile >= 0)

    def copies(tile):
        def make_copy(g, j, slot):
            p = pos_ref[slot * n_tok + tile * ROW + g * SUBLANES + j]
            return pltpu.make_async_copy(y_ref.at[pl.ds(p, 1)], buf_ref.at[tile % 2, slot, g, pl.ds(j, 1)],
                                         sems.at[tile % 2])
        return make_copy

    @pl.when(jnp.logical_and(step == 0, wanted(step)))
    def _():
        _start_rows(copies(step))

    @pl.when(jnp.logical_and(step + 1 < n_steps, wanted(step + 1)))
    def _():
        _start_rows(copies(step + 1))

    @pl.when(wanted(step))
    def _():
        _wait_rows(copies(step))
        meta = meta_ref[...]
        cur = step % 2
        y1 = buf_ref[cur, 0].reshape(ROW, d_model)
        y2 = buf_ref[cur, 1].reshape(ROW, d_model)
        z = meta[:, 2:3] * y1 + meta[:, 3:4] * y2
        o_ref[...] = x_ref[...] + g2_ref[...] * _rms(z, post_ref[...])


def _combine(pos, xs, meta, y_sorted, mod, layer, nw, post_g, latent_only):
    n_batch, ts, d = xs.shape
    nt = ts // ROW
    m_rows = n_batch * ts
    base = (layer * 6 + 5) * nw
    tile = pl.BlockSpec((ROW, d), lambda i, pos: (i, 0))
    if latent_only:
        out_rows = n_batch * (ts - ROW)
        out_tile = pl.BlockSpec((ROW, d), lambda i, pos: ((i // nt) * (nt - 1) + jnp.maximum(i % nt - 1, 0), 0))
    else:
        out_rows, out_tile = m_rows, tile
    out = pl.pallas_call(
        functools.partial(_combine_kernel, tiles_per_sample=nt, latent_only=latent_only),
        grid_spec=pltpu.PrefetchScalarGridSpec(
            num_scalar_prefetch=1,
            grid=(m_rows // ROW,),
            in_specs=[
                tile,
                pl.BlockSpec((ROW, LANES), lambda i, pos: (i, 0)),
                pl.BlockSpec((None, 1, d), lambda i, pos: (base + jnp.where(i % nt == 0, n_batch, i // nt), 0, 0)),
                pl.BlockSpec((1, d), lambda i, pos: (0, 0)),
                pl.BlockSpec(memory_space=pl.ANY),
            ],
            out_specs=out_tile,
            scratch_shapes=[pltpu.VMEM((2, 2, GROUPS, SUBLANES, d), F32), pltpu.SemaphoreType.DMA((2,))],
        ),
        out_shape=jax.ShapeDtypeStruct((out_rows, d), F32),
        compiler_params=_cparams(("arbitrary",)),
        name="combine",
    )(pos, xs.reshape(m_rows, d), meta.reshape(m_rows, LANES), mod, post_g, y_sorted)
    return out.reshape(n_batch, out_rows // n_batch, d)


def _moe(xs, mod, layer, nw, pre_g, post_g, w_r, b_r, wg, wu, wd, latent_only, tm=512):
    n_batch, ts, d = xs.shape
    m_rows = n_batch * ts
    xn, meta, slots, cnt = _route(xs, mod, layer, nw, pre_g, w_r, b_r)
    i1, i2, r1, r2 = (slots[f].astype(jnp.int32) for f in (0, 1, 4, 5))
    counts = cnt[0, :N_EXPERTS].astype(jnp.int32)
    padded = -(-counts // tm) * tm
    ends = jnp.cumsum(padded)
    start = ends - padded
    pos = jnp.concatenate([start[i1] + r1, start[i2] + r2])
    bounds = jnp.concatenate([start, ends])
    n_tiles = 2 * m_rows // tm + N_EXPERTS
    tile_row = jnp.arange(n_tiles, dtype=jnp.int32) * tm
    tile_expert = jnp.sum((ends[None, :] <= tile_row[:, None]).astype(jnp.int32), axis=1)
    x_sorted = _dispatch(pos, bounds, xn.reshape(m_rows, d), n_tiles * tm, tm)
    y_sorted = _experts(tile_expert, x_sorted, wg, wu, wd, tm)
    return _combine(pos, xs, meta, y_sorted, mod, layer, nw, post_g, latent_only)


def _rope_tables(n_tok, n_ctx):
    f32 = np.float32
    n_freq = HEAD_DIM // 4
    pos = np.arange(n_tok)
    row = (pos // GRID_W).astype(f32)
    colp = (pos % GRID_W).astype(f32)
    inv = (f32(ROPE_THETA) ** (-np.arange(n_freq, dtype=f32) / f32(n_freq))).astype(f32)
    lane = np.arange(LANES)
    in_head = lane % HEAD_DIM
    use_col = (in_head // (HEAD_DIM // 2)) == 1
    freq = inv[in_head % n_freq]
    ang = (np.where(use_col[None, :], colp[:, None], row[:, None]) * freq[None, :]).astype(f32)
    lower = (in_head % (HEAD_DIM // 2)) < n_freq
    cos_t = np.cos(ang).astype(f32)
    sin_t = np.sin(ang).astype(f32)
    sin_a = np.where(lower[None, :], -sin_t, f32(0.0)).astype(f32)
    sin_b = np.where(lower[None, :], f32(0.0), sin_t).astype(f32)
    pad = lambda t, v: np.concatenate([np.full((n_ctx, LANES), v, f32), t], axis=0)
    return jnp.asarray(pad(cos_t, 1.0)), jnp.asarray(pad(sin_a, 0.0)), jnp.asarray(pad(sin_b, 0.0))


def _block_diag_mean(width):
    idx = jnp.arange(width) // HEAD_DIM
    return jnp.where(idx[:, None] == idx[None, :], 1.0 / HEAD_DIM, 0.0).astype(BF16)


def _pack_w_in(w):
    a_q = A_HEADS * HEAD_DIM
    a_kv = 2 * A_KV_HEADS * HEAD_DIM
    n_gates = 4 * C_HEADS
    gate_start = w.shape[1] - 3 * w.shape[0]
    g_start = gate_start - n_gates
    main = jnp.concatenate([w[:, gate_start:], w[:, :a_q], w[:, a_q + a_kv:g_start], w[:, a_q:a_q + a_kv]], axis=1)
    gates = jnp.pad(w[:, g_start:gate_start], ((0, 0), (0, LANES - n_gates)))
    return main.astype(BF16), gates.astype(BF16)


def kernel(x, c, ctx, c_ctx, ada_w, ada_b, pre_mix_g, post_mix_g, pre_ffn_g, post_ffn_g, w_in, q_norm_g, k_norm_g, lam_q1, lam_k1, lam_q2, lam_k2, diff_norm_g, conv_w, conv_b, mlstm_gate_b, mlstm_norm_g, w_br_attn, w_br_diff, w_br_mlstm, w_out, w_ff_gate, w_ff_up, w_ff_down, w_router, b_router, w_moe_gate, w_moe_up, w_moe_down):
    n_batch, n_tok, d = x.shape
    n_ctx = ctx.shape[1]
    depth = ada_w.shape[0]
    assert n_ctx == ROW and n_tok % ROW == 0 and d == 1024
    ts = n_ctx + n_tok
    nw = -(-(n_batch + 1) // SUBLANES) * SUBLANES

    c_all = jnp.concatenate([c, c_ctx[None, :], jnp.zeros((nw - n_batch - 1, d), F32)], axis=0)
    mod = _modulation(c_all, ada_w, ada_b)
    tables = _rope_tables(n_tok, n_ctx)
    bd4, bd1 = _block_diag_mean(A_HEADS * HEAD_DIM), _block_diag_mean(LANES)
    t_idx = jnp.arange(MCHUNK)
    tri = (t_idx[:, None] >= t_idx[None, :]).astype(BF16)

    xs = (ctx, x)
    for l in range(depth):
        lam_init = 0.8 - 0.6 * math.exp(-0.3 * l)
        w_main, w_gate = _pack_w_in(w_in[l])
        qg = jnp.tile(q_norm_g[l], A_HEADS)[None, :]
        kg = jnp.tile(k_norm_g[l], A_KV_HEADS)[None, :]
        gate_b = jnp.pad(mlstm_gate_b[l], (0, LANES - 4 * C_HEADS))[None, :]
        merge_gate, out_gate, qaz, ka, va, qbz, kb, vb, qmt, km, vmt, gcol, grow = _inproj(
            xs, mod, l, nw, pre_mix_g[l][None, :], w_main, w_gate,
            tables, qg, kg, bd4, bd1, conv_w[l], conv_b[l][None, :], gate_b, tri)
        a_out = _gqa(qaz, ka, va)
        lam_vecs = jnp.stack([lam_q1[l], lam_k1[l], lam_q2[l], lam_k2[l]], axis=0)
        d_out = _diff(qbz, kb, vb, lam_vecs, diff_norm_g[l][None, :], lam_init)
        hf, hb = _mlstm(qmt, km, vmt, gcol, grow)
        xs = _mix(a_out, d_out, hf, hb, out_gate, merge_gate, xs, mod, l, nw,
                  post_mix_g[l][None, :], mlstm_norm_g[l][None, :],
                  w_br_attn[l].astype(BF16), w_br_diff[l].astype(BF16), w_br_mlstm[l].astype(BF16),
                  w_out[l].astype(BF16))
        j = l // 2
        if l % 2 == 0:
            xs = _ffn(xs, mod, l, nw, pre_ffn_g[l][None, :], post_ffn_g[l][None, :],
                      w_ff_gate[j].astype(BF16), w_ff_up[j].astype(BF16), w_ff_down[j].astype(BF16))
        else:
            w_r = jnp.pad(w_router[j], ((0, 0), (0, LANES - N_EXPERTS)))
            b_r = jnp.pad(b_router[j], (0, LANES - N_EXPERTS))[None, :]
            xs = _moe(xs, mod, l, nw, pre_ffn_g[l][None, :], post_ffn_g[l][None, :], w_r, b_r,
                      w_moe_gate[j].astype(BF16), w_moe_up[j].astype(BF16), w_moe_down[j].astype(BF16),
                      latent_only=l == depth - 1)
    return xs if xs.shape[1] == n_tok else xs[:, n_ctx:, :]
```
